```python
import jax, jax.numpy as jnp
from jax import lax
import numpy as np

D_MODEL = 1024
BATCH = 4
SEQ = 4096
DEPTH = 4
DEC_BATCH = 128
DEC_SEQ = 8
PAST_LEN = 8192
PAGE_SIZE = 128

N_MIXERS = 3
N_ATTN = (DEPTH + 2) // 3
N_RGLRU = (DEPTH + 1) // 3
N_SCONV = DEPTH // 3

HEAD_DIM = 64
N_HEADS = D_MODEL // HEAD_DIM
N_KV_HEADS = max(1, N_HEADS // 8)
GQA_GROUP = N_HEADS // N_KV_HEADS
Q_DIM = N_HEADS * HEAD_DIM
KV_DIM = N_KV_HEADS * HEAD_DIM
WINDOW = 128
ROPE_THETA = 10000.0
NEG_INF = -1e30

D_RNN = D_MODEL
RG_BLOCKS = 4
RG_BLOCK_W = D_RNN // RG_BLOCKS
RG_CONV_W = 4
RG_C = 8.0

D_SCONV = D_MODEL
SCONV_W = 3

D_FF = -(-8 * D_MODEL // (3 * 256)) * 256

EPS = 1e-6

kernel_name = "hybrid_swa_rglru_shortconv_decoder_step"


def rmsnorm(x, g):
    xf = x.astype(jnp.float32)
    y = xf * lax.rsqrt(jnp.mean(xf * xf, axis=-1, keepdims=True) + EPS)
    return (y * g.astype(jnp.float32)).astype(x.dtype)


def rope(x, pos):
    half = HEAD_DIM // 2
    inv = ROPE_THETA ** (-jnp.arange(half, dtype=jnp.float32) / half)
    ang = pos.astype(jnp.float32)[:, None] * inv[None, :]
    cos = jnp.cos(ang)[None, :, None, :]
    sin = jnp.sin(ang)[None, :, None, :]
    xf = x.astype(jnp.float32)
    x1, x2 = xf[..., :half], xf[..., half:]
    out = jnp.concatenate([x1 * cos - x2 * sin, x2 * cos + x1 * sin], axis=-1)
    return out.astype(x.dtype)


def swa_project(h, w_qkv, b_qkv, pos):
    B, T, _ = h.shape
    qkv = h @ w_qkv + b_qkv
    q = qkv[..., :Q_DIM].reshape(B, T, N_HEADS, HEAD_DIM)
    k = qkv[..., Q_DIM:Q_DIM + KV_DIM].reshape(B, T, N_KV_HEADS, HEAD_DIM)
    v = qkv[..., Q_DIM + KV_DIM:].reshape(B, T, N_KV_HEADS, HEAD_DIM)
    q = rope(q, pos).reshape(B, T, N_KV_HEADS, GQA_GROUP, HEAD_DIM)
    k = rope(k, pos)
    return q, k, v


def attn_core(q, k, v, mask, sinks):
    s = jnp.einsum('...qkgd,...skd->...kgqs', q.astype(jnp.float32), k.astype(jnp.float32)) * (HEAD_DIM ** -0.5)
    s = jnp.where(mask, s, NEG_INF)
    sink = jnp.broadcast_to(sinks.astype(jnp.float32).reshape(N_KV_HEADS, GQA_GROUP, 1, 1), s.shape[:-1] + (1,))
    p = jax.nn.softmax(jnp.concatenate([s, sink], axis=-1), axis=-1)[..., :-1]
    return jnp.einsum('...kgqs,...skd->...qkgd', p.astype(v.dtype), v)


def swa_prompt(h, w_qkv, b_qkv, w_o, b_o, sinks):
    B, T, _ = h.shape
    pos = jnp.arange(T, dtype=jnp.int32)
    q, k, v = swa_project(h, w_qkv, b_qkv, pos)
    nb = T // WINDOW
    qb = q.reshape(B, nb, WINDOW, N_KV_HEADS, GQA_GROUP, HEAD_DIM)
    kb = k.reshape(B, nb, WINDOW, N_KV_HEADS, HEAD_DIM)
    vb = v.reshape(B, nb, WINDOW, N_KV_HEADS, HEAD_DIM)
    kk = jnp.concatenate([jnp.concatenate([jnp.zeros_like(kb[:, :1]), kb[:, :-1]], axis=1), kb], axis=2)
    vv = jnp.concatenate([jnp.concatenate([jnp.zeros_like(vb[:, :1]), vb[:, :-1]], axis=1), vb], axis=2)
    i = jnp.arange(WINDOW)[:, None]
    c = jnp.arange(2 * WINDOW)[None, :]
    diff = i + WINDOW - c
    blk = jnp.arange(nb)[:, None, None]
    mask = (diff >= 0) & (diff < WINDOW) & (blk * WINDOW + c - WINDOW >= 0)
    o = attn_core(qb, kk, vv, mask[:, None, None], sinks)
    out = o.reshape(B, T, Q_DIM) @ w_o + b_o
    return out, k[:, T - WINDOW:], v[:, T - WINDOW:]


def swa_sample(h, ck, cv, w_qkv, b_qkv, w_o, b_o, sinks):
    B, T, _ = h.shape
    pos = PAST_LEN + jnp.arange(T, dtype=jnp.int32)
    q, k, v = swa_project(h, w_qkv, b_qkv, pos)
    kk = jnp.concatenate([ck.astype(k.dtype), k], axis=1)
    vv = jnp.concatenate([cv.astype(v.dtype), v], axis=1)
    i = jnp.arange(T)[:, None]
    c = jnp.arange(WINDOW + T)[None, :]
    diff = i + WINDOW - c
    mask = (diff >= 0) & (diff < WINDOW)
    o = attn_core(q, kk, vv, mask, sinks)
    out = o.reshape(B, T, Q_DIM) @ w_o + b_o
    return out, kk[:, T:], vv[:, T:]


def causal_dwconv(u, buf, w):
    T = u.shape[1]
    kw = w.shape[0]
    ext = jnp.concatenate([buf.astype(u.dtype), u], axis=1)
    y = ext[:, 0:T] * w[0]
    for j in range(1, kw):
        y = y + ext[:, j:j + T] * w[j]
    return y, ext[:, -(kw - 1):]


def linear_recurrence(a, b, h0):
    def combine(left, right):
        a1, b1 = left
        a2, b2 = right
        return a1 * a2, a2 * b1 + b2
    a_cum, b_cum = lax.associative_scan(combine, (a, b), axis=1)
    return a_cum * h0[:, None] + b_cum


def rglru_mixer(h, h0, conv_buf, w_gate, w_in, conv_w, conv_b, wa, ba, wx, bx, lam, w_out):
    B, T, _ = h.shape
    gate = jax.nn.gelu(h @ w_gate)
    u, new_buf = causal_dwconv(h @ w_in, conv_buf, conv_w)
    u = u + conv_b
    ub = u.reshape(B, T, RG_BLOCKS, RG_BLOCK_W)
    r = jax.nn.sigmoid(jnp.einsum('btnd,nde->btne', ub, wa).reshape(B, T, D_RNN) + ba)
    ig = jax.nn.sigmoid(jnp.einsum('btnd,nde->btne', ub, wx).reshape(B, T, D_RNN) + bx)
    log_a = RG_C * r.astype(jnp.float32) * jax.nn.log_sigmoid(lam.astype(jnp.float32))
    a = jnp.exp(log_a)
    mult = jnp.sqrt(-jnp.expm1(2.0 * log_a))
    hs = linear_recurrence(a, mult * (ig * u).astype(jnp.float32), h0.astype(jnp.float32))
    y = (hs.astype(h.dtype) * gate) @ w_out
    return y, hs[:, -1].astype(h0.dtype), new_buf


def sconv_mixer(h, buf, w_in, conv_w, w_out):
    bcx = h @ w_in
    bg = bcx[..., :D_SCONV]
    cg = bcx[..., D_SCONV:2 * D_SCONV]
    xv = bcx[..., 2 * D_SCONV:]
    y, new_buf = causal_dwconv(cg * xv, buf, conv_w)
    return (bg * y) @ w_out, new_buf


def swiglu(h, wg, wu, wd):
    return (jax.nn.silu(h @ wg) * (h @ wu)) @ wd


def setup_inputs(seed: int = 0) -> dict:
    key = jax.random.key(seed)
    keys = iter(jax.random.split(key, 64))

    def nrm(shape, scale):
        return scale * jax.random.normal(next(keys), shape, jnp.float32)

    x_prompt = nrm((BATCH, SEQ, D_MODEL), 1.0)
    x_sample = nrm((DEC_BATCH, DEC_SEQ, D_MODEL), 1.0)
    cache_k = nrm((N_ATTN, DEC_BATCH, WINDOW, N_KV_HEADS, HEAD_DIM), 1.0)
    cache_v = nrm((N_ATTN, DEC_BATCH, WINDOW, N_KV_HEADS, HEAD_DIM), 1.0)
    state_rglru_h = nrm((N_RGLRU, DEC_BATCH, D_RNN), 0.5)
    state_rglru_conv = nrm((N_RGLRU, DEC_BATCH, RG_CONV_W - 1, D_RNN), 1.0)
    state_shortconv = nrm((N_SCONV, DEC_BATCH, SCONV_W - 1, D_SCONV), 1.0)
    norm_mixer = 1.0 + nrm((DEPTH, D_MODEL), 0.02)
    norm_ffn = 1.0 + nrm((DEPTH, D_MODEL), 0.02)
    norm_final = 1.0 + nrm((D_MODEL,), 0.02)
    attn_w_qkv = nrm((N_ATTN, D_MODEL, Q_DIM + 2 * KV_DIM), D_MODEL ** -0.5)
    attn_b_qkv = nrm((N_ATTN, Q_DIM + 2 * KV_DIM), 0.02)
    attn_w_o = nrm((N_ATTN, Q_DIM, D_MODEL), Q_DIM ** -0.5)
    attn_b_o = nrm((N_ATTN, D_MODEL), 0.02)
    attn_sinks = nrm((N_ATTN, N_HEADS), 0.5)
    rglru_w_gate = nrm((N_RGLRU, D_MODEL, D_RNN), D_MODEL ** -0.5)
    rglru_w_in = nrm((N_RGLRU, D_MODEL, D_RNN), D_MODEL ** -0.5)
    rglru_conv_w = nrm((N_RGLRU, RG_CONV_W, D_RNN), RG_CONV_W ** -0.5)
    rglru_conv_b = nrm((N_RGLRU, D_RNN), 0.02)
    rglru_wa = nrm((N_RGLRU, RG_BLOCKS, RG_BLOCK_W, RG_BLOCK_W), RG_BLOCK_W ** -0.5)
    rglru_ba = nrm((N_RGLRU, D_RNN), 0.02)
    rglru_wx = nrm((N_RGLRU, RG_BLOCKS, RG_BLOCK_W, RG_BLOCK_W), RG_BLOCK_W ** -0.5)
    rglru_bx = nrm((N_RGLRU, D_RNN), 0.02)
    a_c = jax.random.uniform(next(keys), (N_RGLRU, D_RNN), jnp.float32, 0.9, 0.999)
    sig = a_c ** (1.0 / RG_C)
    rglru_lambda = jnp.log(sig) - jnp.log1p(-sig)
    rglru_w_out = nrm((N_RGLRU, D_RNN, D_MODEL), D_RNN ** -0.5)
    sconv_w_in = nrm((N_SCONV, D_MODEL, 3 * D_SCONV), D_MODEL ** -0.5)
    sconv_conv_w = nrm((N_SCONV, SCONV_W, D_SCONV), SCONV_W ** -0.5)
    sconv_w_out = nrm((N_SCONV, D_SCONV, D_MODEL), D_SCONV ** -0.5)
    ffn_w_gate = nrm((DEPTH, D_MODEL, D_FF), D_MODEL ** -0.5)
    ffn_w_up = nrm((DEPTH, D_MODEL, D_FF), D_MODEL ** -0.5)
    ffn_w_down = nrm((DEPTH, D_FF, D_MODEL), D_FF ** -0.5)
    return {
        "x_prompt": x_prompt, "x_sample": x_sample,
        "cache_k": cache_k, "cache_v": cache_v,
        "state_rglru_h": state_rglru_h, "state_rglru_conv": state_rglru_conv,
        "state_shortconv": state_shortconv,
        "norm_mixer": norm_mixer, "norm_ffn": norm_ffn, "norm_final": norm_final,
        "attn_w_qkv": attn_w_qkv, "attn_b_qkv": attn_b_qkv, "attn_w_o": attn_w_o,
        "attn_b_o": attn_b_o, "attn_sinks": attn_sinks,
        "rglru_w_gate": rglru_w_gate, "rglru_w_in": rglru_w_in, "rglru_conv_w": rglru_conv_w,
        "rglru_conv_b": rglru_conv_b, "rglru_wa": rglru_wa, "rglru_ba": rglru_ba,
        "rglru_wx": rglru_wx, "rglru_bx": rglru_bx, "rglru_lambda": rglru_lambda,
        "rglru_w_out": rglru_w_out,
        "sconv_w_in": sconv_w_in, "sconv_conv_w": sconv_conv_w, "sconv_w_out": sconv_w_out,
        "ffn_w_gate": ffn_w_gate, "ffn_w_up": ffn_w_up, "ffn_w_down": ffn_w_down,
    }


def reference(x_prompt, x_sample, cache_k, cache_v, state_rglru_h, state_rglru_conv, state_shortconv,
              norm_mixer, norm_ffn, norm_final,
              attn_w_qkv, attn_b_qkv, attn_w_o, attn_b_o, attn_sinks,
              rglru_w_gate, rglru_w_in, rglru_conv_w, rglru_conv_b, rglru_wa, rglru_ba,
              rglru_wx, rglru_bx, rglru_lambda, rglru_w_out,
              sconv_w_in, sconv_conv_w, sconv_w_out,
              ffn_w_gate, ffn_w_up, ffn_w_down):
    xp, xs = x_prompt, x_sample
    Bp = xp.shape[0]
    kp_l, vp_l, ks_l, vs_l = [], [], [], []
    hp_l, hs_l, rcp_l, rcs_l = [], [], [], []
    scp_l, scs_l = [], []
    for i in range(DEPTH):
        kind = i % N_MIXERS
        j = i // N_MIXERS
        hp = rmsnorm(xp, norm_mixer[i])
        hs = rmsnorm(xs, norm_mixer[i])
        if kind == 0:
            mp, kp, vp = swa_prompt(hp, attn_w_qkv[j], attn_b_qkv[j], attn_w_o[j], attn_b_o[j], attn_sinks[j])
            ms, ks, vs = swa_sample(hs, cache_k[j], cache_v[j], attn_w_qkv[j], attn_b_qkv[j],
                                    attn_w_o[j], attn_b_o[j], attn_sinks[j])
            kp_l.append(kp); vp_l.append(vp); ks_l.append(ks); vs_l.append(vs)
        elif kind == 1:
            rg = (rglru_w_gate[j], rglru_w_in[j], rglru_conv_w[j], rglru_conv_b[j], rglru_wa[j],
                  rglru_ba[j], rglru_wx[j], rglru_bx[j], rglru_lambda[j], rglru_w_out[j])
            h0p = jnp.zeros((Bp, D_RNN), xp.dtype)
            bufp = jnp.zeros((Bp, RG_CONV_W - 1, D_RNN), xp.dtype)
            mp, hfp, rcp = rglru_mixer(hp, h0p, bufp, *rg)
            ms, hfs, rcs = rglru_mixer(hs, state_rglru_h[j], state_rglru_conv[j], *rg)
            hp_l.append(hfp); hs_l.append(hfs); rcp_l.append(rcp); rcs_l.append(rcs)
        else:
            bufp = jnp.zeros((Bp, SCONV_W - 1, D_SCONV), xp.dtype)
            mp, scp = sconv_mixer(hp, bufp, sconv_w_in[j], sconv_conv_w[j], sconv_w_out[j])
            ms, scs = sconv_mixer(hs, state_shortconv[j], sconv_w_in[j], sconv_conv_w[j], sconv_w_out[j])
            scp_l.append(scp); scs_l.append(scs)
        xp = xp + mp
        xs = xs + ms
        xp = xp + swiglu(rmsnorm(xp, norm_ffn[i]), ffn_w_gate[i], ffn_w_up[i], ffn_w_down[i])
        xs = xs + swiglu(rmsnorm(xs, norm_ffn[i]), ffn_w_gate[i], ffn_w_up[i], ffn_w_down[i])
    y_prompt = rmsnorm(xp, norm_final)
    y_sample = rmsnorm(xs, norm_final)
    new_k_prompt = jnp.stack(kp_l)
    new_v_prompt = jnp.stack(vp_l)
    new_k_sample = jnp.stack(ks_l)
    new_v_sample = jnp.stack(vs_l)
    new_h_prompt = jnp.stack(hp_l)
    new_h_sample = jnp.stack(hs_l)
    new_rconv_prompt = jnp.stack(rcp_l)
    new_rconv_sample = jnp.stack(rcs_l)
    new_sconv_prompt = jnp.stack(scp_l)
    new_sconv_sample = jnp.stack(scs_l)
    return (y_prompt, y_sample, new_k_prompt, new_v_prompt, new_k_sample, new_v_sample,
            new_h_prompt, new_h_sample, new_rconv_prompt, new_rconv_sample,
            new_sconv_prompt, new_sconv_sample)
```

```python
import functools

import jax
import jax.numpy as jnp
from jax import lax
from jax.experimental import pallas as pl
from jax.experimental.pallas import tpu as pltpu

D_MODEL = 1024
HEAD_DIM = 64
N_HEADS = 16
N_KV_HEADS = 2
GQA_GROUP = 8
Q_DIM = N_HEADS * HEAD_DIM
KV_DIM = N_KV_HEADS * HEAD_DIM
WINDOW = 128
ROPE_THETA = 10000.0
NEG_INF = -1e30
RG_BLOCKS = 4
RG_BLOCK_W = 256
RG_CONV_W = 4
RG_C = 8.0
SCONV_W = 3
EPS = 1e-6

V7X_LANES = 128
V7X_SUBLANES = 8
V7X_MXU_DIM = 256
V7X_VMEM_BYTES = 64 * 1024 * 1024

BF16 = jnp.bfloat16
F32 = jnp.float32

FFN_CHUNK = V7X_MXU_DIM
ROW_TILE = 512
RG_ROW_TILE = 256
SAMPLE_ATTN_BATCHES = 32
COLS_PER_KV = (N_HEADS // N_KV_HEADS) * HEAD_DIM // V7X_LANES


def _vmem_limit(nbytes):
    return int(min(nbytes, V7X_VMEM_BYTES - 8 * 1024 * 1024))


def _const_spec(shape):
    nd = len(shape)
    return pl.BlockSpec(shape, lambda *_: (0,) * nd, pipeline_mode=pl.Buffered(1))


def _rms(x, g):
    ms = jnp.mean(x * x, axis=-1, keepdims=True)
    return x * lax.rsqrt(ms + EPS) * g


def _mm(a, w):
    return jnp.dot(a.astype(BF16), w, preferred_element_type=F32)


def _ffn_kernel(x_ref, g_ref, w1_ref, wd_ref, gf_ref, o_ref, *, n_chunks, final_norm):
    x = x_ref[...]
    h = _rms(x, g_ref[...]).astype(BF16)
    acc = x
    for c in range(n_chunks):
        gu = jnp.dot(h, w1_ref[:, c * 2 * FFN_CHUNK:(c + 1) * 2 * FFN_CHUNK],
                     preferred_element_type=F32)
        gate = gu[:, :FFN_CHUNK]
        up = gu[:, FFN_CHUNK:]
        act = (gate * jax.nn.sigmoid(gate)) * up
        acc = acc + jnp.dot(act.astype(BF16), wd_ref[c * FFN_CHUNK:(c + 1) * FFN_CHUNK, :],
                            preferred_element_type=F32)
    if final_norm:
        acc = _rms(acc, gf_ref[...])
    o_ref[...] = acc


def _ffn(x, g, w1, wd, gf, final_norm):
    m = x.shape[0]
    tm = min(ROW_TILE, m)
    d_ff = wd.shape[0]
    n_chunks = d_ff // FFN_CHUNK
    row = pl.BlockSpec((tm, D_MODEL), lambda i: (i, 0))
    weight_bytes = 2 * (w1.size + wd.size)
    return pl.pallas_call(
        functools.partial(_ffn_kernel, n_chunks=n_chunks, final_norm=final_norm),
        grid=(m // tm,),
        in_specs=[row, _const_spec((1, D_MODEL)), _const_spec(w1.shape), _const_spec(wd.shape),
                  _const_spec((1, D_MODEL))],
        out_specs=row,
        out_shape=jax.ShapeDtypeStruct((m, D_MODEL), F32),
        compiler_params=pltpu.CompilerParams(
            dimension_semantics=("arbitrary",),
            vmem_limit_bytes=_vmem_limit(weight_bytes + 24 * tm * D_MODEL * 4)),
        name="ffn",
    )(x, g, w1, wd, gf)


def _rope_cols(cols, cos, sin_signed, first_half):
    swapped = jnp.where(first_half,
                        pltpu.roll(cols, V7X_LANES - HEAD_DIM // 2, 1),
                        pltpu.roll(cols, HEAD_DIM // 2, 1))
    return cols * cos + swapped * sin_signed


def _pair_blockdiag(mat, mat_rolled, kv, lane_lt64):
    zero = jnp.zeros_like(mat)
    if kv == 0:
        top = jnp.where(lane_lt64, mat, zero)
        bottom = jnp.where(lane_lt64, zero, mat_rolled)
    else:
        top = jnp.where(lane_lt64, mat_rolled, zero)
        bottom = jnp.where(lane_lt64, zero, mat)
    return jnp.concatenate([top, bottom], axis=0).astype(BF16)


def _attend(qs, k2, v2, mask, sink_ref, kv, tq):
    nk = 2 * WINDOW
    lane_lt64 = lax.broadcasted_iota(jnp.int32, (nk, V7X_LANES), 1) < HEAD_DIM
    kbd = _pair_blockdiag(k2, pltpu.roll(k2, HEAD_DIM, 1), kv, lane_lt64)
    vbd = _pair_blockdiag(v2, pltpu.roll(v2, HEAD_DIM, 1), kv, lane_lt64)
    s_all = lax.dot_general(qs, kbd, (((1,), (1,)), ((), ())), preferred_element_type=F32)
    p_rows = []
    for c in range(COLS_PER_KV):
        p_halves = []
        for parity in range(2):
            head = kv * GQA_GROUP + 2 * c + parity
            sink = sink_ref[head]
            s = s_all[c * tq:(c + 1) * tq, parity * nk:(parity + 1) * nk]
            s = jnp.where(mask, s, NEG_INF)
            m = jnp.maximum(jnp.max(s, axis=-1, keepdims=True), sink)
            e = jnp.exp(s - m)
            den = jnp.sum(e, axis=-1, keepdims=True) + jnp.exp(sink - m)
            p_halves.append(e / den)
        p_rows.append(jnp.concatenate(p_halves, axis=1))
    p_all = jnp.concatenate(p_rows, axis=0).astype(BF16)
    return jnp.dot(p_all, vbd, preferred_element_type=F32)


def _band_mask(tq, col_min):
    row = lax.broadcasted_iota(jnp.int32, (tq, 2 * WINDOW), 0)
    col = lax.broadcasted_iota(jnp.int32, (tq, 2 * WINDOW), 1)
    prev_ok = (col < WINDOW) & (col > row)
    own_ok = (col >= WINDOW) & (col - WINDOW <= row)
    return (prev_ok | own_ok) & (col >= col_min)


def _attn_prompt_kernel(x_ref, g_ref, wqkv_ref, bqkv_ref, wo_ref, bo_ref, cos_ref, sin_ref,
                        sink_ref, y_ref, knew_ref, vnew_ref,
                        q_scr, k_scr, v_scr, a_scr, *, tq_tile, n_tiles):
    j = pl.program_id(1)
    n_blk = tq_tile // WINDOW

    @pl.when(j == 0)
    def _():
        k_scr[0:WINDOW, :] = jnp.zeros((WINDOW, V7X_LANES), F32)
        v_scr[0:WINDOW, :] = jnp.zeros((WINDOW, V7X_LANES), F32)

    x = x_ref[...]
    h = _rms(x, g_ref[...])
    qkv = _mm(h, wqkv_ref[...]) + bqkv_ref[...]
    cos = cos_ref[...]
    sin = sin_ref[...]
    lane = lax.broadcasted_iota(jnp.int32, (tq_tile, V7X_LANES), 1)
    first_half = (lane % HEAD_DIM) < (HEAD_DIM // 2)
    scale = HEAD_DIM ** -0.5
    for c in range(Q_DIM // V7X_LANES):
        qc = _rope_cols(qkv[:, c * V7X_LANES:(c + 1) * V7X_LANES], cos, sin, first_half)
        q_scr[:, c * V7X_LANES:(c + 1) * V7X_LANES] = (qc * scale).astype(BF16)
    k_new = _rope_cols(qkv[:, Q_DIM:Q_DIM + KV_DIM], cos, sin, first_half)
    v_new = qkv[:, Q_DIM + KV_DIM:]
    k_scr[WINDOW:, :] = k_new
    v_scr[WINDOW:, :] = v_new

    def block(blk, carry):
        r0 = pl.multiple_of(blk * WINDOW, WINDOW)
        k2 = k_scr[pl.ds(r0, 2 * WINDOW), :]
        v2 = v_scr[pl.ds(r0, 2 * WINDOW), :]
        first = jnp.logical_and(j == 0, blk == 0)
        mask = _band_mask(WINDOW, jnp.where(first, WINDOW, 0))
        for kv in range(N_KV_HEADS):
            qs = jnp.concatenate(
                [q_scr[pl.ds(r0, WINDOW),
                       (kv * COLS_PER_KV + c) * V7X_LANES:(kv * COLS_PER_KV + c + 1) * V7X_LANES]
                 for c in range(COLS_PER_KV)], axis=0)
            o = _attend(qs, k2, v2, mask, sink_ref, kv, WINDOW)
            for c in range(COLS_PER_KV):
                col = kv * COLS_PER_KV + c
                a_scr[pl.ds(r0, WINDOW), col * V7X_LANES:(col + 1) * V7X_LANES] = (
                    o[c * WINDOW:(c + 1) * WINDOW].astype(BF16))
        return carry

    lax.fori_loop(0, n_blk, block, 0)

    y_ref[...] = jnp.dot(a_scr[...], wo_ref[...], preferred_element_type=F32) + bo_ref[...] + x

    k_scr[0:WINDOW, :] = k_new[tq_tile - WINDOW:, :]
    v_scr[0:WINDOW, :] = v_new[tq_tile - WINDOW:, :]

    @pl.when(j == n_tiles - 1)
    def _():
        knew_ref[0] = k_new[tq_tile - WINDOW:, :]
        vnew_ref[0] = v_new[tq_tile - WINDOW:, :]


def _attn_prompt(x, g, wqkv, bqkv, wo, bo, cos_t, sin_t, sinks, batch, seq):
    tq = ROW_TILE
    nt = seq // tq
    row = pl.BlockSpec((tq, D_MODEL), lambda b, j: (b * nt + j, 0))
    tab = pl.BlockSpec((tq, V7X_LANES), lambda b, j: (j, 0))
    cache = pl.BlockSpec((1, WINDOW, V7X_LANES), lambda b, j: (b, 0, 0))
    return pl.pallas_call(
        functools.partial(_attn_prompt_kernel, tq_tile=tq, n_tiles=nt),
        grid=(batch, nt),
        in_specs=[row, _const_spec((1, D_MODEL)), _const_spec(wqkv.shape),
                  _const_spec((1, wqkv.shape[1])), _const_spec(wo.shape), _const_spec((1, D_MODEL)),
                  tab, tab, pl.BlockSpec(memory_space=pltpu.SMEM)],
        out_specs=[row, cache, cache],
        out_shape=[jax.ShapeDtypeStruct((batch * seq, D_MODEL), F32),
                   jax.ShapeDtypeStruct((batch, WINDOW, V7X_LANES), F32),
                   jax.ShapeDtypeStruct((batch, WINDOW, V7X_LANES), F32)],
        scratch_shapes=[pltpu.VMEM((tq, Q_DIM), BF16),
                        pltpu.VMEM((WINDOW + tq, V7X_LANES), F32),
                        pltpu.VMEM((WINDOW + tq, V7X_LANES), F32),
                        pltpu.VMEM((tq, Q_DIM), BF16)],
        compiler_params=pltpu.CompilerParams(
            dimension_semantics=("arbitrary", "arbitrary"),
            vmem_limit_bytes=_vmem_limit(48 * 1024 * 1024)),
        name="attn_prompt",
    )(x, g, wqkv, bqkv, wo, bo, cos_t, sin_t, sinks)


def _attn_sample_kernel(x_ref, g_ref, wqkv_ref, bqkv_ref, wo_ref, bo_ref, cos_ref, sin_ref,
                        sink_ref, ck_ref, cv_ref, y_ref, nk_ref, nv_ref,
                        q_scr, k_scr, v_scr, a_scr, *, n_seq, t_new):
    tm = n_seq * t_new
    x = x_ref[...]
    h = _rms(x, g_ref[...])
    qkv = _mm(h, wqkv_ref[...]) + bqkv_ref[...]
    cos = cos_ref[...]
    sin = sin_ref[...]
    lane = lax.broadcasted_iota(jnp.int32, (tm, V7X_LANES), 1)
    first_half = (lane % HEAD_DIM) < (HEAD_DIM // 2)
    scale = HEAD_DIM ** -0.5
    for c in range(Q_DIM // V7X_LANES):
        qc = _rope_cols(qkv[:, c * V7X_LANES:(c + 1) * V7X_LANES], cos, sin, first_half)
        q_scr[:, c * V7X_LANES:(c + 1) * V7X_LANES] = qc * scale
    k_scr[...] = _rope_cols(qkv[:, Q_DIM:Q_DIM + KV_DIM], cos, sin, first_half)
    v_scr[...] = qkv[:, Q_DIM + KV_DIM:]
    mask = _band_mask(t_new, 0)
    pad = jnp.zeros((WINDOW - t_new, V7X_LANES), F32)

    def one_seq(b, carry):
        r0 = pl.multiple_of(b * t_new, t_new)
        k_rows = k_scr[pl.ds(r0, t_new), :]
        v_rows = v_scr[pl.ds(r0, t_new), :]
        k2 = jnp.concatenate([ck_ref[b], k_rows, pad], axis=0)
        v2 = jnp.concatenate([cv_ref[b], v_rows, pad], axis=0)
        for kv in range(N_KV_HEADS):
            qs = jnp.concatenate(
                [q_scr[pl.ds(r0, t_new),
                       (kv * COLS_PER_KV + c) * V7X_LANES:(kv * COLS_PER_KV + c + 1) * V7X_LANES]
                 for c in range(COLS_PER_KV)], axis=0).astype(BF16)
            o = _attend(qs, k2, v2, mask, sink_ref, kv, t_new)
            for c in range(COLS_PER_KV):
                col = kv * COLS_PER_KV + c
                a_scr[pl.ds(r0, t_new), col * V7X_LANES:(col + 1) * V7X_LANES] = (
                    o[c * t_new:(c + 1) * t_new])
        nk_ref[b] = jnp.concatenate([ck_ref[b, t_new:, :], k_rows], axis=0)
        nv_ref[b] = jnp.concatenate([cv_ref[b, t_new:, :], v_rows], axis=0)
        return carry

    lax.fori_loop(0, n_seq, one_seq, 0)
    y_ref[...] = _mm(a_scr[...], wo_ref[...]) + bo_ref[...] + x


def _attn_sample(x, g, wqkv, bqkv, wo, bo, cos_t, sin_t, sinks, ck, cv, n_batch, t_new):
    bg = SAMPLE_ATTN_BATCHES
    tm = bg * t_new
    row = pl.BlockSpec((tm, D_MODEL), lambda i: (i, 0))
    cache = pl.BlockSpec((bg, WINDOW, V7X_LANES), lambda i: (i, 0, 0))
    return pl.pallas_call(
        functools.partial(_attn_sample_kernel, n_seq=bg, t_new=t_new),
        grid=(n_batch // bg,),
        in_specs=[row, _const_spec((1, D_MODEL)), _const_spec(wqkv.shape),
                  _const_spec((1, wqkv.shape[1])), _const_spec(wo.shape), _const_spec((1, D_MODEL)),
                  _const_spec((tm, V7X_LANES)), _const_spec((tm, V7X_LANES)),
                  pl.BlockSpec(memory_space=pltpu.SMEM), cache, cache],
        out_specs=[row, cache, cache],
        out_shape=[jax.ShapeDtypeStruct((n_batch * t_new, D_MODEL), F32),
                   jax.ShapeDtypeStruct((n_batch, WINDOW, V7X_LANES), F32),
                   jax.ShapeDtypeStruct((n_batch, WINDOW, V7X_LANES), F32)],
        scratch_shapes=[pltpu.VMEM((tm, Q_DIM), F32),
                        pltpu.VMEM((tm, V7X_LANES), F32),
                        pltpu.VMEM((tm, V7X_LANES), F32),
                        pltpu.VMEM((tm, Q_DIM), F32)],
        compiler_params=pltpu.CompilerParams(
            dimension_semantics=("arbitrary",),
            vmem_limit_bytes=_vmem_limit(48 * 1024 * 1024)),
        name="attn_sample",
    )(x, g, wqkv, bqkv, wo, bo, cos_t, sin_t, sinks, ck, cv)


def _causal_dwconv(u, prev, w_ref, col0, ncol):
    rows = u.shape[0]
    kw = w_ref.shape[0]
    sub = lax.broadcasted_iota(jnp.int32, u.shape, 0) % V7X_SUBLANES
    y = u * w_ref[kw - 1:kw, col0:col0 + ncol]
    for s in range(1, kw):
        from_prev = pltpu.roll(prev, (rows - V7X_SUBLANES + s) % rows, 0)
        from_self = pltpu.roll(u, s, 0)
        shifted = jnp.where(sub < s, from_prev, from_self)
        y = y + shifted * w_ref[kw - 1 - s:kw - s, col0:col0 + ncol]
    return y


def _prev_groups(carry8, u):
    return jnp.concatenate([carry8, u[:u.shape[0] - V7X_SUBLANES]], axis=0)


def _linear_scan(a, b, seg):
    pos = lax.broadcasted_iota(jnp.int32, a.shape, 0) % seg
    shift = 1
    while shift < seg:
        ok = pos >= shift
        a_sh = jnp.where(ok, pltpu.roll(a, shift, 0), 1.0)
        b_sh = jnp.where(ok, pltpu.roll(b, shift, 0), 0.0)
        b = a * b_sh + b
        a = a * a_sh
        shift *= 2
    return a, b


def _log_sigmoid(x):
    return jnp.minimum(x, 0.0) - jnp.log1p(jnp.exp(-jnp.abs(x)))


def _rglru_body(x, g_ref, wgate_ref, win_ref, cw_ref, cb_ref, wa_ref, ba_ref, wx_ref, bx_ref,
                lam_ref, wout_ref, prev_of, hinit_of, seg, u_sink, h_sink):
    h = _rms(x, g_ref[...]).astype(BF16)
    acc = x
    for n in range(RG_BLOCKS):
        c0 = n * RG_BLOCK_W
        cs = slice(c0, c0 + RG_BLOCK_W)
        gate = jax.nn.gelu(jnp.dot(h, wgate_ref[:, cs], preferred_element_type=F32))
        u0 = jnp.dot(h, win_ref[:, cs], preferred_element_type=F32)
        u_sink(n, u0)
        u = _causal_dwconv(u0, prev_of(n, u0), cw_ref, c0, RG_BLOCK_W) + cb_ref[:, cs]
        ub = u.astype(BF16)
        r = jax.nn.sigmoid(jnp.dot(ub, wa_ref[n], preferred_element_type=F32) + ba_ref[:, cs])
        ig = jax.nn.sigmoid(jnp.dot(ub, wx_ref[n], preferred_element_type=F32) + bx_ref[:, cs])
        log_a = RG_C * r * _log_sigmoid(lam_ref[:, cs])
        a = jnp.exp(log_a)
        mult = jnp.sqrt(-jnp.tanh(log_a) * (1.0 + a * a))
        a_cum, b_cum = _linear_scan(a, mult * (ig * u), seg)
        hs = a_cum * hinit_of(n) + b_cum
        h_sink(n, hs)
        acc = acc + jnp.dot((hs * gate).astype(BF16), wout_ref[cs, :], preferred_element_type=F32)
    return acc


def _rglru_prompt_kernel(x_ref, g_ref, wgate_ref, win_ref, cw_ref, cb_ref, wa_ref, ba_ref,
                         wx_ref, bx_ref, lam_ref, wout_ref, y_ref, hlast_ref, ulast_ref,
                         ucarry, hcarry, *, tm, n_tiles):
    j = pl.program_id(1)

    @pl.when(j == 0)
    def _():
        ucarry[...] = jnp.zeros(ucarry.shape, F32)
        hcarry[...] = jnp.zeros(hcarry.shape, F32)

    def prev_of(n, u0):
        return _prev_groups(ucarry[:, n * RG_BLOCK_W:(n + 1) * RG_BLOCK_W], u0)

    def hinit_of(n):
        return hcarry[V7X_SUBLANES - 1:V7X_SUBLANES, n * RG_BLOCK_W:(n + 1) * RG_BLOCK_W]

    def u_sink(n, u0):
        ulast_ref[0, :, n * RG_BLOCK_W:(n + 1) * RG_BLOCK_W] = u0[tm - V7X_SUBLANES:, :]

    def h_sink(n, hs):
        hlast_ref[0, :, n * RG_BLOCK_W:(n + 1) * RG_BLOCK_W] = hs[tm - V7X_SUBLANES:, :]

    y_ref[...] = _rglru_body(x_ref[...], g_ref, wgate_ref, win_ref, cw_ref, cb_ref, wa_ref, ba_ref,
                             wx_ref, bx_ref, lam_ref, wout_ref, prev_of, hinit_of, tm,
                             u_sink, h_sink)
    ucarry[...] = ulast_ref[0]
    hcarry[...] = hlast_ref[0]


def _rglru_sample_kernel(x_ref, prev_ref, hinit_ref, g_ref, wgate_ref, win_ref, cw_ref, cb_ref,
                         wa_ref, ba_ref, wx_ref, bx_ref, lam_ref, wout_ref, y_ref, hs_ref, u_ref,
                         *, t_new):
    def prev_of(n, u0):
        return prev_ref[:, n * RG_BLOCK_W:(n + 1) * RG_BLOCK_W]

    def hinit_of(n):
        return hinit_ref[:, n * RG_BLOCK_W:(n + 1) * RG_BLOCK_W]

    def u_sink(n, u0):
        u_ref[:, n * RG_BLOCK_W:(n + 1) * RG_BLOCK_W] = u0

    def h_sink(n, hs):
        hs_ref[:, n * RG_BLOCK_W:(n + 1) * RG_BLOCK_W] = hs

    y_ref[...] = _rglru_body(x_ref[...], g_ref, wgate_ref, win_ref, cw_ref, cb_ref, wa_ref, ba_ref,
                             wx_ref, bx_ref, lam_ref, wout_ref, prev_of, hinit_of, t_new,
                             u_sink, h_sink)


def _rglru_weight_specs(p):
    return [_const_spec(a.shape) for a in p]


def _rglru_prompt(x, params, batch, seq):
    tm = RG_ROW_TILE
    nt = seq // tm
    row = pl.BlockSpec((tm, D_MODEL), lambda b, j: (b * nt + j, 0))
    last = pl.BlockSpec((1, V7X_SUBLANES, D_MODEL), lambda b, j: (b, 0, 0))
    return pl.pallas_call(
        functools.partial(_rglru_prompt_kernel, tm=tm, n_tiles=nt),
        grid=(batch, nt),
        in_specs=[row] + _rglru_weight_specs(params),
        out_specs=[row, last, last],
        out_shape=[jax.ShapeDtypeStruct((batch * seq, D_MODEL), F32),
                   jax.ShapeDtypeStruct((batch, V7X_SUBLANES, D_MODEL), F32),
                   jax.ShapeDtypeStruct((batch, V7X_SUBLANES, D_MODEL), F32)],
        scratch_shapes=[pltpu.VMEM((V7X_SUBLANES, D_MODEL), F32),
                        pltpu.VMEM((V7X_SUBLANES, D_MODEL), F32)],
        compiler_params=pltpu.CompilerParams(
            dimension_semantics=("arbitrary", "arbitrary"),
            vmem_limit_bytes=_vmem_limit(48 * 1024 * 1024)),
        name="rglru_prompt",
    )(x, *params)


def _rglru_sample(x, prev, hinit, params, t_new):
    m = x.shape[0]
    tm = min(RG_ROW_TILE, m)
    row = pl.BlockSpec((tm, D_MODEL), lambda i: (i, 0))
    return pl.pallas_call(
        functools.partial(_rglru_sample_kernel, t_new=t_new),
        grid=(m // tm,),
        in_specs=[row, row, row] + _rglru_weight_specs(params),
        out_specs=[row, row, row],
        out_shape=[jax.ShapeDtypeStruct((m, D_MODEL), F32)] * 3,
        compiler_params=pltpu.CompilerParams(
            dimension_semantics=("arbitrary",),
            vmem_limit_bytes=_vmem_limit(48 * 1024 * 1024)),
        name="rglru_sample",
    )(x, prev, hinit, *params)


SCONV_CHUNK = V7X_MXU_DIM


def _sconv_body(x, g_ref, win_ref, cw_ref, wout_ref, prev_of, v_sink):
    h = _rms(x, g_ref[...]).astype(BF16)
    acc = x
    for n in range(D_MODEL // SCONV_CHUNK):
        c0 = n * SCONV_CHUNK
        bg = jnp.dot(h, win_ref[:, c0:c0 + SCONV_CHUNK], preferred_element_type=F32)
        cg = jnp.dot(h, win_ref[:, D_MODEL + c0:D_MODEL + c0 + SCONV_CHUNK],
                     preferred_element_type=F32)
        xv = jnp.dot(h, win_ref[:, 2 * D_MODEL + c0:2 * D_MODEL + c0 + SCONV_CHUNK],
                     preferred_element_type=F32)
        v = cg * xv
        v_sink(n, v)
        y = _causal_dwconv(v, prev_of(n, v), cw_ref, c0, SCONV_CHUNK)
        acc = acc + jnp.dot((bg * y).astype(BF16), wout_ref[c0:c0 + SCONV_CHUNK, :],
                            preferred_element_type=F32)
    return acc


def _sconv_prompt_kernel(x_ref, g_ref, win_ref, cw_ref, wout_ref, y_ref, vlast_ref, vcarry,
                         *, tm):
    j = pl.program_id(1)

    @pl.when(j == 0)
    def _():
        vcarry[...] = jnp.zeros(vcarry.shape, F32)

    def prev_of(n, v):
        return _prev_groups(vcarry[:, n * SCONV_CHUNK:(n + 1) * SCONV_CHUNK], v)

    def v_sink(n, v):
        vlast_ref[0, :, n * SCONV_CHUNK:(n + 1) * SCONV_CHUNK] = v[tm - V7X_SUBLANES:, :]

    y_ref[...] = _sconv_body(x_ref[...], g_ref, win_ref, cw_ref, wout_ref, prev_of, v_sink)
    vcarry[...] = vlast_ref[0]


def _sconv_sample_kernel(x_ref, prev_ref, g_ref, win_ref, cw_ref, wout_ref, y_ref, v_ref):
    def prev_of(n, v):
        return prev_ref[:, n * SCONV_CHUNK:(n + 1) * SCONV_CHUNK]

    def v_sink(n, v):
        v_ref[:, n * SCONV_CHUNK:(n + 1) * SCONV_CHUNK] = v

    y_ref[...] = _sconv_body(x_ref[...], g_ref, win_ref, cw_ref, wout_ref, prev_of, v_sink)


def _sconv_prompt(x, params, batch, seq):
    tm = ROW_TILE
    nt = seq // tm
    row = pl.BlockSpec((tm, D_MODEL), lambda b, j: (b * nt + j, 0))
    last = pl.BlockSpec((1, V7X_SUBLANES, D_MODEL), lambda b, j: (b, 0, 0))
    return pl.pallas_call(
        functools.partial(_sconv_prompt_kernel, tm=tm),
        grid=(batch, nt),
        in_specs=[row] + [_const_spec(a.shape) for a in params],
        out_specs=[row, last],
        out_shape=[jax.ShapeDtypeStruct((batch * seq, D_MODEL), F32),
                   jax.ShapeDtypeStruct((batch, V7X_SUBLANES, D_MODEL), F32)],
        scratch_shapes=[pltpu.VMEM((V7X_SUBLANES, D_MODEL), F32)],
        compiler_params=pltpu.CompilerParams(
            dimension_semantics=("arbitrary", "arbitrary"),
            vmem_limit_bytes=_vmem_limit(48 * 1024 * 1024)),
        name="sconv_prompt",
    )(x, *params)


def _sconv_sample(x, prev, params):
    m = x.shape[0]
    tm = min(ROW_TILE, m)
    row = pl.BlockSpec((tm, D_MODEL), lambda i: (i, 0))
    return pl.pallas_call(
        _sconv_sample_kernel,
        grid=(m // tm,),
        in_specs=[row, row] + [_const_spec(a.shape) for a in params],
        out_specs=[row, row],
        out_shape=[jax.ShapeDtypeStruct((m, D_MODEL), F32)] * 2,
        compiler_params=pltpu.CompilerParams(
            dimension_semantics=("arbitrary",),
            vmem_limit_bytes=_vmem_limit(48 * 1024 * 1024)),
        name="sconv_sample",
    )(x, prev, *params)


def _rope_tables(pos):
    half = HEAD_DIM // 2
    inv = ROPE_THETA ** (-jnp.arange(half, dtype=F32) / half)
    ang = pos.astype(F32)[:, None] * inv[None, :]
    cos = jnp.cos(ang)
    sin = jnp.sin(ang)
    reps = V7X_LANES // HEAD_DIM
    cos_t = jnp.tile(jnp.concatenate([cos, cos], axis=-1), (1, reps))
    sin_t = jnp.tile(jnp.concatenate([-sin, sin], axis=-1), (1, reps))
    return cos_t, sin_t


def _row(v):
    return v.reshape(1, -1)


def _pad_state_rows(buf):
    b, k, c = buf.shape
    padded = jnp.concatenate([jnp.zeros((b, V7X_SUBLANES - k, c), buf.dtype), buf], axis=1)
    return padded.reshape(b * V7X_SUBLANES, c)


def kernel(x_prompt, x_sample, cache_k, cache_v, state_rglru_h, state_rglru_conv, state_shortconv,
           norm_mixer, norm_ffn, norm_final,
           attn_w_qkv, attn_b_qkv, attn_w_o, attn_b_o, attn_sinks,
           rglru_w_gate, rglru_w_in, rglru_conv_w, rglru_conv_b, rglru_wa, rglru_ba,
           rglru_wx, rglru_bx, rglru_lambda, rglru_w_out,
           sconv_w_in, sconv_conv_w, sconv_w_out,
           ffn_w_gate, ffn_w_up, ffn_w_down):
    bp, seq, _ = x_prompt.shape
    bs, t_new, _ = x_sample.shape
    depth = norm_mixer.shape[0]
    past_len = 8192
    assert t_new == V7X_SUBLANES

    xp = x_prompt.reshape(bp * seq, D_MODEL)
    xs = x_sample.reshape(bs * t_new, D_MODEL)

    cos_p, sin_p = _rope_tables(jnp.arange(seq, dtype=jnp.int32))
    cos_s, sin_s = _rope_tables(past_len + jnp.arange(t_new, dtype=jnp.int32))
    cos_s = jnp.tile(cos_s, (SAMPLE_ATTN_BATCHES, 1))
    sin_s = jnp.tile(sin_s, (SAMPLE_ATTN_BATCHES, 1))

    kp_l, vp_l, ks_l, vs_l = [], [], [], []
    hp_l, hs_l, rcp_l, rcs_l = [], [], [], []
    scp_l, scs_l = [], []
    n_chunks = ffn_w_gate.shape[2] // FFN_CHUNK

    for i in range(depth):
        kind = i % 3
        j = i // 3
        g_mix = _row(norm_mixer[i])
        if kind == 0:
            wqkv = attn_w_qkv[j].astype(BF16)
            wo = attn_w_o[j].astype(BF16)
            bqkv = _row(attn_b_qkv[j])
            bo = _row(attn_b_o[j])
            sinks = attn_sinks[j]
            xp, kp, vp = _attn_prompt(xp, g_mix, wqkv, bqkv, wo, bo, cos_p, sin_p, sinks, bp, seq)
            ck = cache_k[j].reshape(bs, WINDOW, KV_DIM)
            cv = cache_v[j].reshape(bs, WINDOW, KV_DIM)
            xs, ks, vs = _attn_sample(xs, g_mix, wqkv, bqkv, wo, bo, cos_s, sin_s, sinks, ck, cv,
                                      bs, t_new)
            kp_l.append(kp.reshape(bp, WINDOW, N_KV_HEADS, HEAD_DIM))
            vp_l.append(vp.reshape(bp, WINDOW, N_KV_HEADS, HEAD_DIM))
            ks_l.append(ks.reshape(bs, WINDOW, N_KV_HEADS, HEAD_DIM))
            vs_l.append(vs.reshape(bs, WINDOW, N_KV_HEADS, HEAD_DIM))
        elif kind == 1:
            params = (g_mix, rglru_w_gate[j].astype(BF16), rglru_w_in[j].astype(BF16),
                      rglru_conv_w[j], _row(rglru_conv_b[j]), rglru_wa[j].astype(BF16),
                      _row(rglru_ba[j]), rglru_wx[j].astype(BF16), _row(rglru_bx[j]),
                      _row(rglru_lambda[j]), rglru_w_out[j].astype(BF16))
            xp, hlast, ulast = _rglru_prompt(xp, params, bp, seq)
            hp_l.append(hlast[:, V7X_SUBLANES - 1])
            rcp_l.append(ulast[:, V7X_SUBLANES - (RG_CONV_W - 1):])
            prev = _pad_state_rows(state_rglru_conv[j])
            hinit = jnp.repeat(state_rglru_h[j], t_new, axis=0)
            xs, hs_all, u_all = _rglru_sample(xs, prev, hinit, params, t_new)
            hs_l.append(hs_all.reshape(bs, t_new, D_MODEL)[:, t_new - 1])
            rcs_l.append(u_all.reshape(bs, t_new, D_MODEL)[:, t_new - (RG_CONV_W - 1):])
        else:
            params = (g_mix, sconv_w_in[j].astype(BF16), sconv_conv_w[j],
                      sconv_w_out[j].astype(BF16))
            xp, vlast = _sconv_prompt(xp, params, bp, seq)
            scp_l.append(vlast[:, V7X_SUBLANES - (SCONV_W - 1):])
            prev = _pad_state_rows(state_shortconv[j])
            xs, v_all = _sconv_sample(xs, prev, params)
            scs_l.append(v_all.reshape(bs, t_new, D_MODEL)[:, t_new - (SCONV_W - 1):])

        d_ff = ffn_w_gate.shape[2]
        w1 = jnp.stack([ffn_w_gate[i].reshape(D_MODEL, n_chunks, FFN_CHUNK),
                        ffn_w_up[i].reshape(D_MODEL, n_chunks, FFN_CHUNK)], axis=2)
        w1 = w1.reshape(D_MODEL, 2 * d_ff).astype(BF16)
        wd = ffn_w_down[i].astype(BF16)
        g_ffn = _row(norm_ffn[i])
        g_fin = _row(norm_final)
        last = i == depth - 1
        xp = _ffn(xp, g_ffn, w1, wd, g_fin, last)
        xs = _ffn(xs, g_ffn, w1, wd, g_fin, last)

    return (xp.reshape(bp, seq, D_MODEL), xs.reshape(bs, t_new, D_MODEL),
            jnp.stack(kp_l), jnp.stack(vp_l), jnp.stack(ks_l), jnp.stack(vs_l),
            jnp.stack(hp_l), jnp.stack(hs_l), jnp.stack(rcp_l), jnp.stack(rcs_l),
            jnp.stack(scp_l), jnp.stack(scs_l))
```

```python
import functools

import jax
import jax.numpy as jnp
from jax import lax
from jax.experimental import pallas as pl
from jax.experimental.pallas import tpu as pltpu

D_MODEL = 1024
HEAD_DIM = 64
N_HEADS = 16
N_KV_HEADS = 2
GQA_GROUP = 8
Q_DIM = N_HEADS * HEAD_DIM
KV_DIM = N_KV_HEADS * HEAD_DIM
WINDOW = 128
ROPE_THETA = 10000.0
NEG_INF = -1e30
RG_BLOCKS = 4
RG_BLOCK_W = 256
RG_CONV_W = 4
RG_C = 8.0
SCONV_W = 3
EPS = 1e-6

V7X_LANES = 128
V7X_SUBLANES = 8
V7X_MXU_DIM = 256
V7X_VMEM_BYTES = 64 * 1024 * 1024

BF16 = jnp.bfloat16
F32 = jnp.float32

FFN_CHUNK = V7X_MXU_DIM
ROW_TILE = 512
RG_ROW_TILE = 256
SAMPLE_ATTN_BATCHES = 32
SAMPLE_ATTN_UNROLL = 8
ATTN_LOOKAHEAD = 2
COLS_PER_KV = (N_HEADS // N_KV_HEADS) * HEAD_DIM // V7X_LANES


def _vmem_limit(nbytes):
    return int(min(nbytes, V7X_VMEM_BYTES - 8 * 1024 * 1024))


def _const_spec(shape):
    nd = len(shape)
    return pl.BlockSpec(shape, lambda *_: (0,) * nd, pipeline_mode=pl.Buffered(1))


def _rms(x, g):
    ms = jnp.mean(x * x, axis=-1, keepdims=True)
    return x * lax.rsqrt(ms + EPS) * g


def _mm(a, w):
    return jnp.dot(a.astype(BF16), w, preferred_element_type=F32)


def _ffn_kernel(x_ref, g_ref, wg_ref, wu_ref, wd_ref, gf_ref, o_ref, *, n_chunks, final_norm):
    x = x_ref[...]
    h = _rms(x, g_ref[...]).astype(BF16)
    acc = x
    for c in range(n_chunks):
        cs = slice(c * FFN_CHUNK, (c + 1) * FFN_CHUNK)
        gate = jnp.dot(h, wg_ref[:, cs], preferred_element_type=F32)
        up = jnp.dot(h, wu_ref[:, cs], preferred_element_type=F32)
        act = (gate * jax.nn.sigmoid(gate)) * up
        acc = acc + jnp.dot(act.astype(BF16), wd_ref[cs, :], preferred_element_type=F32)
    if final_norm:
        acc = _rms(acc, gf_ref[...])
    o_ref[...] = acc


def _ffn(x, g, wg, wu, wd, gf, final_norm):
    m = x.shape[0]
    tm = min(ROW_TILE, m)
    d_ff = wd.shape[0]
    n_chunks = d_ff // FFN_CHUNK
    row = pl.BlockSpec((tm, D_MODEL), lambda i: (i, 0))
    weight_bytes = 2 * (wg.size + wu.size + wd.size)
    return pl.pallas_call(
        functools.partial(_ffn_kernel, n_chunks=n_chunks, final_norm=final_norm),
        grid=(m // tm,),
        in_specs=[row, _const_spec((1, D_MODEL)), _const_spec(wg.shape), _const_spec(wu.shape),
                  _const_spec(wd.shape), _const_spec((1, D_MODEL))],
        out_specs=row,
        out_shape=jax.ShapeDtypeStruct((m, D_MODEL), F32),
        compiler_params=pltpu.CompilerParams(
            dimension_semantics=("arbitrary",),
            vmem_limit_bytes=_vmem_limit(weight_bytes + 24 * tm * D_MODEL * 4)),
        name="ffn",
    )(x, g, wg, wu, wd, gf)


def _rope_cols(cols, cos, sin_signed, first_half):
    swapped = jnp.where(first_half,
                        pltpu.roll(cols, V7X_LANES - HEAD_DIM // 2, 1),
                        pltpu.roll(cols, HEAD_DIM // 2, 1))
    return cols * cos + swapped * sin_signed


def _pair_blockdiag(mat, mat_rolled, kv, lane_lt64):
    zero = jnp.zeros_like(mat)
    if kv == 0:
        top = jnp.where(lane_lt64, mat, zero)
        bottom = jnp.where(lane_lt64, zero, mat_rolled)
    else:
        top = jnp.where(lane_lt64, mat_rolled, zero)
        bottom = jnp.where(lane_lt64, zero, mat)
    return jnp.concatenate([top, bottom], axis=0).astype(BF16)


def _attn_scores(qs, k2, kv):
    lane_lt64 = lax.broadcasted_iota(jnp.int32, k2.shape, 1) < HEAD_DIM
    kbd = _pair_blockdiag(k2, pltpu.roll(k2, HEAD_DIM, 1), kv, lane_lt64)
    return lax.dot_general(qs, kbd, (((1,), (1,)), ((), ())), preferred_element_type=F32)


def _attn_probs(s_all, mask, sink_ref, kv, tq):
    nk = 2 * WINDOW
    p_rows = []
    for c in range(COLS_PER_KV):
        p_halves = []
        for parity in range(2):
            sink = sink_ref[kv * GQA_GROUP + 2 * c + parity]
            s = s_all[c * tq:(c + 1) * tq, parity * nk:(parity + 1) * nk]
            s = jnp.where(mask, s, NEG_INF)
            m = jnp.maximum(jnp.max(s, axis=-1, keepdims=True), sink)
            e = jnp.exp(s - m)
            den = jnp.sum(e, axis=-1, keepdims=True) + jnp.exp(sink - m)
            p_halves.append(e * (1.0 / den))
        p_rows.append(jnp.concatenate(p_halves, axis=1))
    return jnp.concatenate(p_rows, axis=0).astype(BF16)


def _attn_values(p_all, v2, kv):
    lane_lt64 = lax.broadcasted_iota(jnp.int32, v2.shape, 1) < HEAD_DIM
    vbd = _pair_blockdiag(v2, pltpu.roll(v2, HEAD_DIM, 1), kv, lane_lt64)
    return jnp.dot(p_all, vbd, preferred_element_type=F32)


def _attend_units(units, sink_ref, tq):
    n = len(units)
    scores = {}

    def issue_scores(i):
        load_q, load_k, _, _, kv, _ = units[i]
        scores[i] = _attn_scores(load_q(), load_k(), kv)

    for i in range(min(ATTN_LOOKAHEAD, n)):
        issue_scores(i)
    for i in range(n):
        if i + ATTN_LOOKAHEAD < n:
            issue_scores(i + ATTN_LOOKAHEAD)
        _, _, load_v, mask, kv, store_o = units[i]
        p_all = _attn_probs(scores.pop(i), mask, sink_ref, kv, tq)
        store_o(_attn_values(p_all, load_v(), kv))


def _band_mask(tq, col_min):
    row = lax.broadcasted_iota(jnp.int32, (tq, 2 * WINDOW), 0)
    col = lax.broadcasted_iota(jnp.int32, (tq, 2 * WINDOW), 1)
    prev_ok = (col < WINDOW) & (col > row)
    own_ok = (col >= WINDOW) & (col - WINDOW <= row)
    return (prev_ok | own_ok) & (col >= col_min)


def _attn_prompt_kernel(x_ref, g_ref, wqkv_ref, bqkv_ref, wo_ref, bo_ref, cos_ref, sin_ref,
                        sink_ref, y_ref, knew_ref, vnew_ref,
                        q_scr, k_scr, v_scr, a_scr, *, tq_tile, n_tiles):
    j = pl.program_id(1)
    n_blk = tq_tile // WINDOW

    @pl.when(j == 0)
    def _():
        k_scr[0:WINDOW, :] = jnp.zeros((WINDOW, V7X_LANES), F32)
        v_scr[0:WINDOW, :] = jnp.zeros((WINDOW, V7X_LANES), F32)

    x = x_ref[...]
    h = _rms(x, g_ref[...])
    qkv = _mm(h, wqkv_ref[...]) + bqkv_ref[...]
    cos = cos_ref[...]
    sin = sin_ref[...]
    lane = lax.broadcasted_iota(jnp.int32, (tq_tile, V7X_LANES), 1)
    first_half = (lane % HEAD_DIM) < (HEAD_DIM // 2)
    scale = HEAD_DIM ** -0.5
    for c in range(Q_DIM // V7X_LANES):
        qc = _rope_cols(qkv[:, c * V7X_LANES:(c + 1) * V7X_LANES], cos, sin, first_half)
        q_scr[:, c * V7X_LANES:(c + 1) * V7X_LANES] = (qc * scale).astype(BF16)
    k_new = _rope_cols(qkv[:, Q_DIM:Q_DIM + KV_DIM], cos, sin, first_half)
    v_new = qkv[:, Q_DIM + KV_DIM:]
    k_scr[WINDOW:, :] = k_new
    v_scr[WINDOW:, :] = v_new

    def unit(r0, mask, kv):
        def load_q():
            return jnp.concatenate(
                [q_scr[pl.ds(r0, WINDOW),
                       (kv * COLS_PER_KV + c) * V7X_LANES:(kv * COLS_PER_KV + c + 1) * V7X_LANES]
                 for c in range(COLS_PER_KV)], axis=0)

        def store_o(o):
            for c in range(COLS_PER_KV):
                col = kv * COLS_PER_KV + c
                a_scr[pl.ds(r0, WINDOW), col * V7X_LANES:(col + 1) * V7X_LANES] = (
                    o[c * WINDOW:(c + 1) * WINDOW].astype(BF16))

        return (load_q, lambda: k_scr[pl.ds(r0, 2 * WINDOW), :],
                lambda: v_scr[pl.ds(r0, 2 * WINDOW), :], mask, kv, store_o)

    def block(blk, carry):
        r0 = pl.multiple_of(blk * WINDOW, WINDOW)
        first = jnp.logical_and(j == 0, blk == 0)
        mask = _band_mask(WINDOW, jnp.where(first, WINDOW, 0))
        _attend_units([unit(r0, mask, kv) for kv in range(N_KV_HEADS)], sink_ref, WINDOW)
        return carry

    lax.fori_loop(0, n_blk, block, 0)

    y_ref[...] = jnp.dot(a_scr[...], wo_ref[...], preferred_element_type=F32) + bo_ref[...] + x

    k_scr[0:WINDOW, :] = k_new[tq_tile - WINDOW:, :]
    v_scr[0:WINDOW, :] = v_new[tq_tile - WINDOW:, :]

    @pl.when(j == n_tiles - 1)
    def _():
        knew_ref[0] = k_new[tq_tile - WINDOW:, :]
        vnew_ref[0] = v_new[tq_tile - WINDOW:, :]


def _attn_prompt(x, g, wqkv, bqkv, wo, bo, cos_t, sin_t, sinks, batch, seq):
    tq = ROW_TILE
    nt = seq // tq
    row = pl.BlockSpec((tq, D_MODEL), lambda b, j: (b * nt + j, 0))
    tab = pl.BlockSpec((tq, V7X_LANES), lambda b, j: (j, 0))
    cache = pl.BlockSpec((1, WINDOW, V7X_LANES), lambda b, j: (b, 0, 0))
    return pl.pallas_call(
        functools.partial(_attn_prompt_kernel, tq_tile=tq, n_tiles=nt),
        grid=(batch, nt),
        in_specs=[row, _const_spec((1, D_MODEL)), _const_spec(wqkv.shape),
                  _const_spec((1, wqkv.shape[1])), _const_spec(wo.shape), _const_spec((1, D_MODEL)),
                  tab, tab, pl.BlockSpec(memory_space=pltpu.SMEM)],
        out_specs=[row, cache, cache],
        out_shape=[jax.ShapeDtypeStruct((batch * seq, D_MODEL), F32),
                   jax.ShapeDtypeStruct((batch, WINDOW, V7X_LANES), F32),
                   jax.ShapeDtypeStruct((batch, WINDOW, V7X_LANES), F32)],
        scratch_shapes=[pltpu.VMEM((tq, Q_DIM), BF16),
                        pltpu.VMEM((WINDOW + tq, V7X_LANES), F32),
                        pltpu.VMEM((WINDOW + tq, V7X_LANES), F32),
                        pltpu.VMEM((tq, Q_DIM), BF16)],
        compiler_params=pltpu.CompilerParams(
            dimension_semantics=("arbitrary", "arbitrary"),
            vmem_limit_bytes=_vmem_limit(48 * 1024 * 1024)),
        name="attn_prompt",
    )(x, g, wqkv, bqkv, wo, bo, cos_t, sin_t, sinks)


def _attn_sample_kernel(x_ref, g_ref, wqkv_ref, bqkv_ref, wo_ref, bo_ref, cos_ref, sin_ref,
                        sink_ref, ck_ref, cv_ref, y_ref, nk_ref, nv_ref,
                        q_scr, k_scr, v_scr, a_scr, *, n_seq, t_new):
    tm = n_seq * t_new
    x = x_ref[...]
    h = _rms(x, g_ref[...])
    qkv = _mm(h, wqkv_ref[...]) + bqkv_ref[...]
    cos = cos_ref[...]
    sin = sin_ref[...]
    lane = lax.broadcasted_iota(jnp.int32, (tm, V7X_LANES), 1)
    first_half = (lane % HEAD_DIM) < (HEAD_DIM // 2)
    scale = HEAD_DIM ** -0.5
    for c in range(Q_DIM // V7X_LANES):
        qc = _rope_cols(qkv[:, c * V7X_LANES:(c + 1) * V7X_LANES], cos, sin, first_half)
        q_scr[:, c * V7X_LANES:(c + 1) * V7X_LANES] = qc * scale
    k_scr[...] = _rope_cols(qkv[:, Q_DIM:Q_DIM + KV_DIM], cos, sin, first_half)
    v_scr[...] = qkv[:, Q_DIM + KV_DIM:]
    mask = _band_mask(t_new, 0)
    pad = jnp.zeros((WINDOW - t_new, V7X_LANES), F32)

    def unit(b, kv):
        r0 = pl.multiple_of(b * t_new, t_new)

        def load_q():
            return jnp.concatenate(
                [q_scr[pl.ds(r0, t_new),
                       (kv * COLS_PER_KV + c) * V7X_LANES:(kv * COLS_PER_KV + c + 1) * V7X_LANES]
                 for c in range(COLS_PER_KV)], axis=0).astype(BF16)

        def load_k():
            return jnp.concatenate([ck_ref[b], k_scr[pl.ds(r0, t_new), :], pad], axis=0)

        def load_v():
            return jnp.concatenate([cv_ref[b], v_scr[pl.ds(r0, t_new), :], pad], axis=0)

        def store_o(o):
            for c in range(COLS_PER_KV):
                col = kv * COLS_PER_KV + c
                a_scr[pl.ds(r0, t_new), col * V7X_LANES:(col + 1) * V7X_LANES] = (
                    o[c * t_new:(c + 1) * t_new])

        return (load_q, load_k, load_v, mask, kv, store_o)

    def seq_group(i, carry):
        seqs = [i * SAMPLE_ATTN_UNROLL + u for u in range(SAMPLE_ATTN_UNROLL)]
        _attend_units([unit(b, kv) for b in seqs for kv in range(N_KV_HEADS)], sink_ref, t_new)
        for b in seqs:
            r0 = pl.multiple_of(b * t_new, t_new)
            nk_ref[b] = jnp.concatenate([ck_ref[b, t_new:, :], k_scr[pl.ds(r0, t_new), :]], axis=0)
            nv_ref[b] = jnp.concatenate([cv_ref[b, t_new:, :], v_scr[pl.ds(r0, t_new), :]], axis=0)
        return carry

    lax.fori_loop(0, n_seq // SAMPLE_ATTN_UNROLL, seq_group, 0)
    y_ref[...] = _mm(a_scr[...], wo_ref[...]) + bo_ref[...] + x


def _attn_sample(x, g, wqkv, bqkv, wo, bo, cos_t, sin_t, sinks, ck, cv, n_batch, t_new):
    bg = SAMPLE_ATTN_BATCHES
    tm = bg * t_new
    row = pl.BlockSpec((tm, D_MODEL), lambda i: (i, 0))
    cache = pl.BlockSpec((bg, WINDOW, V7X_LANES), lambda i: (i, 0, 0))
    return pl.pallas_call(
        functools.partial(_attn_sample_kernel, n_seq=bg, t_new=t_new),
        grid=(n_batch // bg,),
        in_specs=[row, _const_spec((1, D_MODEL)), _const_spec(wqkv.shape),
                  _const_spec((1, wqkv.shape[1])), _const_spec(wo.shape), _const_spec((1, D_MODEL)),
                  _const_spec((tm, V7X_LANES)), _const_spec((tm, V7X_LANES)),
                  pl.BlockSpec(memory_space=pltpu.SMEM), cache, cache],
        out_specs=[row, cache, cache],
        out_shape=[jax.ShapeDtypeStruct((n_batch * t_new, D_MODEL), F32),
                   jax.ShapeDtypeStruct((n_batch, WINDOW, V7X_LANES), F32),
                   jax.ShapeDtypeStruct((n_batch, WINDOW, V7X_LANES), F32)],
        scratch_shapes=[pltpu.VMEM((tm, Q_DIM), F32),
                        pltpu.VMEM((tm, V7X_LANES), F32),
                        pltpu.VMEM((tm, V7X_LANES), F32),
                        pltpu.VMEM((tm, Q_DIM), F32)],
        compiler_params=pltpu.CompilerParams(
            dimension_semantics=("arbitrary",),
            vmem_limit_bytes=_vmem_limit(48 * 1024 * 1024)),
        name="attn_sample",
    )(x, g, wqkv, bqkv, wo, bo, cos_t, sin_t, sinks, ck, cv)


def _dwconv_groups(u, prev, w_ref, col0, ncol):
    rows = u.shape[0]
    kw = w_ref.shape[0]
    sub = lax.broadcasted_iota(jnp.int32, u.shape, 0) % V7X_SUBLANES
    y = u * w_ref[kw - 1:kw, col0:col0 + ncol]
    for s in range(1, kw):
        from_prev = pltpu.roll(prev, (rows - V7X_SUBLANES + s) % rows, 0)
        from_self = pltpu.roll(u, s, 0)
        shifted = jnp.where(sub < s, from_prev, from_self)
        y = y + shifted * w_ref[kw - 1 - s:kw - s, col0:col0 + ncol]
    return y


def _dwconv_rows(u, carry8, w_ref, col0, ncol):
    kw = w_ref.shape[0]
    sub8 = lax.broadcasted_iota(jnp.int32, carry8.shape, 0)
    y = u * w_ref[kw - 1:kw, col0:col0 + ncol]
    for s in range(1, kw):
        rolled = pltpu.roll(u, s, 0)
        head = jnp.where(sub8 < s, pltpu.roll(carry8, s, 0), rolled[:V7X_SUBLANES])
        shifted = jnp.concatenate([head, rolled[V7X_SUBLANES:]], axis=0)
        y = y + shifted * w_ref[kw - 1 - s:kw - s, col0:col0 + ncol]
    return y


def _scan_groups(a, b):
    pos = lax.broadcasted_iota(jnp.int32, a.shape, 0) % V7X_SUBLANES
    shift = 1
    while shift < V7X_SUBLANES:
        ok = pos >= shift
        a_sh = jnp.where(ok, pltpu.roll(a, shift, 0), 1.0)
        b_sh = jnp.where(ok, pltpu.roll(b, shift, 0), 0.0)
        b = a * b_sh + b
        a = a * a_sh
        shift *= 2
    return a, b


def _scan_rows(a, b, h_row):
    a_grp, b_grp = _scan_groups(a, b)
    out = []
    h = h_row
    for g in range(a.shape[0] // V7X_SUBLANES):
        rows = slice(g * V7X_SUBLANES, (g + 1) * V7X_SUBLANES)
        hs = a_grp[rows] * h + b_grp[rows]
        out.append(hs)
        h = hs[V7X_SUBLANES - 1:, :]
    return jnp.concatenate(out, axis=0)


def _log_sigmoid(x):
    return jnp.minimum(x, 0.0) - jnp.log1p(jnp.exp(-jnp.abs(x)))


def _rglru_body(x, g_ref, wgate_ref, win_ref, cw_ref, cb_ref, wa_ref, ba_ref, wx_ref, bx_ref,
                lam_ref, wout_ref, conv_of, scan_of, u_sink, h_sink):
    h = _rms(x, g_ref[...]).astype(BF16)
    acc = x
    for n in range(RG_BLOCKS):
        c0 = n * RG_BLOCK_W
        cs = slice(c0, c0 + RG_BLOCK_W)
        gate = jax.nn.gelu(jnp.dot(h, wgate_ref[:, cs], preferred_element_type=F32))
        u0 = jnp.dot(h, win_ref[:, cs], preferred_element_type=F32)
        u_sink(n, u0)
        u = conv_of(n, u0) + cb_ref[:, cs]
        ub = u.astype(BF16)
        r = jax.nn.sigmoid(jnp.dot(ub, wa_ref[n], preferred_element_type=F32) + ba_ref[:, cs])
        ig = jax.nn.sigmoid(jnp.dot(ub, wx_ref[n], preferred_element_type=F32) + bx_ref[:, cs])
        log_a = RG_C * r * _log_sigmoid(lam_ref[:, cs])
        a = jnp.exp(log_a)
        mult = jnp.sqrt(-jnp.tanh(log_a) * (1.0 + a * a))
        hs = scan_of(n, a, mult * (ig * u))
        h_sink(n, hs)
        acc = acc + jnp.dot((hs * gate).astype(BF16), wout_ref[cs, :], preferred_element_type=F32)
    return acc


def _rglru_prompt_kernel(x_ref, g_ref, wgate_ref, win_ref, cw_ref, cb_ref, wa_ref, ba_ref,
                         wx_ref, bx_ref, lam_ref, wout_ref, y_ref, hlast_ref, ulast_ref,
                         ucarry, hcarry, *, tm, n_tiles):
    j = pl.program_id(1)

    @pl.when(j == 0)
    def _():
        ucarry[...] = jnp.zeros(ucarry.shape, F32)
        hcarry[...] = jnp.zeros(hcarry.shape, F32)

    def conv_of(n, u0):
        c0 = n * RG_BLOCK_W
        return _dwconv_rows(u0, ucarry[:, c0:c0 + RG_BLOCK_W], cw_ref, c0, RG_BLOCK_W)

    def scan_of(n, a, b):
        c0 = n * RG_BLOCK_W
        return _scan_rows(a, b, hcarry[V7X_SUBLANES - 1:V7X_SUBLANES, c0:c0 + RG_BLOCK_W])

    def u_sink(n, u0):
        ulast_ref[0, :, n * RG_BLOCK_W:(n + 1) * RG_BLOCK_W] = u0[tm - V7X_SUBLANES:, :]

    def h_sink(n, hs):
        hlast_ref[0, :, n * RG_BLOCK_W:(n + 1) * RG_BLOCK_W] = hs[tm - V7X_SUBLANES:, :]

    y_ref[...] = _rglru_body(x_ref[...], g_ref, wgate_ref, win_ref, cw_ref, cb_ref, wa_ref, ba_ref,
                             wx_ref, bx_ref, lam_ref, wout_ref, conv_of, scan_of, u_sink, h_sink)
    ucarry[...] = ulast_ref[0]
    hcarry[...] = hlast_ref[0]


def _rglru_sample_kernel(x_ref, prev_ref, hinit_ref, g_ref, wgate_ref, win_ref, cw_ref, cb_ref,
                         wa_ref, ba_ref, wx_ref, bx_ref, lam_ref, wout_ref, y_ref, hs_ref, u_ref,
                         *, t_new):
    def conv_of(n, u0):
        c0 = n * RG_BLOCK_W
        return _dwconv_groups(u0, prev_ref[:, c0:c0 + RG_BLOCK_W], cw_ref, c0, RG_BLOCK_W)

    def scan_of(n, a, b):
        a_grp, b_grp = _scan_groups(a, b)
        return a_grp * hinit_ref[:, n * RG_BLOCK_W:(n + 1) * RG_BLOCK_W] + b_grp

    def u_sink(n, u0):
        u_ref[:, n * RG_BLOCK_W:(n + 1) * RG_BLOCK_W] = u0

    def h_sink(n, hs):
        hs_ref[:, n * RG_BLOCK_W:(n + 1) * RG_BLOCK_W] = hs

    y_ref[...] = _rglru_body(x_ref[...], g_ref, wgate_ref, win_ref, cw_ref, cb_ref, wa_ref, ba_ref,
                             wx_ref, bx_ref, lam_ref, wout_ref, conv_of, scan_of, u_sink, h_sink)


def _rglru_weight_specs(p):
    return [_const_spec(a.shape) for a in p]


def _rglru_prompt(x, params, batch, seq):
    tm = RG_ROW_TILE
    nt = seq // tm
    row = pl.BlockSpec((tm, D_MODEL), lambda b, j: (b * nt + j, 0))
    last = pl.BlockSpec((1, V7X_SUBLANES, D_MODEL), lambda b, j: (b, 0, 0))
    return pl.pallas_call(
        functools.partial(_rglru_prompt_kernel, tm=tm, n_tiles=nt),
        grid=(batch, nt),
        in_specs=[row] + _rglru_weight_specs(params),
        out_specs=[row, last, last],
        out_shape=[jax.ShapeDtypeStruct((batch * seq, D_MODEL), F32),
                   jax.ShapeDtypeStruct((batch, V7X_SUBLANES, D_MODEL), F32),
                   jax.ShapeDtypeStruct((batch, V7X_SUBLANES, D_MODEL), F32)],
        scratch_shapes=[pltpu.VMEM((V7X_SUBLANES, D_MODEL), F32),
                        pltpu.VMEM((V7X_SUBLANES, D_MODEL), F32)],
        compiler_params=pltpu.CompilerParams(
            dimension_semantics=("arbitrary", "arbitrary"),
            vmem_limit_bytes=_vmem_limit(48 * 1024 * 1024)),
        name="rglru_prompt",
    )(x, *params)


def _rglru_sample(x, prev, hinit, params, t_new):
    m = x.shape[0]
    tm = min(RG_ROW_TILE, m)
    row = pl.BlockSpec((tm, D_MODEL), lambda i: (i, 0))
    return pl.pallas_call(
        functools.partial(_rglru_sample_kernel, t_new=t_new),
        grid=(m // tm,),
        in_specs=[row, row, row] + _rglru_weight_specs(params),
        out_specs=[row, row, row],
        out_shape=[jax.ShapeDtypeStruct((m, D_MODEL), F32)] * 3,
        compiler_params=pltpu.CompilerParams(
            dimension_semantics=("arbitrary",),
            vmem_limit_bytes=_vmem_limit(48 * 1024 * 1024)),
        name="rglru_sample",
    )(x, prev, hinit, *params)


SCONV_CHUNK = V7X_MXU_DIM


def _sconv_body(x, g_ref, win_ref, cw_ref, wout_ref, conv_of, v_sink):
    h = _rms(x, g_ref[...]).astype(BF16)
    acc = x
    for n in range(D_MODEL // SCONV_CHUNK):
        c0 = n * SCONV_CHUNK
        bg = jnp.dot(h, win_ref[:, c0:c0 + SCONV_CHUNK], preferred_element_type=F32)
        cg = jnp.dot(h, win_ref[:, D_MODEL + c0:D_MODEL + c0 + SCONV_CHUNK],
                     preferred_element_type=F32)
        xv = jnp.dot(h, win_ref[:, 2 * D_MODEL + c0:2 * D_MODEL + c0 + SCONV_CHUNK],
                     preferred_element_type=F32)
        v = cg * xv
        v_sink(n, v)
        y = conv_of(n, v)
        acc = acc + jnp.dot((bg * y).astype(BF16), wout_ref[c0:c0 + SCONV_CHUNK, :],
                            preferred_element_type=F32)
    return acc


def _sconv_prompt_kernel(x_ref, g_ref, win_ref, cw_ref, wout_ref, y_ref, vlast_ref, vcarry,
                         *, tm):
    j = pl.program_id(1)

    @pl.when(j == 0)
    def _():
        vcarry[...] = jnp.zeros(vcarry.shape, F32)

    def conv_of(n, v):
        c0 = n * SCONV_CHUNK
        return _dwconv_rows(v, vcarry[:, c0:c0 + SCONV_CHUNK], cw_ref, c0, SCONV_CHUNK)

    def v_sink(n, v):
        vlast_ref[0, :, n * SCONV_CHUNK:(n + 1) * SCONV_CHUNK] = v[tm - V7X_SUBLANES:, :]

    y_ref[...] = _sconv_body(x_ref[...], g_ref, win_ref, cw_ref, wout_ref, conv_of, v_sink)
    vcarry[...] = vlast_ref[0]


def _sconv_sample_kernel(x_ref, prev_ref, g_ref, win_ref, cw_ref, wout_ref, y_ref, v_ref):
    def conv_of(n, v):
        c0 = n * SCONV_CHUNK
        return _dwconv_groups(v, prev_ref[:, c0:c0 + SCONV_CHUNK], cw_ref, c0, SCONV_CHUNK)

    def v_sink(n, v):
        v_ref[:, n * SCONV_CHUNK:(n + 1) * SCONV_CHUNK] = v

    y_ref[...] = _sconv_body(x_ref[...], g_ref, win_ref, cw_ref, wout_ref, conv_of, v_sink)


def _sconv_prompt(x, params, batch, seq):
    tm = ROW_TILE
    nt = seq // tm
    row = pl.BlockSpec((tm, D_MODEL), lambda b, j: (b * nt + j, 0))
    last = pl.BlockSpec((1, V7X_SUBLANES, D_MODEL), lambda b, j: (b, 0, 0))
    return pl.pallas_call(
        functools.partial(_sconv_prompt_kernel, tm=tm),
        grid=(batch, nt),
        in_specs=[row] + [_const_spec(a.shape) for a in params],
        out_specs=[row, last],
        out_shape=[jax.ShapeDtypeStruct((batch * seq, D_MODEL), F32),
                   jax.ShapeDtypeStruct((batch, V7X_SUBLANES, D_MODEL), F32)],
        scratch_shapes=[pltpu.VMEM((V7X_SUBLANES, D_MODEL), F32)],
        compiler_params=pltpu.CompilerParams(
            dimension_semantics=("arbitrary", "arbitrary"),
            vmem_limit_bytes=_vmem_limit(48 * 1024 * 1024)),
        name="sconv_prompt",
    )(x, *params)


def _sconv_sample(x, prev, params):
    m = x.shape[0]
    tm = min(ROW_TILE, m)
    row = pl.BlockSpec((tm, D_MODEL), lambda i: (i, 0))
    return pl.pallas_call(
        _sconv_sample_kernel,
        grid=(m // tm,),
        in_specs=[row, row] + [_const_spec(a.shape) for a in params],
        out_specs=[row, row],
        out_shape=[jax.ShapeDtypeStruct((m, D_MODEL), F32)] * 2,
        compiler_params=pltpu.CompilerParams(
            dimension_semantics=("arbitrary",),
            vmem_limit_bytes=_vmem_limit(48 * 1024 * 1024)),
        name="sconv_sample",
    )(x, prev, *params)


def _rope_tables(pos):
    half = HEAD_DIM // 2
    inv = ROPE_THETA ** (-jnp.arange(half, dtype=F32) / half)
    ang = pos.astype(F32)[:, None] * inv[None, :]
    cos = jnp.cos(ang)
    sin = jnp.sin(ang)
    reps = V7X_LANES // HEAD_DIM
    cos_t = jnp.tile(jnp.concatenate([cos, cos], axis=-1), (1, reps))
    sin_t = jnp.tile(jnp.concatenate([-sin, sin], axis=-1), (1, reps))
    return cos_t, sin_t


def _row(v):
    return v.reshape(1, -1)


def _pad_state_rows(buf):
    b, k, c = buf.shape
    padded = jnp.concatenate([jnp.zeros((b, V7X_SUBLANES - k, c), buf.dtype), buf], axis=1)
    return padded.reshape(b * V7X_SUBLANES, c)


def kernel(x_prompt, x_sample, cache_k, cache_v, state_rglru_h, state_rglru_conv, state_shortconv,
           norm_mixer, norm_ffn, norm_final,
           attn_w_qkv, attn_b_qkv, attn_w_o, attn_b_o, attn_sinks,
           rglru_w_gate, rglru_w_in, rglru_conv_w, rglru_conv_b, rglru_wa, rglru_ba,
           rglru_wx, rglru_bx, rglru_lambda, rglru_w_out,
           sconv_w_in, sconv_conv_w, sconv_w_out,
           ffn_w_gate, ffn_w_up, ffn_w_down):
    bp, seq, _ = x_prompt.shape
    bs, t_new, _ = x_sample.shape
    depth = norm_mixer.shape[0]
    past_len = 8192
    assert t_new == V7X_SUBLANES

    xp = x_prompt.reshape(bp * seq, D_MODEL)
    xs = x_sample.reshape(bs * t_new, D_MODEL)

    cos_p, sin_p = _rope_tables(jnp.arange(seq, dtype=jnp.int32))
    cos_s, sin_s = _rope_tables(past_len + jnp.arange(t_new, dtype=jnp.int32))
    cos_s = jnp.tile(cos_s, (SAMPLE_ATTN_BATCHES, 1))
    sin_s = jnp.tile(sin_s, (SAMPLE_ATTN_BATCHES, 1))

    kp_l, vp_l, ks_l, vs_l = [], [], [], []
    hp_l, hs_l, rcp_l, rcs_l = [], [], [], []
    scp_l, scs_l = [], []

    for i in range(depth):
        kind = i % 3
        j = i // 3
        g_mix = _row(norm_mixer[i])
        if kind == 0:
            wqkv = attn_w_qkv[j].astype(BF16)
            wo = attn_w_o[j].astype(BF16)
            bqkv = _row(attn_b_qkv[j])
            bo = _row(attn_b_o[j])
            sinks = attn_sinks[j]
            xp, kp, vp = _attn_prompt(xp, g_mix, wqkv, bqkv, wo, bo, cos_p, sin_p, sinks, bp, seq)
            ck = cache_k[j].reshape(bs, WINDOW, KV_DIM)
            cv = cache_v[j].reshape(bs, WINDOW, KV_DIM)
            xs, ks, vs = _attn_sample(xs, g_mix, wqkv, bqkv, wo, bo, cos_s, sin_s, sinks, ck, cv,
                                      bs, t_new)
            kp_l.append(kp.reshape(bp, WINDOW, N_KV_HEADS, HEAD_DIM))
            vp_l.append(vp.reshape(bp, WINDOW, N_KV_HEADS, HEAD_DIM))
            ks_l.append(ks.reshape(bs, WINDOW, N_KV_HEADS, HEAD_DIM))
            vs_l.append(vs.reshape(bs, WINDOW, N_KV_HEADS, HEAD_DIM))
        elif kind == 1:
            params = (g_mix, rglru_w_gate[j].astype(BF16), rglru_w_in[j].astype(BF16),
                      rglru_conv_w[j], _row(rglru_conv_b[j]), rglru_wa[j].astype(BF16),
                      _row(rglru_ba[j]), rglru_wx[j].astype(BF16), _row(rglru_bx[j]),
                      _row(rglru_lambda[j]), rglru_w_out[j].astype(BF16))
            xp, hlast, ulast = _rglru_prompt(xp, params, bp, seq)
            hp_l.append(hlast[:, V7X_SUBLANES - 1])
            rcp_l.append(ulast[:, V7X_SUBLANES - (RG_CONV_W - 1):])
            prev = _pad_state_rows(state_rglru_conv[j])
            hinit = jnp.repeat(state_rglru_h[j], t_new, axis=0)
            xs, hs_all, u_all = _rglru_sample(xs, prev, hinit, params, t_new)
            hs_l.append(hs_all.reshape(bs, t_new, D_MODEL)[:, t_new - 1])
            rcs_l.append(u_all.reshape(bs, t_new, D_MODEL)[:, t_new - (RG_CONV_W - 1):])
        else:
            params = (g_mix, sconv_w_in[j].astype(BF16), sconv_conv_w[j],
                      sconv_w_out[j].astype(BF16))
            xp, vlast = _sconv_prompt(xp, params, bp, seq)
            scp_l.append(vlast[:, V7X_SUBLANES - (SCONV_W - 1):])
            prev = _pad_state_rows(state_shortconv[j])
            xs, v_all = _sconv_sample(xs, prev, params)
            scs_l.append(v_all.reshape(bs, t_new, D_MODEL)[:, t_new - (SCONV_W - 1):])

        wg = ffn_w_gate[i].astype(BF16)
        wu = ffn_w_up[i].astype(BF16)
        wd = ffn_w_down[i].astype(BF16)
        g_ffn = _row(norm_ffn[i])
        g_fin = _row(norm_final)
        last = i == depth - 1
        xp = _ffn(xp, g_ffn, wg, wu, wd, g_fin, last)
        xs = _ffn(xs, g_ffn, wg, wu, wd, g_fin, last)

    return (xp.reshape(bp, seq, D_MODEL), xs.reshape(bs, t_new, D_MODEL),
            jnp.stack(kp_l), jnp.stack(vp_l), jnp.stack(ks_l), jnp.stack(vs_l),
            jnp.stack(hp_l), jnp.stack(hs_l), jnp.stack(rcp_l), jnp.stack(rcs_l),
            jnp.stack(scp_l), jnp.stack(scs_l))
```

```python
import functools

import jax
import jax.numpy as jnp
from jax import lax
from jax.experimental import pallas as pl
from jax.experimental.pallas import tpu as pltpu

D_MODEL = 1024
HEAD_DIM = 64
N_HEADS = 16
N_KV_HEADS = 2
GQA_GROUP = 8
Q_DIM = N_HEADS * HEAD_DIM
KV_DIM = N_KV_HEADS * HEAD_DIM
WINDOW = 128
ROPE_THETA = 10000.0
NEG_INF = -1e30
RG_BLOCKS = 4
RG_BLOCK_W = 256
RG_CONV_W = 4
RG_C = 8.0
SCONV_W = 3
EPS = 1e-6

V7X_LANES = 128
V7X_SUBLANES = 8
V7X_MXU_DIM = 256
V7X_VMEM_BYTES = 64 * 1024 * 1024

BF16 = jnp.bfloat16
F32 = jnp.float32

FFN_CHUNK = V7X_MXU_DIM
ROW_TILE = 512
RG_ROW_TILE = 256
SAMPLE_ATTN_BATCHES = 32
SAMPLE_ATTN_UNROLL = 8
ATTN_LOOKAHEAD = 2
PROMPT_ATTN_COLS = 2
COLS_PER_KV = (N_HEADS // N_KV_HEADS) * HEAD_DIM // V7X_LANES


def _vmem_limit(nbytes):
    return int(min(nbytes, V7X_VMEM_BYTES - 8 * 1024 * 1024))


def _const_spec(shape):
    nd = len(shape)
    return pl.BlockSpec(shape, lambda *_: (0,) * nd, pipeline_mode=pl.Buffered(1))


def _layer(stacked, layer):
    return (stacked, layer)


def _spec_of(w):
    if isinstance(w, tuple):
        stacked, layer = w
        nd = stacked.ndim - 1
        return pl.BlockSpec((None,) + stacked.shape[1:], lambda *_: (layer,) + (0,) * nd,
                            pipeline_mode=pl.Buffered(1))
    return _const_spec(w.shape)


def _arg_of(w):
    return w[0] if isinstance(w, tuple) else w


def _nbytes_of(w):
    if isinstance(w, tuple):
        return w[0][0].size * w[0].dtype.itemsize
    return w.size * w.dtype.itemsize


def _rms(x, g):
    ms = jnp.mean(x * x, axis=-1, keepdims=True)
    return x * lax.rsqrt(ms + EPS) * g


def _mm(a, w):
    return jnp.dot(a.astype(BF16), w.astype(BF16), preferred_element_type=F32)


def _ffn_kernel(x_ref, g_ref, wg_ref, wu_ref, wd_ref, gf_ref, o_ref, *, n_chunks, final_norm):
    x = x_ref[...]
    h = _rms(x, g_ref[...]).astype(BF16)
    acc = x
    for c in range(n_chunks):
        cs = slice(c * FFN_CHUNK, (c + 1) * FFN_CHUNK)
        gate = _mm(h, wg_ref[:, cs])
        up = _mm(h, wu_ref[:, cs])
        act = (gate * jax.nn.sigmoid(gate)) * up
        acc = acc + _mm(act, wd_ref[cs, :])
    if final_norm:
        acc = _rms(acc, gf_ref[...])
    o_ref[...] = acc


def _ffn(x, g, wg, wu, wd, d_ff, gf, final_norm):
    m = x.shape[0]
    tm = min(ROW_TILE, m)
    n_chunks = d_ff // FFN_CHUNK
    row = pl.BlockSpec((tm, D_MODEL), lambda i: (i, 0))
    weight_bytes = _nbytes_of(wg) + _nbytes_of(wu) + _nbytes_of(wd)
    return pl.pallas_call(
        functools.partial(_ffn_kernel, n_chunks=n_chunks, final_norm=final_norm),
        grid=(m // tm,),
        in_specs=[row, _const_spec((1, D_MODEL)), _spec_of(wg), _spec_of(wu), _spec_of(wd),
                  _const_spec((1, D_MODEL))],
        out_specs=row,
        out_shape=jax.ShapeDtypeStruct((m, D_MODEL), F32),
        compiler_params=pltpu.CompilerParams(
            dimension_semantics=("arbitrary",),
            vmem_limit_bytes=_vmem_limit(weight_bytes + 24 * tm * D_MODEL * 4)),
        name="ffn",
    )(x, g, _arg_of(wg), _arg_of(wu), _arg_of(wd), gf)


def _rope_cols(cols, cos, sin_signed, first_half):
    swapped = jnp.where(first_half,
                        pltpu.roll(cols, V7X_LANES - HEAD_DIM // 2, 1),
                        pltpu.roll(cols, HEAD_DIM // 2, 1))
    return cols * cos + swapped * sin_signed


def _pair_blockdiag(mat, mat_rolled, kv, lane_lt64):
    zero = jnp.zeros_like(mat)
    if kv == 0:
        top = jnp.where(lane_lt64, mat, zero)
        bottom = jnp.where(lane_lt64, zero, mat_rolled)
    else:
        top = jnp.where(lane_lt64, mat_rolled, zero)
        bottom = jnp.where(lane_lt64, zero, mat)
    return jnp.concatenate([top, bottom], axis=0).astype(BF16)


def _key_blockdiag(k2, kv):
    lane_lt64 = lax.broadcasted_iota(jnp.int32, k2.shape, 1) < HEAD_DIM
    return _pair_blockdiag(k2, pltpu.roll(k2, HEAD_DIM, 1), kv, lane_lt64)


def _value_blockdiag(v2, kv):
    v2 = jnp.where(lax.broadcasted_iota(jnp.int32, v2.shape, 0) == 0, 0.0, v2)
    return _key_blockdiag(v2, kv)


def _attn_scores(qs, kbd):
    return lax.dot_general(qs, kbd, (((1,), (1,)), ((), ())), preferred_element_type=F32)


def _attn_probs(s_all, mask, sink_ref, kv, cols, tq):
    nk = 2 * WINDOW
    key = lax.broadcasted_iota(jnp.int32, (1, nk), 1)
    p_rows = []
    for i, c in enumerate(cols):
        p_halves = []
        for parity in range(2):
            sink = sink_ref[kv * GQA_GROUP + 2 * c + parity]
            fill = jnp.where(key == 0, sink, NEG_INF)
            s = s_all[i * tq:(i + 1) * tq, parity * nk:(parity + 1) * nk]
            s = jnp.where(mask, s, fill)
            m = jnp.max(s, axis=-1, keepdims=True)
            e = jnp.exp(s - m)
            den = jnp.sum(e, axis=-1, keepdims=True)
            p_halves.append(e * (1.0 / den))
        p_rows.append(jnp.concatenate(p_halves, axis=1))
    return jnp.concatenate(p_rows, axis=0).astype(BF16)


def _attend_units(units, sink_ref, tq):
    n = len(units)
    scores, kbd, vbd = {}, {}, {}

    def issue_scores(i):
        load_q, load_k, _, _, kv, _, _, kv_id = units[i]
        if kv_id not in kbd:
            kbd[kv_id] = _key_blockdiag(load_k(), kv)
        scores[i] = _attn_scores(load_q(), kbd[kv_id])

    for i in range(min(ATTN_LOOKAHEAD, n)):
        issue_scores(i)
    for i in range(n):
        if i + ATTN_LOOKAHEAD < n:
            issue_scores(i + ATTN_LOOKAHEAD)
        _, _, load_v, mask, kv, cols, store_o, kv_id = units[i]
        if kv_id not in vbd:
            vbd[kv_id] = _value_blockdiag(load_v(), kv)
        p_all = _attn_probs(scores.pop(i), mask, sink_ref, kv, cols, tq)
        store_o(jnp.dot(p_all, vbd[kv_id], preferred_element_type=F32))


def _band_mask(tq, col_min):
    row = lax.broadcasted_iota(jnp.int32, (tq, 2 * WINDOW), 0)
    col = lax.broadcasted_iota(jnp.int32, (tq, 2 * WINDOW), 1)
    prev_ok = (col < WINDOW) & (col > row)
    own_ok = (col >= WINDOW) & (col - WINDOW <= row)
    return (prev_ok | own_ok) & (col >= col_min)


def _attn_prompt_kernel(x_ref, g_ref, wqkv_ref, bqkv_ref, wo_ref, bo_ref, cos_ref, sin_ref,
                        sink_ref, y_ref, knew_ref, vnew_ref,
                        q_scr, k_scr, v_scr, a_scr, *, tq_tile, n_tiles):
    j = pl.program_id(1)
    n_blk = tq_tile // WINDOW

    @pl.when(j == 0)
    def _():
        k_scr[0:WINDOW, :] = jnp.zeros((WINDOW, V7X_LANES), F32)
        v_scr[0:WINDOW, :] = jnp.zeros((WINDOW, V7X_LANES), F32)

    x = x_ref[...]
    h = _rms(x, g_ref[...])
    qkv = _mm(h, wqkv_ref[...]) + bqkv_ref[...]
    cos = cos_ref[...]
    sin = sin_ref[...]
    lane = lax.broadcasted_iota(jnp.int32, (tq_tile, V7X_LANES), 1)
    first_half = (lane % HEAD_DIM) < (HEAD_DIM // 2)
    scale = HEAD_DIM ** -0.5
    for c in range(Q_DIM // V7X_LANES):
        qc = _rope_cols(qkv[:, c * V7X_LANES:(c + 1) * V7X_LANES], cos, sin, first_half)
        q_scr[:, c * V7X_LANES:(c + 1) * V7X_LANES] = (qc * scale).astype(BF16)
    k_new = _rope_cols(qkv[:, Q_DIM:Q_DIM + KV_DIM], cos, sin, first_half)
    v_new = qkv[:, Q_DIM + KV_DIM:]
    k_scr[WINDOW:, :] = k_new
    v_scr[WINDOW:, :] = v_new

    def unit(r0, mask, kv, cols):
        def load_q():
            return jnp.concatenate(
                [q_scr[pl.ds(r0, WINDOW),
                       (kv * COLS_PER_KV + c) * V7X_LANES:(kv * COLS_PER_KV + c + 1) * V7X_LANES]
                 for c in cols], axis=0)

        def store_o(o):
            for i, c in enumerate(cols):
                col = kv * COLS_PER_KV + c
                a_scr[pl.ds(r0, WINDOW), col * V7X_LANES:(col + 1) * V7X_LANES] = (
                    o[i * WINDOW:(i + 1) * WINDOW].astype(BF16))

        return (load_q, lambda: k_scr[pl.ds(r0, 2 * WINDOW), :],
                lambda: v_scr[pl.ds(r0, 2 * WINDOW), :], mask, kv, cols, store_o, kv)

    col_groups = [tuple(range(c, c + PROMPT_ATTN_COLS))
                  for c in range(0, COLS_PER_KV, PROMPT_ATTN_COLS)]

    def block(blk, carry):
        r0 = pl.multiple_of(blk * WINDOW, WINDOW)
        first = jnp.logical_and(j == 0, blk == 0)
        mask = _band_mask(WINDOW, jnp.where(first, WINDOW, 0))
        _attend_units([unit(r0, mask, kv, cols) for kv in range(N_KV_HEADS)
                       for cols in col_groups], sink_ref, WINDOW)
        return carry

    lax.fori_loop(0, n_blk, block, 0)

    y_ref[...] = _mm(a_scr[...], wo_ref[...]) + bo_ref[...] + x

    k_scr[0:WINDOW, :] = k_new[tq_tile - WINDOW:, :]
    v_scr[0:WINDOW, :] = v_new[tq_tile - WINDOW:, :]

    @pl.when(j == n_tiles - 1)
    def _():
        knew_ref[0] = k_new[tq_tile - WINDOW:, :]
        vnew_ref[0] = v_new[tq_tile - WINDOW:, :]


def _attn_prompt(x, g, wqkv, bqkv, wo, bo, cos_t, sin_t, sinks, batch, seq):
    tq = ROW_TILE
    nt = seq // tq
    row = pl.BlockSpec((tq, D_MODEL), lambda b, j: (b * nt + j, 0))
    tab = pl.BlockSpec((tq, V7X_LANES), lambda b, j: (j, 0))
    cache = pl.BlockSpec((1, WINDOW, V7X_LANES), lambda b, j: (b, 0, 0))
    return pl.pallas_call(
        functools.partial(_attn_prompt_kernel, tq_tile=tq, n_tiles=nt),
        grid=(batch, nt),
        in_specs=[row, _const_spec((1, D_MODEL)), _spec_of(wqkv),
                  _const_spec((1, Q_DIM + 2 * KV_DIM)), _spec_of(wo), _const_spec((1, D_MODEL)),
                  tab, tab, pl.BlockSpec(memory_space=pltpu.SMEM)],
        out_specs=[row, cache, cache],
        out_shape=[jax.ShapeDtypeStruct((batch * seq, D_MODEL), F32),
                   jax.ShapeDtypeStruct((batch, WINDOW, V7X_LANES), F32),
                   jax.ShapeDtypeStruct((batch, WINDOW, V7X_LANES), F32)],
        scratch_shapes=[pltpu.VMEM((tq, Q_DIM), BF16),
                        pltpu.VMEM((WINDOW + tq, V7X_LANES), F32),
                        pltpu.VMEM((WINDOW + tq, V7X_LANES), F32),
                        pltpu.VMEM((tq, Q_DIM), BF16)],
        compiler_params=pltpu.CompilerParams(
            dimension_semantics=("arbitrary", "arbitrary"),
            vmem_limit_bytes=_vmem_limit(48 * 1024 * 1024)),
        name="attn_prompt",
    )(x, g, _arg_of(wqkv), bqkv, _arg_of(wo), bo, cos_t, sin_t, sinks)


def _attn_sample_kernel(x_ref, g_ref, wqkv_ref, bqkv_ref, wo_ref, bo_ref, cos_ref, sin_ref,
                        sink_ref, ck_ref, cv_ref, y_ref, nk_ref, nv_ref,
                        q_scr, k_scr, v_scr, a_scr, *, n_seq, t_new):
    tm = n_seq * t_new
    x = x_ref[...]
    h = _rms(x, g_ref[...])
    qkv = _mm(h, wqkv_ref[...]) + bqkv_ref[...]
    cos = cos_ref[...]
    sin = sin_ref[...]
    lane = lax.broadcasted_iota(jnp.int32, (tm, V7X_LANES), 1)
    first_half = (lane % HEAD_DIM) < (HEAD_DIM // 2)
    scale = HEAD_DIM ** -0.5
    for c in range(Q_DIM // V7X_LANES):
        qc = _rope_cols(qkv[:, c * V7X_LANES:(c + 1) * V7X_LANES], cos, sin, first_half)
        q_scr[:, c * V7X_LANES:(c + 1) * V7X_LANES] = qc * scale
    k_scr[...] = _rope_cols(qkv[:, Q_DIM:Q_DIM + KV_DIM], cos, sin, first_half)
    v_scr[...] = qkv[:, Q_DIM + KV_DIM:]
    mask = _band_mask(t_new, 0)
    pad = jnp.zeros((WINDOW - t_new, V7X_LANES), F32)

    def unit(u, b, kv):
        r0 = pl.multiple_of(b * t_new, t_new)

        def load_q():
            return jnp.concatenate(
                [q_scr[pl.ds(r0, t_new),
                       (kv * COLS_PER_KV + c) * V7X_LANES:(kv * COLS_PER_KV + c + 1) * V7X_LANES]
                 for c in range(COLS_PER_KV)], axis=0).astype(BF16)

        def load_k():
            return jnp.concatenate([ck_ref[b], k_scr[pl.ds(r0, t_new), :], pad], axis=0)

        def load_v():
            return jnp.concatenate([cv_ref[b], v_scr[pl.ds(r0, t_new), :], pad], axis=0)

        def store_o(o):
            for c in range(COLS_PER_KV):
                col = kv * COLS_PER_KV + c
                a_scr[pl.ds(r0, t_new), col * V7X_LANES:(col + 1) * V7X_LANES] = (
                    o[c * t_new:(c + 1) * t_new])

        return (load_q, load_k, load_v, mask, kv, tuple(range(COLS_PER_KV)), store_o, (u, kv))

    def seq_group(i, carry):
        seqs = [i * SAMPLE_ATTN_UNROLL + u for u in range(SAMPLE_ATTN_UNROLL)]
        _attend_units([unit(u, b, kv) for u, b in enumerate(seqs) for kv in range(N_KV_HEADS)],
                      sink_ref, t_new)
        for b in seqs:
            r0 = pl.multiple_of(b * t_new, t_new)
            nk_ref[b] = jnp.concatenate([ck_ref[b, t_new:, :], k_scr[pl.ds(r0, t_new), :]], axis=0)
            nv_ref[b] = jnp.concatenate([cv_ref[b, t_new:, :], v_scr[pl.ds(r0, t_new), :]], axis=0)
        return carry

    lax.fori_loop(0, n_seq // SAMPLE_ATTN_UNROLL, seq_group, 0)
    y_ref[...] = _mm(a_scr[...], wo_ref[...]) + bo_ref[...] + x


def _attn_sample(x, g, wqkv, bqkv, wo, bo, cos_t, sin_t, sinks, ck, cv, n_batch, t_new):
    bg = SAMPLE_ATTN_BATCHES
    tm = bg * t_new
    row = pl.BlockSpec((tm, D_MODEL), lambda i: (i, 0))
    cache = pl.BlockSpec((bg, WINDOW, V7X_LANES), lambda i: (i, 0, 0))
    return pl.pallas_call(
        functools.partial(_attn_sample_kernel, n_seq=bg, t_new=t_new),
        grid=(n_batch // bg,),
        in_specs=[row, _const_spec((1, D_MODEL)), _spec_of(wqkv),
                  _const_spec((1, Q_DIM + 2 * KV_DIM)), _spec_of(wo), _const_spec((1, D_MODEL)),
                  _const_spec((tm, V7X_LANES)), _const_spec((tm, V7X_LANES)),
                  pl.BlockSpec(memory_space=pltpu.SMEM), cache, cache],
        out_specs=[row, cache, cache],
        out_shape=[jax.ShapeDtypeStruct((n_batch * t_new, D_MODEL), F32),
                   jax.ShapeDtypeStruct((n_batch, WINDOW, V7X_LANES), F32),
                   jax.ShapeDtypeStruct((n_batch, WINDOW, V7X_LANES), F32)],
        scratch_shapes=[pltpu.VMEM((tm, Q_DIM), F32),
                        pltpu.VMEM((tm, V7X_LANES), F32),
                        pltpu.VMEM((tm, V7X_LANES), F32),
                        pltpu.VMEM((tm, Q_DIM), F32)],
        compiler_params=pltpu.CompilerParams(
            dimension_semantics=("arbitrary",),
            vmem_limit_bytes=_vmem_limit(48 * 1024 * 1024)),
        name="attn_sample",
    )(x, g, _arg_of(wqkv), bqkv, _arg_of(wo), bo, cos_t, sin_t, sinks, ck, cv)


def _dwconv_groups(u, prev, w_ref, col0, ncol):
    rows = u.shape[0]
    kw = w_ref.shape[0]
    sub = lax.broadcasted_iota(jnp.int32, u.shape, 0) % V7X_SUBLANES
    y = u * w_ref[kw - 1:kw, col0:col0 + ncol]
    for s in range(1, kw):
        from_prev = pltpu.roll(prev, (rows - V7X_SUBLANES + s) % rows, 0)
        from_self = pltpu.roll(u, s, 0)
        shifted = jnp.where(sub < s, from_prev, from_self)
        y = y + shifted * w_ref[kw - 1 - s:kw - s, col0:col0 + ncol]
    return y


def _dwconv_rows(u, carry8, w_ref, col0, ncol):
    kw = w_ref.shape[0]
    sub8 = lax.broadcasted_iota(jnp.int32, carry8.shape, 0)
    y = u * w_ref[kw - 1:kw, col0:col0 + ncol]
    for s in range(1, kw):
        rolled = pltpu.roll(u, s, 0)
        head = jnp.where(sub8 < s, pltpu.roll(carry8, s, 0), rolled[:V7X_SUBLANES])
        shifted = jnp.concatenate([head, rolled[V7X_SUBLANES:]], axis=0)
        y = y + shifted * w_ref[kw - 1 - s:kw - s, col0:col0 + ncol]
    return y


def _scan_groups(a, b):
    pos = lax.broadcasted_iota(jnp.int32, a.shape, 0) % V7X_SUBLANES
    shift = 1
    while shift < V7X_SUBLANES:
        ok = pos >= shift
        a_sh = jnp.where(ok, pltpu.roll(a, shift, 0), 1.0)
        b_sh = jnp.where(ok, pltpu.roll(b, shift, 0), 0.0)
        b = a * b_sh + b
        a = a * a_sh
        shift *= 2
    return a, b


def _scan_rows(a, b, h_row):
    a_grp, b_grp = _scan_groups(a, b)
    out = []
    h = h_row
    for g in range(a.shape[0] // V7X_SUBLANES):
        rows = slice(g * V7X_SUBLANES, (g + 1) * V7X_SUBLANES)
        hs = a_grp[rows] * h + b_grp[rows]
        out.append(hs)
        h = hs[V7X_SUBLANES - 1:, :]
    return jnp.concatenate(out, axis=0)


def _log_sigmoid(x):
    return jnp.minimum(x, 0.0) - jnp.log1p(jnp.exp(-jnp.abs(x)))


def _rglru_body(x, g_ref, wgate_ref, win_ref, cw_ref, cb_ref, wa_ref, ba_ref, wx_ref, bx_ref,
                lam_ref, wout_ref, conv_of, scan_of, u_sink, h_sink):
    h = _rms(x, g_ref[...]).astype(BF16)
    acc = x
    for n in range(RG_BLOCKS):
        c0 = n * RG_BLOCK_W
        cs = slice(c0, c0 + RG_BLOCK_W)
        gate = jax.nn.gelu(jnp.dot(h, wgate_ref[:, cs], preferred_element_type=F32))
        u0 = jnp.dot(h, win_ref[:, cs], preferred_element_type=F32)
        u_sink(n, u0)
        u = conv_of(n, u0) + cb_ref[:, cs]
        ub = u.astype(BF16)
        r = jax.nn.sigmoid(jnp.dot(ub, wa_ref[n], preferred_element_type=F32) + ba_ref[:, cs])
        ig = jax.nn.sigmoid(jnp.dot(ub, wx_ref[n], preferred_element_type=F32) + bx_ref[:, cs])
        log_a = RG_C * r * _log_sigmoid(lam_ref[:, cs])
        a = jnp.exp(log_a)
        mult = jnp.sqrt(-jnp.tanh(log_a) * (1.0 + a * a))
        hs = scan_of(n, a, mult * (ig * u))
        h_sink(n, hs)
        acc = acc + jnp.dot((hs * gate).astype(BF16), wout_ref[cs, :], preferred_element_type=F32)
    return acc


def _rglru_prompt_kernel(x_ref, g_ref, wgate_ref, win_ref, cw_ref, cb_ref, wa_ref, ba_ref,
                         wx_ref, bx_ref, lam_ref, wout_ref, y_ref, hlast_ref, ulast_ref,
                         ucarry, hcarry, *, tm, n_tiles):
    j = pl.program_id(1)

    @pl.when(j == 0)
    def _():
        ucarry[...] = jnp.zeros(ucarry.shape, F32)
        hcarry[...] = jnp.zeros(hcarry.shape, F32)

    def conv_of(n, u0):
        c0 = n * RG_BLOCK_W
        return _dwconv_rows(u0, ucarry[:, c0:c0 + RG_BLOCK_W], cw_ref, c0, RG_BLOCK_W)

    def scan_of(n, a, b):
        c0 = n * RG_BLOCK_W
        return _scan_rows(a, b, hcarry[V7X_SUBLANES - 1:V7X_SUBLANES, c0:c0 + RG_BLOCK_W])

    def u_sink(n, u0):
        ulast_ref[0, :, n * RG_BLOCK_W:(n + 1) * RG_BLOCK_W] = u0[tm - V7X_SUBLANES:, :]

    def h_sink(n, hs):
        hlast_ref[0, :, n * RG_BLOCK_W:(n + 1) * RG_BLOCK_W] = hs[tm - V7X_SUBLANES:, :]

    y_ref[...] = _rglru_body(x_ref[...], g_ref, wgate_ref, win_ref, cw_ref, cb_ref, wa_ref, ba_ref,
                             wx_ref, bx_ref, lam_ref, wout_ref, conv_of, scan_of, u_sink, h_sink)
    ucarry[...] = ulast_ref[0]
    hcarry[...] = hlast_ref[0]


def _rglru_sample_kernel(x_ref, prev_ref, hinit_ref, g_ref, wgate_ref, win_ref, cw_ref, cb_ref,
                         wa_ref, ba_ref, wx_ref, bx_ref, lam_ref, wout_ref, y_ref, hs_ref, u_ref,
                         *, t_new):
    def conv_of(n, u0):
        c0 = n * RG_BLOCK_W
        return _dwconv_groups(u0, prev_ref[:, c0:c0 + RG_BLOCK_W], cw_ref, c0, RG_BLOCK_W)

    def scan_of(n, a, b):
        a_grp, b_grp = _scan_groups(a, b)
        return a_grp * hinit_ref[:, n * RG_BLOCK_W:(n + 1) * RG_BLOCK_W] + b_grp

    def u_sink(n, u0):
        u_ref[:, n * RG_BLOCK_W:(n + 1) * RG_BLOCK_W] = u0

    def h_sink(n, hs):
        hs_ref[:, n * RG_BLOCK_W:(n + 1) * RG_BLOCK_W] = hs

    y_ref[...] = _rglru_body(x_ref[...], g_ref, wgate_ref, win_ref, cw_ref, cb_ref, wa_ref, ba_ref,
                             wx_ref, bx_ref, lam_ref, wout_ref, conv_of, scan_of, u_sink, h_sink)


def _rglru_weight_specs(p):
    return [_const_spec(a.shape) for a in p]


def _rglru_prompt(x, params, batch, seq):
    tm = RG_ROW_TILE
    nt = seq // tm
    row = pl.BlockSpec((tm, D_MODEL), lambda b, j: (b * nt + j, 0))
    last = pl.BlockSpec((1, V7X_SUBLANES, D_MODEL), lambda b, j: (b, 0, 0))
    return pl.pallas_call(
        functools.partial(_rglru_prompt_kernel, tm=tm, n_tiles=nt),
        grid=(batch, nt),
        in_specs=[row] + _rglru_weight_specs(params),
        out_specs=[row, last, last],
        out_shape=[jax.ShapeDtypeStruct((batch * seq, D_MODEL), F32),
                   jax.ShapeDtypeStruct((batch, V7X_SUBLANES, D_MODEL), F32),
                   jax.ShapeDtypeStruct((batch, V7X_SUBLANES, D_MODEL), F32)],
        scratch_shapes=[pltpu.VMEM((V7X_SUBLANES, D_MODEL), F32),
                        pltpu.VMEM((V7X_SUBLANES, D_MODEL), F32)],
        compiler_params=pltpu.CompilerParams(
            dimension_semantics=("arbitrary", "arbitrary"),
            vmem_limit_bytes=_vmem_limit(48 * 1024 * 1024)),
        name="rglru_prompt",
    )(x, *params)


def _rglru_sample(x, prev, hinit, params, t_new):
    m = x.shape[0]
    tm = min(RG_ROW_TILE, m)
    row = pl.BlockSpec((tm, D_MODEL), lambda i: (i, 0))
    return pl.pallas_call(
        functools.partial(_rglru_sample_kernel, t_new=t_new),
        grid=(m // tm,),
        in_specs=[row, row, row] + _rglru_weight_specs(params),
        out_specs=[row, row, row],
        out_shape=[jax.ShapeDtypeStruct((m, D_MODEL), F32)] * 3,
        compiler_params=pltpu.CompilerParams(
            dimension_semantics=("arbitrary",),
            vmem_limit_bytes=_vmem_limit(48 * 1024 * 1024)),
        name="rglru_sample",
    )(x, prev, hinit, *params)


SCONV_CHUNK = V7X_MXU_DIM


def _sconv_body(x, g_ref, win_ref, cw_ref, wout_ref, conv_of, v_sink):
    h = _rms(x, g_ref[...]).astype(BF16)
    acc = x
    for n in range(D_MODEL // SCONV_CHUNK):
        c0 = n * SCONV_CHUNK
        bg = _mm(h, win_ref[:, c0:c0 + SCONV_CHUNK])
        cg = _mm(h, win_ref[:, D_MODEL + c0:D_MODEL + c0 + SCONV_CHUNK])
        xv = _mm(h, win_ref[:, 2 * D_MODEL + c0:2 * D_MODEL + c0 + SCONV_CHUNK])
        v = cg * xv
        v_sink(n, v)
        y = conv_of(n, v)
        acc = acc + _mm(bg * y, wout_ref[c0:c0 + SCONV_CHUNK, :])
    return acc


def _sconv_prompt_kernel(x_ref, g_ref, win_ref, cw_ref, wout_ref, y_ref, vlast_ref, vcarry,
                         *, tm):
    j = pl.program_id(1)

    @pl.when(j == 0)
    def _():
        vcarry[...] = jnp.zeros(vcarry.shape, F32)

    def conv_of(n, v):
        c0 = n * SCONV_CHUNK
        return _dwconv_rows(v, vcarry[:, c0:c0 + SCONV_CHUNK], cw_ref, c0, SCONV_CHUNK)

    def v_sink(n, v):
        vlast_ref[0, :, n * SCONV_CHUNK:(n + 1) * SCONV_CHUNK] = v[tm - V7X_SUBLANES:, :]

    y_ref[...] = _sconv_body(x_ref[...], g_ref, win_ref, cw_ref, wout_ref, conv_of, v_sink)
    vcarry[...] = vlast_ref[0]


def _sconv_sample_kernel(x_ref, prev_ref, g_ref, win_ref, cw_ref, wout_ref, y_ref, v_ref):
    def conv_of(n, v):
        c0 = n * SCONV_CHUNK
        return _dwconv_groups(v, prev_ref[:, c0:c0 + SCONV_CHUNK], cw_ref, c0, SCONV_CHUNK)

    def v_sink(n, v):
        v_ref[:, n * SCONV_CHUNK:(n + 1) * SCONV_CHUNK] = v

    y_ref[...] = _sconv_body(x_ref[...], g_ref, win_ref, cw_ref, wout_ref, conv_of, v_sink)


def _sconv_prompt(x, params, batch, seq):
    tm = ROW_TILE
    nt = seq // tm
    row = pl.BlockSpec((tm, D_MODEL), lambda b, j: (b * nt + j, 0))
    last = pl.BlockSpec((1, V7X_SUBLANES, D_MODEL), lambda b, j: (b, 0, 0))
    return pl.pallas_call(
        functools.partial(_sconv_prompt_kernel, tm=tm),
        grid=(batch, nt),
        in_specs=[row] + [_spec_of(a) for a in params],
        out_specs=[row, last],
        out_shape=[jax.ShapeDtypeStruct((batch * seq, D_MODEL), F32),
                   jax.ShapeDtypeStruct((batch, V7X_SUBLANES, D_MODEL), F32)],
        scratch_shapes=[pltpu.VMEM((V7X_SUBLANES, D_MODEL), F32)],
        compiler_params=pltpu.CompilerParams(
            dimension_semantics=("arbitrary", "arbitrary"),
            vmem_limit_bytes=_vmem_limit(48 * 1024 * 1024)),
        name="sconv_prompt",
    )(x, *[_arg_of(a) for a in params])


def _sconv_sample(x, prev, params):
    m = x.shape[0]
    tm = min(ROW_TILE, m)
    row = pl.BlockSpec((tm, D_MODEL), lambda i: (i, 0))
    return pl.pallas_call(
        _sconv_sample_kernel,
        grid=(m // tm,),
        in_specs=[row, row] + [_spec_of(a) for a in params],
        out_specs=[row, row],
        out_shape=[jax.ShapeDtypeStruct((m, D_MODEL), F32)] * 2,
        compiler_params=pltpu.CompilerParams(
            dimension_semantics=("arbitrary",),
            vmem_limit_bytes=_vmem_limit(48 * 1024 * 1024)),
        name="sconv_sample",
    )(x, prev, *[_arg_of(a) for a in params])


def _rope_tables(pos):
    half = HEAD_DIM // 2
    inv = ROPE_THETA ** (-jnp.arange(half, dtype=F32) / half)
    ang = pos.astype(F32)[:, None] * inv[None, :]
    cos = jnp.cos(ang)
    sin = jnp.sin(ang)
    reps = V7X_LANES // HEAD_DIM
    cos_t = jnp.tile(jnp.concatenate([cos, cos], axis=-1), (1, reps))
    sin_t = jnp.tile(jnp.concatenate([-sin, sin], axis=-1), (1, reps))
    return cos_t, sin_t


def _row(v):
    return v.reshape(1, -1)


def _pad_state_rows(buf):
    b, k, c = buf.shape
    padded = jnp.concatenate([jnp.zeros((b, V7X_SUBLANES - k, c), buf.dtype), buf], axis=1)
    return padded.reshape(b * V7X_SUBLANES, c)


def kernel(x_prompt, x_sample, cache_k, cache_v, state_rglru_h, state_rglru_conv, state_shortconv,
           norm_mixer, norm_ffn, norm_final,
           attn_w_qkv, attn_b_qkv, attn_w_o, attn_b_o, attn_sinks,
           rglru_w_gate, rglru_w_in, rglru_conv_w, rglru_conv_b, rglru_wa, rglru_ba,
           rglru_wx, rglru_bx, rglru_lambda, rglru_w_out,
           sconv_w_in, sconv_conv_w, sconv_w_out,
           ffn_w_gate, ffn_w_up, ffn_w_down):
    bp, seq, _ = x_prompt.shape
    bs, t_new, _ = x_sample.shape
    depth = norm_mixer.shape[0]
    past_len = 8192
    assert t_new == V7X_SUBLANES

    xp = x_prompt.reshape(bp * seq, D_MODEL)
    xs = x_sample.reshape(bs * t_new, D_MODEL)

    cos_p, sin_p = _rope_tables(jnp.arange(seq, dtype=jnp.int32))
    cos_s, sin_s = _rope_tables(past_len + jnp.arange(t_new, dtype=jnp.int32))
    cos_s = jnp.tile(cos_s, (SAMPLE_ATTN_BATCHES, 1))
    sin_s = jnp.tile(sin_s, (SAMPLE_ATTN_BATCHES, 1))

    kp_l, vp_l, ks_l, vs_l = [], [], [], []
    hp_l, hs_l, rcp_l, rcs_l = [], [], [], []
    scp_l, scs_l = [], []

    for i in range(depth):
        kind = i % 3
        j = i // 3
        g_mix = _row(norm_mixer[i])
        if kind == 0:
            wqkv = _layer(attn_w_qkv, j)
            wo = _layer(attn_w_o, j)
            bqkv = _row(attn_b_qkv[j])
            bo = _row(attn_b_o[j])
            sinks = attn_sinks[j]
            xp, kp, vp = _attn_prompt(xp, g_mix, wqkv, bqkv, wo, bo, cos_p, sin_p, sinks, bp, seq)
            ck = cache_k[j].reshape(bs, WINDOW, KV_DIM)
            cv = cache_v[j].reshape(bs, WINDOW, KV_DIM)
            xs, ks, vs = _attn_sample(xs, g_mix, wqkv, bqkv, wo, bo, cos_s, sin_s, sinks, ck, cv,
                                      bs, t_new)
            kp_l.append(kp.reshape(bp, WINDOW, N_KV_HEADS, HEAD_DIM))
            vp_l.append(vp.reshape(bp, WINDOW, N_KV_HEADS, HEAD_DIM))
            ks_l.append(ks.reshape(bs, WINDOW, N_KV_HEADS, HEAD_DIM))
            vs_l.append(vs.reshape(bs, WINDOW, N_KV_HEADS, HEAD_DIM))
        elif kind == 1:
            params = (g_mix, rglru_w_gate[j].astype(BF16), rglru_w_in[j].astype(BF16),
                      rglru_conv_w[j], _row(rglru_conv_b[j]), rglru_wa[j].astype(BF16),
                      _row(rglru_ba[j]), rglru_wx[j].astype(BF16), _row(rglru_bx[j]),
                      _row(rglru_lambda[j]), rglru_w_out[j].astype(BF16))
            xp, hlast, ulast = _rglru_prompt(xp, params, bp, seq)
            hp_l.append(hlast[:, V7X_SUBLANES - 1])
            rcp_l.append(ulast[:, V7X_SUBLANES - (RG_CONV_W - 1):])
            prev = _pad_state_rows(state_rglru_conv[j])
            hinit = jnp.repeat(state_rglru_h[j], t_new, axis=0)
            xs, hs_all, u_all = _rglru_sample(xs, prev, hinit, params, t_new)
            hs_l.append(hs_all.reshape(bs, t_new, D_MODEL)[:, t_new - 1])
            rcs_l.append(u_all.reshape(bs, t_new, D_MODEL)[:, t_new - (RG_CONV_W - 1):])
        else:
            params = (g_mix, _layer(sconv_w_in, j), sconv_conv_w[j], _layer(sconv_w_out, j))
            xp, vlast = _sconv_prompt(xp, params, bp, seq)
            scp_l.append(vlast[:, V7X_SUBLANES - (SCONV_W - 1):])
            prev = _pad_state_rows(state_shortconv[j])
            xs, v_all = _sconv_sample(xs, prev, params)
            scs_l.append(v_all.reshape(bs, t_new, D_MODEL)[:, t_new - (SCONV_W - 1):])

        wg = _layer(ffn_w_gate, i)
        wu = _layer(ffn_w_up, i)
        wd = _layer(ffn_w_down, i)
        d_ff = ffn_w_down.shape[1]
        g_ffn = _row(norm_ffn[i])
        g_fin = _row(norm_final)
        last = i == depth - 1
        xp = _ffn(xp, g_ffn, wg, wu, wd, d_ff, g_fin, last)
        xs = _ffn(xs, g_ffn, wg, wu, wd, d_ff, g_fin, last)

    return (xp.reshape(bp, seq, D_MODEL), xs.reshape(bs, t_new, D_MODEL),
            jnp.stack(kp_l), jnp.stack(vp_l), jnp.stack(ks_l), jnp.stack(vs_l),
            jnp.stack(hp_l), jnp.stack(hs_l), jnp.stack(rcp_l), jnp.stack(rcs_l),
            jnp.stack(scp_l), jnp.stack(scs_l))
```

```python
import functools

import jax
import jax.numpy as jnp
from jax import lax
from jax.experimental import pallas as pl
from jax.experimental.pallas import tpu as pltpu

D_MODEL = 1024
HEAD_DIM = 64
N_HEADS = 16
N_KV_HEADS = 2
GQA_GROUP = 8
Q_DIM = N_HEADS * HEAD_DIM
KV_DIM = N_KV_HEADS * HEAD_DIM
WINDOW = 128
ROPE_THETA = 10000.0
NEG_INF = -1e30
RG_BLOCKS = 4
RG_BLOCK_W = 256
RG_CONV_W = 4
RG_C = 8.0
SCONV_W = 3
EPS = 1e-6

V7X_LANES = 128
V7X_SUBLANES = 8
V7X_MXU_DIM = 256
V7X_VMEM_BYTES = 64 * 1024 * 1024

BF16 = jnp.bfloat16
F32 = jnp.float32

FFN_CHUNK = V7X_MXU_DIM
ROW_TILE = 512
RG_ROW_TILE = 256
RG_PROMPT_TILE = 512
RG_SEG_ARRAYS = 5
SAMPLE_ATTN_BATCHES = 32
SAMPLE_ATTN_UNROLL = 8
ATTN_LOOKAHEAD = 2
PROMPT_ATTN_COLS = 2
COLS_PER_KV = (N_HEADS // N_KV_HEADS) * HEAD_DIM // V7X_LANES


def _vmem_limit(nbytes):
    return int(min(nbytes, V7X_VMEM_BYTES - 8 * 1024 * 1024))


def _const_spec(shape):
    nd = len(shape)
    return pl.BlockSpec(shape, lambda *_: (0,) * nd, pipeline_mode=pl.Buffered(1))


def _layer(stacked, layer):
    return (stacked, layer)


def _spec_of(w):
    if isinstance(w, tuple):
        stacked, layer = w
        nd = stacked.ndim - 1
        return pl.BlockSpec((None,) + stacked.shape[1:], lambda *_: (layer,) + (0,) * nd,
                            pipeline_mode=pl.Buffered(1))
    return _const_spec(w.shape)


def _arg_of(w):
    return w[0] if isinstance(w, tuple) else w


def _nbytes_of(w):
    if isinstance(w, tuple):
        return w[0][0].size * w[0].dtype.itemsize
    return w.size * w.dtype.itemsize


def _rms(x, g):
    ms = jnp.mean(x * x, axis=-1, keepdims=True)
    return x * lax.rsqrt(ms + EPS) * g


def _mm(a, w):
    return jnp.dot(a.astype(BF16), w.astype(BF16), preferred_element_type=F32)


def _ffn_kernel(x_ref, g_ref, wg_ref, wu_ref, wd_ref, gf_ref, o_ref, *, n_chunks, final_norm):
    x = x_ref[...]
    h = _rms(x, g_ref[...]).astype(BF16)
    acc = x
    for c in range(n_chunks):
        cs = slice(c * FFN_CHUNK, (c + 1) * FFN_CHUNK)
        gate = _mm(h, wg_ref[:, cs])
        up = _mm(h, wu_ref[:, cs])
        act = (gate * jax.nn.sigmoid(gate)) * up
        acc = acc + _mm(act, wd_ref[cs, :])
    if final_norm:
        acc = _rms(acc, gf_ref[...])
    o_ref[...] = acc


def _ffn(x, g, wg, wu, wd, d_ff, gf, final_norm):
    m = x.shape[0]
    tm = min(ROW_TILE, m)
    n_chunks = d_ff // FFN_CHUNK
    row = pl.BlockSpec((tm, D_MODEL), lambda i: (i, 0))
    weight_bytes = _nbytes_of(wg) + _nbytes_of(wu) + _nbytes_of(wd)
    return pl.pallas_call(
        functools.partial(_ffn_kernel, n_chunks=n_chunks, final_norm=final_norm),
        grid=(m // tm,),
        in_specs=[row, _const_spec((1, D_MODEL)), _spec_of(wg), _spec_of(wu), _spec_of(wd),
                  _const_spec((1, D_MODEL))],
        out_specs=row,
        out_shape=jax.ShapeDtypeStruct((m, D_MODEL), F32),
        compiler_params=pltpu.CompilerParams(
            dimension_semantics=("arbitrary",),
            vmem_limit_bytes=_vmem_limit(weight_bytes + 24 * tm * D_MODEL * 4)),
        name="ffn",
    )(x, g, _arg_of(wg), _arg_of(wu), _arg_of(wd), gf)


def _rope_cols(cols, cos, sin_signed, first_half):
    swapped = jnp.where(first_half,
                        pltpu.roll(cols, V7X_LANES - HEAD_DIM // 2, 1),
                        pltpu.roll(cols, HEAD_DIM // 2, 1))
    return cols * cos + swapped * sin_signed


def _pair_blockdiag(mat, mat_rolled, kv, lane_lt64):
    zero = jnp.zeros_like(mat)
    if kv == 0:
        top = jnp.where(lane_lt64, mat, zero)
        bottom = jnp.where(lane_lt64, zero, mat_rolled)
    else:
        top = jnp.where(lane_lt64, mat_rolled, zero)
        bottom = jnp.where(lane_lt64, zero, mat)
    return jnp.concatenate([top, bottom], axis=0).astype(BF16)


def _key_blockdiag(k2, kv):
    lane_lt64 = lax.broadcasted_iota(jnp.int32, k2.shape, 1) < HEAD_DIM
    return _pair_blockdiag(k2, pltpu.roll(k2, HEAD_DIM, 1), kv, lane_lt64)


def _value_blockdiag(v2, kv):
    v2 = jnp.where(lax.broadcasted_iota(jnp.int32, v2.shape, 0) == 0, 0.0, v2)
    return _key_blockdiag(v2, kv)


def _attn_scores(qs, kbd):
    return lax.dot_general(qs, kbd, (((1,), (1,)), ((), ())), preferred_element_type=F32)


def _attn_probs(s_all, mask, sink_ref, kv, cols, tq):
    nk = 2 * WINDOW
    key = lax.broadcasted_iota(jnp.int32, (1, nk), 1)
    p_rows = []
    for i, c in enumerate(cols):
        p_halves = []
        for parity in range(2):
            sink = sink_ref[kv * GQA_GROUP + 2 * c + parity]
            fill = jnp.where(key == 0, sink, NEG_INF)
            s = s_all[i * tq:(i + 1) * tq, parity * nk:(parity + 1) * nk]
            s = jnp.where(mask, s, fill)
            m = jnp.max(s, axis=-1, keepdims=True)
            e = jnp.exp(s - m)
            den = jnp.sum(e, axis=-1, keepdims=True)
            p_halves.append(e * (1.0 / den))
        p_rows.append(jnp.concatenate(p_halves, axis=1))
    return jnp.concatenate(p_rows, axis=0).astype(BF16)


def _attend_units(units, sink_ref, tq):
    n = len(units)
    scores, kbd, vbd = {}, {}, {}

    def issue_scores(i):
        load_q, load_k, _, _, kv, _, _, kv_id = units[i]
        if kv_id not in kbd:
            kbd[kv_id] = _key_blockdiag(load_k(), kv)
        scores[i] = _attn_scores(load_q(), kbd[kv_id])

    for i in range(min(ATTN_LOOKAHEAD, n)):
        issue_scores(i)
    for i in range(n):
        if i + ATTN_LOOKAHEAD < n:
            issue_scores(i + ATTN_LOOKAHEAD)
        _, _, load_v, mask, kv, cols, store_o, kv_id = units[i]
        if kv_id not in vbd:
            vbd[kv_id] = _value_blockdiag(load_v(), kv)
        p_all = _attn_probs(scores.pop(i), mask, sink_ref, kv, cols, tq)
        store_o(jnp.dot(p_all, vbd[kv_id], preferred_element_type=F32))


def _band_mask(tq, col_min):
    row = lax.broadcasted_iota(jnp.int32, (tq, 2 * WINDOW), 0)
    col = lax.broadcasted_iota(jnp.int32, (tq, 2 * WINDOW), 1)
    prev_ok = (col < WINDOW) & (col > row)
    own_ok = (col >= WINDOW) & (col - WINDOW <= row)
    return (prev_ok | own_ok) & (col >= col_min)


def _attn_prompt_kernel(x_ref, g_ref, wqkv_ref, bqkv_ref, wo_ref, bo_ref, cos_ref, sin_ref,
                        sink_ref, y_ref, knew_ref, vnew_ref,
                        q_scr, k_scr, v_scr, a_scr, *, tq_tile, n_tiles):
    j = pl.program_id(1)
    n_blk = tq_tile // WINDOW

    @pl.when(j == 0)
    def _():
        k_scr[0:WINDOW, :] = jnp.zeros((WINDOW, V7X_LANES), F32)
        v_scr[0:WINDOW, :] = jnp.zeros((WINDOW, V7X_LANES), F32)

    x = x_ref[...]
    h = _rms(x, g_ref[...])
    qkv = _mm(h, wqkv_ref[...]) + bqkv_ref[...]
    cos = cos_ref[...]
    sin = sin_ref[...]
    lane = lax.broadcasted_iota(jnp.int32, (tq_tile, V7X_LANES), 1)
    first_half = (lane % HEAD_DIM) < (HEAD_DIM // 2)
    scale = HEAD_DIM ** -0.5
    for c in range(Q_DIM // V7X_LANES):
        qc = _rope_cols(qkv[:, c * V7X_LANES:(c + 1) * V7X_LANES], cos, sin, first_half)
        q_scr[:, c * V7X_LANES:(c + 1) * V7X_LANES] = (qc * scale).astype(BF16)
    k_new = _rope_cols(qkv[:, Q_DIM:Q_DIM + KV_DIM], cos, sin, first_half)
    v_new = qkv[:, Q_DIM + KV_DIM:]
    k_scr[WINDOW:, :] = k_new
    v_scr[WINDOW:, :] = v_new

    def unit(r0, mask, kv, cols):
        def load_q():
            return jnp.concatenate(
                [q_scr[pl.ds(r0, WINDOW),
                       (kv * COLS_PER_KV + c) * V7X_LANES:(kv * COLS_PER_KV + c + 1) * V7X_LANES]
                 for c in cols], axis=0)

        def store_o(o):
            for i, c in enumerate(cols):
                col = kv * COLS_PER_KV + c
                a_scr[pl.ds(r0, WINDOW), col * V7X_LANES:(col + 1) * V7X_LANES] = (
                    o[i * WINDOW:(i + 1) * WINDOW].astype(BF16))

        return (load_q, lambda: k_scr[pl.ds(r0, 2 * WINDOW), :],
                lambda: v_scr[pl.ds(r0, 2 * WINDOW), :], mask, kv, cols, store_o, kv)

    col_groups = [tuple(range(c, c + PROMPT_ATTN_COLS))
                  for c in range(0, COLS_PER_KV, PROMPT_ATTN_COLS)]

    def block(blk, carry):
        r0 = pl.multiple_of(blk * WINDOW, WINDOW)
        first = jnp.logical_and(j == 0, blk == 0)
        mask = _band_mask(WINDOW, jnp.where(first, WINDOW, 0))
        _attend_units([unit(r0, mask, kv, cols) for kv in range(N_KV_HEADS)
                       for cols in col_groups], sink_ref, WINDOW)
        return carry

    lax.fori_loop(0, n_blk, block, 0)

    y_ref[...] = _mm(a_scr[...], wo_ref[...]) + bo_ref[...] + x

    k_scr[0:WINDOW, :] = k_new[tq_tile - WINDOW:, :]
    v_scr[0:WINDOW, :] = v_new[tq_tile - WINDOW:, :]

    @pl.when(j == n_tiles - 1)
    def _():
        knew_ref[0] = k_new[tq_tile - WINDOW:, :]
        vnew_ref[0] = v_new[tq_tile - WINDOW:, :]


def _attn_prompt(x, g, wqkv, bqkv, wo, bo, cos_t, sin_t, sinks, batch, seq):
    tq = ROW_TILE
    nt = seq // tq
    row = pl.BlockSpec((tq, D_MODEL), lambda b, j: (b * nt + j, 0))
    tab = pl.BlockSpec((tq, V7X_LANES), lambda b, j: (j, 0))
    cache = pl.BlockSpec((1, WINDOW, V7X_LANES), lambda b, j: (b, 0, 0))
    return pl.pallas_call(
        functools.partial(_attn_prompt_kernel, tq_tile=tq, n_tiles=nt),
        grid=(batch, nt),
        in_specs=[row, _const_spec((1, D_MODEL)), _spec_of(wqkv),
                  _const_spec((1, Q_DIM + 2 * KV_DIM)), _spec_of(wo), _const_spec((1, D_MODEL)),
                  tab, tab, pl.BlockSpec(memory_space=pltpu.SMEM)],
        out_specs=[row, cache, cache],
        out_shape=[jax.ShapeDtypeStruct((batch * seq, D_MODEL), F32),
                   jax.ShapeDtypeStruct((batch, WINDOW, V7X_LANES), F32),
                   jax.ShapeDtypeStruct((batch, WINDOW, V7X_LANES), F32)],
        scratch_shapes=[pltpu.VMEM((tq, Q_DIM), BF16),
                        pltpu.VMEM((WINDOW + tq, V7X_LANES), F32),
                        pltpu.VMEM((WINDOW + tq, V7X_LANES), F32),
                        pltpu.VMEM((tq, Q_DIM), BF16)],
        compiler_params=pltpu.CompilerParams(
            dimension_semantics=("arbitrary", "arbitrary"),
            vmem_limit_bytes=_vmem_limit(48 * 1024 * 1024)),
        name="attn_prompt",
    )(x, g, _arg_of(wqkv), bqkv, _arg_of(wo), bo, cos_t, sin_t, sinks)


def _attn_sample_kernel(x_ref, g_ref, wqkv_ref, bqkv_ref, wo_ref, bo_ref, cos_ref, sin_ref,
                        sink_ref, ck_ref, cv_ref, y_ref, nk_ref, nv_ref,
                        q_scr, k_scr, v_scr, a_scr, *, n_seq, t_new):
    tm = n_seq * t_new
    x = x_ref[...]
    h = _rms(x, g_ref[...])
    qkv = _mm(h, wqkv_ref[...]) + bqkv_ref[...]
    cos = cos_ref[...]
    sin = sin_ref[...]
    lane = lax.broadcasted_iota(jnp.int32, (tm, V7X_LANES), 1)
    first_half = (lane % HEAD_DIM) < (HEAD_DIM // 2)
    scale = HEAD_DIM ** -0.5
    for c in range(Q_DIM // V7X_LANES):
        qc = _rope_cols(qkv[:, c * V7X_LANES:(c + 1) * V7X_LANES], cos, sin, first_half)
        q_scr[:, c * V7X_LANES:(c + 1) * V7X_LANES] = qc * scale
    k_scr[...] = _rope_cols(qkv[:, Q_DIM:Q_DIM + KV_DIM], cos, sin, first_half)
    v_scr[...] = qkv[:, Q_DIM + KV_DIM:]
    mask = _band_mask(t_new, 0)
    pad = jnp.zeros((WINDOW - t_new, V7X_LANES), F32)

    def unit(u, b, kv):
        r0 = pl.multiple_of(b * t_new, t_new)

        def load_q():
            return jnp.concatenate(
                [q_scr[pl.ds(r0, t_new),
                       (kv * COLS_PER_KV + c) * V7X_LANES:(kv * COLS_PER_KV + c + 1) * V7X_LANES]
                 for c in range(COLS_PER_KV)], axis=0).astype(BF16)

        def load_k():
            return jnp.concatenate([ck_ref[b], k_scr[pl.ds(r0, t_new), :], pad], axis=0)

        def load_v():
            return jnp.concatenate([cv_ref[b], v_scr[pl.ds(r0, t_new), :], pad], axis=0)

        def store_o(o):
            for c in range(COLS_PER_KV):
                col = kv * COLS_PER_KV + c
                a_scr[pl.ds(r0, t_new), col * V7X_LANES:(col + 1) * V7X_LANES] = (
                    o[c * t_new:(c + 1) * t_new])

        return (load_q, load_k, load_v, mask, kv, tuple(range(COLS_PER_KV)), store_o, (u, kv))

    def seq_group(i, carry):
        seqs = [i * SAMPLE_ATTN_UNROLL + u for u in range(SAMPLE_ATTN_UNROLL)]
        _attend_units([unit(u, b, kv) for u, b in enumerate(seqs) for kv in range(N_KV_HEADS)],
                      sink_ref, t_new)
        for b in seqs:
            r0 = pl.multiple_of(b * t_new, t_new)
            nk_ref[b] = jnp.concatenate([ck_ref[b, t_new:, :], k_scr[pl.ds(r0, t_new), :]], axis=0)
            nv_ref[b] = jnp.concatenate([cv_ref[b, t_new:, :], v_scr[pl.ds(r0, t_new), :]], axis=0)
        return carry

    lax.fori_loop(0, n_seq // SAMPLE_ATTN_UNROLL, seq_group, 0)
    y_ref[...] = _mm(a_scr[...], wo_ref[...]) + bo_ref[...] + x


def _attn_sample(x, g, wqkv, bqkv, wo, bo, cos_t, sin_t, sinks, ck, cv, n_batch, t_new):
    bg = SAMPLE_ATTN_BATCHES
    tm = bg * t_new
    row = pl.BlockSpec((tm, D_MODEL), lambda i: (i, 0))
    cache = pl.BlockSpec((bg, WINDOW, V7X_LANES), lambda i: (i, 0, 0))
    return pl.pallas_call(
        functools.partial(_attn_sample_kernel, n_seq=bg, t_new=t_new),
        grid=(n_batch // bg,),
        in_specs=[row, _const_spec((1, D_MODEL)), _spec_of(wqkv),
                  _const_spec((1, Q_DIM + 2 * KV_DIM)), _spec_of(wo), _const_spec((1, D_MODEL)),
                  _const_spec((tm, V7X_LANES)), _const_spec((tm, V7X_LANES)),
                  pl.BlockSpec(memory_space=pltpu.SMEM), cache, cache],
        out_specs=[row, cache, cache],
        out_shape=[jax.ShapeDtypeStruct((n_batch * t_new, D_MODEL), F32),
                   jax.ShapeDtypeStruct((n_batch, WINDOW, V7X_LANES), F32),
                   jax.ShapeDtypeStruct((n_batch, WINDOW, V7X_LANES), F32)],
        scratch_shapes=[pltpu.VMEM((tm, Q_DIM), F32),
                        pltpu.VMEM((tm, V7X_LANES), F32),
                        pltpu.VMEM((tm, V7X_LANES), F32),
                        pltpu.VMEM((tm, Q_DIM), F32)],
        compiler_params=pltpu.CompilerParams(
            dimension_semantics=("arbitrary",),
            vmem_limit_bytes=_vmem_limit(48 * 1024 * 1024)),
        name="attn_sample",
    )(x, g, _arg_of(wqkv), bqkv, _arg_of(wo), bo, cos_t, sin_t, sinks, ck, cv)


def _dwconv_groups(u, prev, w_ref, col0, ncol):
    rows = u.shape[0]
    kw = w_ref.shape[0]
    sub = lax.broadcasted_iota(jnp.int32, u.shape, 0) % V7X_SUBLANES
    y = u * w_ref[kw - 1:kw, col0:col0 + ncol]
    for s in range(1, kw):
        from_prev = pltpu.roll(prev, (rows - V7X_SUBLANES + s) % rows, 0)
        from_self = pltpu.roll(u, s, 0)
        shifted = jnp.where(sub < s, from_prev, from_self)
        y = y + shifted * w_ref[kw - 1 - s:kw - s, col0:col0 + ncol]
    return y


def _dwconv_rows(u, carry8, w_ref, col0, ncol):
    kw = w_ref.shape[0]
    sub8 = lax.broadcasted_iota(jnp.int32, carry8.shape, 0)
    y = u * w_ref[kw - 1:kw, col0:col0 + ncol]
    for s in range(1, kw):
        rolled = pltpu.roll(u, s, 0)
        head = jnp.where(sub8 < s, pltpu.roll(carry8, s, 0), rolled[:V7X_SUBLANES])
        shifted = jnp.concatenate([head, rolled[V7X_SUBLANES:]], axis=0)
        y = y + shifted * w_ref[kw - 1 - s:kw - s, col0:col0 + ncol]
    return y


def _scan_groups(a, b):
    pos = lax.broadcasted_iota(jnp.int32, a.shape, 0) % V7X_SUBLANES
    shift = 1
    while shift < V7X_SUBLANES:
        ok = pos >= shift
        a_sh = jnp.where(ok, pltpu.roll(a, shift, 0), 1.0)
        b_sh = jnp.where(ok, pltpu.roll(b, shift, 0), 0.0)
        b = a * b_sh + b
        a = a * a_sh
        shift *= 2
    return a, b


def _scan_rows(a, b, h_row):
    a_grp, b_grp = _scan_groups(a, b)
    out = []
    h = h_row
    for g in range(a.shape[0] // V7X_SUBLANES):
        rows = slice(g * V7X_SUBLANES, (g + 1) * V7X_SUBLANES)
        hs = a_grp[rows] * h + b_grp[rows]
        out.append(hs)
        h = hs[V7X_SUBLANES - 1:, :]
    return jnp.concatenate(out, axis=0)


def _log_sigmoid(x):
    return jnp.minimum(x, 0.0) - jnp.log1p(jnp.exp(-jnp.abs(x)))


def _sigmoid(x):
    return 0.5 * jnp.tanh(0.5 * x) + 0.5


def _rglru_body(x, g_ref, wgate_ref, win_ref, cw_ref, cb_ref, wa_ref, ba_ref, wx_ref, bx_ref,
                lam_ref, wout_ref, conv_of, scan_of, u_sink, h_sink):
    h = _rms(x, g_ref[...]).astype(BF16)
    acc = x
    for n in range(RG_BLOCKS):
        c0 = n * RG_BLOCK_W
        cs = slice(c0, c0 + RG_BLOCK_W)
        gate = jax.nn.gelu(jnp.dot(h, wgate_ref[:, cs], preferred_element_type=F32))
        u0 = jnp.dot(h, win_ref[:, cs], preferred_element_type=F32)
        u_sink(n, u0)
        u = conv_of(n, u0) + cb_ref[:, cs]
        ub = u.astype(BF16)
        r = jax.nn.sigmoid(jnp.dot(ub, wa_ref[n], preferred_element_type=F32) + ba_ref[:, cs])
        ig = jax.nn.sigmoid(jnp.dot(ub, wx_ref[n], preferred_element_type=F32) + bx_ref[:, cs])
        log_a = RG_C * r * _log_sigmoid(lam_ref[:, cs])
        a = jnp.exp(log_a)
        mult = jnp.sqrt(-jnp.tanh(log_a) * (1.0 + a * a))
        hs = scan_of(n, a, mult * (ig * u))
        h_sink(n, hs)
        acc = acc + jnp.dot((hs * gate).astype(BF16), wout_ref[cs, :], preferred_element_type=F32)
    return acc


def _seg_pitch(seg_len):
    return seg_len + V7X_SUBLANES if (seg_len // V7X_SUBLANES) % 2 == 0 else seg_len


def _rows_to_segments(scr, slab0, x, seg_len):
    pitch = _seg_pitch(seg_len)
    for s in range(x.shape[1] // V7X_LANES):
        for i in range(V7X_SUBLANES):
            scr[slab0 + s, i * pitch:i * pitch + seg_len, :] = (
                x[i * seg_len:(i + 1) * seg_len, s * V7X_LANES:(s + 1) * V7X_LANES])


def _segments_to_rows(scr, slab0, n_slabs, seg_len):
    pitch = _seg_pitch(seg_len)
    return jnp.concatenate(
        [jnp.concatenate([scr[slab0 + s, i * pitch:i * pitch + seg_len, :]
                          for i in range(V7X_SUBLANES)], axis=0)
         for s in range(n_slabs)], axis=1)


def _seg_step(scr, slab, k, seg_len):
    return scr[slab, pl.ds(k, V7X_SUBLANES, stride=_seg_pitch(seg_len)), :]


def _seg_step_store(scr, slab, k, seg_len, v):
    scr[slab, pl.ds(k, V7X_SUBLANES, stride=_seg_pitch(seg_len)), :] = v


def _from_prev_segment(v, first):
    sub = lax.broadcasted_iota(jnp.int32, v.shape, 0)
    return jnp.where(sub == 0, first, pltpu.roll(v, 1, 0))


def _rglru_prompt_kernel(x_ref, g_ref, wgate_ref, win_ref, cw_ref, cb_ref, wa_ref, ba_ref,
                         wx_ref, bx_ref, lam_ref, wout_ref, y_ref, hlast_ref, ulast_ref,
                         ucarry, hcarry, *seg_scr, tm, sub_rows):
    j = pl.program_id(1)
    slabs_per_chunk = RG_BLOCK_W // V7X_LANES
    kw = cw_ref.shape[0]
    n_sub = tm // sub_rows
    u0_scr, u_scr, r_scr, ig_scr, hs_scr = (seg_scr[i * n_sub:(i + 1) * n_sub]
                                            for i in range(RG_SEG_ARRAYS))

    @pl.when(j == 0)
    def _():
        ucarry[...] = jnp.zeros(ucarry.shape, F32)
        hcarry[...] = jnp.zeros(hcarry.shape, F32)

    n_slabs = D_MODEL // V7X_LANES
    seg_len = sub_rows // V7X_SUBLANES
    gates, conv_prev, h_prev = {}, {}, {}

    def project_in(sub):
        x = x_ref[sub * sub_rows:(sub + 1) * sub_rows, :]
        h = _rms(x, g_ref[...]).astype(BF16)
        gates[sub] = jax.nn.gelu(jnp.dot(h, wgate_ref[...], preferred_element_type=F32))
        u0 = jnp.dot(h, win_ref[...], preferred_element_type=F32)
        conv_prev[sub + 1] = u0[sub_rows - V7X_SUBLANES:, :]
        _rows_to_segments(u0_scr[sub], 0, u0, seg_len)

    def conv(sub):
        for slab in range(n_slabs):
            lanes = slice(slab * V7X_LANES, (slab + 1) * V7X_LANES)
            taps = [cw_ref[t:t + 1, lanes] for t in range(kw)]
            bias = cb_ref[:, lanes]
            steps = {k: _seg_step(u0_scr[sub], slab, k, seg_len) for k in range(seg_len)}
            for back in range(1, kw):
                steps[-back] = _from_prev_segment(
                    steps[seg_len - back],
                    conv_prev[sub][V7X_SUBLANES - back:V7X_SUBLANES - back + 1, lanes])
            for k in range(seg_len):
                u_k = steps[k] * taps[kw - 1] + bias
                for back in range(1, kw):
                    u_k = u_k + steps[k - back] * taps[kw - 1 - back]
                _seg_step_store(u_scr[sub], slab, k, seg_len, u_k)

    def project_gates(sub):
        for n in range(RG_BLOCKS):
            slab0 = n * slabs_per_chunk
            ub = _segments_to_rows(u_scr[sub], slab0, slabs_per_chunk, seg_len).astype(BF16)
            _rows_to_segments(r_scr[sub], slab0,
                              jnp.dot(ub, wa_ref[n], preferred_element_type=F32), seg_len)
            _rows_to_segments(ig_scr[sub], slab0,
                              jnp.dot(ub, wx_ref[n], preferred_element_type=F32), seg_len)

    def recur(sub):
        h_end = []
        for slab in range(n_slabs):
            lanes = slice(slab * V7X_LANES, (slab + 1) * V7X_LANES)
            log_a_scale = RG_C * _log_sigmoid(lam_ref[:, lanes])
            ba = ba_ref[:, lanes]
            bx = bx_ref[:, lanes]
            h_in = h_prev[sub][slab]
            a_cum, h_loc = [], []
            for k in range(seg_len):
                r = _sigmoid(_seg_step(r_scr[sub], slab, k, seg_len) + ba)
                ig = _sigmoid(_seg_step(ig_scr[sub], slab, k, seg_len) + bx)
                log_a = log_a_scale * r
                a = jnp.exp(log_a)
                mult = jnp.sqrt(-jnp.tanh(log_a) * (1.0 + a * a))
                b = mult * (ig * _seg_step(u_scr[sub], slab, k, seg_len))
                if k == 0:
                    a_cum.append(a)
                    h_loc.append(b)
                else:
                    a_cum.append(a * a_cum[-1])
                    h_loc.append(a * h_loc[-1] + b)
            a_seg, b_seg = _scan_groups(a_cum[-1], h_loc[-1])
            seg_end = a_seg * h_in + b_seg
            h_end.append(seg_end[V7X_SUBLANES - 1:, :])
            h_start = _from_prev_segment(seg_end, h_in)
            for k in range(seg_len):
                _seg_step_store(hs_scr[sub], slab, k, seg_len, a_cum[k] * h_start + h_loc[k])
        h_prev[sub + 1] = h_end

    def project_out(sub):
        hs = _segments_to_rows(hs_scr[sub], 0, n_slabs, seg_len)
        rows = slice(sub * sub_rows, (sub + 1) * sub_rows)
        y_ref[rows, :] = x_ref[rows, :] + jnp.dot(
            (hs * gates.pop(sub)).astype(BF16), wout_ref[...], preferred_element_type=F32)
        if sub == n_sub - 1:
            hlast_ref[0] = hs[sub_rows - V7X_SUBLANES:, :]

    conv_prev[0] = ucarry[...]
    h_prev[0] = [hcarry[V7X_SUBLANES - 1:V7X_SUBLANES, s * V7X_LANES:(s + 1) * V7X_LANES]
                 for s in range(n_slabs)]
    stages = [project_in, conv, project_gates, recur, project_out]
    for step in range(n_sub + len(stages) - 1):
        for sub in range(n_sub):
            stage = step - sub
            if 0 <= stage < len(stages):
                stages[stage](sub)

    ulast_ref[0] = conv_prev[n_sub]
    ucarry[...] = conv_prev[n_sub]
    hcarry[...] = hlast_ref[0]


def _rglru_sample_kernel(x_ref, prev_ref, hinit_ref, g_ref, wgate_ref, win_ref, cw_ref, cb_ref,
                         wa_ref, ba_ref, wx_ref, bx_ref, lam_ref, wout_ref, y_ref, hs_ref, u_ref,
                         *, t_new):
    def conv_of(n, u0):
        c0 = n * RG_BLOCK_W
        return _dwconv_groups(u0, prev_ref[:, c0:c0 + RG_BLOCK_W], cw_ref, c0, RG_BLOCK_W)

    def scan_of(n, a, b):
        a_grp, b_grp = _scan_groups(a, b)
        return a_grp * hinit_ref[:, n * RG_BLOCK_W:(n + 1) * RG_BLOCK_W] + b_grp

    def u_sink(n, u0):
        u_ref[:, n * RG_BLOCK_W:(n + 1) * RG_BLOCK_W] = u0

    def h_sink(n, hs):
        hs_ref[:, n * RG_BLOCK_W:(n + 1) * RG_BLOCK_W] = hs

    y_ref[...] = _rglru_body(x_ref[...], g_ref, wgate_ref, win_ref, cw_ref, cb_ref, wa_ref, ba_ref,
                             wx_ref, bx_ref, lam_ref, wout_ref, conv_of, scan_of, u_sink, h_sink)


def _rglru_weight_specs(p):
    return [_const_spec(a.shape) for a in p]


def _rglru_prompt(x, params, batch, seq):
    tm = RG_PROMPT_TILE
    sub_rows = RG_ROW_TILE
    nt = seq // tm
    row = pl.BlockSpec((tm, D_MODEL), lambda b, j: (b * nt + j, 0))
    last = pl.BlockSpec((1, V7X_SUBLANES, D_MODEL), lambda b, j: (b, 0, 0))
    seg_scratch = pltpu.VMEM((D_MODEL // V7X_LANES,
                              V7X_SUBLANES * _seg_pitch(sub_rows // V7X_SUBLANES), V7X_LANES),
                             F32)
    return pl.pallas_call(
        functools.partial(_rglru_prompt_kernel, tm=tm, sub_rows=sub_rows),
        grid=(batch, nt),
        in_specs=[row] + _rglru_weight_specs(params),
        out_specs=[row, last, last],
        out_shape=[jax.ShapeDtypeStruct((batch * seq, D_MODEL), F32),
                   jax.ShapeDtypeStruct((batch, V7X_SUBLANES, D_MODEL), F32),
                   jax.ShapeDtypeStruct((batch, V7X_SUBLANES, D_MODEL), F32)],
        scratch_shapes=[pltpu.VMEM((V7X_SUBLANES, D_MODEL), F32),
                        pltpu.VMEM((V7X_SUBLANES, D_MODEL), F32)]
        + [seg_scratch] * (RG_SEG_ARRAYS * (tm // sub_rows)),
        compiler_params=pltpu.CompilerParams(
            dimension_semantics=("arbitrary", "arbitrary"),
            vmem_limit_bytes=_vmem_limit(48 * 1024 * 1024)),
        name="rglru_prompt",
    )(x, *params)


def _rglru_sample(x, prev, hinit, params, t_new):
    m = x.shape[0]
    tm = min(RG_ROW_TILE, m)
    row = pl.BlockSpec((tm, D_MODEL), lambda i: (i, 0))
    return pl.pallas_call(
        functools.partial(_rglru_sample_kernel, t_new=t_new),
        grid=(m // tm,),
        in_specs=[row, row, row] + _rglru_weight_specs(params),
        out_specs=[row, row, row],
        out_shape=[jax.ShapeDtypeStruct((m, D_MODEL), F32)] * 3,
        compiler_params=pltpu.CompilerParams(
            dimension_semantics=("arbitrary",),
            vmem_limit_bytes=_vmem_limit(48 * 1024 * 1024)),
        name="rglru_sample",
    )(x, prev, hinit, *params)


SCONV_CHUNK = V7X_MXU_DIM


def _sconv_body(x, g_ref, win_ref, cw_ref, wout_ref, conv_of, v_sink):
    h = _rms(x, g_ref[...]).astype(BF16)
    acc = x
    for n in range(D_MODEL // SCONV_CHUNK):
        c0 = n * SCONV_CHUNK
        bg = _mm(h, win_ref[:, c0:c0 + SCONV_CHUNK])
        cg = _mm(h, win_ref[:, D_MODEL + c0:D_MODEL + c0 + SCONV_CHUNK])
        xv = _mm(h, win_ref[:, 2 * D_MODEL + c0:2 * D_MODEL + c0 + SCONV_CHUNK])
        v = cg * xv
        v_sink(n, v)
        y = conv_of(n, v)
        acc = acc + _mm(bg * y, wout_ref[c0:c0 + SCONV_CHUNK, :])
    return acc


def _sconv_prompt_kernel(x_ref, g_ref, win_ref, cw_ref, wout_ref, y_ref, vlast_ref, vcarry,
                         *, tm):
    j = pl.program_id(1)

    @pl.when(j == 0)
    def _():
        vcarry[...] = jnp.zeros(vcarry.shape, F32)

    def conv_of(n, v):
        c0 = n * SCONV_CHUNK
        return _dwconv_rows(v, vcarry[:, c0:c0 + SCONV_CHUNK], cw_ref, c0, SCONV_CHUNK)

    def v_sink(n, v):
        vlast_ref[0, :, n * SCONV_CHUNK:(n + 1) * SCONV_CHUNK] = v[tm - V7X_SUBLANES:, :]

    y_ref[...] = _sconv_body(x_ref[...], g_ref, win_ref, cw_ref, wout_ref, conv_of, v_sink)
    vcarry[...] = vlast_ref[0]


def _sconv_sample_kernel(x_ref, prev_ref, g_ref, win_ref, cw_ref, wout_ref, y_ref, v_ref):
    def conv_of(n, v):
        c0 = n * SCONV_CHUNK
        return _dwconv_groups(v, prev_ref[:, c0:c0 + SCONV_CHUNK], cw_ref, c0, SCONV_CHUNK)

    def v_sink(n, v):
        v_ref[:, n * SCONV_CHUNK:(n + 1) * SCONV_CHUNK] = v

    y_ref[...] = _sconv_body(x_ref[...], g_ref, win_ref, cw_ref, wout_ref, conv_of, v_sink)


def _sconv_prompt(x, params, batch, seq):
    tm = ROW_TILE
    nt = seq // tm
    row = pl.BlockSpec((tm, D_MODEL), lambda b, j: (b * nt + j, 0))
    last = pl.BlockSpec((1, V7X_SUBLANES, D_MODEL), lambda b, j: (b, 0, 0))
    return pl.pallas_call(
        functools.partial(_sconv_prompt_kernel, tm=tm),
        grid=(batch, nt),
        in_specs=[row] + [_spec_of(a) for a in params],
        out_specs=[row, last],
        out_shape=[jax.ShapeDtypeStruct((batch * seq, D_MODEL), F32),
                   jax.ShapeDtypeStruct((batch, V7X_SUBLANES, D_MODEL), F32)],
        scratch_shapes=[pltpu.VMEM((V7X_SUBLANES, D_MODEL), F32)],
        compiler_params=pltpu.CompilerParams(
            dimension_semantics=("arbitrary", "arbitrary"),
            vmem_limit_bytes=_vmem_limit(48 * 1024 * 1024)),
        name="sconv_prompt",
    )(x, *[_arg_of(a) for a in params])


def _sconv_sample(x, prev, params):
    m = x.shape[0]
    tm = min(ROW_TILE, m)
    row = pl.BlockSpec((tm, D_MODEL), lambda i: (i, 0))
    return pl.pallas_call(
        _sconv_sample_kernel,
        grid=(m // tm,),
        in_specs=[row, row] + [_spec_of(a) for a in params],
        out_specs=[row, row],
        out_shape=[jax.ShapeDtypeStruct((m, D_MODEL), F32)] * 2,
        compiler_params=pltpu.CompilerParams(
            dimension_semantics=("arbitrary",),
            vmem_limit_bytes=_vmem_limit(48 * 1024 * 1024)),
        name="sconv_sample",
    )(x, prev, *[_arg_of(a) for a in params])


def _rope_tables(pos):
    half = HEAD_DIM // 2
    inv = ROPE_THETA ** (-jnp.arange(half, dtype=F32) / half)
    ang = pos.astype(F32)[:, None] * inv[None, :]
    cos = jnp.cos(ang)
    sin = jnp.sin(ang)
    reps = V7X_LANES // HEAD_DIM
    cos_t = jnp.tile(jnp.concatenate([cos, cos], axis=-1), (1, reps))
    sin_t = jnp.tile(jnp.concatenate([-sin, sin], axis=-1), (1, reps))
    return cos_t, sin_t


def _row(v):
    return v.reshape(1, -1)


def _pad_state_rows(buf):
    b, k, c = buf.shape
    padded = jnp.concatenate([jnp.zeros((b, V7X_SUBLANES - k, c), buf.dtype), buf], axis=1)
    return padded.reshape(b * V7X_SUBLANES, c)


def kernel(x_prompt, x_sample, cache_k, cache_v, state_rglru_h, state_rglru_conv, state_shortconv,
           norm_mixer, norm_ffn, norm_final,
           attn_w_qkv, attn_b_qkv, attn_w_o, attn_b_o, attn_sinks,
           rglru_w_gate, rglru_w_in, rglru_conv_w, rglru_conv_b, rglru_wa, rglru_ba,
           rglru_wx, rglru_bx, rglru_lambda, rglru_w_out,
           sconv_w_in, sconv_conv_w, sconv_w_out,
           ffn_w_gate, ffn_w_up, ffn_w_down):
    bp, seq, _ = x_prompt.shape
    bs, t_new, _ = x_sample.shape
    depth = norm_mixer.shape[0]
    past_len = 8192
    assert t_new == V7X_SUBLANES

    xp = x_prompt.reshape(bp * seq, D_MODEL)
    xs = x_sample.reshape(bs * t_new, D_MODEL)

    cos_p, sin_p = _rope_tables(jnp.arange(seq, dtype=jnp.int32))
    cos_s, sin_s = _rope_tables(past_len + jnp.arange(t_new, dtype=jnp.int32))
    cos_s = jnp.tile(cos_s, (SAMPLE_ATTN_BATCHES, 1))
    sin_s = jnp.tile(sin_s, (SAMPLE_ATTN_BATCHES, 1))

    kp_l, vp_l, ks_l, vs_l = [], [], [], []
    hp_l, hs_l, rcp_l, rcs_l = [], [], [], []
    scp_l, scs_l = [], []

    for i in range(depth):
        kind = i % 3
        j = i // 3
        g_mix = _row(norm_mixer[i])
        if kind == 0:
            wqkv = _layer(attn_w_qkv, j)
            wo = _layer(attn_w_o, j)
            bqkv = _row(attn_b_qkv[j])
            bo = _row(attn_b_o[j])
            sinks = attn_sinks[j]
            xp, kp, vp = _attn_prompt(xp, g_mix, wqkv, bqkv, wo, bo, cos_p, sin_p, sinks, bp, seq)
            ck = cache_k[j].reshape(bs, WINDOW, KV_DIM)
            cv = cache_v[j].reshape(bs, WINDOW, KV_DIM)
            xs, ks, vs = _attn_sample(xs, g_mix, wqkv, bqkv, wo, bo, cos_s, sin_s, sinks, ck, cv,
                                      bs, t_new)
            kp_l.append(kp.reshape(bp, WINDOW, N_KV_HEADS, HEAD_DIM))
            vp_l.append(vp.reshape(bp, WINDOW, N_KV_HEADS, HEAD_DIM))
            ks_l.append(ks.reshape(bs, WINDOW, N_KV_HEADS, HEAD_DIM))
            vs_l.append(vs.reshape(bs, WINDOW, N_KV_HEADS, HEAD_DIM))
        elif kind == 1:
            params = (g_mix, rglru_w_gate[j].astype(BF16), rglru_w_in[j].astype(BF16),
                      rglru_conv_w[j], _row(rglru_conv_b[j]), rglru_wa[j].astype(BF16),
                      _row(rglru_ba[j]), rglru_wx[j].astype(BF16), _row(rglru_bx[j]),
                      _row(rglru_lambda[j]), rglru_w_out[j].astype(BF16))
            xp, hlast, ulast = _rglru_prompt(xp, params, bp, seq)
            hp_l.append(hlast[:, V7X_SUBLANES - 1])
            rcp_l.append(ulast[:, V7X_SUBLANES - (RG_CONV_W - 1):])
            prev = _pad_state_rows(state_rglru_conv[j])
            hinit = jnp.repeat(state_rglru_h[j], t_new, axis=0)
            xs, hs_all, u_all = _rglru_sample(xs, prev, hinit, params, t_new)
            hs_l.append(hs_all.reshape(bs, t_new, D_MODEL)[:, t_new - 1])
            rcs_l.append(u_all.reshape(bs, t_new, D_MODEL)[:, t_new - (RG_CONV_W - 1):])
        else:
            params = (g_mix, _layer(sconv_w_in, j), sconv_conv_w[j], _layer(sconv_w_out, j))
            xp, vlast = _sconv_prompt(xp, params, bp, seq)
            scp_l.append(vlast[:, V7X_SUBLANES - (SCONV_W - 1):])
            prev = _pad_state_rows(state_shortconv[j])
            xs, v_all = _sconv_sample(xs, prev, params)
            scs_l.append(v_all.reshape(bs, t_new, D_MODEL)[:, t_new - (SCONV_W - 1):])

        wg = _layer(ffn_w_gate, i)
        wu = _layer(ffn_w_up, i)
        wd = _layer(ffn_w_down, i)
        d_ff = ffn_w_down.shape[1]
        g_ffn = _row(norm_ffn[i])
        g_fin = _row(norm_final)
        last = i == depth - 1
        xp = _ffn(xp, g_ffn, wg, wu, wd, d_ff, g_fin, last)
        xs = _ffn(xs, g_ffn, wg, wu, wd, d_ff, g_fin, last)

    return (xp.reshape(bp, seq, D_MODEL), xs.reshape(bs, t_new, D_MODEL),
            jnp.stack(kp_l), jnp.stack(vp_l), jnp.stack(ks_l), jnp.stack(vs_l),
            jnp.stack(hp_l), jnp.stack(hs_l), jnp.stack(rcp_l), jnp.stack(rcs_l),
            jnp.stack(scp_l), jnp.stack(scs_l))
```

```python
import functools

import jax
import jax.numpy as jnp
from jax import lax
from jax.experimental import pallas as pl
from jax.experimental.pallas import tpu as pltpu

D_MODEL = 1024
HEAD_DIM = 64
N_HEADS = 16
N_KV_HEADS = 2
GQA_GROUP = 8
Q_DIM = N_HEADS * HEAD_DIM
KV_DIM = N_KV_HEADS * HEAD_DIM
WINDOW = 128
ROPE_THETA = 10000.0
NEG_INF = -1e30
RG_BLOCKS = 4
RG_BLOCK_W = 256
RG_CONV_W = 4
RG_C = 8.0
SCONV_W = 3
EPS = 1e-6

V7X_LANES = 128
V7X_SUBLANES = 8
V7X_MXU_DIM = 256
V7X_VMEM_BYTES = 64 * 1024 * 1024

BF16 = jnp.bfloat16
F32 = jnp.float32

FFN_CHUNK = V7X_MXU_DIM
ROW_TILE = 512
RG_ROW_TILE = 256
RG_PROMPT_TILE = 512
RG_SEG_ARRAYS = 5
SAMPLE_ATTN_BATCHES = 32
SAMPLE_ATTN_UNROLL = 8
ATTN_LOOKAHEAD = 2
PROMPT_ATTN_COLS = 2
COLS_PER_KV = (N_HEADS // N_KV_HEADS) * HEAD_DIM // V7X_LANES


def _vmem_limit(nbytes):
    return int(min(nbytes, V7X_VMEM_BYTES - 8 * 1024 * 1024))


def _const_spec(shape):
    nd = len(shape)
    return pl.BlockSpec(shape, lambda *_: (0,) * nd, pipeline_mode=pl.Buffered(1))


def _layer(stacked, layer):
    return (stacked, layer)


def _spec_of(w):
    if isinstance(w, tuple):
        stacked, layer = w
        nd = stacked.ndim - 1
        return pl.BlockSpec((None,) + stacked.shape[1:], lambda *_: (layer,) + (0,) * nd,
                            pipeline_mode=pl.Buffered(1))
    return _const_spec(w.shape)


def _arg_of(w):
    return w[0] if isinstance(w, tuple) else w


def _nbytes_of(w):
    if isinstance(w, tuple):
        return w[0][0].size * w[0].dtype.itemsize
    return w.size * w.dtype.itemsize


def _rms(x, g):
    ms = jnp.mean(x * x, axis=-1, keepdims=True)
    return x * lax.rsqrt(ms + EPS) * g


def _mm(a, w):
    return jnp.dot(a.astype(BF16), w.astype(BF16), preferred_element_type=F32)


def _ffn_kernel(*refs, n_chunks, final_norm, tiles_a, stacked):
    if stacked:
        xa_ref, xb_ref, g_ref, wg_ref, wu_ref, wd_ref, gf_ref, o_ref = refs
        x = jnp.where(pl.program_id(0) < tiles_a, xa_ref[...], xb_ref[...])
    else:
        xa_ref, g_ref, wg_ref, wu_ref, wd_ref, gf_ref, o_ref = refs
        x = xa_ref[...]
    h = _rms(x, g_ref[...]).astype(BF16)
    acc = x
    for c in range(n_chunks):
        cs = slice(c * FFN_CHUNK, (c + 1) * FFN_CHUNK)
        gate = _mm(h, wg_ref[:, cs])
        up = _mm(h, wu_ref[:, cs])
        act = (gate * jax.nn.sigmoid(gate)) * up
        acc = acc + _mm(act, wd_ref[cs, :])
    if final_norm:
        acc = _rms(acc, gf_ref[...])
    o_ref[...] = acc


def _ffn(xa, rows_a, xb, g, wg, wu, wd, d_ff, gf, final_norm):
    tm = ROW_TILE
    tiles_a = rows_a // tm
    stacked = xb is not None
    tiles_b = xb.shape[0] // tm if stacked else 0
    n_chunks = d_ff // FFN_CHUNK
    out_row = pl.BlockSpec((tm, D_MODEL), lambda i: (i, 0))
    if stacked:
        x_specs = [pl.BlockSpec((tm, D_MODEL), lambda i: (jnp.minimum(i, tiles_a - 1), 0)),
                   pl.BlockSpec((tm, D_MODEL), lambda i: (jnp.maximum(i - tiles_a, 0), 0),
                                pipeline_mode=pl.Buffered(1))]
        x_args = [xa, xb]
    else:
        x_specs = [out_row]
        x_args = [xa]
    weight_bytes = _nbytes_of(wg) + _nbytes_of(wu) + _nbytes_of(wd)
    return pl.pallas_call(
        functools.partial(_ffn_kernel, n_chunks=n_chunks, final_norm=final_norm,
                          tiles_a=tiles_a, stacked=stacked),
        grid=(tiles_a + tiles_b,),
        in_specs=x_specs + [_const_spec((1, D_MODEL)), _spec_of(wg), _spec_of(wu), _spec_of(wd),
                            _const_spec((1, D_MODEL))],
        out_specs=out_row,
        out_shape=jax.ShapeDtypeStruct(((tiles_a + tiles_b) * tm, D_MODEL), F32),
        compiler_params=pltpu.CompilerParams(
            dimension_semantics=("arbitrary",),
            vmem_limit_bytes=_vmem_limit(weight_bytes + 24 * tm * D_MODEL * 4)),
        name="ffn",
    )(*x_args, g, _arg_of(wg), _arg_of(wu), _arg_of(wd), gf)


def _rope_cols(cols, cos, sin_signed, first_half):
    swapped = jnp.where(first_half,
                        pltpu.roll(cols, V7X_LANES - HEAD_DIM // 2, 1),
                        pltpu.roll(cols, HEAD_DIM // 2, 1))
    return cols * cos + swapped * sin_signed


def _pair_blockdiag(mat, mat_rolled, kv, lane_lt64):
    zero = jnp.zeros_like(mat)
    if kv == 0:
        top = jnp.where(lane_lt64, mat, zero)
        bottom = jnp.where(lane_lt64, zero, mat_rolled)
    else:
        top = jnp.where(lane_lt64, mat_rolled, zero)
        bottom = jnp.where(lane_lt64, zero, mat)
    return jnp.concatenate([top, bottom], axis=0).astype(BF16)


def _key_blockdiag(k2, kv):
    lane_lt64 = lax.broadcasted_iota(jnp.int32, k2.shape, 1) < HEAD_DIM
    return _pair_blockdiag(k2, pltpu.roll(k2, HEAD_DIM, 1), kv, lane_lt64)


def _value_blockdiag(v2, kv):
    v2 = jnp.where(lax.broadcasted_iota(jnp.int32, v2.shape, 0) == 0, 0.0, v2)
    vbd = _key_blockdiag(v2, kv)
    row = lax.broadcasted_iota(jnp.int32, vbd.shape, 0)
    lane = lax.broadcasted_iota(jnp.int32, vbd.shape, 1)
    ones_bd = jnp.where((lane < HEAD_DIM) == (row < 2 * WINDOW), 1.0, 0.0).astype(BF16)
    return jnp.concatenate([vbd, ones_bd], axis=1)


def _attn_scores(qs, kbd):
    return lax.dot_general(qs, kbd, (((1,), (1,)), ((), ())), preferred_element_type=F32)


def _attn_weights(s_all, mask, sink_ref, kv, cols, tq):
    nk = 2 * WINDOW
    key = lax.broadcasted_iota(jnp.int32, (1, nk), 1)
    e_rows = []
    for i, c in enumerate(cols):
        e_halves = []
        for parity in range(2):
            sink = sink_ref[kv * GQA_GROUP + 2 * c + parity]
            fill = jnp.where(key == 0, sink, NEG_INF)
            s = s_all[i * tq:(i + 1) * tq, parity * nk:(parity + 1) * nk]
            s = jnp.where(mask, s, fill)
            m = jnp.max(s, axis=-1, keepdims=True)
            e_halves.append(jnp.exp(s - m))
        e_rows.append(jnp.concatenate(e_halves, axis=1))
    return jnp.concatenate(e_rows, axis=0).astype(BF16)


def _attend_units(units, sink_ref, tq):
    n = len(units)
    scores, kbd, vbd = {}, {}, {}

    def issue_scores(i):
        load_q, load_k, _, _, kv, _, _, kv_id = units[i]
        if kv_id not in kbd:
            kbd[kv_id] = _key_blockdiag(load_k(), kv)
        scores[i] = _attn_scores(load_q(), kbd[kv_id])

    for i in range(min(ATTN_LOOKAHEAD, n)):
        issue_scores(i)
    for i in range(n):
        if i + ATTN_LOOKAHEAD < n:
            issue_scores(i + ATTN_LOOKAHEAD)
        _, _, load_v, mask, kv, cols, store_o, kv_id = units[i]
        if kv_id not in vbd:
            vbd[kv_id] = _value_blockdiag(load_v(), kv)
        e_all = _attn_weights(scores.pop(i), mask, sink_ref, kv, cols, tq)
        o_den = jnp.dot(e_all, vbd[kv_id], preferred_element_type=F32)
        store_o(o_den[:, :V7X_LANES] * (1.0 / o_den[:, V7X_LANES:]))


def _band_mask(tq, col_min):
    row = lax.broadcasted_iota(jnp.int32, (tq, 2 * WINDOW), 0)
    col = lax.broadcasted_iota(jnp.int32, (tq, 2 * WINDOW), 1)
    prev_ok = (col < WINDOW) & (col > row)
    own_ok = (col >= WINDOW) & (col - WINDOW <= row)
    return (prev_ok | own_ok) & (col >= col_min)


def _attn_prompt_kernel(x_ref, g_ref, wqkv_ref, bqkv_ref, wo_ref, bo_ref, cos_ref, sin_ref,
                        sink_ref, y_ref, knew_ref, vnew_ref,
                        q_scr, k_scr, v_scr, a_scr, *, tq_tile, n_tiles):
    j = pl.program_id(1)
    n_blk = tq_tile // WINDOW

    @pl.when(j == 0)
    def _():
        k_scr[0:WINDOW, :] = jnp.zeros((WINDOW, V7X_LANES), F32)
        v_scr[0:WINDOW, :] = jnp.zeros((WINDOW, V7X_LANES), F32)

    x = x_ref[...]
    h = _rms(x, g_ref[...])
    qkv = _mm(h, wqkv_ref[...]) + bqkv_ref[...]
    cos = cos_ref[...]
    sin = sin_ref[...]
    lane = lax.broadcasted_iota(jnp.int32, (tq_tile, V7X_LANES), 1)
    first_half = (lane % HEAD_DIM) < (HEAD_DIM // 2)
    scale = HEAD_DIM ** -0.5
    for c in range(Q_DIM // V7X_LANES):
        qc = _rope_cols(qkv[:, c * V7X_LANES:(c + 1) * V7X_LANES], cos, sin, first_half)
        q_scr[:, c * V7X_LANES:(c + 1) * V7X_LANES] = (qc * scale).astype(BF16)
    k_new = _rope_cols(qkv[:, Q_DIM:Q_DIM + KV_DIM], cos, sin, first_half)
    v_new = qkv[:, Q_DIM + KV_DIM:]
    k_scr[WINDOW:, :] = k_new
    v_scr[WINDOW:, :] = v_new

    def unit(r0, mask, kv, cols):
        def load_q():
            return jnp.concatenate(
                [q_scr[pl.ds(r0, WINDOW),
                       (kv * COLS_PER_KV + c) * V7X_LANES:(kv * COLS_PER_KV + c + 1) * V7X_LANES]
                 for c in cols], axis=0)

        def store_o(o):
            for i, c in enumerate(cols):
                col = kv * COLS_PER_KV + c
                a_scr[pl.ds(r0, WINDOW), col * V7X_LANES:(col + 1) * V7X_LANES] = (
                    o[i * WINDOW:(i + 1) * WINDOW].astype(BF16))

        return (load_q, lambda: k_scr[pl.ds(r0, 2 * WINDOW), :],
                lambda: v_scr[pl.ds(r0, 2 * WINDOW), :], mask, kv, cols, store_o, kv)

    col_groups = [tuple(range(c, c + PROMPT_ATTN_COLS))
                  for c in range(0, COLS_PER_KV, PROMPT_ATTN_COLS)]

    def block(blk, carry):
        r0 = pl.multiple_of(blk * WINDOW, WINDOW)
        first = jnp.logical_and(j == 0, blk == 0)
        mask = _band_mask(WINDOW, jnp.where(first, WINDOW, 0))
        _attend_units([unit(r0, mask, kv, cols) for kv in range(N_KV_HEADS)
                       for cols in col_groups], sink_ref, WINDOW)
        return carry

    lax.fori_loop(0, n_blk, block, 0)

    y_ref[...] = _mm(a_scr[...], wo_ref[...]) + bo_ref[...] + x

    k_scr[0:WINDOW, :] = k_new[tq_tile - WINDOW:, :]
    v_scr[0:WINDOW, :] = v_new[tq_tile - WINDOW:, :]

    @pl.when(j == n_tiles - 1)
    def _():
        knew_ref[0] = k_new[tq_tile - WINDOW:, :]
        vnew_ref[0] = v_new[tq_tile - WINDOW:, :]


def _attn_prompt(x, g, wqkv, bqkv, wo, bo, cos_t, sin_t, sinks, batch, seq):
    tq = ROW_TILE
    nt = seq // tq
    row = pl.BlockSpec((tq, D_MODEL), lambda b, j: (b * nt + j, 0))
    tab = pl.BlockSpec((tq, V7X_LANES), lambda b, j: (j, 0))
    cache = pl.BlockSpec((1, WINDOW, V7X_LANES), lambda b, j: (b, 0, 0))
    return pl.pallas_call(
        functools.partial(_attn_prompt_kernel, tq_tile=tq, n_tiles=nt),
        grid=(batch, nt),
        in_specs=[row, _const_spec((1, D_MODEL)), _spec_of(wqkv),
                  _const_spec((1, Q_DIM + 2 * KV_DIM)), _spec_of(wo), _const_spec((1, D_MODEL)),
                  tab, tab, pl.BlockSpec(memory_space=pltpu.SMEM)],
        out_specs=[row, cache, cache],
        out_shape=[jax.ShapeDtypeStruct((batch * seq, D_MODEL), F32),
                   jax.ShapeDtypeStruct((batch, WINDOW, V7X_LANES), F32),
                   jax.ShapeDtypeStruct((batch, WINDOW, V7X_LANES), F32)],
        scratch_shapes=[pltpu.VMEM((tq, Q_DIM), BF16),
                        pltpu.VMEM((WINDOW + tq, V7X_LANES), F32),
                        pltpu.VMEM((WINDOW + tq, V7X_LANES), F32),
                        pltpu.VMEM((tq, Q_DIM), BF16)],
        compiler_params=pltpu.CompilerParams(
            dimension_semantics=("arbitrary", "arbitrary"),
            vmem_limit_bytes=_vmem_limit(48 * 1024 * 1024)),
        name="attn_prompt",
    )(x, g, _arg_of(wqkv), bqkv, _arg_of(wo), bo, cos_t, sin_t, sinks)


def _attn_sample_kernel(x_ref, g_ref, wqkv_ref, bqkv_ref, wo_ref, bo_ref, cos_ref, sin_ref,
                        sink_ref, ck_ref, cv_ref, y_ref, nk_ref, nv_ref,
                        q_scr, k_scr, v_scr, a_scr, *, n_seq, t_new):
    tm = n_seq * t_new
    x = x_ref[...]
    h = _rms(x, g_ref[...])
    qkv = _mm(h, wqkv_ref[...]) + bqkv_ref[...]
    cos = cos_ref[...]
    sin = sin_ref[...]
    lane = lax.broadcasted_iota(jnp.int32, (tm, V7X_LANES), 1)
    first_half = (lane % HEAD_DIM) < (HEAD_DIM // 2)
    scale = HEAD_DIM ** -0.5
    for c in range(Q_DIM // V7X_LANES):
        qc = _rope_cols(qkv[:, c * V7X_LANES:(c + 1) * V7X_LANES], cos, sin, first_half)
        q_scr[:, c * V7X_LANES:(c + 1) * V7X_LANES] = qc * scale
    k_scr[...] = _rope_cols(qkv[:, Q_DIM:Q_DIM + KV_DIM], cos, sin, first_half)
    v_scr[...] = qkv[:, Q_DIM + KV_DIM:]
    mask = _band_mask(t_new, 0)
    pad = jnp.zeros((WINDOW - t_new, V7X_LANES), F32)

    def unit(u, b, kv):
        r0 = pl.multiple_of(b * t_new, t_new)

        def load_q():
            return jnp.concatenate(
                [q_scr[pl.ds(r0, t_new),
                       (kv * COLS_PER_KV + c) * V7X_LANES:(kv * COLS_PER_KV + c + 1) * V7X_LANES]
                 for c in range(COLS_PER_KV)], axis=0).astype(BF16)

        def load_k():
            return jnp.concatenate([ck_ref[b], k_scr[pl.ds(r0, t_new), :], pad], axis=0)

        def load_v():
            return jnp.concatenate([cv_ref[b], v_scr[pl.ds(r0, t_new), :], pad], axis=0)

        def store_o(o):
            for c in range(COLS_PER_KV):
                col = kv * COLS_PER_KV + c
                a_scr[pl.ds(r0, t_new), col * V7X_LANES:(col + 1) * V7X_LANES] = (
                    o[c * t_new:(c + 1) * t_new])

        return (load_q, load_k, load_v, mask, kv, tuple(range(COLS_PER_KV)), store_o, (u, kv))

    def seq_group(i, carry):
        seqs = [i * SAMPLE_ATTN_UNROLL + u for u in range(SAMPLE_ATTN_UNROLL)]
        _attend_units([unit(u, b, kv) for u, b in enumerate(seqs) for kv in range(N_KV_HEADS)],
                      sink_ref, t_new)
        for b in seqs:
            r0 = pl.multiple_of(b * t_new, t_new)
            nk_ref[b] = jnp.concatenate([ck_ref[b, t_new:, :], k_scr[pl.ds(r0, t_new), :]], axis=0)
            nv_ref[b] = jnp.concatenate([cv_ref[b, t_new:, :], v_scr[pl.ds(r0, t_new), :]], axis=0)
        return carry

    lax.fori_loop(0, n_seq // SAMPLE_ATTN_UNROLL, seq_group, 0)
    y_ref[...] = _mm(a_scr[...], wo_ref[...]) + bo_ref[...] + x


def _attn_sample(x, row0, g, wqkv, bqkv, wo, bo, cos_t, sin_t, sinks, ck, cv, n_batch, t_new):
    bg = SAMPLE_ATTN_BATCHES
    tm = bg * t_new
    row = pl.BlockSpec((tm, D_MODEL), lambda i: (i, 0))
    row_in = pl.BlockSpec((tm, D_MODEL), lambda i: (i + row0 // tm, 0))
    cache = pl.BlockSpec((bg, WINDOW, V7X_LANES), lambda i: (i, 0, 0))
    return pl.pallas_call(
        functools.partial(_attn_sample_kernel, n_seq=bg, t_new=t_new),
        grid=(n_batch // bg,),
        in_specs=[row_in, _const_spec((1, D_MODEL)), _spec_of(wqkv),
                  _const_spec((1, Q_DIM + 2 * KV_DIM)), _spec_of(wo), _const_spec((1, D_MODEL)),
                  _const_spec((tm, V7X_LANES)), _const_spec((tm, V7X_LANES)),
                  pl.BlockSpec(memory_space=pltpu.SMEM), cache, cache],
        out_specs=[row, cache, cache],
        out_shape=[jax.ShapeDtypeStruct((n_batch * t_new, D_MODEL), F32),
                   jax.ShapeDtypeStruct((n_batch, WINDOW, V7X_LANES), F32),
                   jax.ShapeDtypeStruct((n_batch, WINDOW, V7X_LANES), F32)],
        scratch_shapes=[pltpu.VMEM((tm, Q_DIM), F32),
                        pltpu.VMEM((tm, V7X_LANES), F32),
                        pltpu.VMEM((tm, V7X_LANES), F32),
                        pltpu.VMEM((tm, Q_DIM), F32)],
        compiler_params=pltpu.CompilerParams(
            dimension_semantics=("arbitrary",),
            vmem_limit_bytes=_vmem_limit(48 * 1024 * 1024)),
        name="attn_sample",
    )(x, g, _arg_of(wqkv), bqkv, _arg_of(wo), bo, cos_t, sin_t, sinks, ck, cv)


def _dwconv_groups(u, prev, w_ref, col0, ncol):
    rows = u.shape[0]
    kw = w_ref.shape[0]
    sub = lax.broadcasted_iota(jnp.int32, u.shape, 0) % V7X_SUBLANES
    y = u * w_ref[kw - 1:kw, col0:col0 + ncol]
    for s in range(1, kw):
        from_prev = pltpu.roll(prev, (rows - V7X_SUBLANES + s) % rows, 0)
        from_self = pltpu.roll(u, s, 0)
        shifted = jnp.where(sub < s, from_prev, from_self)
        y = y + shifted * w_ref[kw - 1 - s:kw - s, col0:col0 + ncol]
    return y


def _dwconv_rows(u, carry8, w_ref, col0, ncol):
    kw = w_ref.shape[0]
    sub8 = lax.broadcasted_iota(jnp.int32, carry8.shape, 0)
    y = u * w_ref[kw - 1:kw, col0:col0 + ncol]
    for s in range(1, kw):
        rolled = pltpu.roll(u, s, 0)
        head = jnp.where(sub8 < s, pltpu.roll(carry8, s, 0), rolled[:V7X_SUBLANES])
        shifted = jnp.concatenate([head, rolled[V7X_SUBLANES:]], axis=0)
        y = y + shifted * w_ref[kw - 1 - s:kw - s, col0:col0 + ncol]
    return y


def _scan_groups(a, b):
    pos = lax.broadcasted_iota(jnp.int32, a.shape, 0) % V7X_SUBLANES
    shift = 1
    while shift < V7X_SUBLANES:
        ok = pos >= shift
        a_sh = jnp.where(ok, pltpu.roll(a, shift, 0), 1.0)
        b_sh = jnp.where(ok, pltpu.roll(b, shift, 0), 0.0)
        b = a * b_sh + b
        a = a * a_sh
        shift *= 2
    return a, b


def _scan_rows(a, b, h_row):
    a_grp, b_grp = _scan_groups(a, b)
    out = []
    h = h_row
    for g in range(a.shape[0] // V7X_SUBLANES):
        rows = slice(g * V7X_SUBLANES, (g + 1) * V7X_SUBLANES)
        hs = a_grp[rows] * h + b_grp[rows]
        out.append(hs)
        h = hs[V7X_SUBLANES - 1:, :]
    return jnp.concatenate(out, axis=0)


def _log_sigmoid(x):
    return jnp.minimum(x, 0.0) - jnp.log1p(jnp.exp(-jnp.abs(x)))


def _sigmoid(x):
    return 0.5 * jnp.tanh(0.5 * x) + 0.5


def _rglru_body(x, g_ref, wgate_ref, win_ref, cw_ref, cb_ref, wa_ref, ba_ref, wx_ref, bx_ref,
                lam_ref, wout_ref, conv_of, scan_of, u_sink, h_sink):
    h = _rms(x, g_ref[...]).astype(BF16)
    acc = x
    for n in range(RG_BLOCKS):
        c0 = n * RG_BLOCK_W
        cs = slice(c0, c0 + RG_BLOCK_W)
        gate = jax.nn.gelu(jnp.dot(h, wgate_ref[:, cs], preferred_element_type=F32))
        u0 = jnp.dot(h, win_ref[:, cs], preferred_element_type=F32)
        u_sink(n, u0)
        u = conv_of(n, u0) + cb_ref[:, cs]
        ub = u.astype(BF16)
        r = jax.nn.sigmoid(jnp.dot(ub, wa_ref[n], preferred_element_type=F32) + ba_ref[:, cs])
        ig = jax.nn.sigmoid(jnp.dot(ub, wx_ref[n], preferred_element_type=F32) + bx_ref[:, cs])
        log_a = RG_C * r * _log_sigmoid(lam_ref[:, cs])
        a = jnp.exp(log_a)
        mult = jnp.sqrt(-jnp.tanh(log_a) * (1.0 + a * a))
        hs = scan_of(n, a, mult * (ig * u))
        h_sink(n, hs)
        acc = acc + jnp.dot((hs * gate).astype(BF16), wout_ref[cs, :], preferred_element_type=F32)
    return acc


def _seg_pitch(seg_len):
    return seg_len + V7X_SUBLANES if (seg_len // V7X_SUBLANES) % 2 == 0 else seg_len


def _rows_to_segments(scr, slab0, x, seg_len):
    pitch = _seg_pitch(seg_len)
    for s in range(x.shape[1] // V7X_LANES):
        for i in range(V7X_SUBLANES):
            scr[slab0 + s, i * pitch:i * pitch + seg_len, :] = (
                x[i * seg_len:(i + 1) * seg_len, s * V7X_LANES:(s + 1) * V7X_LANES])


def _segments_to_rows(scr, slab0, n_slabs, seg_len):
    pitch = _seg_pitch(seg_len)
    return jnp.concatenate(
        [jnp.concatenate([scr[slab0 + s, i * pitch:i * pitch + seg_len, :]
                          for i in range(V7X_SUBLANES)], axis=0)
         for s in range(n_slabs)], axis=1)


def _seg_step(scr, slab, k, seg_len):
    return scr[slab, pl.ds(k, V7X_SUBLANES, stride=_seg_pitch(seg_len)), :]


def _seg_step_store(scr, slab, k, seg_len, v):
    scr[slab, pl.ds(k, V7X_SUBLANES, stride=_seg_pitch(seg_len)), :] = v


def _from_prev_segment(v, first):
    sub = lax.broadcasted_iota(jnp.int32, v.shape, 0)
    return jnp.where(sub == 0, first, pltpu.roll(v, 1, 0))


def _rglru_prompt_kernel(x_ref, g_ref, wgate_ref, win_ref, cw_ref, cb_ref, wa_ref, ba_ref,
                         wx_ref, bx_ref, lam_ref, wout_ref, y_ref, hlast_ref, ulast_ref,
                         ucarry, hcarry, *seg_scr, tm, sub_rows):
    j = pl.program_id(1)
    slabs_per_chunk = RG_BLOCK_W // V7X_LANES
    kw = cw_ref.shape[0]
    n_sub = tm // sub_rows
    u0_scr, u_scr, r_scr, ig_scr, hs_scr = (seg_scr[i * n_sub:(i + 1) * n_sub]
                                            for i in range(RG_SEG_ARRAYS))

    @pl.when(j == 0)
    def _():
        ucarry[...] = jnp.zeros(ucarry.shape, F32)
        hcarry[...] = jnp.zeros(hcarry.shape, F32)

    n_slabs = D_MODEL // V7X_LANES
    seg_len = sub_rows // V7X_SUBLANES
    gates, conv_prev, h_prev = {}, {}, {}

    def project_in(sub):
        x = x_ref[sub * sub_rows:(sub + 1) * sub_rows, :]
        h = _rms(x, g_ref[...]).astype(BF16)
        gates[sub] = jax.nn.gelu(jnp.dot(h, wgate_ref[...], preferred_element_type=F32))
        u0 = jnp.dot(h, win_ref[...], preferred_element_type=F32)
        conv_prev[sub + 1] = u0[sub_rows - V7X_SUBLANES:, :]
        _rows_to_segments(u0_scr[sub], 0, u0, seg_len)

    def conv(sub):
        for slab in range(n_slabs):
            lanes = slice(slab * V7X_LANES, (slab + 1) * V7X_LANES)
            taps = [cw_ref[t:t + 1, lanes] for t in range(kw)]
            bias = cb_ref[:, lanes]
            steps = {k: _seg_step(u0_scr[sub], slab, k, seg_len) for k in range(seg_len)}
            for back in range(1, kw):
                steps[-back] = _from_prev_segment(
                    steps[seg_len - back],
                    conv_prev[sub][V7X_SUBLANES - back:V7X_SUBLANES - back + 1, lanes])
            for k in range(seg_len):
                u_k = steps[k] * taps[kw - 1] + bias
                for back in range(1, kw):
                    u_k = u_k + steps[k - back] * taps[kw - 1 - back]
                _seg_step_store(u_scr[sub], slab, k, seg_len, u_k)

    def project_gates(sub):
        for n in range(RG_BLOCKS):
            slab0 = n * slabs_per_chunk
            ub = _segments_to_rows(u_scr[sub], slab0, slabs_per_chunk, seg_len).astype(BF16)
            _rows_to_segments(r_scr[sub], slab0,
                              jnp.dot(ub, wa_ref[n], preferred_element_type=F32), seg_len)
            _rows_to_segments(ig_scr[sub], slab0,
                              jnp.dot(ub, wx_ref[n], preferred_element_type=F32), seg_len)

    def recur(sub):
        h_end = []
        for slab in range(n_slabs):
            lanes = slice(slab * V7X_LANES, (slab + 1) * V7X_LANES)
            log_a_scale = RG_C * _log_sigmoid(lam_ref[:, lanes])
            ba = ba_ref[:, lanes]
            bx = bx_ref[:, lanes]
            h_in = h_prev[sub][slab]
            a_cum, h_loc = [], []
            for k in range(seg_len):
                r = _sigmoid(_seg_step(r_scr[sub], slab, k, seg_len) + ba)
                ig = _sigmoid(_seg_step(ig_scr[sub], slab, k, seg_len) + bx)
                log_a = log_a_scale * r
                a = jnp.exp(log_a)
                mult = jnp.sqrt(-jnp.tanh(log_a) * (1.0 + a * a))
                b = mult * (ig * _seg_step(u_scr[sub], slab, k, seg_len))
                if k == 0:
                    a_cum.append(a)
                    h_loc.append(b)
                else:
                    a_cum.append(a * a_cum[-1])
                    h_loc.append(a * h_loc[-1] + b)
            a_seg, b_seg = _scan_groups(a_cum[-1], h_loc[-1])
            seg_end = a_seg * h_in + b_seg
            h_end.append(seg_end[V7X_SUBLANES - 1:, :])
            h_start = _from_prev_segment(seg_end, h_in)
            for k in range(seg_len):
                _seg_step_store(hs_scr[sub], slab, k, seg_len, a_cum[k] * h_start + h_loc[k])
        h_prev[sub + 1] = h_end

    def project_out(sub):
        hs = _segments_to_rows(hs_scr[sub], 0, n_slabs, seg_len)
        rows = slice(sub * sub_rows, (sub + 1) * sub_rows)
        y_ref[rows, :] = x_ref[rows, :] + jnp.dot(
            (hs * gates.pop(sub)).astype(BF16), wout_ref[...], preferred_element_type=F32)
        if sub == n_sub - 1:
            hlast_ref[0] = hs[sub_rows - V7X_SUBLANES:, :]

    conv_prev[0] = ucarry[...]
    h_prev[0] = [hcarry[V7X_SUBLANES - 1:V7X_SUBLANES, s * V7X_LANES:(s + 1) * V7X_LANES]
                 for s in range(n_slabs)]
    stages = [project_in, conv, project_gates, recur, project_out]
    for step in range(n_sub + len(stages) - 1):
        for sub in range(n_sub):
            stage = step - sub
            if 0 <= stage < len(stages):
                stages[stage](sub)

    ulast_ref[0] = conv_prev[n_sub]
    ucarry[...] = conv_prev[n_sub]
    hcarry[...] = hlast_ref[0]


def _rglru_sample_kernel(x_ref, prev_ref, hinit_ref, g_ref, wgate_ref, win_ref, cw_ref, cb_ref,
                         wa_ref, ba_ref, wx_ref, bx_ref, lam_ref, wout_ref, y_ref, hs_ref, u_ref,
                         *, t_new):
    def conv_of(n, u0):
        c0 = n * RG_BLOCK_W
        return _dwconv_groups(u0, prev_ref[:, c0:c0 + RG_BLOCK_W], cw_ref, c0, RG_BLOCK_W)

    def scan_of(n, a, b):
        a_grp, b_grp = _scan_groups(a, b)
        return a_grp * hinit_ref[:, n * RG_BLOCK_W:(n + 1) * RG_BLOCK_W] + b_grp

    def u_sink(n, u0):
        u_ref[:, n * RG_BLOCK_W:(n + 1) * RG_BLOCK_W] = u0

    def h_sink(n, hs):
        hs_ref[:, n * RG_BLOCK_W:(n + 1) * RG_BLOCK_W] = hs

    y_ref[...] = _rglru_body(x_ref[...], g_ref, wgate_ref, win_ref, cw_ref, cb_ref, wa_ref, ba_ref,
                             wx_ref, bx_ref, lam_ref, wout_ref, conv_of, scan_of, u_sink, h_sink)


def _rglru_weight_specs(p):
    return [_const_spec(a.shape) for a in p]


def _rglru_prompt(x, params, batch, seq):
    tm = RG_PROMPT_TILE
    sub_rows = RG_ROW_TILE
    nt = seq // tm
    row = pl.BlockSpec((tm, D_MODEL), lambda b, j: (b * nt + j, 0))
    last = pl.BlockSpec((1, V7X_SUBLANES, D_MODEL), lambda b, j: (b, 0, 0))
    seg_scratch = pltpu.VMEM((D_MODEL // V7X_LANES,
                              V7X_SUBLANES * _seg_pitch(sub_rows // V7X_SUBLANES), V7X_LANES),
                             F32)
    return pl.pallas_call(
        functools.partial(_rglru_prompt_kernel, tm=tm, sub_rows=sub_rows),
        grid=(batch, nt),
        in_specs=[row] + _rglru_weight_specs(params),
        out_specs=[row, last, last],
        out_shape=[jax.ShapeDtypeStruct((batch * seq, D_MODEL), F32),
                   jax.ShapeDtypeStruct((batch, V7X_SUBLANES, D_MODEL), F32),
                   jax.ShapeDtypeStruct((batch, V7X_SUBLANES, D_MODEL), F32)],
        scratch_shapes=[pltpu.VMEM((V7X_SUBLANES, D_MODEL), F32),
                        pltpu.VMEM((V7X_SUBLANES, D_MODEL), F32)]
        + [seg_scratch] * (RG_SEG_ARRAYS * (tm // sub_rows)),
        compiler_params=pltpu.CompilerParams(
            dimension_semantics=("arbitrary", "arbitrary"),
            vmem_limit_bytes=_vmem_limit(48 * 1024 * 1024)),
        name="rglru_prompt",
    )(x, *params)


def _rglru_sample(x, row0, prev, hinit, params, t_new):
    m = prev.shape[0]
    tm = min(RG_ROW_TILE, m)
    row = pl.BlockSpec((tm, D_MODEL), lambda i: (i, 0))
    row_in = pl.BlockSpec((tm, D_MODEL), lambda i: (i + row0 // tm, 0))
    return pl.pallas_call(
        functools.partial(_rglru_sample_kernel, t_new=t_new),
        grid=(m // tm,),
        in_specs=[row_in, row, row] + _rglru_weight_specs(params),
        out_specs=[row, row, row],
        out_shape=[jax.ShapeDtypeStruct((m, D_MODEL), F32)] * 3,
        compiler_params=pltpu.CompilerParams(
            dimension_semantics=("arbitrary",),
            vmem_limit_bytes=_vmem_limit(48 * 1024 * 1024)),
        name="rglru_sample",
    )(x, prev, hinit, *params)


SCONV_CHUNK = V7X_MXU_DIM


def _sconv_body(x, g_ref, win_ref, cw_ref, wout_ref, conv_of, v_sink):
    h = _rms(x, g_ref[...]).astype(BF16)
    bcx = _mm(h, win_ref[...])
    gated = []
    for n in range(D_MODEL // SCONV_CHUNK):
        c0 = n * SCONV_CHUNK
        bg = bcx[:, c0:c0 + SCONV_CHUNK]
        cg = bcx[:, D_MODEL + c0:D_MODEL + c0 + SCONV_CHUNK]
        xv = bcx[:, 2 * D_MODEL + c0:2 * D_MODEL + c0 + SCONV_CHUNK]
        v = cg * xv
        v_sink(n, v)
        gated.append((bg * conv_of(n, v)).astype(BF16))
    return x + _mm(jnp.concatenate(gated, axis=1), wout_ref[...])


def _sconv_prompt_kernel(x_ref, g_ref, win_ref, cw_ref, wout_ref, y_ref, vlast_ref, vcarry,
                         *, tm):
    j = pl.program_id(1)

    @pl.when(j == 0)
    def _():
        vcarry[...] = jnp.zeros(vcarry.shape, F32)

    def conv_of(n, v):
        c0 = n * SCONV_CHUNK
        return _dwconv_rows(v, vcarry[:, c0:c0 + SCONV_CHUNK], cw_ref, c0, SCONV_CHUNK)

    def v_sink(n, v):
        vlast_ref[0, :, n * SCONV_CHUNK:(n + 1) * SCONV_CHUNK] = v[tm - V7X_SUBLANES:, :]

    y_ref[...] = _sconv_body(x_ref[...], g_ref, win_ref, cw_ref, wout_ref, conv_of, v_sink)
    vcarry[...] = vlast_ref[0]


def _sconv_sample_kernel(x_ref, prev_ref, g_ref, win_ref, cw_ref, wout_ref, y_ref, v_ref):
    def conv_of(n, v):
        c0 = n * SCONV_CHUNK
        return _dwconv_groups(v, prev_ref[:, c0:c0 + SCONV_CHUNK], cw_ref, c0, SCONV_CHUNK)

    def v_sink(n, v):
        v_ref[:, n * SCONV_CHUNK:(n + 1) * SCONV_CHUNK] = v

    y_ref[...] = _sconv_body(x_ref[...], g_ref, win_ref, cw_ref, wout_ref, conv_of, v_sink)


def _sconv_prompt(x, params, batch, seq):
    tm = ROW_TILE
    nt = seq // tm
    row = pl.BlockSpec((tm, D_MODEL), lambda b, j: (b * nt + j, 0))
    last = pl.BlockSpec((1, V7X_SUBLANES, D_MODEL), lambda b, j: (b, 0, 0))
    return pl.pallas_call(
        functools.partial(_sconv_prompt_kernel, tm=tm),
        grid=(batch, nt),
        in_specs=[row] + [_spec_of(a) for a in params],
        out_specs=[row, last],
        out_shape=[jax.ShapeDtypeStruct((batch * seq, D_MODEL), F32),
                   jax.ShapeDtypeStruct((batch, V7X_SUBLANES, D_MODEL), F32)],
        scratch_shapes=[pltpu.VMEM((V7X_SUBLANES, D_MODEL), F32)],
        compiler_params=pltpu.CompilerParams(
            dimension_semantics=("arbitrary", "arbitrary"),
            vmem_limit_bytes=_vmem_limit(48 * 1024 * 1024)),
        name="sconv_prompt",
    )(x, *[_arg_of(a) for a in params])


def _sconv_sample(x, row0, prev, params):
    m = prev.shape[0]
    tm = min(ROW_TILE, m)
    row = pl.BlockSpec((tm, D_MODEL), lambda i: (i, 0))
    row_in = pl.BlockSpec((tm, D_MODEL), lambda i: (i + row0 // tm, 0))
    return pl.pallas_call(
        _sconv_sample_kernel,
        grid=(m // tm,),
        in_specs=[row_in, row] + [_spec_of(a) for a in params],
        out_specs=[row, row],
        out_shape=[jax.ShapeDtypeStruct((m, D_MODEL), F32)] * 2,
        compiler_params=pltpu.CompilerParams(
            dimension_semantics=("arbitrary",),
            vmem_limit_bytes=_vmem_limit(48 * 1024 * 1024)),
        name="sconv_sample",
    )(x, prev, *[_arg_of(a) for a in params])


def _rope_tables(pos):
    half = HEAD_DIM // 2
    inv = ROPE_THETA ** (-jnp.arange(half, dtype=F32) / half)
    ang = pos.astype(F32)[:, None] * inv[None, :]
    cos = jnp.cos(ang)
    sin = jnp.sin(ang)
    reps = V7X_LANES // HEAD_DIM
    cos_t = jnp.tile(jnp.concatenate([cos, cos], axis=-1), (1, reps))
    sin_t = jnp.tile(jnp.concatenate([-sin, sin], axis=-1), (1, reps))
    return cos_t, sin_t


def _row(v):
    return v.reshape(1, -1)


def _pad_state_rows(buf):
    b, k, c = buf.shape
    padded = jnp.concatenate([jnp.zeros((b, V7X_SUBLANES - k, c), buf.dtype), buf], axis=1)
    return padded.reshape(b * V7X_SUBLANES, c)


def kernel(x_prompt, x_sample, cache_k, cache_v, state_rglru_h, state_rglru_conv, state_shortconv,
           norm_mixer, norm_ffn, norm_final,
           attn_w_qkv, attn_b_qkv, attn_w_o, attn_b_o, attn_sinks,
           rglru_w_gate, rglru_w_in, rglru_conv_w, rglru_conv_b, rglru_wa, rglru_ba,
           rglru_wx, rglru_bx, rglru_lambda, rglru_w_out,
           sconv_w_in, sconv_conv_w, sconv_w_out,
           ffn_w_gate, ffn_w_up, ffn_w_down):
    bp, seq, _ = x_prompt.shape
    bs, t_new, _ = x_sample.shape
    depth = norm_mixer.shape[0]
    past_len = 8192
    assert t_new == V7X_SUBLANES

    xp = x_prompt.reshape(bp * seq, D_MODEL)
    xs = x_sample.reshape(bs * t_new, D_MODEL)
    s_row0 = 0

    cos_p, sin_p = _rope_tables(jnp.arange(seq, dtype=jnp.int32))
    cos_s, sin_s = _rope_tables(past_len + jnp.arange(t_new, dtype=jnp.int32))
    cos_s = jnp.tile(cos_s, (SAMPLE_ATTN_BATCHES, 1))
    sin_s = jnp.tile(sin_s, (SAMPLE_ATTN_BATCHES, 1))

    kp_l, vp_l, ks_l, vs_l = [], [], [], []
    hp_l, hs_l, rcp_l, rcs_l = [], [], [], []
    scp_l, scs_l = [], []

    for i in range(depth):
        kind = i % 3
        j = i // 3
        g_mix = _row(norm_mixer[i])
        if kind == 0:
            wqkv = _layer(attn_w_qkv, j)
            wo = _layer(attn_w_o, j)
            bqkv = _row(attn_b_qkv[j])
            bo = _row(attn_b_o[j])
            sinks = attn_sinks[j]
            xp, kp, vp = _attn_prompt(xp, g_mix, wqkv, bqkv, wo, bo, cos_p, sin_p, sinks, bp, seq)
            ck = cache_k[j].reshape(bs, WINDOW, KV_DIM)
            cv = cache_v[j].reshape(bs, WINDOW, KV_DIM)
            xs, ks, vs = _attn_sample(xs, s_row0, g_mix, wqkv, bqkv, wo, bo, cos_s, sin_s, sinks,
                                      ck, cv, bs, t_new)
            kp_l.append(kp.reshape(bp, WINDOW, N_KV_HEADS, HEAD_DIM))
            vp_l.append(vp.reshape(bp, WINDOW, N_KV_HEADS, HEAD_DIM))
            ks_l.append(ks.reshape(bs, WINDOW, N_KV_HEADS, HEAD_DIM))
            vs_l.append(vs.reshape(bs, WINDOW, N_KV_HEADS, HEAD_DIM))
        elif kind == 1:
            params = (g_mix, rglru_w_gate[j].astype(BF16), rglru_w_in[j].astype(BF16),
                      rglru_conv_w[j], _row(rglru_conv_b[j]), rglru_wa[j].astype(BF16),
                      _row(rglru_ba[j]), rglru_wx[j].astype(BF16), _row(rglru_bx[j]),
                      _row(rglru_lambda[j]), rglru_w_out[j].astype(BF16))
            xp, hlast, ulast = _rglru_prompt(xp, params, bp, seq)
            hp_l.append(hlast[:, V7X_SUBLANES - 1])
            rcp_l.append(ulast[:, V7X_SUBLANES - (RG_CONV_W - 1):])
            prev = _pad_state_rows(state_rglru_conv[j])
            hinit = jnp.repeat(state_rglru_h[j], t_new, axis=0)
            xs, hs_all, u_all = _rglru_sample(xs, s_row0, prev, hinit, params, t_new)
            hs_l.append(hs_all.reshape(bs, t_new, D_MODEL)[:, t_new - 1])
            rcs_l.append(u_all.reshape(bs, t_new, D_MODEL)[:, t_new - (RG_CONV_W - 1):])
        else:
            params = (g_mix, _layer(sconv_w_in, j), sconv_conv_w[j], _layer(sconv_w_out, j))
            xp, vlast = _sconv_prompt(xp, params, bp, seq)
            scp_l.append(vlast[:, V7X_SUBLANES - (SCONV_W - 1):])
            prev = _pad_state_rows(state_shortconv[j])
            xs, v_all = _sconv_sample(xs, s_row0, prev, params)
            scs_l.append(v_all.reshape(bs, t_new, D_MODEL)[:, t_new - (SCONV_W - 1):])

        wg = _layer(ffn_w_gate, i)
        wu = _layer(ffn_w_up, i)
        wd = _layer(ffn_w_down, i)
        d_ff = ffn_w_down.shape[1]
        g_ffn = _row(norm_ffn[i])
        g_fin = _row(norm_final)
        last = i == depth - 1
        if last:
            xp = _ffn(xp, bp * seq, None, g_ffn, wg, wu, wd, d_ff, g_fin, True)
            xs = _ffn(xs, bs * t_new, None, g_ffn, wg, wu, wd, d_ff, g_fin, True)
        else:
            xp = xs = _ffn(xp, bp * seq, xs, g_ffn, wg, wu, wd, d_ff, g_fin, False)
            s_row0 = bp * seq

    return (xp.reshape(bp, seq, D_MODEL), xs.reshape(bs, t_new, D_MODEL),
            jnp.stack(kp_l), jnp.stack(vp_l), jnp.stack(ks_l), jnp.stack(vs_l),
            jnp.stack(hp_l), jnp.stack(hs_l), jnp.stack(rcp_l), jnp.stack(rcs_l),
            jnp.stack(scp_l), jnp.stack(scs_l))
```

```python
import functools

import jax
import jax.numpy as jnp
from jax import lax
from jax.experimental import pallas as pl
from jax.experimental.pallas import tpu as pltpu

D_MODEL = 1024
HEAD_DIM = 64
N_HEADS = 16
N_KV_HEADS = 2
GQA_GROUP = 8
Q_DIM = N_HEADS * HEAD_DIM
KV_DIM = N_KV_HEADS * HEAD_DIM
WINDOW = 128
ROPE_THETA = 10000.0
NEG_INF = -1e30
RG_BLOCKS = 4
RG_BLOCK_W = 256
RG_CONV_W = 4
RG_C = 8.0
SCONV_W = 3
EPS = 1e-6

V7X_LANES = 128
V7X_SUBLANES = 8
V7X_MXU_DIM = 256
V7X_VMEM_BYTES = 64 * 1024 * 1024

BF16 = jnp.bfloat16
F32 = jnp.float32

FFN_CHUNK = V7X_MXU_DIM
ROW_TILE = 512
ATTN_ROW_TILE = 1024
RG_ROW_TILE = 256
RG_PROMPT_TILE = 512
RG_SEG_ARRAYS = 5
SAMPLE_ATTN_BATCHES = 32
SAMPLE_ATTN_UNROLL = 8
ATTN_LOOKAHEAD = 2
PROMPT_ATTN_COLS = 2
COLS_PER_KV = (N_HEADS // N_KV_HEADS) * HEAD_DIM // V7X_LANES


def _vmem_limit(nbytes):
    return int(min(nbytes, V7X_VMEM_BYTES - 8 * 1024 * 1024))


def _const_spec(shape):
    nd = len(shape)
    return pl.BlockSpec(shape, lambda *_: (0,) * nd, pipeline_mode=pl.Buffered(1))


def _layer(stacked, layer):
    return (stacked, layer)


def _spec_of(w):
    if isinstance(w, tuple):
        stacked, layer = w
        nd = stacked.ndim - 1
        return pl.BlockSpec((None,) + stacked.shape[1:], lambda *_: (layer,) + (0,) * nd,
                            pipeline_mode=pl.Buffered(1))
    return _const_spec(w.shape)


def _arg_of(w):
    return w[0] if isinstance(w, tuple) else w


def _nbytes_of(w):
    if isinstance(w, tuple):
        return w[0][0].size * w[0].dtype.itemsize
    return w.size * w.dtype.itemsize


def _rms(x, g):
    ms = jnp.mean(x * x, axis=-1, keepdims=True)
    return x * lax.rsqrt(ms + EPS) * g


def _mm(a, w):
    return jnp.dot(a.astype(BF16), w.astype(BF16), preferred_element_type=F32)


def _ffn_kernel(*refs, n_chunks, final_norm, tiles_a, stacked):
    if stacked:
        xa_ref, xb_ref, g_ref, wg_ref, wu_ref, wd_ref, gf_ref, o_ref = refs
        x = jnp.where(pl.program_id(0) < tiles_a, xa_ref[...], xb_ref[...])
    else:
        xa_ref, g_ref, wg_ref, wu_ref, wd_ref, gf_ref, o_ref = refs
        x = xa_ref[...]
    h = _rms(x, g_ref[...]).astype(BF16)
    acc = x
    for c in range(n_chunks):
        cs = slice(c * FFN_CHUNK, (c + 1) * FFN_CHUNK)
        gate = _mm(h, wg_ref[:, cs])
        up = _mm(h, wu_ref[:, cs])
        act = (gate * jax.nn.sigmoid(gate)) * up
        acc = acc + _mm(act, wd_ref[cs, :])
    if final_norm:
        acc = _rms(acc, gf_ref[...])
    o_ref[...] = acc


def _ffn(xa, rows_a, xb, g, wg, wu, wd, d_ff, gf, final_norm):
    tm = ROW_TILE
    tiles_a = rows_a // tm
    stacked = xb is not None
    tiles_b = xb.shape[0] // tm if stacked else 0
    n_chunks = d_ff // FFN_CHUNK
    out_row = pl.BlockSpec((tm, D_MODEL), lambda i: (i, 0))
    if stacked:
        x_specs = [pl.BlockSpec((tm, D_MODEL), lambda i: (jnp.minimum(i, tiles_a - 1), 0)),
                   pl.BlockSpec((tm, D_MODEL), lambda i: (jnp.maximum(i - tiles_a, 0), 0),
                                pipeline_mode=pl.Buffered(1))]
        x_args = [xa, xb]
    else:
        x_specs = [out_row]
        x_args = [xa]
    weight_bytes = _nbytes_of(wg) + _nbytes_of(wu) + _nbytes_of(wd)
    return pl.pallas_call(
        functools.partial(_ffn_kernel, n_chunks=n_chunks, final_norm=final_norm,
                          tiles_a=tiles_a, stacked=stacked),
        grid=(tiles_a + tiles_b,),
        in_specs=x_specs + [_const_spec((1, D_MODEL)), _spec_of(wg), _spec_of(wu), _spec_of(wd),
                            _const_spec((1, D_MODEL))],
        out_specs=out_row,
        out_shape=jax.ShapeDtypeStruct(((tiles_a + tiles_b) * tm, D_MODEL), F32),
        compiler_params=pltpu.CompilerParams(
            dimension_semantics=("arbitrary",),
            vmem_limit_bytes=_vmem_limit(weight_bytes + 24 * tm * D_MODEL * 4)),
        name="ffn",
    )(*x_args, g, _arg_of(wg), _arg_of(wu), _arg_of(wd), gf)


def _rope_cols(cols, cos, sin_signed, first_half):
    swapped = jnp.where(first_half,
                        pltpu.roll(cols, V7X_LANES - HEAD_DIM // 2, 1),
                        pltpu.roll(cols, HEAD_DIM // 2, 1))
    return cols * cos + swapped * sin_signed


def _pair_blockdiag(mat, mat_rolled, kv, lane_lt64):
    zero = jnp.zeros_like(mat)
    if kv == 0:
        top = jnp.where(lane_lt64, mat, zero)
        bottom = jnp.where(lane_lt64, zero, mat_rolled)
    else:
        top = jnp.where(lane_lt64, mat_rolled, zero)
        bottom = jnp.where(lane_lt64, zero, mat)
    return jnp.concatenate([top, bottom], axis=0).astype(BF16)


def _key_blockdiag(k2, kv):
    lane_lt64 = lax.broadcasted_iota(jnp.int32, k2.shape, 1) < HEAD_DIM
    return _pair_blockdiag(k2, pltpu.roll(k2, HEAD_DIM, 1), kv, lane_lt64)


def _value_blockdiag(v2, kv):
    v2 = jnp.where(lax.broadcasted_iota(jnp.int32, v2.shape, 0) == 0, 0.0, v2)
    vbd = _key_blockdiag(v2, kv)
    row = lax.broadcasted_iota(jnp.int32, vbd.shape, 0)
    lane = lax.broadcasted_iota(jnp.int32, vbd.shape, 1)
    ones_bd = jnp.where((lane < HEAD_DIM) == (row < 2 * WINDOW), 1.0, 0.0).astype(BF16)
    return jnp.concatenate([vbd, ones_bd], axis=1)


def _attn_weights(s_all, mask, sink_ref, kv, cols, tq):
    nk = 2 * WINDOW
    key = lax.broadcasted_iota(jnp.int32, (1, nk), 1)
    e_rows = []
    for i, c in enumerate(cols):
        e_halves = []
        for parity in range(2):
            sink = sink_ref[kv * GQA_GROUP + 2 * c + parity]
            fill = jnp.where(key == 0, sink, NEG_INF)
            s = s_all[i * tq:(i + 1) * tq, parity * nk:(parity + 1) * nk]
            s = jnp.where(mask, s, fill)
            m = jnp.max(s, axis=-1, keepdims=True)
            e_halves.append(jnp.exp(s - m))
        e_rows.append(jnp.concatenate(e_halves, axis=1))
    return jnp.concatenate(e_rows, axis=0).astype(BF16)


def _key_blockdiag_t(kt, kv):
    del kv
    zero = jnp.zeros_like(kt)
    return jnp.concatenate([jnp.concatenate([kt, zero], axis=1),
                            jnp.concatenate([zero, kt], axis=1)], axis=0).astype(BF16)


def _value_blockdiag_t(vt, kv):
    del kv
    vt = jnp.where(lax.broadcasted_iota(jnp.int32, vt.shape, 1) == 0, 0.0, vt)
    zero = jnp.zeros_like(vt)
    one = jnp.ones_like(vt)
    return jnp.concatenate([jnp.concatenate([vt, zero], axis=1),
                            jnp.concatenate([zero, vt], axis=1),
                            jnp.concatenate([one, zero], axis=1),
                            jnp.concatenate([zero, one], axis=1)], axis=0).astype(BF16)


def _dot_nt(a, b):
    return lax.dot_general(a, b, (((1,), (1,)), ((), ())), preferred_element_type=F32)


def _dot_nn(a, b):
    return jnp.dot(a, b, preferred_element_type=F32)


ROW_MAJOR_KV = (_key_blockdiag, _dot_nt, _value_blockdiag, _dot_nn)
LANE_MAJOR_KV = (_key_blockdiag_t, _dot_nn, _value_blockdiag_t, _dot_nt)


def _attend_units(units, sink_ref, tq, kv_ops):
    make_kbd, score_dot, make_vbd, value_dot = kv_ops
    n = len(units)
    scores, kbd, vbd = {}, {}, {}

    def issue_scores(i):
        load_q, load_k, _, _, kv, _, _, kv_id = units[i]
        if kv_id not in kbd:
            kbd[kv_id] = make_kbd(load_k(), kv)
        scores[i] = score_dot(load_q(), kbd[kv_id])

    for i in range(min(ATTN_LOOKAHEAD, n)):
        issue_scores(i)
    for i in range(n):
        if i + ATTN_LOOKAHEAD < n:
            issue_scores(i + ATTN_LOOKAHEAD)
        _, _, load_v, mask, kv, cols, store_o, kv_id = units[i]
        if kv_id not in vbd:
            vbd[kv_id] = make_vbd(load_v(), kv)
        e_all = _attn_weights(scores.pop(i), mask, sink_ref, kv, cols, tq)
        o_den = value_dot(e_all, vbd[kv_id])
        store_o(o_den[:, :V7X_LANES] * (1.0 / o_den[:, V7X_LANES:]))


def _band_mask(tq, col_min):
    row = lax.broadcasted_iota(jnp.int32, (tq, 2 * WINDOW), 0)
    col = lax.broadcasted_iota(jnp.int32, (tq, 2 * WINDOW), 1)
    prev_ok = (col < WINDOW) & (col > row)
    own_ok = (col >= WINDOW) & (col - WINDOW <= row)
    return (prev_ok | own_ok) & (col >= col_min)


def _attn_prompt_kernel(x_ref, g_ref, wqkv_ref, bqkv_ref, wo_ref, bo_ref, cos_ref, sin_ref,
                        sink_ref, y_ref, knew_ref, vnew_ref,
                        q_scr, k_scr, v_scr, a_scr, *, tq_tile, n_tiles):
    j = pl.program_id(1)
    n_blk = tq_tile // WINDOW

    @pl.when(j == 0)
    def _():
        k_scr[0:WINDOW, :] = jnp.zeros((WINDOW, V7X_LANES), F32)
        v_scr[0:WINDOW, :] = jnp.zeros((WINDOW, V7X_LANES), F32)

    x = x_ref[...]
    h = _rms(x, g_ref[...])
    qkv = _mm(h, wqkv_ref[...]) + bqkv_ref[...]
    cos = cos_ref[...]
    sin = sin_ref[...]
    lane = lax.broadcasted_iota(jnp.int32, (tq_tile, V7X_LANES), 1)
    first_half = (lane % HEAD_DIM) < (HEAD_DIM // 2)
    scale = HEAD_DIM ** -0.5
    for c in range(Q_DIM // V7X_LANES):
        qc = _rope_cols(qkv[:, c * V7X_LANES:(c + 1) * V7X_LANES], cos, sin, first_half)
        q_scr[:, c * V7X_LANES:(c + 1) * V7X_LANES] = (qc * scale).astype(BF16)
    k_new = _rope_cols(qkv[:, Q_DIM:Q_DIM + KV_DIM], cos, sin, first_half)
    v_new = qkv[:, Q_DIM + KV_DIM:]
    k_scr[WINDOW:, :] = k_new
    v_scr[WINDOW:, :] = v_new

    def unit(r0, mask, kv, cols):
        def load_q():
            return jnp.concatenate(
                [q_scr[pl.ds(r0, WINDOW),
                       (kv * COLS_PER_KV + c) * V7X_LANES:(kv * COLS_PER_KV + c + 1) * V7X_LANES]
                 for c in cols], axis=0)

        def store_o(o):
            for i, c in enumerate(cols):
                col = kv * COLS_PER_KV + c
                a_scr[pl.ds(r0, WINDOW), col * V7X_LANES:(col + 1) * V7X_LANES] = (
                    o[i * WINDOW:(i + 1) * WINDOW].astype(BF16))

        return (load_q, lambda: k_scr[pl.ds(r0, 2 * WINDOW), :],
                lambda: v_scr[pl.ds(r0, 2 * WINDOW), :], mask, kv, cols, store_o, kv)

    col_groups = [tuple(range(c, c + PROMPT_ATTN_COLS))
                  for c in range(0, COLS_PER_KV, PROMPT_ATTN_COLS)]

    def block(blk, carry):
        r0 = pl.multiple_of(blk * WINDOW, WINDOW)
        first = jnp.logical_and(j == 0, blk == 0)
        mask = _band_mask(WINDOW, jnp.where(first, WINDOW, 0))
        _attend_units([unit(r0, mask, kv, cols) for kv in range(N_KV_HEADS)
                       for cols in col_groups], sink_ref, WINDOW, ROW_MAJOR_KV)
        return carry

    lax.fori_loop(0, n_blk, block, 0)

    y_ref[...] = _mm(a_scr[...], wo_ref[...]) + bo_ref[...] + x

    k_scr[0:WINDOW, :] = k_new[tq_tile - WINDOW:, :]
    v_scr[0:WINDOW, :] = v_new[tq_tile - WINDOW:, :]

    @pl.when(j == n_tiles - 1)
    def _():
        knew_ref[0] = k_new[tq_tile - WINDOW:, :]
        vnew_ref[0] = v_new[tq_tile - WINDOW:, :]


def _attn_prompt(x, g, wqkv, bqkv, wo, bo, cos_t, sin_t, sinks, batch, seq):
    tq = ATTN_ROW_TILE
    nt = seq // tq
    row = pl.BlockSpec((tq, D_MODEL), lambda b, j: (b * nt + j, 0))
    tab = pl.BlockSpec((tq, V7X_LANES), lambda b, j: (j, 0))
    cache = pl.BlockSpec((1, WINDOW, V7X_LANES), lambda b, j: (b, 0, 0))
    return pl.pallas_call(
        functools.partial(_attn_prompt_kernel, tq_tile=tq, n_tiles=nt),
        grid=(batch, nt),
        in_specs=[row, _const_spec((1, D_MODEL)), _spec_of(wqkv),
                  _const_spec((1, Q_DIM + 2 * KV_DIM)), _spec_of(wo), _const_spec((1, D_MODEL)),
                  tab, tab, pl.BlockSpec(memory_space=pltpu.SMEM)],
        out_specs=[row, cache, cache],
        out_shape=[jax.ShapeDtypeStruct((batch * seq, D_MODEL), F32),
                   jax.ShapeDtypeStruct((batch, WINDOW, V7X_LANES), F32),
                   jax.ShapeDtypeStruct((batch, WINDOW, V7X_LANES), F32)],
        scratch_shapes=[pltpu.VMEM((tq, Q_DIM), BF16),
                        pltpu.VMEM((WINDOW + tq, V7X_LANES), F32),
                        pltpu.VMEM((WINDOW + tq, V7X_LANES), F32),
                        pltpu.VMEM((tq, Q_DIM), BF16)],
        compiler_params=pltpu.CompilerParams(
            dimension_semantics=("arbitrary", "arbitrary"),
            vmem_limit_bytes=_vmem_limit(48 * 1024 * 1024)),
        name="attn_prompt",
    )(x, g, _arg_of(wqkv), bqkv, _arg_of(wo), bo, cos_t, sin_t, sinks)


def _attn_sample_kernel(x_ref, g_ref, wqkv_ref, bqkv_ref, wo_ref, bo_ref, cos_ref, sin_ref,
                        sink_ref, ck_ref, cv_ref, nk_in, nv_in, y_ref, nk_ref, nv_ref,
                        q_scr, k_scr, v_scr, a_scr, *, n_seq, t_new):
    del nk_in, nv_in
    tm = n_seq * t_new
    x = x_ref[...]
    h = _rms(x, g_ref[...])
    qkv = _mm(h, wqkv_ref[...]) + bqkv_ref[...]
    cos = cos_ref[...]
    sin = sin_ref[...]
    lane = lax.broadcasted_iota(jnp.int32, (tm, V7X_LANES), 1)
    first_half = (lane % HEAD_DIM) < (HEAD_DIM // 2)
    scale = HEAD_DIM ** -0.5
    for c in range(Q_DIM // V7X_LANES):
        qc = _rope_cols(qkv[:, c * V7X_LANES:(c + 1) * V7X_LANES], cos, sin, first_half)
        q_scr[:, c * V7X_LANES:(c + 1) * V7X_LANES] = qc * scale
    k_scr[...] = _rope_cols(qkv[:, Q_DIM:Q_DIM + KV_DIM], cos, sin, first_half)
    v_scr[...] = qkv[:, Q_DIM + KV_DIM:]
    mask = _band_mask(t_new, 0)
    pad = jnp.zeros((WINDOW - t_new, V7X_LANES), F32)
    lane = lax.broadcasted_iota(jnp.int32, (HEAD_DIM, WINDOW), 1)

    def new_rows_t(scr, b):
        r0 = pl.multiple_of(b * t_new, t_new)
        return jnp.concatenate([scr[pl.ds(r0, t_new), :], pad], axis=0).T

    def seq_group(i, carry):
        seqs = [i * SAMPLE_ATTN_UNROLL + u for u in range(SAMPLE_ATTN_UNROLL)]
        k_new_t = [new_rows_t(k_scr, b) for b in seqs]
        v_new_t = [new_rows_t(v_scr, b) for b in seqs]

        def unit(u, b, kv):
            r0 = pl.multiple_of(b * t_new, t_new)
            head_rows = slice(kv * HEAD_DIM, (kv + 1) * HEAD_DIM)

            def load_q():
                return jnp.concatenate(
                    [q_scr[pl.ds(r0, t_new), (kv * COLS_PER_KV + c) * V7X_LANES:
                           (kv * COLS_PER_KV + c + 1) * V7X_LANES]
                     for c in range(COLS_PER_KV)], axis=0).astype(BF16)

            def load_k():
                return jnp.concatenate([ck_ref[b, kv], k_new_t[u][head_rows, :]], axis=1)

            def load_v():
                return jnp.concatenate([cv_ref[b, kv], v_new_t[u][head_rows, :]], axis=1)

            def store_o(o):
                for c in range(COLS_PER_KV):
                    col = kv * COLS_PER_KV + c
                    a_scr[pl.ds(r0, t_new), col * V7X_LANES:(col + 1) * V7X_LANES] = (
                        o[c * t_new:(c + 1) * t_new])

            return (load_q, load_k, load_v, mask, kv, tuple(range(COLS_PER_KV)), store_o,
                    (u, kv))

        _attend_units([unit(u, b, kv) for u, b in enumerate(seqs) for kv in range(N_KV_HEADS)],
                      sink_ref, t_new, LANE_MAJOR_KV)
        keep = lane < WINDOW - t_new
        for u, b in enumerate(seqs):
            for kv in range(N_KV_HEADS):
                head_rows = slice(kv * HEAD_DIM, (kv + 1) * HEAD_DIM)
                nk_ref[b, kv] = jnp.where(
                    keep, pltpu.roll(ck_ref[b, kv], WINDOW - t_new, 1),
                    pltpu.roll(k_new_t[u][head_rows, :], WINDOW - t_new, 1))
                nv_ref[b, kv] = jnp.where(
                    keep, pltpu.roll(cv_ref[b, kv], WINDOW - t_new, 1),
                    pltpu.roll(v_new_t[u][head_rows, :], WINDOW - t_new, 1))
        return carry

    lax.fori_loop(0, n_seq // SAMPLE_ATTN_UNROLL, seq_group, 0)
    y_ref[...] = _mm(a_scr[...], wo_ref[...]) + bo_ref[...] + x


def _attn_sample(x, row0, g, wqkv, bqkv, wo, bo, cos_t, sin_t, sinks, ck_t, cv_t, nk_t, nv_t,
                 layer, t_new):
    bg = SAMPLE_ATTN_BATCHES
    tm = bg * t_new
    n_batch = ck_t.shape[1]
    row = pl.BlockSpec((tm, D_MODEL), lambda i: (i, 0))
    row_in = pl.BlockSpec((tm, D_MODEL), lambda i: (i + row0 // tm, 0))
    cache = pl.BlockSpec((None, bg, N_KV_HEADS, HEAD_DIM, WINDOW), lambda i: (layer, i, 0, 0, 0))
    whole = pl.BlockSpec(memory_space=pl.ANY)
    n_in = 13
    return pl.pallas_call(
        functools.partial(_attn_sample_kernel, n_seq=bg, t_new=t_new),
        grid=(n_batch // bg,),
        in_specs=[row_in, _const_spec((1, D_MODEL)), _spec_of(wqkv),
                  _const_spec((1, Q_DIM + 2 * KV_DIM)), _spec_of(wo), _const_spec((1, D_MODEL)),
                  _const_spec((tm, V7X_LANES)), _const_spec((tm, V7X_LANES)),
                  pl.BlockSpec(memory_space=pltpu.SMEM), cache, cache, whole, whole],
        out_specs=[row, cache, cache],
        out_shape=[jax.ShapeDtypeStruct((n_batch * t_new, D_MODEL), F32),
                   jax.ShapeDtypeStruct(nk_t.shape, F32),
                   jax.ShapeDtypeStruct(nv_t.shape, F32)],
        input_output_aliases={n_in - 2: 1, n_in - 1: 2},
        scratch_shapes=[pltpu.VMEM((tm, Q_DIM), F32),
                        pltpu.VMEM((tm, V7X_LANES), F32),
                        pltpu.VMEM((tm, V7X_LANES), F32),
                        pltpu.VMEM((tm, Q_DIM), F32)],
        compiler_params=pltpu.CompilerParams(
            dimension_semantics=("arbitrary",),
            vmem_limit_bytes=_vmem_limit(48 * 1024 * 1024)),
        name="attn_sample",
    )(x, g, _arg_of(wqkv), bqkv, _arg_of(wo), bo, cos_t, sin_t, sinks, ck_t, cv_t, nk_t, nv_t)


def _dwconv_groups(u, prev, w_ref, col0, ncol):
    rows = u.shape[0]
    kw = w_ref.shape[0]
    sub = lax.broadcasted_iota(jnp.int32, u.shape, 0) % V7X_SUBLANES
    y = u * w_ref[kw - 1:kw, col0:col0 + ncol]
    for s in range(1, kw):
        from_prev = pltpu.roll(prev, (rows - V7X_SUBLANES + s) % rows, 0)
        from_self = pltpu.roll(u, s, 0)
        shifted = jnp.where(sub < s, from_prev, from_self)
        y = y + shifted * w_ref[kw - 1 - s:kw - s, col0:col0 + ncol]
    return y


def _dwconv_rows(u, carry8, w_ref, col0, ncol):
    kw = w_ref.shape[0]
    sub8 = lax.broadcasted_iota(jnp.int32, carry8.shape, 0)
    y = u * w_ref[kw - 1:kw, col0:col0 + ncol]
    for s in range(1, kw):
        rolled = pltpu.roll(u, s, 0)
        head = jnp.where(sub8 < s, pltpu.roll(carry8, s, 0), rolled[:V7X_SUBLANES])
        shifted = jnp.concatenate([head, rolled[V7X_SUBLANES:]], axis=0)
        y = y + shifted * w_ref[kw - 1 - s:kw - s, col0:col0 + ncol]
    return y


def _scan_groups(a, b):
    pos = lax.broadcasted_iota(jnp.int32, a.shape, 0) % V7X_SUBLANES
    shift = 1
    while shift < V7X_SUBLANES:
        ok = pos >= shift
        a_sh = jnp.where(ok, pltpu.roll(a, shift, 0), 1.0)
        b_sh = jnp.where(ok, pltpu.roll(b, shift, 0), 0.0)
        b = a * b_sh + b
        a = a * a_sh
        shift *= 2
    return a, b


def _scan_rows(a, b, h_row):
    a_grp, b_grp = _scan_groups(a, b)
    out = []
    h = h_row
    for g in range(a.shape[0] // V7X_SUBLANES):
        rows = slice(g * V7X_SUBLANES, (g + 1) * V7X_SUBLANES)
        hs = a_grp[rows] * h + b_grp[rows]
        out.append(hs)
        h = hs[V7X_SUBLANES - 1:, :]
    return jnp.concatenate(out, axis=0)


def _log_sigmoid(x):
    return jnp.minimum(x, 0.0) - jnp.log1p(jnp.exp(-jnp.abs(x)))


def _sigmoid(x):
    return 0.5 * jnp.tanh(0.5 * x) + 0.5


def _rglru_body(x, g_ref, wgate_ref, win_ref, cw_ref, cb_ref, wa_ref, ba_ref, wx_ref, bx_ref,
                lam_ref, wout_ref, conv_of, scan_of, u_sink, h_sink):
    h = _rms(x, g_ref[...]).astype(BF16)
    acc = x
    for n in range(RG_BLOCKS):
        c0 = n * RG_BLOCK_W
        cs = slice(c0, c0 + RG_BLOCK_W)
        gate = jax.nn.gelu(jnp.dot(h, wgate_ref[:, cs], preferred_element_type=F32))
        u0 = jnp.dot(h, win_ref[:, cs], preferred_element_type=F32)
        u_sink(n, u0)
        u = conv_of(n, u0) + cb_ref[:, cs]
        ub = u.astype(BF16)
        r = jax.nn.sigmoid(jnp.dot(ub, wa_ref[n], preferred_element_type=F32) + ba_ref[:, cs])
        ig = jax.nn.sigmoid(jnp.dot(ub, wx_ref[n], preferred_element_type=F32) + bx_ref[:, cs])
        log_a = RG_C * r * _log_sigmoid(lam_ref[:, cs])
        a = jnp.exp(log_a)
        mult = jnp.sqrt(-jnp.tanh(log_a) * (1.0 + a * a))
        hs = scan_of(n, a, mult * (ig * u))
        h_sink(n, hs)
        acc = acc + jnp.dot((hs * gate).astype(BF16), wout_ref[cs, :], preferred_element_type=F32)
    return acc


def _seg_pitch(seg_len):
    return seg_len + V7X_SUBLANES if (seg_len // V7X_SUBLANES) % 2 == 0 else seg_len


def _rows_to_segments(scr, slab0, x, seg_len):
    pitch = _seg_pitch(seg_len)
    for s in range(x.shape[1] // V7X_LANES):
        for i in range(V7X_SUBLANES):
            scr[slab0 + s, i * pitch:i * pitch + seg_len, :] = (
                x[i * seg_len:(i + 1) * seg_len, s * V7X_LANES:(s + 1) * V7X_LANES])


def _segments_to_rows(scr, slab0, n_slabs, seg_len):
    pitch = _seg_pitch(seg_len)
    return jnp.concatenate(
        [jnp.concatenate([scr[slab0 + s, i * pitch:i * pitch + seg_len, :]
                          for i in range(V7X_SUBLANES)], axis=0)
         for s in range(n_slabs)], axis=1)


def _seg_step(scr, slab, k, seg_len):
    return scr[slab, pl.ds(k, V7X_SUBLANES, stride=_seg_pitch(seg_len)), :]


def _seg_step_store(scr, slab, k, seg_len, v):
    scr[slab, pl.ds(k, V7X_SUBLANES, stride=_seg_pitch(seg_len)), :] = v


def _from_prev_segment(v, first):
    sub = lax.broadcasted_iota(jnp.int32, v.shape, 0)
    return jnp.where(sub == 0, first, pltpu.roll(v, 1, 0))


def _rglru_prompt_kernel(x_ref, g_ref, wgate_ref, win_ref, cw_ref, cb_ref, wa_ref, ba_ref,
                         wx_ref, bx_ref, lam_ref, wout_ref, y_ref, hlast_ref, ulast_ref,
                         ucarry, hcarry, *seg_scr, tm, sub_rows):
    j = pl.program_id(1)
    slabs_per_chunk = RG_BLOCK_W // V7X_LANES
    kw = cw_ref.shape[0]
    n_sub = tm // sub_rows
    u0_scr, u_scr, r_scr, ig_scr, hs_scr = (seg_scr[i * n_sub:(i + 1) * n_sub]
                                            for i in range(RG_SEG_ARRAYS))

    @pl.when(j == 0)
    def _():
        ucarry[...] = jnp.zeros(ucarry.shape, F32)
        hcarry[...] = jnp.zeros(hcarry.shape, F32)

    n_slabs = D_MODEL // V7X_LANES
    seg_len = sub_rows // V7X_SUBLANES
    gates, conv_prev, h_prev = {}, {}, {}

    def project_in(sub):
        x = x_ref[sub * sub_rows:(sub + 1) * sub_rows, :]
        h = _rms(x, g_ref[...]).astype(BF16)
        gate_chunks, last_rows = [], []
        for n in range(RG_BLOCKS):
            cs = slice(n * RG_BLOCK_W, (n + 1) * RG_BLOCK_W)
            gate_chunks.append(
                jax.nn.gelu(jnp.dot(h, wgate_ref[:, cs], preferred_element_type=F32)))
            u0 = jnp.dot(h, win_ref[:, cs], preferred_element_type=F32)
            last_rows.append(u0[sub_rows - V7X_SUBLANES:, :])
            _rows_to_segments(u0_scr[sub], n * slabs_per_chunk, u0, seg_len)
            yield
        gates[sub] = gate_chunks
        conv_prev[sub + 1] = jnp.concatenate(last_rows, axis=1)

    def conv(sub):
        for slab in range(n_slabs):
            lanes = slice(slab * V7X_LANES, (slab + 1) * V7X_LANES)
            taps = [cw_ref[t:t + 1, lanes] for t in range(kw)]
            bias = cb_ref[:, lanes]
            steps = {k: _seg_step(u0_scr[sub], slab, k, seg_len) for k in range(seg_len)}
            for back in range(1, kw):
                steps[-back] = _from_prev_segment(
                    steps[seg_len - back],
                    conv_prev[sub][V7X_SUBLANES - back:V7X_SUBLANES - back + 1, lanes])
            for k in range(seg_len):
                u_k = steps[k] * taps[kw - 1] + bias
                for back in range(1, kw):
                    u_k = u_k + steps[k - back] * taps[kw - 1 - back]
                _seg_step_store(u_scr[sub], slab, k, seg_len, u_k)
            yield

    def project_gates(sub):
        for n in range(RG_BLOCKS):
            slab0 = n * slabs_per_chunk
            ub = _segments_to_rows(u_scr[sub], slab0, slabs_per_chunk, seg_len).astype(BF16)
            _rows_to_segments(r_scr[sub], slab0,
                              jnp.dot(ub, wa_ref[n], preferred_element_type=F32), seg_len)
            _rows_to_segments(ig_scr[sub], slab0,
                              jnp.dot(ub, wx_ref[n], preferred_element_type=F32), seg_len)
            yield

    def recur(sub):
        h_prev[sub + 1] = []
        for slab in range(n_slabs):
            lanes = slice(slab * V7X_LANES, (slab + 1) * V7X_LANES)
            log_a_scale = RG_C * _log_sigmoid(lam_ref[:, lanes])
            ba = ba_ref[:, lanes]
            bx = bx_ref[:, lanes]
            h_in = h_prev[sub][slab]
            a_cum, h_loc = [], []
            for k in range(seg_len):
                r = _sigmoid(_seg_step(r_scr[sub], slab, k, seg_len) + ba)
                ig = _sigmoid(_seg_step(ig_scr[sub], slab, k, seg_len) + bx)
                log_a = log_a_scale * r
                a = jnp.exp(log_a)
                one_minus_a2 = -jnp.tanh(log_a) * (1.0 + a * a)
                mult = jnp.where(one_minus_a2 > 0.0, one_minus_a2 * lax.rsqrt(one_minus_a2), 0.0)
                b = mult * (ig * _seg_step(u_scr[sub], slab, k, seg_len))
                if k == 0:
                    a_cum.append(a)
                    h_loc.append(b)
                else:
                    a_cum.append(a * a_cum[-1])
                    h_loc.append(a * h_loc[-1] + b)
            a_seg, b_seg = _scan_groups(a_cum[-1], h_loc[-1])
            seg_end = a_seg * h_in + b_seg
            h_prev[sub + 1].append(seg_end[V7X_SUBLANES - 1:, :])
            h_start = _from_prev_segment(seg_end, h_in)
            for k in range(seg_len):
                _seg_step_store(hs_scr[sub], slab, k, seg_len, a_cum[k] * h_start + h_loc[k])
            yield

    def project_out(sub):
        rows = slice(sub * sub_rows, (sub + 1) * sub_rows)
        gate_chunks = gates.pop(sub)
        gated = []
        for n in range(RG_BLOCKS):
            hs = _segments_to_rows(hs_scr[sub], n * slabs_per_chunk, slabs_per_chunk, seg_len)
            if sub == n_sub - 1:
                hlast_ref[0, :, n * RG_BLOCK_W:(n + 1) * RG_BLOCK_W] = (
                    hs[sub_rows - V7X_SUBLANES:, :])
            gated.append((hs * gate_chunks[n]).astype(BF16))
        gated = jnp.concatenate(gated, axis=1)
        for n in range(RG_BLOCKS):
            cs = slice(n * RG_BLOCK_W, (n + 1) * RG_BLOCK_W)
            y_ref[rows, cs] = x_ref[rows, cs] + jnp.dot(gated, wout_ref[:, cs],
                                                        preferred_element_type=F32)
            yield

    conv_prev[0] = ucarry[...]
    h_prev[0] = [hcarry[V7X_SUBLANES - 1:V7X_SUBLANES, s * V7X_LANES:(s + 1) * V7X_LANES]
                 for s in range(n_slabs)]
    stages = [project_in, conv, project_gates, recur, project_out]
    for step in range(n_sub + len(stages) - 1):
        active = [stages[step - sub](sub) for sub in range(n_sub)
                  if 0 <= step - sub < len(stages)]
        while active:
            for piece in list(active):
                if next(piece, "done") == "done":
                    active.remove(piece)

    ulast_ref[0] = conv_prev[n_sub]
    ucarry[...] = conv_prev[n_sub]
    hcarry[...] = hlast_ref[0]


def _rglru_sample_kernel(x_ref, prev_ref, hinit_ref, g_ref, wgate_ref, win_ref, cw_ref, cb_ref,
                         wa_ref, ba_ref, wx_ref, bx_ref, lam_ref, wout_ref, y_ref, hs_ref, u_ref,
                         *, t_new):
    def conv_of(n, u0):
        c0 = n * RG_BLOCK_W
        return _dwconv_groups(u0, prev_ref[:, c0:c0 + RG_BLOCK_W], cw_ref, c0, RG_BLOCK_W)

    def scan_of(n, a, b):
        a_grp, b_grp = _scan_groups(a, b)
        return a_grp * hinit_ref[:, n * RG_BLOCK_W:(n + 1) * RG_BLOCK_W] + b_grp

    def u_sink(n, u0):
        u_ref[:, n * RG_BLOCK_W:(n + 1) * RG_BLOCK_W] = u0

    def h_sink(n, hs):
        hs_ref[:, n * RG_BLOCK_W:(n + 1) * RG_BLOCK_W] = hs

    y_ref[...] = _rglru_body(x_ref[...], g_ref, wgate_ref, win_ref, cw_ref, cb_ref, wa_ref, ba_ref,
                             wx_ref, bx_ref, lam_ref, wout_ref, conv_of, scan_of, u_sink, h_sink)


def _rglru_weight_specs(p):
    return [_const_spec(a.shape) for a in p]


def _rglru_prompt(x, params, batch, seq):
    tm = RG_PROMPT_TILE
    sub_rows = RG_ROW_TILE
    nt = seq // tm
    row = pl.BlockSpec((tm, D_MODEL), lambda b, j: (b * nt + j, 0))
    last = pl.BlockSpec((1, V7X_SUBLANES, D_MODEL), lambda b, j: (b, 0, 0))
    seg_scratch = pltpu.VMEM((D_MODEL // V7X_LANES,
                              V7X_SUBLANES * _seg_pitch(sub_rows // V7X_SUBLANES), V7X_LANES),
                             F32)
    return pl.pallas_call(
        functools.partial(_rglru_prompt_kernel, tm=tm, sub_rows=sub_rows),
        grid=(batch, nt),
        in_specs=[row] + _rglru_weight_specs(params),
        out_specs=[row, last, last],
        out_shape=[jax.ShapeDtypeStruct((batch * seq, D_MODEL), F32),
                   jax.ShapeDtypeStruct((batch, V7X_SUBLANES, D_MODEL), F32),
                   jax.ShapeDtypeStruct((batch, V7X_SUBLANES, D_MODEL), F32)],
        scratch_shapes=[pltpu.VMEM((V7X_SUBLANES, D_MODEL), F32),
                        pltpu.VMEM((V7X_SUBLANES, D_MODEL), F32)]
        + [seg_scratch] * (RG_SEG_ARRAYS * (tm // sub_rows)),
        compiler_params=pltpu.CompilerParams(
            dimension_semantics=("arbitrary", "arbitrary"),
            vmem_limit_bytes=_vmem_limit(48 * 1024 * 1024)),
        name="rglru_prompt",
    )(x, *params)


def _rglru_sample(x, row0, prev, hinit, params, t_new):
    m = prev.shape[0]
    tm = min(RG_ROW_TILE, m)
    row = pl.BlockSpec((tm, D_MODEL), lambda i: (i, 0))
    row_in = pl.BlockSpec((tm, D_MODEL), lambda i: (i + row0 // tm, 0))
    return pl.pallas_call(
        functools.partial(_rglru_sample_kernel, t_new=t_new),
        grid=(m // tm,),
        in_specs=[row_in, row, row] + _rglru_weight_specs(params),
        out_specs=[row, row, row],
        out_shape=[jax.ShapeDtypeStruct((m, D_MODEL), F32)] * 3,
        compiler_params=pltpu.CompilerParams(
            dimension_semantics=("arbitrary",),
            vmem_limit_bytes=_vmem_limit(48 * 1024 * 1024)),
        name="rglru_sample",
    )(x, prev, hinit, *params)


SCONV_CHUNK = V7X_MXU_DIM


def _sconv_body(x, g_ref, win_ref, cw_ref, wout_ref, conv_of, v_sink):
    h = _rms(x, g_ref[...]).astype(BF16)
    bcx = _mm(h, win_ref[...])
    gated = []
    for n in range(D_MODEL // SCONV_CHUNK):
        c0 = n * SCONV_CHUNK
        bg = bcx[:, c0:c0 + SCONV_CHUNK]
        cg = bcx[:, D_MODEL + c0:D_MODEL + c0 + SCONV_CHUNK]
        xv = bcx[:, 2 * D_MODEL + c0:2 * D_MODEL + c0 + SCONV_CHUNK]
        v = cg * xv
        v_sink(n, v)
        gated.append((bg * conv_of(n, v)).astype(BF16))
    return x + _mm(jnp.concatenate(gated, axis=1), wout_ref[...])


def _sconv_prompt_kernel(x_ref, g_ref, win_ref, cw_ref, wout_ref, y_ref, vlast_ref, vcarry,
                         *, tm):
    j = pl.program_id(1)

    @pl.when(j == 0)
    def _():
        vcarry[...] = jnp.zeros(vcarry.shape, F32)

    def conv_of(n, v):
        c0 = n * SCONV_CHUNK
        return _dwconv_rows(v, vcarry[:, c0:c0 + SCONV_CHUNK], cw_ref, c0, SCONV_CHUNK)

    def v_sink(n, v):
        vlast_ref[0, :, n * SCONV_CHUNK:(n + 1) * SCONV_CHUNK] = v[tm - V7X_SUBLANES:, :]

    y_ref[...] = _sconv_body(x_ref[...], g_ref, win_ref, cw_ref, wout_ref, conv_of, v_sink)
    vcarry[...] = vlast_ref[0]


def _sconv_sample_kernel(x_ref, prev_ref, g_ref, win_ref, cw_ref, wout_ref, y_ref, v_ref):
    def conv_of(n, v):
        c0 = n * SCONV_CHUNK
        return _dwconv_groups(v, prev_ref[:, c0:c0 + SCONV_CHUNK], cw_ref, c0, SCONV_CHUNK)

    def v_sink(n, v):
        v_ref[:, n * SCONV_CHUNK:(n + 1) * SCONV_CHUNK] = v

    y_ref[...] = _sconv_body(x_ref[...], g_ref, win_ref, cw_ref, wout_ref, conv_of, v_sink)


def _sconv_prompt(x, params, batch, seq):
    tm = ROW_TILE
    nt = seq // tm
    row = pl.BlockSpec((tm, D_MODEL), lambda b, j: (b * nt + j, 0))
    last = pl.BlockSpec((1, V7X_SUBLANES, D_MODEL), lambda b, j: (b, 0, 0))
    return pl.pallas_call(
        functools.partial(_sconv_prompt_kernel, tm=tm),
        grid=(batch, nt),
        in_specs=[row] + [_spec_of(a) for a in params],
        out_specs=[row, last],
        out_shape=[jax.ShapeDtypeStruct((batch * seq, D_MODEL), F32),
                   jax.ShapeDtypeStruct((batch, V7X_SUBLANES, D_MODEL), F32)],
        scratch_shapes=[pltpu.VMEM((V7X_SUBLANES, D_MODEL), F32)],
        compiler_params=pltpu.CompilerParams(
            dimension_semantics=("arbitrary", "arbitrary"),
            vmem_limit_bytes=_vmem_limit(48 * 1024 * 1024)),
        name="sconv_prompt",
    )(x, *[_arg_of(a) for a in params])


def _sconv_sample(x, row0, prev, params):
    m = prev.shape[0]
    tm = min(ROW_TILE, m)
    row = pl.BlockSpec((tm, D_MODEL), lambda i: (i, 0))
    row_in = pl.BlockSpec((tm, D_MODEL), lambda i: (i + row0 // tm, 0))
    return pl.pallas_call(
        _sconv_sample_kernel,
        grid=(m // tm,),
        in_specs=[row_in, row] + [_spec_of(a) for a in params],
        out_specs=[row, row],
        out_shape=[jax.ShapeDtypeStruct((m, D_MODEL), F32)] * 2,
        compiler_params=pltpu.CompilerParams(
            dimension_semantics=("arbitrary",),
            vmem_limit_bytes=_vmem_limit(48 * 1024 * 1024)),
        name="sconv_sample",
    )(x, prev, *[_arg_of(a) for a in params])


def _rope_tables(pos):
    half = HEAD_DIM // 2
    inv = ROPE_THETA ** (-jnp.arange(half, dtype=F32) / half)
    ang = pos.astype(F32)[:, None] * inv[None, :]
    cos = jnp.cos(ang)
    sin = jnp.sin(ang)
    reps = V7X_LANES // HEAD_DIM
    cos_t = jnp.tile(jnp.concatenate([cos, cos], axis=-1), (1, reps))
    sin_t = jnp.tile(jnp.concatenate([-sin, sin], axis=-1), (1, reps))
    return cos_t, sin_t


def _row(v):
    return v.reshape(1, -1)


def _pad_state_rows(buf):
    b, k, c = buf.shape
    padded = jnp.concatenate([jnp.zeros((b, V7X_SUBLANES - k, c), buf.dtype), buf], axis=1)
    return padded.reshape(b * V7X_SUBLANES, c)


def kernel(x_prompt, x_sample, cache_k, cache_v, state_rglru_h, state_rglru_conv, state_shortconv,
           norm_mixer, norm_ffn, norm_final,
           attn_w_qkv, attn_b_qkv, attn_w_o, attn_b_o, attn_sinks,
           rglru_w_gate, rglru_w_in, rglru_conv_w, rglru_conv_b, rglru_wa, rglru_ba,
           rglru_wx, rglru_bx, rglru_lambda, rglru_w_out,
           sconv_w_in, sconv_conv_w, sconv_w_out,
           ffn_w_gate, ffn_w_up, ffn_w_down):
    bp, seq, _ = x_prompt.shape
    bs, t_new, _ = x_sample.shape
    depth = norm_mixer.shape[0]
    past_len = 8192
    assert t_new == V7X_SUBLANES

    xp = x_prompt.reshape(bp * seq, D_MODEL)
    xs = x_sample.reshape(bs * t_new, D_MODEL)
    s_row0 = 0

    cos_p, sin_p = _rope_tables(jnp.arange(seq, dtype=jnp.int32))
    cos_s, sin_s = _rope_tables(past_len + jnp.arange(t_new, dtype=jnp.int32))
    cos_s = jnp.tile(cos_s, (SAMPLE_ATTN_BATCHES, 1))
    sin_s = jnp.tile(sin_s, (SAMPLE_ATTN_BATCHES, 1))

    to_lane_major = (0, 1, 3, 4, 2)
    from_lane_major = (0, 1, 4, 2, 3)
    ck_t = jnp.transpose(cache_k, to_lane_major)
    cv_t = jnp.transpose(cache_v, to_lane_major)
    nk_t = jnp.zeros(ck_t.shape, F32)
    nv_t = jnp.zeros(cv_t.shape, F32)

    kp_l, vp_l = [], []
    hp_l, hs_l, rcp_l, rcs_l = [], [], [], []
    scp_l, scs_l = [], []

    for i in range(depth):
        kind = i % 3
        j = i // 3
        g_mix = _row(norm_mixer[i])
        if kind == 0:
            wqkv = _layer(attn_w_qkv, j)
            wo = _layer(attn_w_o, j)
            bqkv = _row(attn_b_qkv[j])
            bo = _row(attn_b_o[j])
            sinks = attn_sinks[j]
            xp, kp, vp = _attn_prompt(xp, g_mix, wqkv, bqkv, wo, bo, cos_p, sin_p, sinks, bp, seq)
            xs, nk_t, nv_t = _attn_sample(xs, s_row0, g_mix, wqkv, bqkv, wo, bo, cos_s, sin_s,
                                          sinks, ck_t, cv_t, nk_t, nv_t, j, t_new)
            kp_l.append(kp.reshape(bp, WINDOW, N_KV_HEADS, HEAD_DIM))
            vp_l.append(vp.reshape(bp, WINDOW, N_KV_HEADS, HEAD_DIM))
        elif kind == 1:
            params = (g_mix, rglru_w_gate[j].astype(BF16), rglru_w_in[j].astype(BF16),
                      rglru_conv_w[j], _row(rglru_conv_b[j]), rglru_wa[j].astype(BF16),
                      _row(rglru_ba[j]), rglru_wx[j].astype(BF16), _row(rglru_bx[j]),
                      _row(rglru_lambda[j]), rglru_w_out[j].astype(BF16))
            xp, hlast, ulast = _rglru_prompt(xp, params, bp, seq)
            hp_l.append(hlast[:, V7X_SUBLANES - 1])
            rcp_l.append(ulast[:, V7X_SUBLANES - (RG_CONV_W - 1):])
            prev = _pad_state_rows(state_rglru_conv[j])
            hinit = jnp.repeat(state_rglru_h[j], t_new, axis=0)
            xs, hs_all, u_all = _rglru_sample(xs, s_row0, prev, hinit, params, t_new)
            hs_l.append(hs_all.reshape(bs, t_new, D_MODEL)[:, t_new - 1])
            rcs_l.append(u_all.reshape(bs, t_new, D_MODEL)[:, t_new - (RG_CONV_W - 1):])
        else:
            params = (g_mix, _layer(sconv_w_in, j), sconv_conv_w[j], _layer(sconv_w_out, j))
            xp, vlast = _sconv_prompt(xp, params, bp, seq)
            scp_l.append(vlast[:, V7X_SUBLANES - (SCONV_W - 1):])
            prev = _pad_state_rows(state_shortconv[j])
            xs, v_all = _sconv_sample(xs, s_row0, prev, params)
            scs_l.append(v_all.reshape(bs, t_new, D_MODEL)[:, t_new - (SCONV_W - 1):])

        wg = _layer(ffn_w_gate, i)
        wu = _layer(ffn_w_up, i)
        wd = _layer(ffn_w_down, i)
        d_ff = ffn_w_down.shape[1]
        g_ffn = _row(norm_ffn[i])
        g_fin = _row(norm_final)
        last = i == depth - 1
        if last:
            xp = _ffn(xp, bp * seq, None, g_ffn, wg, wu, wd, d_ff, g_fin, True)
            xs = _ffn(xs, bs * t_new, None, g_ffn, wg, wu, wd, d_ff, g_fin, True)
        else:
            xp = xs = _ffn(xp, bp * seq, xs, g_ffn, wg, wu, wd, d_ff, g_fin, False)
            s_row0 = bp * seq

    return (xp.reshape(bp, seq, D_MODEL), xs.reshape(bs, t_new, D_MODEL),
            jnp.stack(kp_l), jnp.stack(vp_l),
            jnp.transpose(nk_t, from_lane_major), jnp.transpose(nv_t, from_lane_major),
            jnp.stack(hp_l), jnp.stack(hs_l), jnp.stack(rcp_l), jnp.stack(rcs_l),
            jnp.stack(scp_l), jnp.stack(scs_l))
```

```python
import functools

import jax
import jax.numpy as jnp
from jax import lax
from jax.experimental import pallas as pl
from jax.experimental.pallas import tpu as pltpu

D_MODEL = 1024
HEAD_DIM = 64
N_HEADS = 16
N_KV_HEADS = 2
GQA_GROUP = 8
Q_DIM = N_HEADS * HEAD_DIM
KV_DIM = N_KV_HEADS * HEAD_DIM
WINDOW = 128
ROPE_THETA = 10000.0
NEG_INF = -1e30
RG_BLOCKS = 4
RG_BLOCK_W = 256
RG_CONV_W = 4
RG_C = 8.0
SCONV_W = 3
EPS = 1e-6

V7X_LANES = 128
V7X_SUBLANES = 8
V7X_MXU_DIM = 256
V7X_VMEM_BYTES = 64 * 1024 * 1024

BF16 = jnp.bfloat16
F32 = jnp.float32

FFN_CHUNK = V7X_MXU_DIM
ROW_TILE = 512
ATTN_ROW_TILE = 1024
RG_ROW_TILE = 256
RG_PROMPT_TILE = 512
RG_SEG_ARRAYS = 5
SAMPLE_ATTN_BATCHES = 32
SAMPLE_ATTN_UNROLL = 8
ATTN_LOOKAHEAD = 2
PROMPT_ATTN_COLS = 2
COLS_PER_KV = (N_HEADS // N_KV_HEADS) * HEAD_DIM // V7X_LANES


def _vmem_limit(nbytes):
    return int(min(nbytes, V7X_VMEM_BYTES - 8 * 1024 * 1024))


def _const_spec(shape):
    nd = len(shape)
    return pl.BlockSpec(shape, lambda *_: (0,) * nd, pipeline_mode=pl.Buffered(1))


def _layer(stacked, layer):
    return (stacked, layer)


def _spec_of(w):
    if isinstance(w, tuple):
        stacked, layer = w
        nd = stacked.ndim - 1
        return pl.BlockSpec((None,) + stacked.shape[1:], lambda *_: (layer,) + (0,) * nd,
                            pipeline_mode=pl.Buffered(1))
    return _const_spec(w.shape)


def _arg_of(w):
    return w[0] if isinstance(w, tuple) else w


def _nbytes_of(w):
    if isinstance(w, tuple):
        return w[0][0].size * w[0].dtype.itemsize
    return w.size * w.dtype.itemsize


def _rms(x, g):
    ms = jnp.mean(x * x, axis=-1, keepdims=True)
    return x * lax.rsqrt(ms + EPS) * g


def _mm(a, w):
    return jnp.dot(a.astype(BF16), w.astype(BF16), preferred_element_type=F32)


def _ffn_kernel(*refs, n_chunks, final_norm, tiles_a, stacked, layer):
    if stacked:
        xa_ref, xb_ref = refs[:2]
        refs = refs[2:]
    else:
        xa_ref = refs[0]
        refs = refs[1:]
    g_ref, wg_hbm, wu_hbm, wd_hbm, gf_ref, o_ref, wg_ref, wu_ref, wd_ref, sem = refs
    step = pl.program_id(0)

    def chunk_copies(c):
        cs = slice(c * FFN_CHUNK, (c + 1) * FFN_CHUNK)
        return (pltpu.make_async_copy(wg_hbm.at[layer, :, cs], wg_ref.at[:, cs], sem.at[0, c]),
                pltpu.make_async_copy(wu_hbm.at[layer, :, cs], wu_ref.at[:, cs], sem.at[1, c]),
                pltpu.make_async_copy(wd_hbm.at[layer, cs, :], wd_ref.at[cs, :], sem.at[2, c]))

    def run(fetch_weights):
        if fetch_weights:
            for c in range(n_chunks):
                for copy in chunk_copies(c):
                    copy.start()
        if stacked:
            x = jnp.where(step < tiles_a, xa_ref[...], xb_ref[...])
        else:
            x = xa_ref[...]
        h = _rms(x, g_ref[...]).astype(BF16)
        acc = x
        for c in range(n_chunks):
            cs = slice(c * FFN_CHUNK, (c + 1) * FFN_CHUNK)
            if fetch_weights:
                for copy in chunk_copies(c):
                    copy.wait()
            gate = _mm(h, wg_ref[:, cs])
            up = _mm(h, wu_ref[:, cs])
            act = (gate * jax.nn.sigmoid(gate)) * up
            acc = acc + _mm(act, wd_ref[cs, :])
        if final_norm:
            acc = _rms(acc, gf_ref[...])
        o_ref[...] = acc

    @pl.when(step == 0)
    def _():
        run(True)

    @pl.when(step != 0)
    def _():
        run(False)


def _ffn(xa, rows_a, xb, g, wg, wu, wd, d_ff, gf, final_norm):
    tm = ROW_TILE
    tiles_a = rows_a // tm
    stacked = xb is not None
    tiles_b = xb.shape[0] // tm if stacked else 0
    n_chunks = d_ff // FFN_CHUNK
    out_row = pl.BlockSpec((tm, D_MODEL), lambda i: (i, 0))
    if stacked:
        x_specs = [pl.BlockSpec((tm, D_MODEL), lambda i: (jnp.minimum(i, tiles_a - 1), 0)),
                   pl.BlockSpec((tm, D_MODEL), lambda i: (jnp.maximum(i - tiles_a, 0), 0),
                                pipeline_mode=pl.Buffered(1))]
        x_args = [xa, xb]
    else:
        x_specs = [out_row]
        x_args = [xa]
    (wg_all, layer), (wu_all, _), (wd_all, _) = wg, wu, wd
    weight_bytes = _nbytes_of(wg) + _nbytes_of(wu) + _nbytes_of(wd)
    in_hbm = pl.BlockSpec(memory_space=pl.ANY)
    return pl.pallas_call(
        functools.partial(_ffn_kernel, n_chunks=n_chunks, final_norm=final_norm,
                          tiles_a=tiles_a, stacked=stacked, layer=layer),
        grid=(tiles_a + tiles_b,),
        in_specs=x_specs + [_const_spec((1, D_MODEL)), in_hbm, in_hbm, in_hbm,
                            _const_spec((1, D_MODEL))],
        out_specs=out_row,
        out_shape=jax.ShapeDtypeStruct(((tiles_a + tiles_b) * tm, D_MODEL), F32),
        scratch_shapes=[pltpu.VMEM(wg_all.shape[1:], wg_all.dtype),
                        pltpu.VMEM(wu_all.shape[1:], wu_all.dtype),
                        pltpu.VMEM(wd_all.shape[1:], wd_all.dtype),
                        pltpu.SemaphoreType.DMA((3, n_chunks))],
        compiler_params=pltpu.CompilerParams(
            dimension_semantics=("arbitrary",),
            vmem_limit_bytes=_vmem_limit(weight_bytes + 24 * tm * D_MODEL * 4)),
        name="ffn",
    )(*x_args, g, wg_all, wu_all, wd_all, gf)


def _rope_cols(cols, cos, sin_signed, first_half):
    swapped = jnp.where(first_half,
                        pltpu.roll(cols, V7X_LANES - HEAD_DIM // 2, 1),
                        pltpu.roll(cols, HEAD_DIM // 2, 1))
    return cols * cos + swapped * sin_signed


def _pair_blockdiag(mat, mat_rolled, kv, lane_lt64):
    zero = jnp.zeros_like(mat)
    if kv == 0:
        top = jnp.where(lane_lt64, mat, zero)
        bottom = jnp.where(lane_lt64, zero, mat_rolled)
    else:
        top = jnp.where(lane_lt64, mat_rolled, zero)
        bottom = jnp.where(lane_lt64, zero, mat)
    return jnp.concatenate([top, bottom], axis=0).astype(BF16)


def _key_blockdiag(k2, kv):
    lane_lt64 = lax.broadcasted_iota(jnp.int32, k2.shape, 1) < HEAD_DIM
    return _pair_blockdiag(k2, pltpu.roll(k2, HEAD_DIM, 1), kv, lane_lt64)


def _value_blockdiag(v2, kv):
    v2 = jnp.where(lax.broadcasted_iota(jnp.int32, v2.shape, 0) == 0, 0.0, v2)
    vbd = _key_blockdiag(v2, kv)
    row = lax.broadcasted_iota(jnp.int32, vbd.shape, 0)
    lane = lax.broadcasted_iota(jnp.int32, vbd.shape, 1)
    ones_bd = jnp.where((lane < HEAD_DIM) == (row < 2 * WINDOW), 1.0, 0.0).astype(BF16)
    return jnp.concatenate([vbd, ones_bd], axis=1)


def _attn_weights(s_all, mask, sink_ref, kv, cols, tq):
    nk = 2 * WINDOW
    key = lax.broadcasted_iota(jnp.int32, (1, nk), 1)
    e_rows = []
    for i, c in enumerate(cols):
        e_halves = []
        for parity in range(2):
            sink = sink_ref[kv * GQA_GROUP + 2 * c + parity]
            fill = jnp.where(key == 0, sink, NEG_INF)
            s = s_all[i * tq:(i + 1) * tq, parity * nk:(parity + 1) * nk]
            s = jnp.where(mask, s, fill)
            m = jnp.max(s, axis=-1, keepdims=True)
            e_halves.append(jnp.exp(s - m))
        e_rows.append(jnp.concatenate(e_halves, axis=1))
    return jnp.concatenate(e_rows, axis=0).astype(BF16)


def _key_blockdiag_t(kt, kv):
    del kv
    zero = jnp.zeros_like(kt)
    return jnp.concatenate([jnp.concatenate([kt, zero], axis=1),
                            jnp.concatenate([zero, kt], axis=1)], axis=0).astype(BF16)


def _value_blockdiag_t(vt, kv):
    del kv
    vt = jnp.where(lax.broadcasted_iota(jnp.int32, vt.shape, 1) == 0, 0.0, vt)
    zero = jnp.zeros_like(vt)
    one = jnp.ones_like(vt)
    return jnp.concatenate([jnp.concatenate([vt, zero], axis=1),
                            jnp.concatenate([zero, vt], axis=1),
                            jnp.concatenate([one, zero], axis=1),
                            jnp.concatenate([zero, one], axis=1)], axis=0).astype(BF16)


def _dot_nt(a, b):
    return lax.dot_general(a, b, (((1,), (1,)), ((), ())), preferred_element_type=F32)


def _dot_nn(a, b):
    return jnp.dot(a, b, preferred_element_type=F32)


ROW_MAJOR_KV = (_key_blockdiag, _dot_nt, _value_blockdiag, _dot_nn)
LANE_MAJOR_KV = (_key_blockdiag_t, _dot_nn, _value_blockdiag_t, _dot_nt)


def _attend_units(units, sink_ref, tq, kv_ops):
    make_kbd, score_dot, make_vbd, value_dot = kv_ops
    n = len(units)
    scores, kbd, vbd = {}, {}, {}

    def issue_scores(i):
        load_q, load_k, _, _, kv, _, _, kv_id = units[i]
        if kv_id not in kbd:
            kbd[kv_id] = make_kbd(load_k(), kv)
        scores[i] = score_dot(load_q(), kbd[kv_id])

    for i in range(min(ATTN_LOOKAHEAD, n)):
        issue_scores(i)
    for i in range(n):
        if i + ATTN_LOOKAHEAD < n:
            issue_scores(i + ATTN_LOOKAHEAD)
        _, _, load_v, mask, kv, cols, store_o, kv_id = units[i]
        if kv_id not in vbd:
            vbd[kv_id] = make_vbd(load_v(), kv)
        e_all = _attn_weights(scores.pop(i), mask, sink_ref, kv, cols, tq)
        o_den = value_dot(e_all, vbd[kv_id])
        store_o(o_den[:, :V7X_LANES] * (1.0 / o_den[:, V7X_LANES:]))


def _band_mask(tq, col_min):
    row = lax.broadcasted_iota(jnp.int32, (tq, 2 * WINDOW), 0)
    col = lax.broadcasted_iota(jnp.int32, (tq, 2 * WINDOW), 1)
    prev_ok = (col < WINDOW) & (col > row)
    own_ok = (col >= WINDOW) & (col - WINDOW <= row)
    return (prev_ok | own_ok) & (col >= col_min)


def _attn_prompt_kernel(x_ref, g_ref, wqkv_ref, bqkv_ref, wo_ref, bo_ref, cos_ref, sin_ref,
                        sink_ref, y_ref, knew_ref, vnew_ref,
                        q_scr, k_scr, v_scr, a_scr, *, tq_tile, n_tiles):
    j = pl.program_id(1)
    n_blk = tq_tile // WINDOW

    @pl.when(j == 0)
    def _():
        k_scr[0:WINDOW, :] = jnp.zeros((WINDOW, V7X_LANES), F32)
        v_scr[0:WINDOW, :] = jnp.zeros((WINDOW, V7X_LANES), F32)

    x = x_ref[...]
    h = _rms(x, g_ref[...])
    qkv = _mm(h, wqkv_ref[...]) + bqkv_ref[...]
    cos = cos_ref[...]
    sin = sin_ref[...]
    lane = lax.broadcasted_iota(jnp.int32, (tq_tile, V7X_LANES), 1)
    first_half = (lane % HEAD_DIM) < (HEAD_DIM // 2)
    scale = HEAD_DIM ** -0.5
    for c in range(Q_DIM // V7X_LANES):
        qc = _rope_cols(qkv[:, c * V7X_LANES:(c + 1) * V7X_LANES], cos, sin, first_half)
        q_scr[:, c * V7X_LANES:(c + 1) * V7X_LANES] = (qc * scale).astype(BF16)
    k_new = _rope_cols(qkv[:, Q_DIM:Q_DIM + KV_DIM], cos, sin, first_half)
    v_new = qkv[:, Q_DIM + KV_DIM:]
    k_scr[WINDOW:, :] = k_new
    v_scr[WINDOW:, :] = v_new

    def unit(r0, mask, kv, cols):
        def load_q():
            return jnp.concatenate(
                [q_scr[pl.ds(r0, WINDOW),
                       (kv * COLS_PER_KV + c) * V7X_LANES:(kv * COLS_PER_KV + c + 1) * V7X_LANES]
                 for c in cols], axis=0)

        def store_o(o):
            for i, c in enumerate(cols):
                col = kv * COLS_PER_KV + c
                a_scr[pl.ds(r0, WINDOW), col * V7X_LANES:(col + 1) * V7X_LANES] = (
                    o[i * WINDOW:(i + 1) * WINDOW].astype(BF16))

        return (load_q, lambda: k_scr[pl.ds(r0, 2 * WINDOW), :],
                lambda: v_scr[pl.ds(r0, 2 * WINDOW), :], mask, kv, cols, store_o, kv)

    col_groups = [tuple(range(c, c + PROMPT_ATTN_COLS))
                  for c in range(0, COLS_PER_KV, PROMPT_ATTN_COLS)]

    def block(blk, carry):
        r0 = pl.multiple_of(blk * WINDOW, WINDOW)
        first = jnp.logical_and(j == 0, blk == 0)
        mask = _band_mask(WINDOW, jnp.where(first, WINDOW, 0))
        _attend_units([unit(r0, mask, kv, cols) for kv in range(N_KV_HEADS)
                       for cols in col_groups], sink_ref, WINDOW, ROW_MAJOR_KV)
        return carry

    lax.fori_loop(0, n_blk, block, 0)

    y_ref[...] = _mm(a_scr[...], wo_ref[...]) + bo_ref[...] + x

    k_scr[0:WINDOW, :] = k_new[tq_tile - WINDOW:, :]
    v_scr[0:WINDOW, :] = v_new[tq_tile - WINDOW:, :]

    @pl.when(j == n_tiles - 1)
    def _():
        knew_ref[0] = k_new[tq_tile - WINDOW:, :]
        vnew_ref[0] = v_new[tq_tile - WINDOW:, :]


def _attn_prompt(x, g, wqkv, bqkv, wo, bo, cos_t, sin_t, sinks, batch, seq):
    tq = ATTN_ROW_TILE
    nt = seq // tq
    row = pl.BlockSpec((tq, D_MODEL), lambda b, j: (b * nt + j, 0))
    tab = pl.BlockSpec((tq, V7X_LANES), lambda b, j: (j, 0))
    cache = pl.BlockSpec((1, WINDOW, V7X_LANES), lambda b, j: (b, 0, 0))
    return pl.pallas_call(
        functools.partial(_attn_prompt_kernel, tq_tile=tq, n_tiles=nt),
        grid=(batch, nt),
        in_specs=[row, _const_spec((1, D_MODEL)), _spec_of(wqkv),
                  _const_spec((1, Q_DIM + 2 * KV_DIM)), _spec_of(wo), _const_spec((1, D_MODEL)),
                  tab, tab, pl.BlockSpec(memory_space=pltpu.SMEM)],
        out_specs=[row, cache, cache],
        out_shape=[jax.ShapeDtypeStruct((batch * seq, D_MODEL), F32),
                   jax.ShapeDtypeStruct((batch, WINDOW, V7X_LANES), F32),
                   jax.ShapeDtypeStruct((batch, WINDOW, V7X_LANES), F32)],
        scratch_shapes=[pltpu.VMEM((tq, Q_DIM), BF16),
                        pltpu.VMEM((WINDOW + tq, V7X_LANES), F32),
                        pltpu.VMEM((WINDOW + tq, V7X_LANES), F32),
                        pltpu.VMEM((tq, Q_DIM), BF16)],
        compiler_params=pltpu.CompilerParams(
            dimension_semantics=("arbitrary", "arbitrary"),
            vmem_limit_bytes=_vmem_limit(48 * 1024 * 1024)),
        name="attn_prompt",
    )(x, g, _arg_of(wqkv), bqkv, _arg_of(wo), bo, cos_t, sin_t, sinks)


def _attn_sample_kernel(x_ref, g_ref, wqkv_ref, bqkv_ref, wo_ref, bo_ref, cos_ref, sin_ref,
                        sink_ref, ck_ref, cv_ref, nk_in, nv_in, y_ref, nk_ref, nv_ref,
                        q_scr, k_scr, v_scr, a_scr, *, n_seq, t_new):
    del nk_in, nv_in
    tm = n_seq * t_new
    x = x_ref[...]
    h = _rms(x, g_ref[...])
    qkv = _mm(h, wqkv_ref[...]) + bqkv_ref[...]
    cos = cos_ref[...]
    sin = sin_ref[...]
    lane = lax.broadcasted_iota(jnp.int32, (tm, V7X_LANES), 1)
    first_half = (lane % HEAD_DIM) < (HEAD_DIM // 2)
    scale = HEAD_DIM ** -0.5
    for c in range(Q_DIM // V7X_LANES):
        qc = _rope_cols(qkv[:, c * V7X_LANES:(c + 1) * V7X_LANES], cos, sin, first_half)
        q_scr[:, c * V7X_LANES:(c + 1) * V7X_LANES] = qc * scale
    k_scr[...] = _rope_cols(qkv[:, Q_DIM:Q_DIM + KV_DIM], cos, sin, first_half)
    v_scr[...] = qkv[:, Q_DIM + KV_DIM:]
    mask = _band_mask(t_new, 0)
    pad = jnp.zeros((WINDOW - t_new, V7X_LANES), F32)
    lane = lax.broadcasted_iota(jnp.int32, (HEAD_DIM, WINDOW), 1)

    def new_rows_t(scr, b):
        r0 = pl.multiple_of(b * t_new, t_new)
        return jnp.concatenate([scr[pl.ds(r0, t_new), :], pad], axis=0).T

    def seq_group(i, carry):
        seqs = [i * SAMPLE_ATTN_UNROLL + u for u in range(SAMPLE_ATTN_UNROLL)]
        k_new_t = [new_rows_t(k_scr, b) for b in seqs]
        v_new_t = [new_rows_t(v_scr, b) for b in seqs]

        def unit(u, b, kv):
            r0 = pl.multiple_of(b * t_new, t_new)
            head_rows = slice(kv * HEAD_DIM, (kv + 1) * HEAD_DIM)

            def load_q():
                return jnp.concatenate(
                    [q_scr[pl.ds(r0, t_new), (kv * COLS_PER_KV + c) * V7X_LANES:
                           (kv * COLS_PER_KV + c + 1) * V7X_LANES]
                     for c in range(COLS_PER_KV)], axis=0).astype(BF16)

            def load_k():
                return jnp.concatenate([ck_ref[b, kv], k_new_t[u][head_rows, :]], axis=1)

            def load_v():
                return jnp.concatenate([cv_ref[b, kv], v_new_t[u][head_rows, :]], axis=1)

            def store_o(o):
                for c in range(COLS_PER_KV):
                    col = kv * COLS_PER_KV + c
                    a_scr[pl.ds(r0, t_new), col * V7X_LANES:(col + 1) * V7X_LANES] = (
                        o[c * t_new:(c + 1) * t_new])

            return (load_q, load_k, load_v, mask, kv, tuple(range(COLS_PER_KV)), store_o,
                    (u, kv))

        _attend_units([unit(u, b, kv) for u, b in enumerate(seqs) for kv in range(N_KV_HEADS)],
                      sink_ref, t_new, LANE_MAJOR_KV)
        keep = lane < WINDOW - t_new
        for u, b in enumerate(seqs):
            for kv in range(N_KV_HEADS):
                head_rows = slice(kv * HEAD_DIM, (kv + 1) * HEAD_DIM)
                nk_ref[b, kv] = jnp.where(
                    keep, pltpu.roll(ck_ref[b, kv], WINDOW - t_new, 1),
                    pltpu.roll(k_new_t[u][head_rows, :], WINDOW - t_new, 1))
                nv_ref[b, kv] = jnp.where(
                    keep, pltpu.roll(cv_ref[b, kv], WINDOW - t_new, 1),
                    pltpu.roll(v_new_t[u][head_rows, :], WINDOW - t_new, 1))
        return carry

    lax.fori_loop(0, n_seq // SAMPLE_ATTN_UNROLL, seq_group, 0)
    y_ref[...] = _mm(a_scr[...], wo_ref[...]) + bo_ref[...] + x


def _attn_sample(x, row0, g, wqkv, bqkv, wo, bo, cos_t, sin_t, sinks, ck_t, cv_t, nk_t, nv_t,
                 layer, t_new):
    bg = SAMPLE_ATTN_BATCHES
    tm = bg * t_new
    n_batch = ck_t.shape[1]
    row = pl.BlockSpec((tm, D_MODEL), lambda i: (i, 0))
    row_in = pl.BlockSpec((tm, D_MODEL), lambda i: (i + row0 // tm, 0))
    cache = pl.BlockSpec((None, bg, N_KV_HEADS, HEAD_DIM, WINDOW), lambda i: (layer, i, 0, 0, 0))
    whole = pl.BlockSpec(memory_space=pl.ANY)
    n_in = 13
    return pl.pallas_call(
        functools.partial(_attn_sample_kernel, n_seq=bg, t_new=t_new),
        grid=(n_batch // bg,),
        in_specs=[row_in, _const_spec((1, D_MODEL)), _spec_of(wqkv),
                  _const_spec((1, Q_DIM + 2 * KV_DIM)), _spec_of(wo), _const_spec((1, D_MODEL)),
                  _const_spec((tm, V7X_LANES)), _const_spec((tm, V7X_LANES)),
                  pl.BlockSpec(memory_space=pltpu.SMEM), cache, cache, whole, whole],
        out_specs=[row, cache, cache],
        out_shape=[jax.ShapeDtypeStruct((n_batch * t_new, D_MODEL), F32),
                   jax.ShapeDtypeStruct(nk_t.shape, F32),
                   jax.ShapeDtypeStruct(nv_t.shape, F32)],
        input_output_aliases={n_in - 2: 1, n_in - 1: 2},
        scratch_shapes=[pltpu.VMEM((tm, Q_DIM), F32),
                        pltpu.VMEM((tm, V7X_LANES), F32),
                        pltpu.VMEM((tm, V7X_LANES), F32),
                        pltpu.VMEM((tm, Q_DIM), F32)],
        compiler_params=pltpu.CompilerParams(
            dimension_semantics=("arbitrary",),
            vmem_limit_bytes=_vmem_limit(48 * 1024 * 1024)),
        name="attn_sample",
    )(x, g, _arg_of(wqkv), bqkv, _arg_of(wo), bo, cos_t, sin_t, sinks, ck_t, cv_t, nk_t, nv_t)


def _dwconv_groups(u, prev, w_ref, col0, ncol):
    rows = u.shape[0]
    kw = w_ref.shape[0]
    sub = lax.broadcasted_iota(jnp.int32, u.shape, 0) % V7X_SUBLANES
    y = u * w_ref[kw - 1:kw, col0:col0 + ncol]
    for s in range(1, kw):
        from_prev = pltpu.roll(prev, (rows - V7X_SUBLANES + s) % rows, 0)
        from_self = pltpu.roll(u, s, 0)
        shifted = jnp.where(sub < s, from_prev, from_self)
        y = y + shifted * w_ref[kw - 1 - s:kw - s, col0:col0 + ncol]
    return y


def _dwconv_rows(u, carry8, w_ref, col0, ncol):
    kw = w_ref.shape[0]
    sub8 = lax.broadcasted_iota(jnp.int32, carry8.shape, 0)
    y = u * w_ref[kw - 1:kw, col0:col0 + ncol]
    for s in range(1, kw):
        rolled = pltpu.roll(u, s, 0)
        head = jnp.where(sub8 < s, pltpu.roll(carry8, s, 0), rolled[:V7X_SUBLANES])
        shifted = jnp.concatenate([head, rolled[V7X_SUBLANES:]], axis=0)
        y = y + shifted * w_ref[kw - 1 - s:kw - s, col0:col0 + ncol]
    return y


def _scan_groups(a, b):
    pos = lax.broadcasted_iota(jnp.int32, a.shape, 0) % V7X_SUBLANES
    shift = 1
    while shift < V7X_SUBLANES:
        ok = pos >= shift
        a_sh = jnp.where(ok, pltpu.roll(a, shift, 0), 1.0)
        b_sh = jnp.where(ok, pltpu.roll(b, shift, 0), 0.0)
        b = a * b_sh + b
        a = a * a_sh
        shift *= 2
    return a, b


def _scan_rows(a, b, h_row):
    a_grp, b_grp = _scan_groups(a, b)
    out = []
    h = h_row
    for g in range(a.shape[0] // V7X_SUBLANES):
        rows = slice(g * V7X_SUBLANES, (g + 1) * V7X_SUBLANES)
        hs = a_grp[rows] * h + b_grp[rows]
        out.append(hs)
        h = hs[V7X_SUBLANES - 1:, :]
    return jnp.concatenate(out, axis=0)


def _log_sigmoid(x):
    return jnp.minimum(x, 0.0) - jnp.log1p(jnp.exp(-jnp.abs(x)))


def _sigmoid(x):
    return 0.5 * jnp.tanh(0.5 * x) + 0.5


def _rglru_body(x, g_ref, wgate_ref, win_ref, cw_ref, cb_ref, wa_ref, ba_ref, wx_ref, bx_ref,
                lam_ref, wout_ref, conv_of, scan_of, u_sink, h_sink):
    h = _rms(x, g_ref[...]).astype(BF16)
    acc = x
    for n in range(RG_BLOCKS):
        c0 = n * RG_BLOCK_W
        cs = slice(c0, c0 + RG_BLOCK_W)
        gate = jax.nn.gelu(jnp.dot(h, wgate_ref[:, cs], preferred_element_type=F32))
        u0 = jnp.dot(h, win_ref[:, cs], preferred_element_type=F32)
        u_sink(n, u0)
        u = conv_of(n, u0) + cb_ref[:, cs]
        ub = u.astype(BF16)
        r = jax.nn.sigmoid(jnp.dot(ub, wa_ref[n], preferred_element_type=F32) + ba_ref[:, cs])
        ig = jax.nn.sigmoid(jnp.dot(ub, wx_ref[n], preferred_element_type=F32) + bx_ref[:, cs])
        log_a = RG_C * r * _log_sigmoid(lam_ref[:, cs])
        a = jnp.exp(log_a)
        mult = jnp.sqrt(-jnp.tanh(log_a) * (1.0 + a * a))
        hs = scan_of(n, a, mult * (ig * u))
        h_sink(n, hs)
        acc = acc + jnp.dot((hs * gate).astype(BF16), wout_ref[cs, :], preferred_element_type=F32)
    return acc


def _seg_pitch(seg_len):
    return seg_len + V7X_SUBLANES if (seg_len // V7X_SUBLANES) % 2 == 0 else seg_len


def _rows_to_segments(scr, slab0, x, seg_len):
    pitch = _seg_pitch(seg_len)
    for s in range(x.shape[1] // V7X_LANES):
        for i in range(V7X_SUBLANES):
            scr[slab0 + s, i * pitch:i * pitch + seg_len, :] = (
                x[i * seg_len:(i + 1) * seg_len, s * V7X_LANES:(s + 1) * V7X_LANES])


def _segments_to_rows(scr, slab0, n_slabs, seg_len):
    pitch = _seg_pitch(seg_len)
    return jnp.concatenate(
        [jnp.concatenate([scr[slab0 + s, i * pitch:i * pitch + seg_len, :]
                          for i in range(V7X_SUBLANES)], axis=0)
         for s in range(n_slabs)], axis=1)


def _seg_step(scr, slab, k, seg_len):
    return scr[slab, pl.ds(k, V7X_SUBLANES, stride=_seg_pitch(seg_len)), :]


def _seg_step_store(scr, slab, k, seg_len, v):
    scr[slab, pl.ds(k, V7X_SUBLANES, stride=_seg_pitch(seg_len)), :] = v


def _from_prev_segment(v, first):
    sub = lax.broadcasted_iota(jnp.int32, v.shape, 0)
    return jnp.where(sub == 0, first, pltpu.roll(v, 1, 0))


def _rglru_prompt_kernel(x_ref, g_ref, wgate_ref, win_ref, cw_ref, cb_ref, wa_ref, ba_ref,
                         wx_ref, bx_ref, lam_ref, wout_ref, y_ref, hlast_ref, ulast_ref,
                         ucarry, hcarry, *seg_scr, tm, sub_rows):
    j = pl.program_id(1)
    slabs_per_chunk = RG_BLOCK_W // V7X_LANES
    kw = cw_ref.shape[0]
    n_sub = tm // sub_rows
    u0_scr, u_scr, r_scr, ig_scr, hs_scr = (seg_scr[i * n_sub:(i + 1) * n_sub]
                                            for i in range(RG_SEG_ARRAYS))

    @pl.when(j == 0)
    def _():
        ucarry[...] = jnp.zeros(ucarry.shape, F32)
        hcarry[...] = jnp.zeros(hcarry.shape, F32)

    n_slabs = D_MODEL // V7X_LANES
    seg_len = sub_rows // V7X_SUBLANES
    gates, conv_prev, h_prev = {}, {}, {}

    def project_in(sub):
        x = x_ref[sub * sub_rows:(sub + 1) * sub_rows, :]
        h = _rms(x, g_ref[...]).astype(BF16)
        gate_chunks, last_rows = [], []
        for n in range(RG_BLOCKS):
            cs = slice(n * RG_BLOCK_W, (n + 1) * RG_BLOCK_W)
            gate_chunks.append(
                jax.nn.gelu(jnp.dot(h, wgate_ref[:, cs], preferred_element_type=F32)))
            u0 = jnp.dot(h, win_ref[:, cs], preferred_element_type=F32)
            last_rows.append(u0[sub_rows - V7X_SUBLANES:, :])
            _rows_to_segments(u0_scr[sub], n * slabs_per_chunk, u0, seg_len)
            yield
        gates[sub] = gate_chunks
        conv_prev[sub + 1] = jnp.concatenate(last_rows, axis=1)

    def conv(sub):
        for slab in range(n_slabs):
            lanes = slice(slab * V7X_LANES, (slab + 1) * V7X_LANES)
            taps = [cw_ref[t:t + 1, lanes] for t in range(kw)]
            bias = cb_ref[:, lanes]
            steps = {k: _seg_step(u0_scr[sub], slab, k, seg_len) for k in range(seg_len)}
            for back in range(1, kw):
                steps[-back] = _from_prev_segment(
                    steps[seg_len - back],
                    conv_prev[sub][V7X_SUBLANES - back:V7X_SUBLANES - back + 1, lanes])
            for k in range(seg_len):
                u_k = steps[k] * taps[kw - 1] + bias
                for back in range(1, kw):
                    u_k = u_k + steps[k - back] * taps[kw - 1 - back]
                _seg_step_store(u_scr[sub], slab, k, seg_len, u_k)
            yield

    def project_gates(sub):
        for n in range(RG_BLOCKS):
            slab0 = n * slabs_per_chunk
            ub = _segments_to_rows(u_scr[sub], slab0, slabs_per_chunk, seg_len).astype(BF16)
            _rows_to_segments(r_scr[sub], slab0,
                              jnp.dot(ub, wa_ref[n], preferred_element_type=F32), seg_len)
            _rows_to_segments(ig_scr[sub], slab0,
                              jnp.dot(ub, wx_ref[n], preferred_element_type=F32), seg_len)
            yield

    def recur(sub):
        h_prev[sub + 1] = []
        for slab in range(n_slabs):
            lanes = slice(slab * V7X_LANES, (slab + 1) * V7X_LANES)
            log_a_scale = RG_C * _log_sigmoid(lam_ref[:, lanes])
            ba = ba_ref[:, lanes]
            bx = bx_ref[:, lanes]
            h_in = h_prev[sub][slab]
            a_cum, h_loc = [], []
            for k in range(seg_len):
                r = _sigmoid(_seg_step(r_scr[sub], slab, k, seg_len) + ba)
                ig = _sigmoid(_seg_step(ig_scr[sub], slab, k, seg_len) + bx)
                log_a = log_a_scale * r
                a = jnp.exp(log_a)
                one_minus_a2 = -jnp.tanh(log_a) * (1.0 + a * a)
                mult = jnp.where(one_minus_a2 > 0.0, one_minus_a2 * lax.rsqrt(one_minus_a2), 0.0)
                b = mult * (ig * _seg_step(u_scr[sub], slab, k, seg_len))
                if k == 0:
                    a_cum.append(a)
                    h_loc.append(b)
                else:
                    a_cum.append(a * a_cum[-1])
                    h_loc.append(a * h_loc[-1] + b)
            a_seg, b_seg = _scan_groups(a_cum[-1], h_loc[-1])
            seg_end = a_seg * h_in + b_seg
            h_prev[sub + 1].append(seg_end[V7X_SUBLANES - 1:, :])
            h_start = _from_prev_segment(seg_end, h_in)
            for k in range(seg_len):
                _seg_step_store(hs_scr[sub], slab, k, seg_len, a_cum[k] * h_start + h_loc[k])
            yield

    def project_out(sub):
        rows = slice(sub * sub_rows, (sub + 1) * sub_rows)
        gate_chunks = gates.pop(sub)
        gated = []
        for n in range(RG_BLOCKS):
            hs = _segments_to_rows(hs_scr[sub], n * slabs_per_chunk, slabs_per_chunk, seg_len)
            if sub == n_sub - 1:
                hlast_ref[0, :, n * RG_BLOCK_W:(n + 1) * RG_BLOCK_W] = (
                    hs[sub_rows - V7X_SUBLANES:, :])
            gated.append((hs * gate_chunks[n]).astype(BF16))
        gated = jnp.concatenate(gated, axis=1)
        for n in range(RG_BLOCKS):
            cs = slice(n * RG_BLOCK_W, (n + 1) * RG_BLOCK_W)
            y_ref[rows, cs] = x_ref[rows, cs] + jnp.dot(gated, wout_ref[:, cs],
                                                        preferred_element_type=F32)
            yield

    conv_prev[0] = ucarry[...]
    h_prev[0] = [hcarry[V7X_SUBLANES - 1:V7X_SUBLANES, s * V7X_LANES:(s + 1) * V7X_LANES]
                 for s in range(n_slabs)]
    stages = [project_in, conv, project_gates, recur, project_out]
    for step in range(n_sub + len(stages) - 1):
        active = [stages[step - sub](sub) for sub in range(n_sub)
                  if 0 <= step - sub < len(stages)]
        while active:
            for piece in list(active):
                if next(piece, "done") == "done":
                    active.remove(piece)

    ulast_ref[0] = conv_prev[n_sub]
    ucarry[...] = conv_prev[n_sub]
    hcarry[...] = hlast_ref[0]


def _rglru_sample_kernel(x_ref, prev_ref, hinit_ref, g_ref, wgate_ref, win_ref, cw_ref, cb_ref,
                         wa_ref, ba_ref, wx_ref, bx_ref, lam_ref, wout_ref, y_ref, hs_ref, u_ref,
                         *, t_new):
    def conv_of(n, u0):
        c0 = n * RG_BLOCK_W
        return _dwconv_groups(u0, prev_ref[:, c0:c0 + RG_BLOCK_W], cw_ref, c0, RG_BLOCK_W)

    def scan_of(n, a, b):
        a_grp, b_grp = _scan_groups(a, b)
        return a_grp * hinit_ref[:, n * RG_BLOCK_W:(n + 1) * RG_BLOCK_W] + b_grp

    def u_sink(n, u0):
        u_ref[:, n * RG_BLOCK_W:(n + 1) * RG_BLOCK_W] = u0

    def h_sink(n, hs):
        hs_ref[:, n * RG_BLOCK_W:(n + 1) * RG_BLOCK_W] = hs

    y_ref[...] = _rglru_body(x_ref[...], g_ref, wgate_ref, win_ref, cw_ref, cb_ref, wa_ref, ba_ref,
                             wx_ref, bx_ref, lam_ref, wout_ref, conv_of, scan_of, u_sink, h_sink)


def _rglru_weight_specs(p):
    return [_const_spec(a.shape) for a in p]


def _rglru_prompt(x, params, batch, seq):
    tm = RG_PROMPT_TILE
    sub_rows = RG_ROW_TILE
    nt = seq // tm
    row = pl.BlockSpec((tm, D_MODEL), lambda b, j: (b * nt + j, 0))
    last = pl.BlockSpec((1, V7X_SUBLANES, D_MODEL), lambda b, j: (b, 0, 0))
    seg_scratch = pltpu.VMEM((D_MODEL // V7X_LANES,
                              V7X_SUBLANES * _seg_pitch(sub_rows // V7X_SUBLANES), V7X_LANES),
                             F32)
    return pl.pallas_call(
        functools.partial(_rglru_prompt_kernel, tm=tm, sub_rows=sub_rows),
        grid=(batch, nt),
        in_specs=[row] + _rglru_weight_specs(params),
        out_specs=[row, last, last],
        out_shape=[jax.ShapeDtypeStruct((batch * seq, D_MODEL), F32),
                   jax.ShapeDtypeStruct((batch, V7X_SUBLANES, D_MODEL), F32),
                   jax.ShapeDtypeStruct((batch, V7X_SUBLANES, D_MODEL), F32)],
        scratch_shapes=[pltpu.VMEM((V7X_SUBLANES, D_MODEL), F32),
                        pltpu.VMEM((V7X_SUBLANES, D_MODEL), F32)]
        + [seg_scratch] * (RG_SEG_ARRAYS * (tm // sub_rows)),
        compiler_params=pltpu.CompilerParams(
            dimension_semantics=("arbitrary", "arbitrary"),
            vmem_limit_bytes=_vmem_limit(48 * 1024 * 1024)),
        name="rglru_prompt",
    )(x, *params)


def _rglru_sample(x, row0, prev, hinit, params, t_new):
    m = prev.shape[0]
    tm = min(RG_ROW_TILE, m)
    row = pl.BlockSpec((tm, D_MODEL), lambda i: (i, 0))
    row_in = pl.BlockSpec((tm, D_MODEL), lambda i: (i + row0 // tm, 0))
    return pl.pallas_call(
        functools.partial(_rglru_sample_kernel, t_new=t_new),
        grid=(m // tm,),
        in_specs=[row_in, row, row] + _rglru_weight_specs(params),
        out_specs=[row, row, row],
        out_shape=[jax.ShapeDtypeStruct((m, D_MODEL), F32)] * 3,
        compiler_params=pltpu.CompilerParams(
            dimension_semantics=("arbitrary",),
            vmem_limit_bytes=_vmem_limit(48 * 1024 * 1024)),
        name="rglru_sample",
    )(x, prev, hinit, *params)


SCONV_CHUNK = V7X_MXU_DIM


def _sconv_body(x, g_ref, win_ref, cw_ref, wout_ref, conv_of, v_sink):
    h = _rms(x, g_ref[...]).astype(BF16)
    bcx = _mm(h, win_ref[...])
    gated = []
    for n in range(D_MODEL // SCONV_CHUNK):
        c0 = n * SCONV_CHUNK
        bg = bcx[:, c0:c0 + SCONV_CHUNK]
        cg = bcx[:, D_MODEL + c0:D_MODEL + c0 + SCONV_CHUNK]
        xv = bcx[:, 2 * D_MODEL + c0:2 * D_MODEL + c0 + SCONV_CHUNK]
        v = cg * xv
        v_sink(n, v)
        gated.append((bg * conv_of(n, v)).astype(BF16))
    return x + _mm(jnp.concatenate(gated, axis=1), wout_ref[...])


def _sconv_prompt_kernel(x_ref, g_ref, win_ref, cw_ref, wout_ref, y_ref, vlast_ref, vcarry,
                         *, tm):
    j = pl.program_id(1)

    @pl.when(j == 0)
    def _():
        vcarry[...] = jnp.zeros(vcarry.shape, F32)

    def conv_of(n, v):
        c0 = n * SCONV_CHUNK
        return _dwconv_rows(v, vcarry[:, c0:c0 + SCONV_CHUNK], cw_ref, c0, SCONV_CHUNK)

    def v_sink(n, v):
        vlast_ref[0, :, n * SCONV_CHUNK:(n + 1) * SCONV_CHUNK] = v[tm - V7X_SUBLANES:, :]

    y_ref[...] = _sconv_body(x_ref[...], g_ref, win_ref, cw_ref, wout_ref, conv_of, v_sink)
    vcarry[...] = vlast_ref[0]


def _sconv_sample_kernel(x_ref, prev_ref, g_ref, win_ref, cw_ref, wout_ref, y_ref, v_ref):
    def conv_of(n, v):
        c0 = n * SCONV_CHUNK
        return _dwconv_groups(v, prev_ref[:, c0:c0 + SCONV_CHUNK], cw_ref, c0, SCONV_CHUNK)

    def v_sink(n, v):
        v_ref[:, n * SCONV_CHUNK:(n + 1) * SCONV_CHUNK] = v

    y_ref[...] = _sconv_body(x_ref[...], g_ref, win_ref, cw_ref, wout_ref, conv_of, v_sink)


def _sconv_prompt(x, params, batch, seq):
    tm = ROW_TILE
    nt = seq // tm
    row = pl.BlockSpec((tm, D_MODEL), lambda b, j: (b * nt + j, 0))
    last = pl.BlockSpec((1, V7X_SUBLANES, D_MODEL), lambda b, j: (b, 0, 0))
    return pl.pallas_call(
        functools.partial(_sconv_prompt_kernel, tm=tm),
        grid=(batch, nt),
        in_specs=[row] + [_spec_of(a) for a in params],
        out_specs=[row, last],
        out_shape=[jax.ShapeDtypeStruct((batch * seq, D_MODEL), F32),
                   jax.ShapeDtypeStruct((batch, V7X_SUBLANES, D_MODEL), F32)],
        scratch_shapes=[pltpu.VMEM((V7X_SUBLANES, D_MODEL), F32)],
        compiler_params=pltpu.CompilerParams(
            dimension_semantics=("arbitrary", "arbitrary"),
            vmem_limit_bytes=_vmem_limit(48 * 1024 * 1024)),
        name="sconv_prompt",
    )(x, *[_arg_of(a) for a in params])


def _sconv_sample(x, row0, prev, params):
    m = prev.shape[0]
    tm = min(ROW_TILE, m)
    row = pl.BlockSpec((tm, D_MODEL), lambda i: (i, 0))
    row_in = pl.BlockSpec((tm, D_MODEL), lambda i: (i + row0 // tm, 0))
    return pl.pallas_call(
        _sconv_sample_kernel,
        grid=(m // tm,),
        in_specs=[row_in, row] + [_spec_of(a) for a in params],
        out_specs=[row, row],
        out_shape=[jax.ShapeDtypeStruct((m, D_MODEL), F32)] * 2,
        compiler_params=pltpu.CompilerParams(
            dimension_semantics=("arbitrary",),
            vmem_limit_bytes=_vmem_limit(48 * 1024 * 1024)),
        name="sconv_sample",
    )(x, prev, *[_arg_of(a) for a in params])


def _rope_tables(pos):
    half = HEAD_DIM // 2
    inv = ROPE_THETA ** (-jnp.arange(half, dtype=F32) / half)
    ang = pos.astype(F32)[:, None] * inv[None, :]
    cos = jnp.cos(ang)
    sin = jnp.sin(ang)
    reps = V7X_LANES // HEAD_DIM
    cos_t = jnp.tile(jnp.concatenate([cos, cos], axis=-1), (1, reps))
    sin_t = jnp.tile(jnp.concatenate([-sin, sin], axis=-1), (1, reps))
    return cos_t, sin_t


def _row(v):
    return v.reshape(1, -1)


def _pad_state_rows(buf):
    b, k, c = buf.shape
    padded = jnp.concatenate([jnp.zeros((b, V7X_SUBLANES - k, c), buf.dtype), buf], axis=1)
    return padded.reshape(b * V7X_SUBLANES, c)


def kernel(x_prompt, x_sample, cache_k, cache_v, state_rglru_h, state_rglru_conv, state_shortconv,
           norm_mixer, norm_ffn, norm_final,
           attn_w_qkv, attn_b_qkv, attn_w_o, attn_b_o, attn_sinks,
           rglru_w_gate, rglru_w_in, rglru_conv_w, rglru_conv_b, rglru_wa, rglru_ba,
           rglru_wx, rglru_bx, rglru_lambda, rglru_w_out,
           sconv_w_in, sconv_conv_w, sconv_w_out,
           ffn_w_gate, ffn_w_up, ffn_w_down):
    bp, seq, _ = x_prompt.shape
    bs, t_new, _ = x_sample.shape
    depth = norm_mixer.shape[0]
    past_len = 8192
    assert t_new == V7X_SUBLANES

    xp = x_prompt.reshape(bp * seq, D_MODEL)
    xs = x_sample.reshape(bs * t_new, D_MODEL)
    s_row0 = 0

    cos_p, sin_p = _rope_tables(jnp.arange(seq, dtype=jnp.int32))
    cos_s, sin_s = _rope_tables(past_len + jnp.arange(t_new, dtype=jnp.int32))
    cos_s = jnp.tile(cos_s, (SAMPLE_ATTN_BATCHES, 1))
    sin_s = jnp.tile(sin_s, (SAMPLE_ATTN_BATCHES, 1))

    to_lane_major = (0, 1, 3, 4, 2)
    from_lane_major = (0, 1, 4, 2, 3)
    ck_t = jnp.transpose(cache_k, to_lane_major)
    cv_t = jnp.transpose(cache_v, to_lane_major)
    nk_t = jnp.zeros(ck_t.shape, F32)
    nv_t = jnp.zeros(cv_t.shape, F32)

    kp_l, vp_l = [], []
    hp_l, hs_l, rcp_l, rcs_l = [], [], [], []
    scp_l, scs_l = [], []

    for i in range(depth):
        kind = i % 3
        j = i // 3
        g_mix = _row(norm_mixer[i])
        if kind == 0:
            wqkv = _layer(attn_w_qkv, j)
            wo = _layer(attn_w_o, j)
            bqkv = _row(attn_b_qkv[j])
            bo = _row(attn_b_o[j])
            sinks = attn_sinks[j]
            xp, kp, vp = _attn_prompt(xp, g_mix, wqkv, bqkv, wo, bo, cos_p, sin_p, sinks, bp, seq)
            xs, nk_t, nv_t = _attn_sample(xs, s_row0, g_mix, wqkv, bqkv, wo, bo, cos_s, sin_s,
                                          sinks, ck_t, cv_t, nk_t, nv_t, j, t_new)
            kp_l.append(kp.reshape(bp, WINDOW, N_KV_HEADS, HEAD_DIM))
            vp_l.append(vp.reshape(bp, WINDOW, N_KV_HEADS, HEAD_DIM))
        elif kind == 1:
            params = (g_mix, rglru_w_gate[j].astype(BF16), rglru_w_in[j].astype(BF16),
                      rglru_conv_w[j], _row(rglru_conv_b[j]), rglru_wa[j].astype(BF16),
                      _row(rglru_ba[j]), rglru_wx[j].astype(BF16), _row(rglru_bx[j]),
                      _row(rglru_lambda[j]), rglru_w_out[j].astype(BF16))
            xp, hlast, ulast = _rglru_prompt(xp, params, bp, seq)
            hp_l.append(hlast[:, V7X_SUBLANES - 1])
            rcp_l.append(ulast[:, V7X_SUBLANES - (RG_CONV_W - 1):])
            prev = _pad_state_rows(state_rglru_conv[j])
            hinit = jnp.repeat(state_rglru_h[j], t_new, axis=0)
            xs, hs_all, u_all = _rglru_sample(xs, s_row0, prev, hinit, params, t_new)
            hs_l.append(hs_all.reshape(bs, t_new, D_MODEL)[:, t_new - 1])
            rcs_l.append(u_all.reshape(bs, t_new, D_MODEL)[:, t_new - (RG_CONV_W - 1):])
        else:
            params = (g_mix, _layer(sconv_w_in, j), sconv_conv_w[j], _layer(sconv_w_out, j))
            xp, vlast = _sconv_prompt(xp, params, bp, seq)
            scp_l.append(vlast[:, V7X_SUBLANES - (SCONV_W - 1):])
            prev = _pad_state_rows(state_shortconv[j])
            xs, v_all = _sconv_sample(xs, s_row0, prev, params)
            scs_l.append(v_all.reshape(bs, t_new, D_MODEL)[:, t_new - (SCONV_W - 1):])

        wg = _layer(ffn_w_gate, i)
        wu = _layer(ffn_w_up, i)
        wd = _layer(ffn_w_down, i)
        d_ff = ffn_w_down.shape[1]
        g_ffn = _row(norm_ffn[i])
        g_fin = _row(norm_final)
        last = i == depth - 1
        if last:
            xp = _ffn(xp, bp * seq, None, g_ffn, wg, wu, wd, d_ff, g_fin, True)
            xs = _ffn(xs, bs * t_new, None, g_ffn, wg, wu, wd, d_ff, g_fin, True)
        else:
            xp = xs = _ffn(xp, bp * seq, xs, g_ffn, wg, wu, wd, d_ff, g_fin, False)
            s_row0 = bp * seq

    return (xp.reshape(bp, seq, D_MODEL), xs.reshape(bs, t_new, D_MODEL),
            jnp.stack(kp_l), jnp.stack(vp_l),
            jnp.transpose(nk_t, from_lane_major), jnp.transpose(nv_t, from_lane_major),
            jnp.stack(hp_l), jnp.stack(hs_l), jnp.stack(rcp_l), jnp.stack(rcs_l),
            jnp.stack(scp_l), jnp.stack(scs_l))
```

```python
import functools

import jax
import jax.numpy as jnp
from jax import lax
from jax.experimental import pallas as pl
from jax.experimental.pallas import tpu as pltpu

D_MODEL = 1024
HEAD_DIM = 64
N_HEADS = 16
N_KV_HEADS = 2
GQA_GROUP = 8
Q_DIM = N_HEADS * HEAD_DIM
KV_DIM = N_KV_HEADS * HEAD_DIM
WINDOW = 128
ROPE_THETA = 10000.0
NEG_INF = -1e30
RG_BLOCKS = 4
RG_BLOCK_W = 256
RG_CONV_W = 4
RG_C = 8.0
SCONV_W = 3
EPS = 1e-6

V7X_LANES = 128
V7X_SUBLANES = 8
V7X_MXU_DIM = 256
V7X_VMEM_BYTES = 64 * 1024 * 1024

BF16 = jnp.bfloat16
F32 = jnp.float32

FFN_CHUNK = V7X_MXU_DIM
FFN_ROW_TILE = 1024
FFN_STAGING_SLOTS = 2
ROW_TILE = 512
SCONV_ROW_TILE = 1024
ATTN_ROW_TILE = 1024
RG_ROW_TILE = 256
RG_PROMPT_TILE = 512
RG_SEG_ARRAYS = 5
SAMPLE_ATTN_BATCHES = 32
SAMPLE_ATTN_UNROLL = 8
ATTN_LOOKAHEAD = 2
PROMPT_ATTN_COLS = 2
PROMPT_ATTN_BLOCKS = 1
COLS_PER_KV = (N_HEADS // N_KV_HEADS) * HEAD_DIM // V7X_LANES


def _vmem_limit(nbytes):
    return int(min(nbytes, V7X_VMEM_BYTES - 8 * 1024 * 1024))


def _const_spec(shape):
    nd = len(shape)
    return pl.BlockSpec(shape, lambda *_: (0,) * nd, pipeline_mode=pl.Buffered(1))


def _layer(stacked, layer):
    return (stacked, layer)


def _spec_of(w):
    if isinstance(w, tuple):
        stacked, layer = w
        nd = stacked.ndim - 1
        return pl.BlockSpec((None,) + stacked.shape[1:], lambda *_: (layer,) + (0,) * nd,
                            pipeline_mode=pl.Buffered(1))
    return _const_spec(w.shape)


def _arg_of(w):
    return w[0] if isinstance(w, tuple) else w


def _nbytes_of(w):
    if isinstance(w, tuple):
        return w[0][0].size * w[0].dtype.itemsize
    return w.size * w.dtype.itemsize


def _rms(x, g):
    ms = jnp.mean(x * x, axis=-1, keepdims=True)
    return x * lax.rsqrt(ms + EPS) * g


def _mm(a, w):
    return jnp.dot(a.astype(BF16), w.astype(BF16), preferred_element_type=F32)


def _ffn_kernel(*refs, n_chunks, final_norm, tiles_a, stacked, layer):
    if stacked:
        xa_ref, xb_ref = refs[:2]
        refs = refs[2:]
    else:
        xa_ref = refs[0]
        refs = refs[1:]
    (g_ref, wg_hbm, wu_hbm, wd_hbm, gf_ref, o_ref,
     wg_ref, wu_ref, wd_ref, stage_g, stage_u, stage_d, sem) = refs
    step = pl.program_id(0)

    def chunk_copies(c):
        cs = slice(c * FFN_CHUNK, (c + 1) * FFN_CHUNK)
        slot = c % FFN_STAGING_SLOTS
        return (pltpu.make_async_copy(wg_hbm.at[layer, :, cs], stage_g.at[slot], sem.at[0, slot]),
                pltpu.make_async_copy(wu_hbm.at[layer, :, cs], stage_u.at[slot], sem.at[1, slot]),
                pltpu.make_async_copy(wd_hbm.at[layer, cs, :], stage_d.at[slot], sem.at[2, slot]))

    def run(fetch_weights):
        if fetch_weights:
            for c in range(min(FFN_STAGING_SLOTS, n_chunks)):
                for copy in chunk_copies(c):
                    copy.start()
        if stacked:
            x = jnp.where(step < tiles_a, xa_ref[...], xb_ref[...])
        else:
            x = xa_ref[...]
        h = _rms(x, g_ref[...]).astype(BF16)
        acc = x
        for c in range(n_chunks):
            cs = slice(c * FFN_CHUNK, (c + 1) * FFN_CHUNK)
            if fetch_weights:
                slot = c % FFN_STAGING_SLOTS
                for copy in chunk_copies(c):
                    copy.wait()
                wg_ref[:, cs] = stage_g[slot].astype(BF16)
                wu_ref[:, cs] = stage_u[slot].astype(BF16)
                wd_ref[cs, :] = stage_d[slot].astype(BF16)
                if c + FFN_STAGING_SLOTS < n_chunks:
                    for copy in chunk_copies(c + FFN_STAGING_SLOTS):
                        copy.start()
            gate = _mm(h, wg_ref[:, cs])
            up = _mm(h, wu_ref[:, cs])
            act = (gate * jax.nn.sigmoid(gate)) * up
            acc = acc + _mm(act, wd_ref[cs, :])
        if final_norm:
            acc = _rms(acc, gf_ref[...])
        o_ref[...] = acc

    @pl.when(step == 0)
    def _():
        run(True)

    @pl.when(step != 0)
    def _():
        run(False)


def _ffn(xa, rows_a, xb, g, wg, wu, wd, d_ff, gf, final_norm):
    tm = FFN_ROW_TILE
    tiles_a = rows_a // tm
    stacked = xb is not None
    tiles_b = xb.shape[0] // tm if stacked else 0
    n_chunks = d_ff // FFN_CHUNK
    out_row = pl.BlockSpec((tm, D_MODEL), lambda i: (i, 0))
    if stacked:
        x_specs = [pl.BlockSpec((tm, D_MODEL), lambda i: (jnp.minimum(i, tiles_a - 1), 0)),
                   pl.BlockSpec((tm, D_MODEL), lambda i: (jnp.maximum(i - tiles_a, 0), 0),
                                pipeline_mode=pl.Buffered(1))]
        x_args = [xa, xb]
    else:
        x_specs = [out_row]
        x_args = [xa]
    (wg_all, layer), (wu_all, _), (wd_all, _) = wg, wu, wd
    bf16_weight_bytes = 2 * 3 * D_MODEL * d_ff
    staging_bytes = 4 * 3 * FFN_STAGING_SLOTS * D_MODEL * FFN_CHUNK
    in_hbm = pl.BlockSpec(memory_space=pl.ANY)
    return pl.pallas_call(
        functools.partial(_ffn_kernel, n_chunks=n_chunks, final_norm=final_norm,
                          tiles_a=tiles_a, stacked=stacked, layer=layer),
        grid=(tiles_a + tiles_b,),
        in_specs=x_specs + [_const_spec((1, D_MODEL)), in_hbm, in_hbm, in_hbm,
                            _const_spec((1, D_MODEL))],
        out_specs=out_row,
        out_shape=jax.ShapeDtypeStruct(((tiles_a + tiles_b) * tm, D_MODEL), F32),
        scratch_shapes=[pltpu.VMEM(wg_all.shape[1:], BF16),
                        pltpu.VMEM(wu_all.shape[1:], BF16),
                        pltpu.VMEM(wd_all.shape[1:], BF16),
                        pltpu.VMEM((FFN_STAGING_SLOTS, D_MODEL, FFN_CHUNK), F32),
                        pltpu.VMEM((FFN_STAGING_SLOTS, D_MODEL, FFN_CHUNK), F32),
                        pltpu.VMEM((FFN_STAGING_SLOTS, FFN_CHUNK, D_MODEL), F32),
                        pltpu.SemaphoreType.DMA((3, FFN_STAGING_SLOTS))],
        compiler_params=pltpu.CompilerParams(
            dimension_semantics=("arbitrary",),
            vmem_limit_bytes=_vmem_limit(bf16_weight_bytes + staging_bytes
                                         + 16 * tm * D_MODEL * 4)),
        name="ffn",
    )(*x_args, g, wg_all, wu_all, wd_all, gf)


def _rope_cols(cols, cos, sin_signed, first_half):
    swapped = jnp.where(first_half,
                        pltpu.roll(cols, V7X_LANES - HEAD_DIM // 2, 1),
                        pltpu.roll(cols, HEAD_DIM // 2, 1))
    return cols * cos + swapped * sin_signed


def _pair_blockdiag(mat, mat_rolled, kv, lane_lt64):
    zero = jnp.zeros_like(mat)
    if kv == 0:
        top = jnp.where(lane_lt64, mat, zero)
        bottom = jnp.where(lane_lt64, zero, mat_rolled)
    else:
        top = jnp.where(lane_lt64, mat_rolled, zero)
        bottom = jnp.where(lane_lt64, zero, mat)
    return jnp.concatenate([top, bottom], axis=0).astype(BF16)


def _key_blockdiag(k2, kv):
    lane_lt64 = lax.broadcasted_iota(jnp.int32, k2.shape, 1) < HEAD_DIM
    return _pair_blockdiag(k2, pltpu.roll(k2, HEAD_DIM, 1), kv, lane_lt64)


def _value_blockdiag(v2, kv):
    v2 = jnp.where(lax.broadcasted_iota(jnp.int32, v2.shape, 0) == 0, 0.0, v2)
    vbd = _key_blockdiag(v2, kv)
    row = lax.broadcasted_iota(jnp.int32, vbd.shape, 0)
    lane = lax.broadcasted_iota(jnp.int32, vbd.shape, 1)
    ones_bd = jnp.where((lane < HEAD_DIM) == (row < 2 * WINDOW), 1.0, 0.0).astype(BF16)
    return jnp.concatenate([vbd, ones_bd], axis=1)


def _attn_weights(s_all, mask, sink_ref, kv, cols, tq):
    nk = 2 * WINDOW
    key = lax.broadcasted_iota(jnp.int32, (1, nk), 1)
    e_rows = []
    for i, c in enumerate(cols):
        e_halves = []
        for parity in range(2):
            sink = sink_ref[kv * GQA_GROUP + 2 * c + parity]
            fill = jnp.where(key == 0, sink, NEG_INF)
            s = s_all[i * tq:(i + 1) * tq, parity * nk:(parity + 1) * nk]
            s = jnp.where(mask, s, fill)
            m = jnp.max(s, axis=-1, keepdims=True)
            e_halves.append(jnp.exp(s - m))
        e_rows.append(jnp.concatenate(e_halves, axis=1))
    return jnp.concatenate(e_rows, axis=0).astype(BF16)


def _key_blockdiag_t(kt, kv):
    del kv
    zero = jnp.zeros_like(kt)
    return jnp.concatenate([jnp.concatenate([kt, zero], axis=1),
                            jnp.concatenate([zero, kt], axis=1)], axis=0).astype(BF16)


def _value_blockdiag_t(vt, kv):
    del kv
    vt = jnp.where(lax.broadcasted_iota(jnp.int32, vt.shape, 1) == 0, 0.0, vt)
    zero = jnp.zeros_like(vt)
    one = jnp.ones_like(vt)
    return jnp.concatenate([jnp.concatenate([vt, zero], axis=1),
                            jnp.concatenate([zero, vt], axis=1),
                            jnp.concatenate([one, zero], axis=1),
                            jnp.concatenate([zero, one], axis=1)], axis=0).astype(BF16)


def _dot_nt(a, b):
    return lax.dot_general(a, b, (((1,), (1,)), ((), ())), preferred_element_type=F32)


def _dot_nn(a, b):
    return jnp.dot(a, b, preferred_element_type=F32)


ROW_MAJOR_KV = (_key_blockdiag, _dot_nt, _value_blockdiag, _dot_nn)
LANE_MAJOR_KV = (_key_blockdiag_t, _dot_nn, _value_blockdiag_t, _dot_nt)


def _attend_units(units, sink_ref, tq, kv_ops):
    make_kbd, score_dot, make_vbd, value_dot = kv_ops
    n = len(units)
    scores, kbd, vbd = {}, {}, {}

    def issue_scores(i):
        load_q, load_k, _, _, kv, _, _, kv_id = units[i]
        if kv_id not in kbd:
            kbd[kv_id] = make_kbd(load_k(), kv)
        scores[i] = score_dot(load_q(), kbd[kv_id])

    for i in range(min(ATTN_LOOKAHEAD, n)):
        issue_scores(i)
    for i in range(n):
        if i + ATTN_LOOKAHEAD < n:
            issue_scores(i + ATTN_LOOKAHEAD)
        _, _, load_v, mask, kv, cols, store_o, kv_id = units[i]
        if kv_id not in vbd:
            vbd[kv_id] = make_vbd(load_v(), kv)
        e_all = _attn_weights(scores.pop(i), mask, sink_ref, kv, cols, tq)
        o_den = value_dot(e_all, vbd[kv_id])
        store_o(o_den[:, :V7X_LANES] * (1.0 / o_den[:, V7X_LANES:]))


def _band_mask(tq, col_min):
    row = lax.broadcasted_iota(jnp.int32, (tq, 2 * WINDOW), 0)
    col = lax.broadcasted_iota(jnp.int32, (tq, 2 * WINDOW), 1)
    prev_ok = (col < WINDOW) & (col > row)
    own_ok = (col >= WINDOW) & (col - WINDOW <= row)
    return (prev_ok | own_ok) & (col >= col_min)


def _attn_prompt_kernel(x_ref, g_ref, wqkv_ref, bqkv_ref, wo_ref, bo_ref, cos_ref, sin_ref,
                        sink_ref, y_ref, knew_ref, vnew_ref,
                        q_scr, k_scr, v_scr, a_scr, *, tq_tile, n_tiles):
    j = pl.program_id(1)
    n_blk = tq_tile // WINDOW

    @pl.when(j == 0)
    def _():
        k_scr[0:WINDOW, :] = jnp.zeros((WINDOW, V7X_LANES), F32)
        v_scr[0:WINDOW, :] = jnp.zeros((WINDOW, V7X_LANES), F32)

    x = x_ref[...]
    h = _rms(x, g_ref[...])
    qkv = _mm(h, wqkv_ref[...]) + bqkv_ref[...]
    cos = cos_ref[...]
    sin = sin_ref[...]
    lane = lax.broadcasted_iota(jnp.int32, (tq_tile, V7X_LANES), 1)
    first_half = (lane % HEAD_DIM) < (HEAD_DIM // 2)
    scale = HEAD_DIM ** -0.5
    for c in range(Q_DIM // V7X_LANES):
        qc = _rope_cols(qkv[:, c * V7X_LANES:(c + 1) * V7X_LANES], cos, sin, first_half)
        q_scr[:, c * V7X_LANES:(c + 1) * V7X_LANES] = (qc * scale).astype(BF16)
    k_new = _rope_cols(qkv[:, Q_DIM:Q_DIM + KV_DIM], cos, sin, first_half)
    v_new = qkv[:, Q_DIM + KV_DIM:]
    k_scr[WINDOW:, :] = k_new
    v_scr[WINDOW:, :] = v_new

    def unit(r0, mask, kv, cols, kv_id):
        def load_q():
            return jnp.concatenate(
                [q_scr[pl.ds(r0, WINDOW),
                       (kv * COLS_PER_KV + c) * V7X_LANES:(kv * COLS_PER_KV + c + 1) * V7X_LANES]
                 for c in cols], axis=0)

        def store_o(o):
            for i, c in enumerate(cols):
                col = kv * COLS_PER_KV + c
                a_scr[pl.ds(r0, WINDOW), col * V7X_LANES:(col + 1) * V7X_LANES] = (
                    o[i * WINDOW:(i + 1) * WINDOW].astype(BF16))

        return (load_q, lambda: k_scr[pl.ds(r0, 2 * WINDOW), :],
                lambda: v_scr[pl.ds(r0, 2 * WINDOW), :], mask, kv, cols, store_o, kv_id)

    col_groups = [tuple(range(c, c + PROMPT_ATTN_COLS))
                  for c in range(0, COLS_PER_KV, PROMPT_ATTN_COLS)]

    def blocks(i, carry):
        units = []
        for local in range(PROMPT_ATTN_BLOCKS):
            blk = i * PROMPT_ATTN_BLOCKS + local
            r0 = pl.multiple_of(blk * WINDOW, WINDOW)
            first = jnp.logical_and(j == 0, blk == 0)
            mask = _band_mask(WINDOW, jnp.where(first, WINDOW, 0))
            units += [unit(r0, mask, kv, cols, (local, kv)) for kv in range(N_KV_HEADS)
                      for cols in col_groups]
        _attend_units(units, sink_ref, WINDOW, ROW_MAJOR_KV)
        return carry

    lax.fori_loop(0, n_blk // PROMPT_ATTN_BLOCKS, blocks, 0)

    y_ref[...] = _mm(a_scr[...], wo_ref[...]) + bo_ref[...] + x

    k_scr[0:WINDOW, :] = k_new[tq_tile - WINDOW:, :]
    v_scr[0:WINDOW, :] = v_new[tq_tile - WINDOW:, :]

    @pl.when(j == n_tiles - 1)
    def _():
        knew_ref[0] = k_new[tq_tile - WINDOW:, :]
        vnew_ref[0] = v_new[tq_tile - WINDOW:, :]


def _attn_prompt(x, g, wqkv, bqkv, wo, bo, cos_t, sin_t, sinks, batch, seq):
    tq = ATTN_ROW_TILE
    nt = seq // tq
    row = pl.BlockSpec((tq, D_MODEL), lambda b, j: (b * nt + j, 0))
    tab = pl.BlockSpec((tq, V7X_LANES), lambda b, j: (j, 0))
    cache = pl.BlockSpec((1, WINDOW, V7X_LANES), lambda b, j: (b, 0, 0))
    return pl.pallas_call(
        functools.partial(_attn_prompt_kernel, tq_tile=tq, n_tiles=nt),
        grid=(batch, nt),
        in_specs=[row, _const_spec((1, D_MODEL)), _spec_of(wqkv),
                  _const_spec((1, Q_DIM + 2 * KV_DIM)), _spec_of(wo), _const_spec((1, D_MODEL)),
                  tab, tab, pl.BlockSpec(memory_space=pltpu.SMEM)],
        out_specs=[row, cache, cache],
        out_shape=[jax.ShapeDtypeStruct((batch * seq, D_MODEL), F32),
                   jax.ShapeDtypeStruct((batch, WINDOW, V7X_LANES), F32),
                   jax.ShapeDtypeStruct((batch, WINDOW, V7X_LANES), F32)],
        scratch_shapes=[pltpu.VMEM((tq, Q_DIM), BF16),
                        pltpu.VMEM((WINDOW + tq, V7X_LANES), F32),
                        pltpu.VMEM((WINDOW + tq, V7X_LANES), F32),
                        pltpu.VMEM((tq, Q_DIM), BF16)],
        compiler_params=pltpu.CompilerParams(
            dimension_semantics=("arbitrary", "arbitrary"),
            vmem_limit_bytes=_vmem_limit(48 * 1024 * 1024)),
        name="attn_prompt",
    )(x, g, _arg_of(wqkv), bqkv, _arg_of(wo), bo, cos_t, sin_t, sinks)


def _attn_sample_kernel(x_ref, g_ref, wqkv_ref, bqkv_ref, wo_ref, bo_ref, cos_ref, sin_ref,
                        sink_ref, ck_ref, cv_ref, nk_in, nv_in, y_ref, nk_ref, nv_ref,
                        q_scr, k_scr, v_scr, a_scr, *, n_seq, t_new):
    del nk_in, nv_in
    tm = n_seq * t_new
    x = x_ref[...]
    h = _rms(x, g_ref[...])
    qkv = _mm(h, wqkv_ref[...]) + bqkv_ref[...]
    cos = cos_ref[...]
    sin = sin_ref[...]
    lane = lax.broadcasted_iota(jnp.int32, (tm, V7X_LANES), 1)
    first_half = (lane % HEAD_DIM) < (HEAD_DIM // 2)
    scale = HEAD_DIM ** -0.5
    for c in range(Q_DIM // V7X_LANES):
        qc = _rope_cols(qkv[:, c * V7X_LANES:(c + 1) * V7X_LANES], cos, sin, first_half)
        q_scr[:, c * V7X_LANES:(c + 1) * V7X_LANES] = qc * scale
    k_scr[...] = _rope_cols(qkv[:, Q_DIM:Q_DIM + KV_DIM], cos, sin, first_half)
    v_scr[...] = qkv[:, Q_DIM + KV_DIM:]
    mask = _band_mask(t_new, 0)
    pad = jnp.zeros((WINDOW - t_new, V7X_LANES), F32)
    lane = lax.broadcasted_iota(jnp.int32, (HEAD_DIM, WINDOW), 1)

    def new_rows_t(scr, b):
        r0 = pl.multiple_of(b * t_new, t_new)
        return jnp.concatenate([scr[pl.ds(r0, t_new), :], pad], axis=0).T

    def seq_group(i, carry):
        seqs = [i * SAMPLE_ATTN_UNROLL + u for u in range(SAMPLE_ATTN_UNROLL)]
        k_new_t = [new_rows_t(k_scr, b) for b in seqs]
        v_new_t = [new_rows_t(v_scr, b) for b in seqs]

        def unit(u, b, kv):
            r0 = pl.multiple_of(b * t_new, t_new)
            head_rows = slice(kv * HEAD_DIM, (kv + 1) * HEAD_DIM)

            def load_q():
                return jnp.concatenate(
                    [q_scr[pl.ds(r0, t_new), (kv * COLS_PER_KV + c) * V7X_LANES:
                           (kv * COLS_PER_KV + c + 1) * V7X_LANES]
                     for c in range(COLS_PER_KV)], axis=0).astype(BF16)

            def load_k():
                return jnp.concatenate([ck_ref[b, kv], k_new_t[u][head_rows, :]], axis=1)

            def load_v():
                return jnp.concatenate([cv_ref[b, kv], v_new_t[u][head_rows, :]], axis=1)

            def store_o(o):
                for c in range(COLS_PER_KV):
                    col = kv * COLS_PER_KV + c
                    a_scr[pl.ds(r0, t_new), col * V7X_LANES:(col + 1) * V7X_LANES] = (
                        o[c * t_new:(c + 1) * t_new])

            return (load_q, load_k, load_v, mask, kv, tuple(range(COLS_PER_KV)), store_o,
                    (u, kv))

        _attend_units([unit(u, b, kv) for u, b in enumerate(seqs) for kv in range(N_KV_HEADS)],
                      sink_ref, t_new, LANE_MAJOR_KV)
        keep = lane < WINDOW - t_new
        for u, b in enumerate(seqs):
            for kv in range(N_KV_HEADS):
                head_rows = slice(kv * HEAD_DIM, (kv + 1) * HEAD_DIM)
                nk_ref[b, kv] = jnp.where(
                    keep, pltpu.roll(ck_ref[b, kv], WINDOW - t_new, 1),
                    pltpu.roll(k_new_t[u][head_rows, :], WINDOW - t_new, 1))
                nv_ref[b, kv] = jnp.where(
                    keep, pltpu.roll(cv_ref[b, kv], WINDOW - t_new, 1),
                    pltpu.roll(v_new_t[u][head_rows, :], WINDOW - t_new, 1))
        return carry

    lax.fori_loop(0, n_seq // SAMPLE_ATTN_UNROLL, seq_group, 0)
    y_ref[...] = _mm(a_scr[...], wo_ref[...]) + bo_ref[...] + x


def _attn_sample(x, row0, g, wqkv, bqkv, wo, bo, cos_t, sin_t, sinks, ck_t, cv_t, nk_t, nv_t,
                 layer, t_new):
    bg = SAMPLE_ATTN_BATCHES
    tm = bg * t_new
    n_batch = ck_t.shape[1]
    row = pl.BlockSpec((tm, D_MODEL), lambda i: (i, 0))
    row_in = pl.BlockSpec((tm, D_MODEL), lambda i: (i + row0 // tm, 0))
    cache = pl.BlockSpec((None, bg, N_KV_HEADS, HEAD_DIM, WINDOW), lambda i: (layer, i, 0, 0, 0))
    whole = pl.BlockSpec(memory_space=pl.ANY)
    n_in = 13
    return pl.pallas_call(
        functools.partial(_attn_sample_kernel, n_seq=bg, t_new=t_new),
        grid=(n_batch // bg,),
        in_specs=[row_in, _const_spec((1, D_MODEL)), _spec_of(wqkv),
                  _const_spec((1, Q_DIM + 2 * KV_DIM)), _spec_of(wo), _const_spec((1, D_MODEL)),
                  _const_spec((tm, V7X_LANES)), _const_spec((tm, V7X_LANES)),
                  pl.BlockSpec(memory_space=pltpu.SMEM), cache, cache, whole, whole],
        out_specs=[row, cache, cache],
        out_shape=[jax.ShapeDtypeStruct((n_batch * t_new, D_MODEL), F32),
                   jax.ShapeDtypeStruct(nk_t.shape, F32),
                   jax.ShapeDtypeStruct(nv_t.shape, F32)],
        input_output_aliases={n_in - 2: 1, n_in - 1: 2},
        scratch_shapes=[pltpu.VMEM((tm, Q_DIM), F32),
                        pltpu.VMEM((tm, V7X_LANES), F32),
                        pltpu.VMEM((tm, V7X_LANES), F32),
                        pltpu.VMEM((tm, Q_DIM), F32)],
        compiler_params=pltpu.CompilerParams(
            dimension_semantics=("arbitrary",),
            vmem_limit_bytes=_vmem_limit(48 * 1024 * 1024)),
        name="attn_sample",
    )(x, g, _arg_of(wqkv), bqkv, _arg_of(wo), bo, cos_t, sin_t, sinks, ck_t, cv_t, nk_t, nv_t)


def _dwconv_groups(u, prev, w_ref, col0, ncol):
    rows = u.shape[0]
    kw = w_ref.shape[0]
    sub = lax.broadcasted_iota(jnp.int32, u.shape, 0) % V7X_SUBLANES
    y = u * w_ref[kw - 1:kw, col0:col0 + ncol]
    for s in range(1, kw):
        from_prev = pltpu.roll(prev, (rows - V7X_SUBLANES + s) % rows, 0)
        from_self = pltpu.roll(u, s, 0)
        shifted = jnp.where(sub < s, from_prev, from_self)
        y = y + shifted * w_ref[kw - 1 - s:kw - s, col0:col0 + ncol]
    return y


def _dwconv_rows(u, carry8, w_ref, col0, ncol):
    kw = w_ref.shape[0]
    sub8 = lax.broadcasted_iota(jnp.int32, carry8.shape, 0)
    y = u * w_ref[kw - 1:kw, col0:col0 + ncol]
    for s in range(1, kw):
        rolled = pltpu.roll(u, s, 0)
        head = jnp.where(sub8 < s, pltpu.roll(carry8, s, 0), rolled[:V7X_SUBLANES])
        shifted = jnp.concatenate([head, rolled[V7X_SUBLANES:]], axis=0)
        y = y + shifted * w_ref[kw - 1 - s:kw - s, col0:col0 + ncol]
    return y


def _scan_groups(a, b):
    pos = lax.broadcasted_iota(jnp.int32, a.shape, 0) % V7X_SUBLANES
    shift = 1
    while shift < V7X_SUBLANES:
        ok = pos >= shift
        a_sh = jnp.where(ok, pltpu.roll(a, shift, 0), 1.0)
        b_sh = jnp.where(ok, pltpu.roll(b, shift, 0), 0.0)
        b = a * b_sh + b
        a = a * a_sh
        shift *= 2
    return a, b


def _scan_rows(a, b, h_row):
    a_grp, b_grp = _scan_groups(a, b)
    out = []
    h = h_row
    for g in range(a.shape[0] // V7X_SUBLANES):
        rows = slice(g * V7X_SUBLANES, (g + 1) * V7X_SUBLANES)
        hs = a_grp[rows] * h + b_grp[rows]
        out.append(hs)
        h = hs[V7X_SUBLANES - 1:, :]
    return jnp.concatenate(out, axis=0)


def _log_sigmoid(x):
    return jnp.minimum(x, 0.0) - jnp.log1p(jnp.exp(-jnp.abs(x)))


def _sigmoid(x):
    return 0.5 * jnp.tanh(0.5 * x) + 0.5


def _rglru_body(x, g_ref, wgate_ref, win_ref, cw_ref, cb_ref, wa_ref, ba_ref, wx_ref, bx_ref,
                lam_ref, wout_ref, conv_of, scan_of, u_sink, h_sink):
    h = _rms(x, g_ref[...]).astype(BF16)
    acc = x
    for n in range(RG_BLOCKS):
        c0 = n * RG_BLOCK_W
        cs = slice(c0, c0 + RG_BLOCK_W)
        gate = jax.nn.gelu(jnp.dot(h, wgate_ref[:, cs], preferred_element_type=F32))
        u0 = jnp.dot(h, win_ref[:, cs], preferred_element_type=F32)
        u_sink(n, u0)
        u = conv_of(n, u0) + cb_ref[:, cs]
        ub = u.astype(BF16)
        r = jax.nn.sigmoid(jnp.dot(ub, wa_ref[n], preferred_element_type=F32) + ba_ref[:, cs])
        ig = jax.nn.sigmoid(jnp.dot(ub, wx_ref[n], preferred_element_type=F32) + bx_ref[:, cs])
        log_a = RG_C * r * _log_sigmoid(lam_ref[:, cs])
        a = jnp.exp(log_a)
        mult = jnp.sqrt(-jnp.tanh(log_a) * (1.0 + a * a))
        hs = scan_of(n, a, mult * (ig * u))
        h_sink(n, hs)
        acc = acc + jnp.dot((hs * gate).astype(BF16), wout_ref[cs, :], preferred_element_type=F32)
    return acc


def _seg_pitch(seg_len):
    return seg_len + V7X_SUBLANES if (seg_len // V7X_SUBLANES) % 2 == 0 else seg_len


def _rows_to_segments(scr, slab0, x, seg_len):
    pitch = _seg_pitch(seg_len)
    for s in range(x.shape[1] // V7X_LANES):
        for i in range(V7X_SUBLANES):
            scr[slab0 + s, i * pitch:i * pitch + seg_len, :] = (
                x[i * seg_len:(i + 1) * seg_len, s * V7X_LANES:(s + 1) * V7X_LANES])


def _segments_to_rows(scr, slab0, n_slabs, seg_len):
    pitch = _seg_pitch(seg_len)
    return jnp.concatenate(
        [jnp.concatenate([scr[slab0 + s, i * pitch:i * pitch + seg_len, :]
                          for i in range(V7X_SUBLANES)], axis=0)
         for s in range(n_slabs)], axis=1)


def _seg_step(scr, slab, k, seg_len):
    return scr[slab, pl.ds(k, V7X_SUBLANES, stride=_seg_pitch(seg_len)), :]


def _seg_step_store(scr, slab, k, seg_len, v):
    scr[slab, pl.ds(k, V7X_SUBLANES, stride=_seg_pitch(seg_len)), :] = v


def _from_prev_segment(v, first):
    sub = lax.broadcasted_iota(jnp.int32, v.shape, 0)
    return jnp.where(sub == 0, first, pltpu.roll(v, 1, 0))


def _rglru_prompt_kernel(x_ref, g_ref, wgate_ref, win_ref, cw_ref, cb_ref, wa_ref, ba_ref,
                         wx_ref, bx_ref, lam_ref, wout_ref, y_ref, hlast_ref, ulast_ref,
                         ucarry, hcarry, *seg_scr, tm, sub_rows):
    j = pl.program_id(1)
    slabs_per_chunk = RG_BLOCK_W // V7X_LANES
    kw = cw_ref.shape[0]
    n_sub = tm // sub_rows
    u0_scr, u_scr, r_scr, ig_scr, hs_scr = (seg_scr[i * n_sub:(i + 1) * n_sub]
                                            for i in range(RG_SEG_ARRAYS))

    @pl.when(j == 0)
    def _():
        ucarry[...] = jnp.zeros(ucarry.shape, F32)
        hcarry[...] = jnp.zeros(hcarry.shape, F32)

    n_slabs = D_MODEL // V7X_LANES
    seg_len = sub_rows // V7X_SUBLANES
    gates, conv_prev, h_prev = {}, {}, {}

    def project_in(sub):
        x = x_ref[sub * sub_rows:(sub + 1) * sub_rows, :]
        h = _rms(x, g_ref[...]).astype(BF16)
        gate_chunks, last_rows = [], []
        for n in range(RG_BLOCKS):
            cs = slice(n * RG_BLOCK_W, (n + 1) * RG_BLOCK_W)
            gate_chunks.append(
                jax.nn.gelu(jnp.dot(h, wgate_ref[:, cs], preferred_element_type=F32)))
            u0 = jnp.dot(h, win_ref[:, cs], preferred_element_type=F32)
            last_rows.append(u0[sub_rows - V7X_SUBLANES:, :])
            _rows_to_segments(u0_scr[sub], n * slabs_per_chunk, u0, seg_len)
            yield
        gates[sub] = gate_chunks
        conv_prev[sub + 1] = jnp.concatenate(last_rows, axis=1)

    def conv(sub):
        for slab in range(n_slabs):
            lanes = slice(slab * V7X_LANES, (slab + 1) * V7X_LANES)
            taps = [cw_ref[t:t + 1, lanes] for t in range(kw)]
            bias = cb_ref[:, lanes]
            steps = {k: _seg_step(u0_scr[sub], slab, k, seg_len) for k in range(seg_len)}
            for back in range(1, kw):
                steps[-back] = _from_prev_segment(
                    steps[seg_len - back],
                    conv_prev[sub][V7X_SUBLANES - back:V7X_SUBLANES - back + 1, lanes])
            for k in range(seg_len):
                u_k = steps[k] * taps[kw - 1] + bias
                for back in range(1, kw):
                    u_k = u_k + steps[k - back] * taps[kw - 1 - back]
                _seg_step_store(u_scr[sub], slab, k, seg_len, u_k)
            yield

    def project_gates(sub):
        for n in range(RG_BLOCKS):
            slab0 = n * slabs_per_chunk
            ub = _segments_to_rows(u_scr[sub], slab0, slabs_per_chunk, seg_len).astype(BF16)
            _rows_to_segments(r_scr[sub], slab0,
                              jnp.dot(ub, wa_ref[n], preferred_element_type=F32), seg_len)
            _rows_to_segments(ig_scr[sub], slab0,
                              jnp.dot(ub, wx_ref[n], preferred_element_type=F32), seg_len)
            yield

    def recur(sub):
        h_prev[sub + 1] = []
        for slab in range(n_slabs):
            lanes = slice(slab * V7X_LANES, (slab + 1) * V7X_LANES)
            log_a_scale = RG_C * _log_sigmoid(lam_ref[:, lanes])
            ba = ba_ref[:, lanes]
            bx = bx_ref[:, lanes]
            h_in = h_prev[sub][slab]
            a_cum, h_loc = [], []
            for k in range(seg_len):
                r = _sigmoid(_seg_step(r_scr[sub], slab, k, seg_len) + ba)
                ig = _sigmoid(_seg_step(ig_scr[sub], slab, k, seg_len) + bx)
                log_a = log_a_scale * r
                a = jnp.exp(log_a)
                one_minus_a2 = -jnp.tanh(log_a) * (1.0 + a * a)
                mult = jnp.where(one_minus_a2 > 0.0, one_minus_a2 * lax.rsqrt(one_minus_a2), 0.0)
                b = mult * (ig * _seg_step(u_scr[sub], slab, k, seg_len))
                if k == 0:
                    a_cum.append(a)
                    h_loc.append(b)
                else:
                    a_cum.append(a * a_cum[-1])
                    h_loc.append(a * h_loc[-1] + b)
            a_seg, b_seg = _scan_groups(a_cum[-1], h_loc[-1])
            seg_end = a_seg * h_in + b_seg
            h_prev[sub + 1].append(seg_end[V7X_SUBLANES - 1:, :])
            h_start = _from_prev_segment(seg_end, h_in)
            for k in range(seg_len):
                _seg_step_store(hs_scr[sub], slab, k, seg_len, a_cum[k] * h_start + h_loc[k])
            yield

    def project_out(sub):
        rows = slice(sub * sub_rows, (sub + 1) * sub_rows)
        gate_chunks = gates.pop(sub)
        gated = []
        for n in range(RG_BLOCKS):
            hs = _segments_to_rows(hs_scr[sub], n * slabs_per_chunk, slabs_per_chunk, seg_len)
            if sub == n_sub - 1:
                hlast_ref[0, :, n * RG_BLOCK_W:(n + 1) * RG_BLOCK_W] = (
                    hs[sub_rows - V7X_SUBLANES:, :])
            gated.append((hs * gate_chunks[n]).astype(BF16))
        gated = jnp.concatenate(gated, axis=1)
        for n in range(RG_BLOCKS):
            cs = slice(n * RG_BLOCK_W, (n + 1) * RG_BLOCK_W)
            y_ref[rows, cs] = x_ref[rows, cs] + jnp.dot(gated, wout_ref[:, cs],
                                                        preferred_element_type=F32)
            yield

    conv_prev[0] = ucarry[...]
    h_prev[0] = [hcarry[V7X_SUBLANES - 1:V7X_SUBLANES, s * V7X_LANES:(s + 1) * V7X_LANES]
                 for s in range(n_slabs)]
    stages = [project_in, conv, project_gates, recur, project_out]
    for step in range(n_sub + len(stages) - 1):
        active = [stages[step - sub](sub) for sub in range(n_sub)
                  if 0 <= step - sub < len(stages)]
        while active:
            for piece in list(active):
                if next(piece, "done") == "done":
                    active.remove(piece)

    ulast_ref[0] = conv_prev[n_sub]
    ucarry[...] = conv_prev[n_sub]
    hcarry[...] = hlast_ref[0]


def _rglru_sample_kernel(x_ref, prev_ref, hinit_ref, g_ref, wgate_ref, win_ref, cw_ref, cb_ref,
                         wa_ref, ba_ref, wx_ref, bx_ref, lam_ref, wout_ref, y_ref, hs_ref, u_ref,
                         *, t_new):
    def conv_of(n, u0):
        c0 = n * RG_BLOCK_W
        return _dwconv_groups(u0, prev_ref[:, c0:c0 + RG_BLOCK_W], cw_ref, c0, RG_BLOCK_W)

    def scan_of(n, a, b):
        a_grp, b_grp = _scan_groups(a, b)
        return a_grp * hinit_ref[:, n * RG_BLOCK_W:(n + 1) * RG_BLOCK_W] + b_grp

    def u_sink(n, u0):
        u_ref[:, n * RG_BLOCK_W:(n + 1) * RG_BLOCK_W] = u0

    def h_sink(n, hs):
        hs_ref[:, n * RG_BLOCK_W:(n + 1) * RG_BLOCK_W] = hs

    y_ref[...] = _rglru_body(x_ref[...], g_ref, wgate_ref, win_ref, cw_ref, cb_ref, wa_ref, ba_ref,
                             wx_ref, bx_ref, lam_ref, wout_ref, conv_of, scan_of, u_sink, h_sink)


def _rglru_weight_specs(p):
    return [_const_spec(a.shape) for a in p]


def _rglru_prompt(x, params, batch, seq):
    tm = RG_PROMPT_TILE
    sub_rows = RG_ROW_TILE
    nt = seq // tm
    row = pl.BlockSpec((tm, D_MODEL), lambda b, j: (b * nt + j, 0))
    last = pl.BlockSpec((1, V7X_SUBLANES, D_MODEL), lambda b, j: (b, 0, 0))
    seg_scratch = pltpu.VMEM((D_MODEL // V7X_LANES,
                              V7X_SUBLANES * _seg_pitch(sub_rows // V7X_SUBLANES), V7X_LANES),
                             F32)
    return pl.pallas_call(
        functools.partial(_rglru_prompt_kernel, tm=tm, sub_rows=sub_rows),
        grid=(batch, nt),
        in_specs=[row] + _rglru_weight_specs(params),
        out_specs=[row, last, last],
        out_shape=[jax.ShapeDtypeStruct((batch * seq, D_MODEL), F32),
                   jax.ShapeDtypeStruct((batch, V7X_SUBLANES, D_MODEL), F32),
                   jax.ShapeDtypeStruct((batch, V7X_SUBLANES, D_MODEL), F32)],
        scratch_shapes=[pltpu.VMEM((V7X_SUBLANES, D_MODEL), F32),
                        pltpu.VMEM((V7X_SUBLANES, D_MODEL), F32)]
        + [seg_scratch] * (RG_SEG_ARRAYS * (tm // sub_rows)),
        compiler_params=pltpu.CompilerParams(
            dimension_semantics=("arbitrary", "arbitrary"),
            vmem_limit_bytes=_vmem_limit(48 * 1024 * 1024)),
        name="rglru_prompt",
    )(x, *params)


def _rglru_sample(x, row0, prev, hinit, params, t_new):
    m = prev.shape[0]
    tm = min(RG_ROW_TILE, m)
    row = pl.BlockSpec((tm, D_MODEL), lambda i: (i, 0))
    row_in = pl.BlockSpec((tm, D_MODEL), lambda i: (i + row0 // tm, 0))
    return pl.pallas_call(
        functools.partial(_rglru_sample_kernel, t_new=t_new),
        grid=(m // tm,),
        in_specs=[row_in, row, row] + _rglru_weight_specs(params),
        out_specs=[row, row, row],
        out_shape=[jax.ShapeDtypeStruct((m, D_MODEL), F32)] * 3,
        compiler_params=pltpu.CompilerParams(
            dimension_semantics=("arbitrary",),
            vmem_limit_bytes=_vmem_limit(48 * 1024 * 1024)),
        name="rglru_sample",
    )(x, prev, hinit, *params)


SCONV_CHUNK = V7X_MXU_DIM


def _sconv_body(x, g_ref, win_ref, cw_ref, wout_ref, conv_of, v_sink):
    h = _rms(x, g_ref[...]).astype(BF16)
    bcx = _mm(h, win_ref[...])
    gated = []
    for n in range(D_MODEL // SCONV_CHUNK):
        c0 = n * SCONV_CHUNK
        bg = bcx[:, c0:c0 + SCONV_CHUNK]
        cg = bcx[:, D_MODEL + c0:D_MODEL + c0 + SCONV_CHUNK]
        xv = bcx[:, 2 * D_MODEL + c0:2 * D_MODEL + c0 + SCONV_CHUNK]
        v = cg * xv
        v_sink(n, v)
        gated.append((bg * conv_of(n, v)).astype(BF16))
    return x + _mm(jnp.concatenate(gated, axis=1), wout_ref[...])


def _sconv_prompt_kernel(x_ref, g_ref, win_ref, cw_ref, wout_ref, y_ref, vlast_ref, vcarry,
                         *, tm):
    j = pl.program_id(1)

    @pl.when(j == 0)
    def _():
        vcarry[...] = jnp.zeros(vcarry.shape, F32)

    def conv_of(n, v):
        c0 = n * SCONV_CHUNK
        return _dwconv_rows(v, vcarry[:, c0:c0 + SCONV_CHUNK], cw_ref, c0, SCONV_CHUNK)

    def v_sink(n, v):
        vlast_ref[0, :, n * SCONV_CHUNK:(n + 1) * SCONV_CHUNK] = v[tm - V7X_SUBLANES:, :]

    y_ref[...] = _sconv_body(x_ref[...], g_ref, win_ref, cw_ref, wout_ref, conv_of, v_sink)
    vcarry[...] = vlast_ref[0]


def _sconv_sample_kernel(x_ref, prev_ref, g_ref, win_ref, cw_ref, wout_ref, y_ref, v_ref):
    def conv_of(n, v):
        c0 = n * SCONV_CHUNK
        return _dwconv_groups(v, prev_ref[:, c0:c0 + SCONV_CHUNK], cw_ref, c0, SCONV_CHUNK)

    def v_sink(n, v):
        v_ref[:, n * SCONV_CHUNK:(n + 1) * SCONV_CHUNK] = v

    y_ref[...] = _sconv_body(x_ref[...], g_ref, win_ref, cw_ref, wout_ref, conv_of, v_sink)


def _sconv_prompt(x, params, batch, seq):
    tm = SCONV_ROW_TILE
    nt = seq // tm
    row = pl.BlockSpec((tm, D_MODEL), lambda b, j: (b * nt + j, 0))
    last = pl.BlockSpec((1, V7X_SUBLANES, D_MODEL), lambda b, j: (b, 0, 0))
    return pl.pallas_call(
        functools.partial(_sconv_prompt_kernel, tm=tm),
        grid=(batch, nt),
        in_specs=[row] + [_spec_of(a) for a in params],
        out_specs=[row, last],
        out_shape=[jax.ShapeDtypeStruct((batch * seq, D_MODEL), F32),
                   jax.ShapeDtypeStruct((batch, V7X_SUBLANES, D_MODEL), F32)],
        scratch_shapes=[pltpu.VMEM((V7X_SUBLANES, D_MODEL), F32)],
        compiler_params=pltpu.CompilerParams(
            dimension_semantics=("arbitrary", "arbitrary"),
            vmem_limit_bytes=_vmem_limit(48 * 1024 * 1024)),
        name="sconv_prompt",
    )(x, *[_arg_of(a) for a in params])


def _sconv_sample(x, row0, prev, params):
    m = prev.shape[0]
    tm = min(ROW_TILE, m)
    row = pl.BlockSpec((tm, D_MODEL), lambda i: (i, 0))
    row_in = pl.BlockSpec((tm, D_MODEL), lambda i: (i + row0 // tm, 0))
    return pl.pallas_call(
        _sconv_sample_kernel,
        grid=(m // tm,),
        in_specs=[row_in, row] + [_spec_of(a) for a in params],
        out_specs=[row, row],
        out_shape=[jax.ShapeDtypeStruct((m, D_MODEL), F32)] * 2,
        compiler_params=pltpu.CompilerParams(
            dimension_semantics=("arbitrary",),
            vmem_limit_bytes=_vmem_limit(48 * 1024 * 1024)),
        name="sconv_sample",
    )(x, prev, *[_arg_of(a) for a in params])


def _rope_tables(pos):
    half = HEAD_DIM // 2
    inv = ROPE_THETA ** (-jnp.arange(half, dtype=F32) / half)
    ang = pos.astype(F32)[:, None] * inv[None, :]
    cos = jnp.cos(ang)
    sin = jnp.sin(ang)
    reps = V7X_LANES // HEAD_DIM
    cos_t = jnp.tile(jnp.concatenate([cos, cos], axis=-1), (1, reps))
    sin_t = jnp.tile(jnp.concatenate([-sin, sin], axis=-1), (1, reps))
    return cos_t, sin_t


def _row(v):
    return v.reshape(1, -1)


def _pad_state_rows(buf):
    b, k, c = buf.shape
    padded = jnp.concatenate([jnp.zeros((b, V7X_SUBLANES - k, c), buf.dtype), buf], axis=1)
    return padded.reshape(b * V7X_SUBLANES, c)


def kernel(x_prompt, x_sample, cache_k, cache_v, state_rglru_h, state_rglru_conv, state_shortconv,
           norm_mixer, norm_ffn, norm_final,
           attn_w_qkv, attn_b_qkv, attn_w_o, attn_b_o, attn_sinks,
           rglru_w_gate, rglru_w_in, rglru_conv_w, rglru_conv_b, rglru_wa, rglru_ba,
           rglru_wx, rglru_bx, rglru_lambda, rglru_w_out,
           sconv_w_in, sconv_conv_w, sconv_w_out,
           ffn_w_gate, ffn_w_up, ffn_w_down):
    bp, seq, _ = x_prompt.shape
    bs, t_new, _ = x_sample.shape
    depth = norm_mixer.shape[0]
    past_len = 8192
    assert t_new == V7X_SUBLANES

    xp = x_prompt.reshape(bp * seq, D_MODEL)
    xs = x_sample.reshape(bs * t_new, D_MODEL)
    s_row0 = 0

    cos_p, sin_p = _rope_tables(jnp.arange(seq, dtype=jnp.int32))
    cos_s, sin_s = _rope_tables(past_len + jnp.arange(t_new, dtype=jnp.int32))
    cos_s = jnp.tile(cos_s, (SAMPLE_ATTN_BATCHES, 1))
    sin_s = jnp.tile(sin_s, (SAMPLE_ATTN_BATCHES, 1))

    to_lane_major = (0, 1, 3, 4, 2)
    from_lane_major = (0, 1, 4, 2, 3)
    ck_t = jnp.transpose(cache_k, to_lane_major)
    cv_t = jnp.transpose(cache_v, to_lane_major)
    nk_t = jnp.zeros(ck_t.shape, F32)
    nv_t = jnp.zeros(cv_t.shape, F32)

    kp_l, vp_l = [], []
    hp_l, hs_l, rcp_l, rcs_l = [], [], [], []
    scp_l, scs_l = [], []

    for i in range(depth):
        kind = i % 3
        j = i // 3
        g_mix = _row(norm_mixer[i])
        if kind == 0:
            wqkv = _layer(attn_w_qkv, j)
            wo = _layer(attn_w_o, j)
            bqkv = _row(attn_b_qkv[j])
            bo = _row(attn_b_o[j])
            sinks = attn_sinks[j]
            xp, kp, vp = _attn_prompt(xp, g_mix, wqkv, bqkv, wo, bo, cos_p, sin_p, sinks, bp, seq)
            xs, nk_t, nv_t = _attn_sample(xs, s_row0, g_mix, wqkv, bqkv, wo, bo, cos_s, sin_s,
                                          sinks, ck_t, cv_t, nk_t, nv_t, j, t_new)
            kp_l.append(kp.reshape(bp, WINDOW, N_KV_HEADS, HEAD_DIM))
            vp_l.append(vp.reshape(bp, WINDOW, N_KV_HEADS, HEAD_DIM))
        elif kind == 1:
            params = (g_mix, rglru_w_gate[j].astype(BF16), rglru_w_in[j].astype(BF16),
                      rglru_conv_w[j], _row(rglru_conv_b[j]), rglru_wa[j].astype(BF16),
                      _row(rglru_ba[j]), rglru_wx[j].astype(BF16), _row(rglru_bx[j]),
                      _row(rglru_lambda[j]), rglru_w_out[j].astype(BF16))
            xp, hlast, ulast = _rglru_prompt(xp, params, bp, seq)
            hp_l.append(hlast[:, V7X_SUBLANES - 1])
            rcp_l.append(ulast[:, V7X_SUBLANES - (RG_CONV_W - 1):])
            prev = _pad_state_rows(state_rglru_conv[j])
            hinit = jnp.repeat(state_rglru_h[j], t_new, axis=0)
            xs, hs_all, u_all = _rglru_sample(xs, s_row0, prev, hinit, params, t_new)
            hs_l.append(hs_all.reshape(bs, t_new, D_MODEL)[:, t_new - 1])
            rcs_l.append(u_all.reshape(bs, t_new, D_MODEL)[:, t_new - (RG_CONV_W - 1):])
        else:
            params = (g_mix, _layer(sconv_w_in, j), sconv_conv_w[j], _layer(sconv_w_out, j))
            xp, vlast = _sconv_prompt(xp, params, bp, seq)
            scp_l.append(vlast[:, V7X_SUBLANES - (SCONV_W - 1):])
            prev = _pad_state_rows(state_shortconv[j])
            xs, v_all = _sconv_sample(xs, s_row0, prev, params)
            scs_l.append(v_all.reshape(bs, t_new, D_MODEL)[:, t_new - (SCONV_W - 1):])

        wg = _layer(ffn_w_gate, i)
        wu = _layer(ffn_w_up, i)
        wd = _layer(ffn_w_down, i)
        d_ff = ffn_w_down.shape[1]
        g_ffn = _row(norm_ffn[i])
        g_fin = _row(norm_final)
        last = i == depth - 1
        if last:
            xp = _ffn(xp, bp * seq, None, g_ffn, wg, wu, wd, d_ff, g_fin, True)
            xs = _ffn(xs, bs * t_new, None, g_ffn, wg, wu, wd, d_ff, g_fin, True)
        else:
            xp = xs = _ffn(xp, bp * seq, xs, g_ffn, wg, wu, wd, d_ff, g_fin, False)
            s_row0 = bp * seq

    return (xp.reshape(bp, seq, D_MODEL), xs.reshape(bs, t_new, D_MODEL),
            jnp.stack(kp_l), jnp.stack(vp_l),
            jnp.transpose(nk_t, from_lane_major), jnp.transpose(nv_t, from_lane_major),
            jnp.stack(hp_l), jnp.stack(hs_l), jnp.stack(rcp_l), jnp.stack(rcs_l),
            jnp.stack(scp_l), jnp.stack(scs_l))
```

```python
import functools

import jax
import jax.numpy as jnp
from jax import lax
from jax.experimental import pallas as pl
from jax.experimental.pallas import tpu as pltpu

D_MODEL = 1024
HEAD_DIM = 64
N_HEADS = 16
N_KV_HEADS = 2
GQA_GROUP = 8
Q_DIM = N_HEADS * HEAD_DIM
KV_DIM = N_KV_HEADS * HEAD_DIM
WINDOW = 128
ROPE_THETA = 10000.0
NEG_INF = -1e30
RG_BLOCKS = 4
RG_BLOCK_W = 256
RG_CONV_W = 4
RG_C = 8.0
SCONV_W = 3
LOG2_E = 1.4426950408889634
EPS = 1e-6

V7X_LANES = 128
V7X_SUBLANES = 8
V7X_MXU_DIM = 256
V7X_VMEM_BYTES = 64 * 1024 * 1024

BF16 = jnp.bfloat16
F32 = jnp.float32

FFN_CHUNK = V7X_MXU_DIM
FFN_ROW_TILE = 512
FFN_STAGING_SLOTS = 2
ROW_TILE = 512
SCONV_ROW_TILE = 1024
ATTN_ROW_TILE = 1024
RG_ROW_TILE = 256
RG_PROMPT_TILE = 512
RG_SEG_ARRAYS = 5
SAMPLE_ATTN_BATCHES = 32
SAMPLE_ATTN_UNROLL = 8
ATTN_LOOKAHEAD = 2
PROMPT_ATTN_COLS = 2
PROMPT_ATTN_BLOCKS = 1
COLS_PER_KV = (N_HEADS // N_KV_HEADS) * HEAD_DIM // V7X_LANES


def _vmem_limit(nbytes):
    return int(min(nbytes, V7X_VMEM_BYTES - 8 * 1024 * 1024))


def _const_spec(shape):
    nd = len(shape)
    return pl.BlockSpec(shape, lambda *_: (0,) * nd, pipeline_mode=pl.Buffered(1))


def _layer(stacked, layer):
    return (stacked, layer)


def _spec_of(w):
    if isinstance(w, tuple):
        stacked, layer = w
        nd = stacked.ndim - 1
        return pl.BlockSpec((None,) + stacked.shape[1:], lambda *_: (layer,) + (0,) * nd,
                            pipeline_mode=pl.Buffered(1))
    return _const_spec(w.shape)


def _arg_of(w):
    return w[0] if isinstance(w, tuple) else w


def _nbytes_of(w):
    if isinstance(w, tuple):
        return w[0][0].size * w[0].dtype.itemsize
    return w.size * w.dtype.itemsize


def _rms(x, g):
    ms = jnp.mean(x * x, axis=-1, keepdims=True)
    return x * lax.rsqrt(ms + EPS) * g


def _mm(a, w):
    return jnp.dot(a.astype(BF16), w.astype(BF16), preferred_element_type=F32)


def _ffn_kernel(*refs, n_chunks, final_norm, tiles_a, stacked, layer):
    if stacked:
        xa_ref, xb_ref = refs[:2]
        refs = refs[2:]
    else:
        xa_ref = refs[0]
        refs = refs[1:]
    (g_ref, wg_hbm, wu_hbm, wd_hbm, gf_ref, o_ref,
     wg_ref, wu_ref, wd_ref, stage_g, stage_u, stage_d, act_ref, sem) = refs
    step = pl.program_id(0)

    def chunk_copies(c):
        cs = slice(c * FFN_CHUNK, (c + 1) * FFN_CHUNK)
        slot = c % FFN_STAGING_SLOTS
        return (pltpu.make_async_copy(wg_hbm.at[layer, :, cs], stage_g.at[slot], sem.at[0, slot]),
                pltpu.make_async_copy(wu_hbm.at[layer, :, cs], stage_u.at[slot], sem.at[1, slot]),
                pltpu.make_async_copy(wd_hbm.at[layer, cs, :], stage_d.at[slot], sem.at[2, slot]))

    def run(fetch_weights):
        if fetch_weights:
            for c in range(min(FFN_STAGING_SLOTS, n_chunks)):
                for copy in chunk_copies(c):
                    copy.start()
        if stacked:
            x = jnp.where(step < tiles_a, xa_ref[...], xb_ref[...])
        else:
            x = xa_ref[...]
        h = _rms(x, g_ref[...]).astype(BF16)
        for c in range(n_chunks):
            cs = slice(c * FFN_CHUNK, (c + 1) * FFN_CHUNK)
            if fetch_weights:
                slot = c % FFN_STAGING_SLOTS
                for copy in chunk_copies(c):
                    copy.wait()
                wg_ref[:, cs] = stage_g[slot].astype(BF16)
                wu_ref[:, cs] = stage_u[slot].astype(BF16)
                wd_ref[cs, :] = stage_d[slot].astype(BF16)
                if c + FFN_STAGING_SLOTS < n_chunks:
                    for copy in chunk_copies(c + FFN_STAGING_SLOTS):
                        copy.start()
            gate = _mm(h, wg_ref[:, cs])
            up = _mm(h, wu_ref[:, cs])
            act_ref[:, cs] = ((gate * jax.nn.sigmoid(gate)) * up).astype(BF16)
        acc = x + jnp.dot(act_ref[...], wd_ref[...], preferred_element_type=F32)
        if final_norm:
            acc = _rms(acc, gf_ref[...])
        o_ref[...] = acc

    @pl.when(step == 0)
    def _():
        run(True)

    @pl.when(step != 0)
    def _():
        run(False)


def _ffn(xa, rows_a, xb, g, wg, wu, wd, d_ff, gf, final_norm):
    tm = FFN_ROW_TILE
    tiles_a = rows_a // tm
    stacked = xb is not None
    tiles_b = xb.shape[0] // tm if stacked else 0
    n_chunks = d_ff // FFN_CHUNK
    out_row = pl.BlockSpec((tm, D_MODEL), lambda i: (i, 0))
    if stacked:
        x_specs = [pl.BlockSpec((tm, D_MODEL), lambda i: (jnp.minimum(i, tiles_a - 1), 0)),
                   pl.BlockSpec((tm, D_MODEL), lambda i: (jnp.maximum(i - tiles_a, 0), 0),
                                pipeline_mode=pl.Buffered(1))]
        x_args = [xa, xb]
    else:
        x_specs = [out_row]
        x_args = [xa]
    (wg_all, layer), (wu_all, _), (wd_all, _) = wg, wu, wd
    bf16_weight_bytes = 2 * 3 * D_MODEL * d_ff
    staging_bytes = 4 * 3 * FFN_STAGING_SLOTS * D_MODEL * FFN_CHUNK
    in_hbm = pl.BlockSpec(memory_space=pl.ANY)
    return pl.pallas_call(
        functools.partial(_ffn_kernel, n_chunks=n_chunks, final_norm=final_norm,
                          tiles_a=tiles_a, stacked=stacked, layer=layer),
        grid=(tiles_a + tiles_b,),
        in_specs=x_specs + [_const_spec((1, D_MODEL)), in_hbm, in_hbm, in_hbm,
                            _const_spec((1, D_MODEL))],
        out_specs=out_row,
        out_shape=jax.ShapeDtypeStruct(((tiles_a + tiles_b) * tm, D_MODEL), F32),
        scratch_shapes=[pltpu.VMEM(wg_all.shape[1:], BF16),
                        pltpu.VMEM(wu_all.shape[1:], BF16),
                        pltpu.VMEM(wd_all.shape[1:], BF16),
                        pltpu.VMEM((FFN_STAGING_SLOTS, D_MODEL, FFN_CHUNK), F32),
                        pltpu.VMEM((FFN_STAGING_SLOTS, D_MODEL, FFN_CHUNK), F32),
                        pltpu.VMEM((FFN_STAGING_SLOTS, FFN_CHUNK, D_MODEL), F32),
                        pltpu.VMEM((tm, d_ff), BF16),
                        pltpu.SemaphoreType.DMA((3, FFN_STAGING_SLOTS))],
        compiler_params=pltpu.CompilerParams(
            dimension_semantics=("arbitrary",),
            vmem_limit_bytes=_vmem_limit(bf16_weight_bytes + staging_bytes
                                         + 16 * tm * D_MODEL * 4)),
        name="ffn",
    )(*x_args, g, wg_all, wu_all, wd_all, gf)


def _rope_cols(cols, cos, sin_signed, first_half):
    swapped = jnp.where(first_half,
                        pltpu.roll(cols, V7X_LANES - HEAD_DIM // 2, 1),
                        pltpu.roll(cols, HEAD_DIM // 2, 1))
    return cols * cos + swapped * sin_signed


def _pair_blockdiag(mat, mat_rolled, kv, lane_lt64):
    zero = jnp.zeros_like(mat)
    if kv == 0:
        top = jnp.where(lane_lt64, mat, zero)
        bottom = jnp.where(lane_lt64, zero, mat_rolled)
    else:
        top = jnp.where(lane_lt64, mat_rolled, zero)
        bottom = jnp.where(lane_lt64, zero, mat)
    return jnp.concatenate([top, bottom], axis=0).astype(BF16)


def _key_blockdiag(k2, kv):
    lane_lt64 = lax.broadcasted_iota(jnp.int32, k2.shape, 1) < HEAD_DIM
    return _pair_blockdiag(k2, pltpu.roll(k2, HEAD_DIM, 1), kv, lane_lt64)


def _value_blockdiag(v2, kv):
    v2 = jnp.where(lax.broadcasted_iota(jnp.int32, v2.shape, 0) == 0, 0.0, v2)
    vbd = _key_blockdiag(v2, kv)
    row = lax.broadcasted_iota(jnp.int32, vbd.shape, 0)
    lane = lax.broadcasted_iota(jnp.int32, vbd.shape, 1)
    ones_bd = jnp.where((lane < HEAD_DIM) == (row < 2 * WINDOW), 1.0, 0.0).astype(BF16)
    return jnp.concatenate([vbd, ones_bd], axis=1)


def _attn_weights(s_all, mask, sink_ref, kv, cols, tq):
    nk = 2 * WINDOW
    key = lax.broadcasted_iota(jnp.int32, (1, nk), 1)
    e_rows = []
    for i, c in enumerate(cols):
        e_halves = []
        for parity in range(2):
            sink = sink_ref[kv * GQA_GROUP + 2 * c + parity]
            fill = jnp.where(key == 0, sink, NEG_INF)
            s = s_all[i * tq:(i + 1) * tq, parity * nk:(parity + 1) * nk]
            s = jnp.where(mask, s, fill)
            m = jnp.max(s, axis=-1, keepdims=True)
            e_halves.append(jnp.exp(s - m))
        e_rows.append(jnp.concatenate(e_halves, axis=1))
    return jnp.concatenate(e_rows, axis=0).astype(BF16)


def _key_blockdiag_t(kt, kv):
    del kv
    zero = jnp.zeros_like(kt)
    return jnp.concatenate([jnp.concatenate([kt, zero], axis=1),
                            jnp.concatenate([zero, kt], axis=1)], axis=0).astype(BF16)


def _value_blockdiag_t(vt, kv):
    del kv
    vt = jnp.where(lax.broadcasted_iota(jnp.int32, vt.shape, 1) == 0, 0.0, vt)
    zero = jnp.zeros_like(vt)
    one = jnp.ones_like(vt)
    return jnp.concatenate([jnp.concatenate([vt, zero], axis=1),
                            jnp.concatenate([zero, vt], axis=1),
                            jnp.concatenate([one, zero], axis=1),
                            jnp.concatenate([zero, one], axis=1)], axis=0).astype(BF16)


def _dot_nt(a, b):
    return lax.dot_general(a, b, (((1,), (1,)), ((), ())), preferred_element_type=F32)


def _dot_nn(a, b):
    return jnp.dot(a, b, preferred_element_type=F32)


ROW_MAJOR_KV = (_key_blockdiag, _dot_nt, _value_blockdiag, _dot_nn)
LANE_MAJOR_KV = (_key_blockdiag_t, _dot_nn, _value_blockdiag_t, _dot_nt)


def _attend_units(units, sink_ref, tq, kv_ops):
    make_kbd, score_dot, make_vbd, value_dot = kv_ops
    n = len(units)
    scores, kbd, vbd = {}, {}, {}

    def issue_scores(i):
        load_q, load_k, _, _, kv, _, _, kv_id = units[i]
        if kv_id not in kbd:
            kbd[kv_id] = make_kbd(load_k(), kv)
        scores[i] = score_dot(load_q(), kbd[kv_id])

    for i in range(min(ATTN_LOOKAHEAD, n)):
        issue_scores(i)
    for i in range(n):
        if i + ATTN_LOOKAHEAD < n:
            issue_scores(i + ATTN_LOOKAHEAD)
        _, _, load_v, mask, kv, cols, store_o, kv_id = units[i]
        if kv_id not in vbd:
            vbd[kv_id] = make_vbd(load_v(), kv)
        e_all = _attn_weights(scores.pop(i), mask, sink_ref, kv, cols, tq)
        o_den = value_dot(e_all, vbd[kv_id])
        store_o(o_den[:, :V7X_LANES] * (1.0 / o_den[:, V7X_LANES:]))


def _band_mask(tq, col_min):
    row = lax.broadcasted_iota(jnp.int32, (tq, 2 * WINDOW), 0)
    col = lax.broadcasted_iota(jnp.int32, (tq, 2 * WINDOW), 1)
    prev_ok = (col < WINDOW) & (col > row)
    own_ok = (col >= WINDOW) & (col - WINDOW <= row)
    return (prev_ok | own_ok) & (col >= col_min)


def _attn_prompt_kernel(x_ref, g_ref, wqkv_ref, bqkv_ref, wo_ref, bo_ref, cos_ref, sin_ref,
                        sink_ref, y_ref, knew_ref, vnew_ref,
                        q_scr, k_scr, v_scr, a_scr, *, tq_tile, n_tiles):
    j = pl.program_id(1)
    n_blk = tq_tile // WINDOW

    @pl.when(j == 0)
    def _():
        k_scr[0:WINDOW, :] = jnp.zeros((WINDOW, V7X_LANES), F32)
        v_scr[0:WINDOW, :] = jnp.zeros((WINDOW, V7X_LANES), F32)

    x = x_ref[...]
    h = _rms(x, g_ref[...])
    qkv = _mm(h, wqkv_ref[...]) + bqkv_ref[...]
    cos = cos_ref[...]
    sin = sin_ref[...]
    lane = lax.broadcasted_iota(jnp.int32, (tq_tile, V7X_LANES), 1)
    first_half = (lane % HEAD_DIM) < (HEAD_DIM // 2)
    scale = HEAD_DIM ** -0.5
    for c in range(Q_DIM // V7X_LANES):
        qc = _rope_cols(qkv[:, c * V7X_LANES:(c + 1) * V7X_LANES], cos, sin, first_half)
        q_scr[:, c * V7X_LANES:(c + 1) * V7X_LANES] = (qc * scale).astype(BF16)
    k_new = _rope_cols(qkv[:, Q_DIM:Q_DIM + KV_DIM], cos, sin, first_half)
    v_new = qkv[:, Q_DIM + KV_DIM:]
    k_scr[WINDOW:, :] = k_new
    v_scr[WINDOW:, :] = v_new

    def unit(r0, mask, kv, cols, kv_id):
        def load_q():
            return jnp.concatenate(
                [q_scr[pl.ds(r0, WINDOW),
                       (kv * COLS_PER_KV + c) * V7X_LANES:(kv * COLS_PER_KV + c + 1) * V7X_LANES]
                 for c in cols], axis=0)

        def store_o(o):
            for i, c in enumerate(cols):
                col = kv * COLS_PER_KV + c
                a_scr[pl.ds(r0, WINDOW), col * V7X_LANES:(col + 1) * V7X_LANES] = (
                    o[i * WINDOW:(i + 1) * WINDOW].astype(BF16))

        return (load_q, lambda: k_scr[pl.ds(r0, 2 * WINDOW), :],
                lambda: v_scr[pl.ds(r0, 2 * WINDOW), :], mask, kv, cols, store_o, kv_id)

    col_groups = [tuple(range(c, c + PROMPT_ATTN_COLS))
                  for c in range(0, COLS_PER_KV, PROMPT_ATTN_COLS)]

    def blocks(i, carry):
        units = []
        for local in range(PROMPT_ATTN_BLOCKS):
            blk = i * PROMPT_ATTN_BLOCKS + local
            r0 = pl.multiple_of(blk * WINDOW, WINDOW)
            first = jnp.logical_and(j == 0, blk == 0)
            mask = _band_mask(WINDOW, jnp.where(first, WINDOW, 0))
            units += [unit(r0, mask, kv, cols, (local, kv)) for kv in range(N_KV_HEADS)
                      for cols in col_groups]
        _attend_units(units, sink_ref, WINDOW, ROW_MAJOR_KV)
        return carry

    lax.fori_loop(0, n_blk // PROMPT_ATTN_BLOCKS, blocks, 0)

    y_ref[...] = _mm(a_scr[...], wo_ref[...]) + bo_ref[...] + x

    k_scr[0:WINDOW, :] = k_new[tq_tile - WINDOW:, :]
    v_scr[0:WINDOW, :] = v_new[tq_tile - WINDOW:, :]

    @pl.when(j == n_tiles - 1)
    def _():
        knew_ref[0] = k_new[tq_tile - WINDOW:, :]
        vnew_ref[0] = v_new[tq_tile - WINDOW:, :]


def _attn_prompt(x, g, wqkv, bqkv, wo, bo, cos_t, sin_t, sinks, batch, seq):
    tq = ATTN_ROW_TILE
    nt = seq // tq
    row = pl.BlockSpec((tq, D_MODEL), lambda b, j: (b * nt + j, 0))
    tab = pl.BlockSpec((tq, V7X_LANES), lambda b, j: (j, 0))
    cache = pl.BlockSpec((1, WINDOW, V7X_LANES), lambda b, j: (b, 0, 0))
    return pl.pallas_call(
        functools.partial(_attn_prompt_kernel, tq_tile=tq, n_tiles=nt),
        grid=(batch, nt),
        in_specs=[row, _const_spec((1, D_MODEL)), _spec_of(wqkv),
                  _const_spec((1, Q_DIM + 2 * KV_DIM)), _spec_of(wo), _const_spec((1, D_MODEL)),
                  tab, tab, pl.BlockSpec(memory_space=pltpu.SMEM)],
        out_specs=[row, cache, cache],
        out_shape=[jax.ShapeDtypeStruct((batch * seq, D_MODEL), F32),
                   jax.ShapeDtypeStruct((batch, WINDOW, V7X_LANES), F32),
                   jax.ShapeDtypeStruct((batch, WINDOW, V7X_LANES), F32)],
        scratch_shapes=[pltpu.VMEM((tq, Q_DIM), BF16),
                        pltpu.VMEM((WINDOW + tq, V7X_LANES), F32),
                        pltpu.VMEM((WINDOW + tq, V7X_LANES), F32),
                        pltpu.VMEM((tq, Q_DIM), BF16)],
        compiler_params=pltpu.CompilerParams(
            dimension_semantics=("arbitrary", "arbitrary"),
            vmem_limit_bytes=_vmem_limit(48 * 1024 * 1024)),
        name="attn_prompt",
    )(x, g, _arg_of(wqkv), bqkv, _arg_of(wo), bo, cos_t, sin_t, sinks)


def _attn_sample_kernel(x_ref, g_ref, wqkv_ref, bqkv_ref, wo_ref, bo_ref, cos_ref, sin_ref,
                        sink_ref, ck_ref, cv_ref, nk_in, nv_in, y_ref, nk_ref, nv_ref,
                        q_scr, k_scr, v_scr, a_scr, *, n_seq, t_new):
    del nk_in, nv_in
    tm = n_seq * t_new
    x = x_ref[...]
    h = _rms(x, g_ref[...])
    qkv = _mm(h, wqkv_ref[...]) + bqkv_ref[...]
    cos = cos_ref[...]
    sin = sin_ref[...]
    lane = lax.broadcasted_iota(jnp.int32, (tm, V7X_LANES), 1)
    first_half = (lane % HEAD_DIM) < (HEAD_DIM // 2)
    scale = HEAD_DIM ** -0.5
    for c in range(Q_DIM // V7X_LANES):
        qc = _rope_cols(qkv[:, c * V7X_LANES:(c + 1) * V7X_LANES], cos, sin, first_half)
        q_scr[:, c * V7X_LANES:(c + 1) * V7X_LANES] = qc * scale
    k_scr[...] = _rope_cols(qkv[:, Q_DIM:Q_DIM + KV_DIM], cos, sin, first_half)
    v_scr[...] = qkv[:, Q_DIM + KV_DIM:]
    mask = _band_mask(t_new, 0)
    pad = jnp.zeros((WINDOW - t_new, V7X_LANES), F32)
    lane = lax.broadcasted_iota(jnp.int32, (HEAD_DIM, WINDOW), 1)

    def new_rows_t(scr, b):
        r0 = pl.multiple_of(b * t_new, t_new)
        return jnp.concatenate([scr[pl.ds(r0, t_new), :], pad], axis=0).T

    def seq_group(i, carry):
        seqs = [i * SAMPLE_ATTN_UNROLL + u for u in range(SAMPLE_ATTN_UNROLL)]
        k_new_t = [new_rows_t(k_scr, b) for b in seqs]
        v_new_t = [new_rows_t(v_scr, b) for b in seqs]

        def unit(u, b, kv):
            r0 = pl.multiple_of(b * t_new, t_new)
            head_rows = slice(kv * HEAD_DIM, (kv + 1) * HEAD_DIM)

            def load_q():
                return jnp.concatenate(
                    [q_scr[pl.ds(r0, t_new), (kv * COLS_PER_KV + c) * V7X_LANES:
                           (kv * COLS_PER_KV + c + 1) * V7X_LANES]
                     for c in range(COLS_PER_KV)], axis=0).astype(BF16)

            def load_k():
                return jnp.concatenate([ck_ref[b, kv], k_new_t[u][head_rows, :]], axis=1)

            def load_v():
                return jnp.concatenate([cv_ref[b, kv], v_new_t[u][head_rows, :]], axis=1)

            def store_o(o):
                for c in range(COLS_PER_KV):
                    col = kv * COLS_PER_KV + c
                    a_scr[pl.ds(r0, t_new), col * V7X_LANES:(col + 1) * V7X_LANES] = (
                        o[c * t_new:(c + 1) * t_new])

            return (load_q, load_k, load_v, mask, kv, tuple(range(COLS_PER_KV)), store_o,
                    (u, kv))

        _attend_units([unit(u, b, kv) for u, b in enumerate(seqs) for kv in range(N_KV_HEADS)],
                      sink_ref, t_new, LANE_MAJOR_KV)
        keep = lane < WINDOW - t_new
        for u, b in enumerate(seqs):
            for kv in range(N_KV_HEADS):
                head_rows = slice(kv * HEAD_DIM, (kv + 1) * HEAD_DIM)
                nk_ref[b, kv] = jnp.where(
                    keep, pltpu.roll(ck_ref[b, kv], WINDOW - t_new, 1),
                    pltpu.roll(k_new_t[u][head_rows, :], WINDOW - t_new, 1))
                nv_ref[b, kv] = jnp.where(
                    keep, pltpu.roll(cv_ref[b, kv], WINDOW - t_new, 1),
                    pltpu.roll(v_new_t[u][head_rows, :], WINDOW - t_new, 1))
        return carry

    lax.fori_loop(0, n_seq // SAMPLE_ATTN_UNROLL, seq_group, 0)
    y_ref[...] = _mm(a_scr[...], wo_ref[...]) + bo_ref[...] + x


def _attn_sample(x, row0, g, wqkv, bqkv, wo, bo, cos_t, sin_t, sinks, ck_t, cv_t, nk_t, nv_t,
                 layer, t_new):
    bg = SAMPLE_ATTN_BATCHES
    tm = bg * t_new
    n_batch = ck_t.shape[1]
    row = pl.BlockSpec((tm, D_MODEL), lambda i: (i, 0))
    row_in = pl.BlockSpec((tm, D_MODEL), lambda i: (i + row0 // tm, 0))
    cache = pl.BlockSpec((None, bg, N_KV_HEADS, HEAD_DIM, WINDOW), lambda i: (layer, i, 0, 0, 0))
    whole = pl.BlockSpec(memory_space=pl.ANY)
    n_in = 13
    return pl.pallas_call(
        functools.partial(_attn_sample_kernel, n_seq=bg, t_new=t_new),
        grid=(n_batch // bg,),
        in_specs=[row_in, _const_spec((1, D_MODEL)), _spec_of(wqkv),
                  _const_spec((1, Q_DIM + 2 * KV_DIM)), _spec_of(wo), _const_spec((1, D_MODEL)),
                  _const_spec((tm, V7X_LANES)), _const_spec((tm, V7X_LANES)),
                  pl.BlockSpec(memory_space=pltpu.SMEM), cache, cache, whole, whole],
        out_specs=[row, cache, cache],
        out_shape=[jax.ShapeDtypeStruct((n_batch * t_new, D_MODEL), F32),
                   jax.ShapeDtypeStruct(nk_t.shape, F32),
                   jax.ShapeDtypeStruct(nv_t.shape, F32)],
        input_output_aliases={n_in - 2: 1, n_in - 1: 2},
        scratch_shapes=[pltpu.VMEM((tm, Q_DIM), F32),
                        pltpu.VMEM((tm, V7X_LANES), F32),
                        pltpu.VMEM((tm, V7X_LANES), F32),
                        pltpu.VMEM((tm, Q_DIM), F32)],
        compiler_params=pltpu.CompilerParams(
            dimension_semantics=("arbitrary",),
            vmem_limit_bytes=_vmem_limit(48 * 1024 * 1024)),
        name="attn_sample",
    )(x, g, _arg_of(wqkv), bqkv, _arg_of(wo), bo, cos_t, sin_t, sinks, ck_t, cv_t, nk_t, nv_t)


def _dwconv_groups(u, prev, w_ref, col0, ncol):
    rows = u.shape[0]
    kw = w_ref.shape[0]
    sub = lax.broadcasted_iota(jnp.int32, u.shape, 0) % V7X_SUBLANES
    y = u * w_ref[kw - 1:kw, col0:col0 + ncol]
    for s in range(1, kw):
        from_prev = pltpu.roll(prev, (rows - V7X_SUBLANES + s) % rows, 0)
        from_self = pltpu.roll(u, s, 0)
        shifted = jnp.where(sub < s, from_prev, from_self)
        y = y + shifted * w_ref[kw - 1 - s:kw - s, col0:col0 + ncol]
    return y


def _dwconv_rows(u, carry8, w_ref, col0, ncol):
    kw = w_ref.shape[0]
    sub8 = lax.broadcasted_iota(jnp.int32, carry8.shape, 0)
    y = u * w_ref[kw - 1:kw, col0:col0 + ncol]
    for s in range(1, kw):
        rolled = pltpu.roll(u, s, 0)
        head = jnp.where(sub8 < s, pltpu.roll(carry8, s, 0), rolled[:V7X_SUBLANES])
        shifted = jnp.concatenate([head, rolled[V7X_SUBLANES:]], axis=0)
        y = y + shifted * w_ref[kw - 1 - s:kw - s, col0:col0 + ncol]
    return y


def _scan_groups(a, b):
    pos = lax.broadcasted_iota(jnp.int32, a.shape, 0) % V7X_SUBLANES
    shift = 1
    while shift < V7X_SUBLANES:
        ok = pos >= shift
        a_sh = jnp.where(ok, pltpu.roll(a, shift, 0), 1.0)
        b_sh = jnp.where(ok, pltpu.roll(b, shift, 0), 0.0)
        b = a * b_sh + b
        a = a * a_sh
        shift *= 2
    return a, b


def _scan_rows(a, b, h_row):
    a_grp, b_grp = _scan_groups(a, b)
    out = []
    h = h_row
    for g in range(a.shape[0] // V7X_SUBLANES):
        rows = slice(g * V7X_SUBLANES, (g + 1) * V7X_SUBLANES)
        hs = a_grp[rows] * h + b_grp[rows]
        out.append(hs)
        h = hs[V7X_SUBLANES - 1:, :]
    return jnp.concatenate(out, axis=0)


def _log_sigmoid(x):
    return jnp.minimum(x, 0.0) - jnp.log1p(jnp.exp(-jnp.abs(x)))


def _rglru_body(x, g_ref, wgate_ref, win_ref, cw_ref, cb_ref, wa_ref, ba_ref, wx_ref, bx_ref,
                lam_ref, wout_ref, conv_of, scan_of, u_sink, h_sink):
    h = _rms(x, g_ref[...]).astype(BF16)
    acc = x
    for n in range(RG_BLOCKS):
        c0 = n * RG_BLOCK_W
        cs = slice(c0, c0 + RG_BLOCK_W)
        gate = jax.nn.gelu(jnp.dot(h, wgate_ref[:, cs], preferred_element_type=F32))
        u0 = jnp.dot(h, win_ref[:, cs], preferred_element_type=F32)
        u_sink(n, u0)
        u = conv_of(n, u0) + cb_ref[:, cs]
        ub = u.astype(BF16)
        r = jax.nn.sigmoid(jnp.dot(ub, wa_ref[n], preferred_element_type=F32) + ba_ref[:, cs])
        ig = jax.nn.sigmoid(jnp.dot(ub, wx_ref[n], preferred_element_type=F32) + bx_ref[:, cs])
        log_a = RG_C * r * _log_sigmoid(lam_ref[:, cs])
        a = jnp.exp(log_a)
        mult = jnp.sqrt(-jnp.tanh(log_a) * (1.0 + a * a))
        hs = scan_of(n, a, mult * (ig * u))
        h_sink(n, hs)
        acc = acc + jnp.dot((hs * gate).astype(BF16), wout_ref[cs, :], preferred_element_type=F32)
    return acc


def _seg_pitch(seg_len):
    assert seg_len % V7X_SUBLANES == 0
    return seg_len + V7X_SUBLANES // 2


def _rows_to_segments(scr, slab0, x, seg_len):
    pitch = _seg_pitch(seg_len)
    for s in range(x.shape[1] // V7X_LANES):
        for i in range(V7X_SUBLANES):
            scr[slab0 + s, i * pitch:i * pitch + seg_len, :] = (
                x[i * seg_len:(i + 1) * seg_len, s * V7X_LANES:(s + 1) * V7X_LANES])


def _segments_to_rows(scr, slab0, n_slabs, seg_len):
    pitch = _seg_pitch(seg_len)
    return jnp.concatenate(
        [jnp.concatenate([scr[slab0 + s, i * pitch:i * pitch + seg_len, :]
                          for i in range(V7X_SUBLANES)], axis=0)
         for s in range(n_slabs)], axis=1)


def _seg_step(scr, slab, k, seg_len):
    return scr[slab, pl.ds(k, V7X_SUBLANES, stride=_seg_pitch(seg_len)), :]


def _seg_step_store(scr, slab, k, seg_len, v):
    scr[slab, pl.ds(k, V7X_SUBLANES, stride=_seg_pitch(seg_len)), :] = v


def _from_prev_segment(v, first):
    sub = lax.broadcasted_iota(jnp.int32, v.shape, 0)
    return jnp.where(sub == 0, first, pltpu.roll(v, 1, 0))


def _rglru_prompt_kernel(x_ref, g_ref, wgate_ref, win_ref, cw_ref, cb_ref, wa_ref, ba_ref,
                         wx_ref, bx_ref, lam_ref, wout_ref, y_ref, hlast_ref, ulast_ref,
                         ucarry, hcarry, *seg_scr, tm, sub_rows):
    j = pl.program_id(1)
    slabs_per_chunk = RG_BLOCK_W // V7X_LANES
    kw = cw_ref.shape[0]
    n_sub = tm // sub_rows
    u0_scr, u_scr, r_scr, ig_scr, hs_scr = (seg_scr[i * n_sub:(i + 1) * n_sub]
                                            for i in range(RG_SEG_ARRAYS))

    @pl.when(j == 0)
    def _():
        ucarry[...] = jnp.zeros(ucarry.shape, F32)
        hcarry[...] = jnp.zeros(hcarry.shape, F32)

    n_slabs = D_MODEL // V7X_LANES
    seg_len = sub_rows // V7X_SUBLANES
    gates, conv_prev, h_prev = {}, {}, {}

    def project_in(sub):
        x = x_ref[sub * sub_rows:(sub + 1) * sub_rows, :]
        h = _rms(x, g_ref[...]).astype(BF16)
        gate_chunks, last_rows = [], []
        for n in range(RG_BLOCKS):
            cs = slice(n * RG_BLOCK_W, (n + 1) * RG_BLOCK_W)
            gate_chunks.append(
                jax.nn.gelu(jnp.dot(h, wgate_ref[:, cs], preferred_element_type=F32)))
            u0 = jnp.dot(h, win_ref[:, cs], preferred_element_type=F32)
            last_rows.append(u0[sub_rows - V7X_SUBLANES:, :])
            _rows_to_segments(u0_scr[sub], n * slabs_per_chunk, u0, seg_len)
            yield
        gates[sub] = gate_chunks
        conv_prev[sub + 1] = jnp.concatenate(last_rows, axis=1)

    def conv(sub):
        for slab in range(n_slabs):
            lanes = slice(slab * V7X_LANES, (slab + 1) * V7X_LANES)
            taps = [cw_ref[t:t + 1, lanes] for t in range(kw)]
            bias = cb_ref[:, lanes]
            steps = {k: _seg_step(u0_scr[sub], slab, k, seg_len) for k in range(seg_len)}
            for back in range(1, kw):
                steps[-back] = _from_prev_segment(
                    steps[seg_len - back],
                    conv_prev[sub][V7X_SUBLANES - back:V7X_SUBLANES - back + 1, lanes])
            for k in range(seg_len):
                u_k = steps[k] * taps[kw - 1] + bias
                for back in range(1, kw):
                    u_k = u_k + steps[k - back] * taps[kw - 1 - back]
                _seg_step_store(u_scr[sub], slab, k, seg_len, u_k)
            yield

    def project_gates(sub):
        for n in range(RG_BLOCKS):
            slab0 = n * slabs_per_chunk
            ub = _segments_to_rows(u_scr[sub], slab0, slabs_per_chunk, seg_len).astype(BF16)
            _rows_to_segments(r_scr[sub], slab0,
                              jnp.dot(ub, wa_ref[n], preferred_element_type=F32), seg_len)
            _rows_to_segments(ig_scr[sub], slab0,
                              jnp.dot(ub, wx_ref[n], preferred_element_type=F32), seg_len)
            yield

    def recur(sub):
        h_prev[sub + 1] = []
        for slab in range(n_slabs):
            lanes = slice(slab * V7X_LANES, (slab + 1) * V7X_LANES)
            half_scale = (-0.5 * RG_C) * _log_sigmoid(lam_ref[:, lanes])
            ba = 0.5 * ba_ref[:, lanes]
            bx = 0.5 * bx_ref[:, lanes]
            h_in = h_prev[sub][slab]
            a_cum, h_loc = [], []
            for k in range(seg_len):
                neg_log_a = half_scale * jnp.tanh(_seg_step(r_scr[sub], slab, k, seg_len) + ba) \
                    + half_scale
                ig = 0.5 * jnp.tanh(_seg_step(ig_scr[sub], slab, k, seg_len) + bx) + 0.5
                a = jnp.exp2(neg_log_a * (-LOG2_E))
                one_minus_a2 = jnp.tanh(neg_log_a) * (1.0 + a * a)
                mult = jnp.where(one_minus_a2 > 0.0, one_minus_a2 * lax.rsqrt(one_minus_a2), 0.0)
                b = mult * (ig * _seg_step(u_scr[sub], slab, k, seg_len))
                if k == 0:
                    a_cum.append(a)
                    h_loc.append(b)
                else:
                    a_cum.append(a * a_cum[-1])
                    h_loc.append(a * h_loc[-1] + b)
            a_seg, b_seg = _scan_groups(a_cum[-1], h_loc[-1])
            seg_end = a_seg * h_in + b_seg
            h_prev[sub + 1].append(seg_end[V7X_SUBLANES - 1:, :])
            h_start = _from_prev_segment(seg_end, h_in)
            for k in range(seg_len):
                _seg_step_store(hs_scr[sub], slab, k, seg_len, a_cum[k] * h_start + h_loc[k])
            yield

    def project_out(sub):
        rows = slice(sub * sub_rows, (sub + 1) * sub_rows)
        gate_chunks = gates.pop(sub)
        gated = []
        for n in range(RG_BLOCKS):
            hs = _segments_to_rows(hs_scr[sub], n * slabs_per_chunk, slabs_per_chunk, seg_len)
            if sub == n_sub - 1:
                hlast_ref[0, :, n * RG_BLOCK_W:(n + 1) * RG_BLOCK_W] = (
                    hs[sub_rows - V7X_SUBLANES:, :])
            gated.append((hs * gate_chunks[n]).astype(BF16))
        gated = jnp.concatenate(gated, axis=1)
        for n in range(RG_BLOCKS):
            cs = slice(n * RG_BLOCK_W, (n + 1) * RG_BLOCK_W)
            y_ref[rows, cs] = x_ref[rows, cs] + jnp.dot(gated, wout_ref[:, cs],
                                                        preferred_element_type=F32)
            yield

    conv_prev[0] = ucarry[...]
    h_prev[0] = [hcarry[V7X_SUBLANES - 1:V7X_SUBLANES, s * V7X_LANES:(s + 1) * V7X_LANES]
                 for s in range(n_slabs)]
    stages = [project_in, conv, project_gates, recur, project_out]
    for step in range(n_sub + len(stages) - 1):
        active = [stages[step - sub](sub) for sub in range(n_sub)
                  if 0 <= step - sub < len(stages)]
        while active:
            for piece in list(active):
                if next(piece, "done") == "done":
                    active.remove(piece)

    ulast_ref[0] = conv_prev[n_sub]
    ucarry[...] = conv_prev[n_sub]
    hcarry[...] = hlast_ref[0]


def _rglru_sample_kernel(x_ref, prev_ref, hinit_ref, g_ref, wgate_ref, win_ref, cw_ref, cb_ref,
                         wa_ref, ba_ref, wx_ref, bx_ref, lam_ref, wout_ref, y_ref, hs_ref, u_ref,
                         *, t_new):
    def conv_of(n, u0):
        c0 = n * RG_BLOCK_W
        return _dwconv_groups(u0, prev_ref[:, c0:c0 + RG_BLOCK_W], cw_ref, c0, RG_BLOCK_W)

    def scan_of(n, a, b):
        a_grp, b_grp = _scan_groups(a, b)
        return a_grp * hinit_ref[:, n * RG_BLOCK_W:(n + 1) * RG_BLOCK_W] + b_grp

    def u_sink(n, u0):
        u_ref[:, n * RG_BLOCK_W:(n + 1) * RG_BLOCK_W] = u0

    def h_sink(n, hs):
        hs_ref[:, n * RG_BLOCK_W:(n + 1) * RG_BLOCK_W] = hs

    y_ref[...] = _rglru_body(x_ref[...], g_ref, wgate_ref, win_ref, cw_ref, cb_ref, wa_ref, ba_ref,
                             wx_ref, bx_ref, lam_ref, wout_ref, conv_of, scan_of, u_sink, h_sink)


def _rglru_weight_specs(p):
    return [_const_spec(a.shape) for a in p]


def _rglru_prompt(x, params, batch, seq):
    tm = RG_PROMPT_TILE
    sub_rows = RG_ROW_TILE
    nt = seq // tm
    row = pl.BlockSpec((tm, D_MODEL), lambda b, j: (b * nt + j, 0))
    last = pl.BlockSpec((1, V7X_SUBLANES, D_MODEL), lambda b, j: (b, 0, 0))
    seg_scratch = pltpu.VMEM((D_MODEL // V7X_LANES,
                              V7X_SUBLANES * _seg_pitch(sub_rows // V7X_SUBLANES), V7X_LANES),
                             F32)
    return pl.pallas_call(
        functools.partial(_rglru_prompt_kernel, tm=tm, sub_rows=sub_rows),
        grid=(batch, nt),
        in_specs=[row] + _rglru_weight_specs(params),
        out_specs=[row, last, last],
        out_shape=[jax.ShapeDtypeStruct((batch * seq, D_MODEL), F32),
                   jax.ShapeDtypeStruct((batch, V7X_SUBLANES, D_MODEL), F32),
                   jax.ShapeDtypeStruct((batch, V7X_SUBLANES, D_MODEL), F32)],
        scratch_shapes=[pltpu.VMEM((V7X_SUBLANES, D_MODEL), F32),
                        pltpu.VMEM((V7X_SUBLANES, D_MODEL), F32)]
        + [seg_scratch] * (RG_SEG_ARRAYS * (tm // sub_rows)),
        compiler_params=pltpu.CompilerParams(
            dimension_semantics=("arbitrary", "arbitrary"),
            vmem_limit_bytes=_vmem_limit(48 * 1024 * 1024)),
        name="rglru_prompt",
    )(x, *params)


def _rglru_sample(x, row0, prev, hinit, params, t_new):
    m = prev.shape[0]
    tm = min(RG_ROW_TILE, m)
    row = pl.BlockSpec((tm, D_MODEL), lambda i: (i, 0))
    row_in = pl.BlockSpec((tm, D_MODEL), lambda i: (i + row0 // tm, 0))
    return pl.pallas_call(
        functools.partial(_rglru_sample_kernel, t_new=t_new),
        grid=(m // tm,),
        in_specs=[row_in, row, row] + _rglru_weight_specs(params),
        out_specs=[row, row, row],
        out_shape=[jax.ShapeDtypeStruct((m, D_MODEL), F32)] * 3,
        compiler_params=pltpu.CompilerParams(
            dimension_semantics=("arbitrary",),
            vmem_limit_bytes=_vmem_limit(48 * 1024 * 1024)),
        name="rglru_sample",
    )(x, prev, hinit, *params)


SCONV_CHUNK = V7X_MXU_DIM


def _sconv_body(x, g_ref, win_ref, cw_ref, wout_ref, conv_of, v_sink):
    h = _rms(x, g_ref[...]).astype(BF16)
    bcx = _mm(h, win_ref[...])
    gated = []
    for n in range(D_MODEL // SCONV_CHUNK):
        c0 = n * SCONV_CHUNK
        bg = bcx[:, c0:c0 + SCONV_CHUNK]
        cg = bcx[:, D_MODEL + c0:D_MODEL + c0 + SCONV_CHUNK]
        xv = bcx[:, 2 * D_MODEL + c0:2 * D_MODEL + c0 + SCONV_CHUNK]
        v = cg * xv
        v_sink(n, v)
        gated.append((bg * conv_of(n, v)).astype(BF16))
    return x + _mm(jnp.concatenate(gated, axis=1), wout_ref[...])


def _sconv_prompt_kernel(x_ref, g_ref, win_ref, cw_ref, wout_ref, y_ref, vlast_ref, vcarry,
                         *, tm):
    j = pl.program_id(1)

    @pl.when(j == 0)
    def _():
        vcarry[...] = jnp.zeros(vcarry.shape, F32)

    def conv_of(n, v):
        c0 = n * SCONV_CHUNK
        return _dwconv_rows(v, vcarry[:, c0:c0 + SCONV_CHUNK], cw_ref, c0, SCONV_CHUNK)

    def v_sink(n, v):
        vlast_ref[0, :, n * SCONV_CHUNK:(n + 1) * SCONV_CHUNK] = v[tm - V7X_SUBLANES:, :]

    y_ref[...] = _sconv_body(x_ref[...], g_ref, win_ref, cw_ref, wout_ref, conv_of, v_sink)
    vcarry[...] = vlast_ref[0]


def _sconv_sample_kernel(x_ref, prev_ref, g_ref, win_ref, cw_ref, wout_ref, y_ref, v_ref):
    def conv_of(n, v):
        c0 = n * SCONV_CHUNK
        return _dwconv_groups(v, prev_ref[:, c0:c0 + SCONV_CHUNK], cw_ref, c0, SCONV_CHUNK)

    def v_sink(n, v):
        v_ref[:, n * SCONV_CHUNK:(n + 1) * SCONV_CHUNK] = v

    y_ref[...] = _sconv_body(x_ref[...], g_ref, win_ref, cw_ref, wout_ref, conv_of, v_sink)


def _sconv_prompt(x, params, batch, seq):
    tm = SCONV_ROW_TILE
    nt = seq // tm
    row = pl.BlockSpec((tm, D_MODEL), lambda b, j: (b * nt + j, 0))
    last = pl.BlockSpec((1, V7X_SUBLANES, D_MODEL), lambda b, j: (b, 0, 0))
    return pl.pallas_call(
        functools.partial(_sconv_prompt_kernel, tm=tm),
        grid=(batch, nt),
        in_specs=[row] + [_spec_of(a) for a in params],
        out_specs=[row, last],
        out_shape=[jax.ShapeDtypeStruct((batch * seq, D_MODEL), F32),
                   jax.ShapeDtypeStruct((batch, V7X_SUBLANES, D_MODEL), F32)],
        scratch_shapes=[pltpu.VMEM((V7X_SUBLANES, D_MODEL), F32)],
        compiler_params=pltpu.CompilerParams(
            dimension_semantics=("arbitrary", "arbitrary"),
            vmem_limit_bytes=_vmem_limit(48 * 1024 * 1024)),
        name="sconv_prompt",
    )(x, *[_arg_of(a) for a in params])


def _sconv_sample(x, row0, prev, params):
    m = prev.shape[0]
    tm = min(ROW_TILE, m)
    row = pl.BlockSpec((tm, D_MODEL), lambda i: (i, 0))
    row_in = pl.BlockSpec((tm, D_MODEL), lambda i: (i + row0 // tm, 0))
    return pl.pallas_call(
        _sconv_sample_kernel,
        grid=(m // tm,),
        in_specs=[row_in, row] + [_spec_of(a) for a in params],
        out_specs=[row, row],
        out_shape=[jax.ShapeDtypeStruct((m, D_MODEL), F32)] * 2,
        compiler_params=pltpu.CompilerParams(
            dimension_semantics=("arbitrary",),
            vmem_limit_bytes=_vmem_limit(48 * 1024 * 1024)),
        name="sconv_sample",
    )(x, prev, *[_arg_of(a) for a in params])


def _rope_tables(pos):
    half = HEAD_DIM // 2
    inv = ROPE_THETA ** (-jnp.arange(half, dtype=F32) / half)
    ang = pos.astype(F32)[:, None] * inv[None, :]
    cos = jnp.cos(ang)
    sin = jnp.sin(ang)
    reps = V7X_LANES // HEAD_DIM
    cos_t = jnp.tile(jnp.concatenate([cos, cos], axis=-1), (1, reps))
    sin_t = jnp.tile(jnp.concatenate([-sin, sin], axis=-1), (1, reps))
    return cos_t, sin_t


def _row(v):
    return v.reshape(1, -1)


def _pad_state_rows(buf):
    b, k, c = buf.shape
    padded = jnp.concatenate([jnp.zeros((b, V7X_SUBLANES - k, c), buf.dtype), buf], axis=1)
    return padded.reshape(b * V7X_SUBLANES, c)


def kernel(x_prompt, x_sample, cache_k, cache_v, state_rglru_h, state_rglru_conv, state_shortconv,
           norm_mixer, norm_ffn, norm_final,
           attn_w_qkv, attn_b_qkv, attn_w_o, attn_b_o, attn_sinks,
           rglru_w_gate, rglru_w_in, rglru_conv_w, rglru_conv_b, rglru_wa, rglru_ba,
           rglru_wx, rglru_bx, rglru_lambda, rglru_w_out,
           sconv_w_in, sconv_conv_w, sconv_w_out,
           ffn_w_gate, ffn_w_up, ffn_w_down):
    bp, seq, _ = x_prompt.shape
    bs, t_new, _ = x_sample.shape
    depth = norm_mixer.shape[0]
    past_len = 8192
    assert t_new == V7X_SUBLANES

    xp = x_prompt.reshape(bp * seq, D_MODEL)
    xs = x_sample.reshape(bs * t_new, D_MODEL)
    s_row0 = 0

    cos_p, sin_p = _rope_tables(jnp.arange(seq, dtype=jnp.int32))
    cos_s, sin_s = _rope_tables(past_len + jnp.arange(t_new, dtype=jnp.int32))
    cos_s = jnp.tile(cos_s, (SAMPLE_ATTN_BATCHES, 1))
    sin_s = jnp.tile(sin_s, (SAMPLE_ATTN_BATCHES, 1))

    to_lane_major = (0, 1, 3, 4, 2)
    from_lane_major = (0, 1, 4, 2, 3)
    ck_t = jnp.transpose(cache_k, to_lane_major)
    cv_t = jnp.transpose(cache_v, to_lane_major)
    nk_t = jnp.zeros(ck_t.shape, F32)
    nv_t = jnp.zeros(cv_t.shape, F32)

    kp_l, vp_l = [], []
    hp_l, hs_l, rcp_l, rcs_l = [], [], [], []
    scp_l, scs_l = [], []

    for i in range(depth):
        kind = i % 3
        j = i // 3
        g_mix = _row(norm_mixer[i])
        if kind == 0:
            wqkv = _layer(attn_w_qkv, j)
            wo = _layer(attn_w_o, j)
            bqkv = _row(attn_b_qkv[j])
            bo = _row(attn_b_o[j])
            sinks = attn_sinks[j]
            xp, kp, vp = _attn_prompt(xp, g_mix, wqkv, bqkv, wo, bo, cos_p, sin_p, sinks, bp, seq)
            xs, nk_t, nv_t = _attn_sample(xs, s_row0, g_mix, wqkv, bqkv, wo, bo, cos_s, sin_s,
                                          sinks, ck_t, cv_t, nk_t, nv_t, j, t_new)
            kp_l.append(kp.reshape(bp, WINDOW, N_KV_HEADS, HEAD_DIM))
            vp_l.append(vp.reshape(bp, WINDOW, N_KV_HEADS, HEAD_DIM))
        elif kind == 1:
            params = (g_mix, rglru_w_gate[j].astype(BF16), rglru_w_in[j].astype(BF16),
                      rglru_conv_w[j], _row(rglru_conv_b[j]), rglru_wa[j].astype(BF16),
                      _row(rglru_ba[j]), rglru_wx[j].astype(BF16), _row(rglru_bx[j]),
                      _row(rglru_lambda[j]), rglru_w_out[j].astype(BF16))
            half = [(rglru_wa[j] * 0.5).astype(BF16), (rglru_wx[j] * 0.5).astype(BF16)]
            prompt_params = params[:5] + (half[0], params[6], half[1]) + params[8:]
            xp, hlast, ulast = _rglru_prompt(xp, prompt_params, bp, seq)
            hp_l.append(hlast[:, V7X_SUBLANES - 1])
            rcp_l.append(ulast[:, V7X_SUBLANES - (RG_CONV_W - 1):])
            prev = _pad_state_rows(state_rglru_conv[j])
            hinit = jnp.repeat(state_rglru_h[j], t_new, axis=0)
            xs, hs_all, u_all = _rglru_sample(xs, s_row0, prev, hinit, params, t_new)
            hs_l.append(hs_all.reshape(bs, t_new, D_MODEL)[:, t_new - 1])
            rcs_l.append(u_all.reshape(bs, t_new, D_MODEL)[:, t_new - (RG_CONV_W - 1):])
        else:
            params = (g_mix, _layer(sconv_w_in, j), sconv_conv_w[j], _layer(sconv_w_out, j))
            xp, vlast = _sconv_prompt(xp, params, bp, seq)
            scp_l.append(vlast[:, V7X_SUBLANES - (SCONV_W - 1):])
            prev = _pad_state_rows(state_shortconv[j])
            xs, v_all = _sconv_sample(xs, s_row0, prev, params)
            scs_l.append(v_all.reshape(bs, t_new, D_MODEL)[:, t_new - (SCONV_W - 1):])

        wg = _layer(ffn_w_gate, i)
        wu = _layer(ffn_w_up, i)
        wd = _layer(ffn_w_down, i)
        d_ff = ffn_w_down.shape[1]
        g_ffn = _row(norm_ffn[i])
        g_fin = _row(norm_final)
        last = i == depth - 1
        if last:
            xp = _ffn(xp, bp * seq, None, g_ffn, wg, wu, wd, d_ff, g_fin, True)
            xs = _ffn(xs, bs * t_new, None, g_ffn, wg, wu, wd, d_ff, g_fin, True)
        else:
            xp = xs = _ffn(xp, bp * seq, xs, g_ffn, wg, wu, wd, d_ff, g_fin, False)
            s_row0 = bp * seq

    return (xp.reshape(bp, seq, D_MODEL), xs.reshape(bs, t_new, D_MODEL),
            jnp.stack(kp_l), jnp.stack(vp_l),
            jnp.transpose(nk_t, from_lane_major), jnp.transpose(nv_t, from_lane_major),
            jnp.stack(hp_l), jnp.stack(hs_l), jnp.stack(rcp_l), jnp.stack(rcs_l),
            jnp.stack(scp_l), jnp.stack(scs_l))
```

```python
import functools

import jax
import jax.numpy as jnp
from jax import lax
from jax.experimental import pallas as pl
from jax.experimental.pallas import tpu as pltpu

D_MODEL = 1024
HEAD_DIM = 64
N_HEADS = 16
N_KV_HEADS = 2
GQA_GROUP = 8
Q_DIM = N_HEADS * HEAD_DIM
KV_DIM = N_KV_HEADS * HEAD_DIM
WINDOW = 128
ROPE_THETA = 10000.0
NEG_INF = -1e30
RG_BLOCKS = 4
RG_BLOCK_W = 256
RG_CONV_W = 4
RG_C = 8.0
SCONV_W = 3
LOG2_E = 1.4426950408889634
EPS = 1e-6

V7X_LANES = 128
V7X_SUBLANES = 8
V7X_MXU_DIM = 256
V7X_VMEM_BYTES = 64 * 1024 * 1024

BF16 = jnp.bfloat16
F32 = jnp.float32

FFN_CHUNK = V7X_MXU_DIM
FFN_ROW_TILE = 512
FFN_STAGING_SLOTS = 2
ROW_TILE = 512
SCONV_ROW_TILE = 1024
ATTN_ROW_TILE = 1024
RG_ROW_TILE = 256
RG_PROMPT_TILE = 512
RG_PROMPT_SUB = 512
RG_SEG_ARRAYS = 5
SAMPLE_ATTN_BATCHES = 32
SAMPLE_ATTN_UNROLL = 8
ATTN_LOOKAHEAD = 2
PROMPT_ATTN_COLS = 2
PROMPT_ATTN_BLOCKS = 1
COLS_PER_KV = (N_HEADS // N_KV_HEADS) * HEAD_DIM // V7X_LANES


def _vmem_limit(nbytes):
    return int(min(nbytes, V7X_VMEM_BYTES - 8 * 1024 * 1024))


def _const_spec(shape):
    nd = len(shape)
    return pl.BlockSpec(shape, lambda *_: (0,) * nd, pipeline_mode=pl.Buffered(1))


def _layer(stacked, layer):
    return (stacked, layer)


def _spec_of(w):
    if isinstance(w, tuple):
        stacked, layer = w
        nd = stacked.ndim - 1
        return pl.BlockSpec((None,) + stacked.shape[1:], lambda *_: (layer,) + (0,) * nd,
                            pipeline_mode=pl.Buffered(1))
    return _const_spec(w.shape)


def _arg_of(w):
    return w[0] if isinstance(w, tuple) else w


def _nbytes_of(w):
    if isinstance(w, tuple):
        return w[0][0].size * w[0].dtype.itemsize
    return w.size * w.dtype.itemsize


def _rms(x, g):
    ms = jnp.mean(x * x, axis=-1, keepdims=True)
    return x * lax.rsqrt(ms + EPS) * g


def _mm(a, w):
    return jnp.dot(a.astype(BF16), w.astype(BF16), preferred_element_type=F32)


def _ffn_kernel(*refs, n_chunks, final_norm, tiles_a, stacked, layer):
    if stacked:
        xa_ref, xb_ref = refs[:2]
        refs = refs[2:]
    else:
        xa_ref = refs[0]
        refs = refs[1:]
    (g_ref, wg_hbm, wu_hbm, wd_hbm, gf_ref, o_ref,
     wg_ref, wu_ref, wd_ref, stage_g, stage_u, stage_d, act_ref, sem) = refs
    step = pl.program_id(0)

    def chunk_copies(c):
        cs = slice(c * FFN_CHUNK, (c + 1) * FFN_CHUNK)
        slot = c % FFN_STAGING_SLOTS
        return (pltpu.make_async_copy(wg_hbm.at[layer, :, cs], stage_g.at[slot], sem.at[0, slot]),
                pltpu.make_async_copy(wu_hbm.at[layer, :, cs], stage_u.at[slot], sem.at[1, slot]),
                pltpu.make_async_copy(wd_hbm.at[layer, cs, :], stage_d.at[slot], sem.at[2, slot]))

    def run(fetch_weights):
        if fetch_weights:
            for c in range(min(FFN_STAGING_SLOTS, n_chunks)):
                for copy in chunk_copies(c):
                    copy.start()
        if stacked:
            x = jnp.where(step < tiles_a, xa_ref[...], xb_ref[...])
        else:
            x = xa_ref[...]
        h = _rms(x, g_ref[...]).astype(BF16)
        for c in range(n_chunks):
            cs = slice(c * FFN_CHUNK, (c + 1) * FFN_CHUNK)
            if fetch_weights:
                slot = c % FFN_STAGING_SLOTS
                for copy in chunk_copies(c):
                    copy.wait()
                wg_ref[:, cs] = stage_g[slot].astype(BF16)
                wu_ref[:, cs] = stage_u[slot].astype(BF16)
                wd_ref[cs, :] = stage_d[slot].astype(BF16)
                if c + FFN_STAGING_SLOTS < n_chunks:
                    for copy in chunk_copies(c + FFN_STAGING_SLOTS):
                        copy.start()
            gate = _mm(h, wg_ref[:, cs])
            up = _mm(h, wu_ref[:, cs])
            act_ref[:, cs] = ((gate * jax.nn.sigmoid(gate)) * up).astype(BF16)
        acc = x + jnp.dot(act_ref[...], wd_ref[...], preferred_element_type=F32)
        if final_norm:
            acc = _rms(acc, gf_ref[...])
        o_ref[...] = acc

    @pl.when(step == 0)
    def _():
        run(True)

    @pl.when(step != 0)
    def _():
        run(False)


def _ffn(xa, rows_a, xb, g, wg, wu, wd, d_ff, gf, final_norm):
    tm = FFN_ROW_TILE
    tiles_a = rows_a // tm
    stacked = xb is not None
    tiles_b = xb.shape[0] // tm if stacked else 0
    n_chunks = d_ff // FFN_CHUNK
    out_row = pl.BlockSpec((tm, D_MODEL), lambda i: (i, 0))
    if stacked:
        x_specs = [pl.BlockSpec((tm, D_MODEL), lambda i: (jnp.minimum(i, tiles_a - 1), 0)),
                   pl.BlockSpec((tm, D_MODEL), lambda i: (jnp.maximum(i - tiles_a, 0), 0),
                                pipeline_mode=pl.Buffered(1))]
        x_args = [xa, xb]
    else:
        x_specs = [out_row]
        x_args = [xa]
    (wg_all, layer), (wu_all, _), (wd_all, _) = wg, wu, wd
    bf16_weight_bytes = 2 * 3 * D_MODEL * d_ff
    staging_bytes = 4 * 3 * FFN_STAGING_SLOTS * D_MODEL * FFN_CHUNK
    in_hbm = pl.BlockSpec(memory_space=pl.ANY)
    return pl.pallas_call(
        functools.partial(_ffn_kernel, n_chunks=n_chunks, final_norm=final_norm,
                          tiles_a=tiles_a, stacked=stacked, layer=layer),
        grid=(tiles_a + tiles_b,),
        in_specs=x_specs + [_const_spec((1, D_MODEL)), in_hbm, in_hbm, in_hbm,
                            _const_spec((1, D_MODEL))],
        out_specs=out_row,
        out_shape=jax.ShapeDtypeStruct(((tiles_a + tiles_b) * tm, D_MODEL), F32),
        scratch_shapes=[pltpu.VMEM(wg_all.shape[1:], BF16),
                        pltpu.VMEM(wu_all.shape[1:], BF16),
                        pltpu.VMEM(wd_all.shape[1:], BF16),
                        pltpu.VMEM((FFN_STAGING_SLOTS, D_MODEL, FFN_CHUNK), F32),
                        pltpu.VMEM((FFN_STAGING_SLOTS, D_MODEL, FFN_CHUNK), F32),
                        pltpu.VMEM((FFN_STAGING_SLOTS, FFN_CHUNK, D_MODEL), F32),
                        pltpu.VMEM((tm, d_ff), BF16),
                        pltpu.SemaphoreType.DMA((3, FFN_STAGING_SLOTS))],
        compiler_params=pltpu.CompilerParams(
            dimension_semantics=("arbitrary",),
            vmem_limit_bytes=_vmem_limit(bf16_weight_bytes + staging_bytes
                                         + 16 * tm * D_MODEL * 4)),
        name="ffn",
    )(*x_args, g, wg_all, wu_all, wd_all, gf)


def _rope_cols(cols, cos, sin_signed, first_half):
    swapped = jnp.where(first_half,
                        pltpu.roll(cols, V7X_LANES - HEAD_DIM // 2, 1),
                        pltpu.roll(cols, HEAD_DIM // 2, 1))
    return cols * cos + swapped * sin_signed


def _pair_blockdiag(mat, mat_rolled, kv, lane_lt64):
    zero = jnp.zeros_like(mat)
    if kv == 0:
        top = jnp.where(lane_lt64, mat, zero)
        bottom = jnp.where(lane_lt64, zero, mat_rolled)
    else:
        top = jnp.where(lane_lt64, mat_rolled, zero)
        bottom = jnp.where(lane_lt64, zero, mat)
    return jnp.concatenate([top, bottom], axis=0).astype(BF16)


def _key_blockdiag(k2, kv):
    lane_lt64 = lax.broadcasted_iota(jnp.int32, k2.shape, 1) < HEAD_DIM
    return _pair_blockdiag(k2, pltpu.roll(k2, HEAD_DIM, 1), kv, lane_lt64)


def _value_blockdiag(v2, kv):
    v2 = jnp.where(lax.broadcasted_iota(jnp.int32, v2.shape, 0) == 0, 0.0, v2)
    vbd = _key_blockdiag(v2, kv)
    row = lax.broadcasted_iota(jnp.int32, vbd.shape, 0)
    lane = lax.broadcasted_iota(jnp.int32, vbd.shape, 1)
    ones_bd = jnp.where((lane < HEAD_DIM) == (row < 2 * WINDOW), 1.0, 0.0).astype(BF16)
    return jnp.concatenate([vbd, ones_bd], axis=1)


def _attn_weights(s_all, mask, sink_ref, kv, cols, tq):
    nk = 2 * WINDOW
    key = lax.broadcasted_iota(jnp.int32, (1, nk), 1)
    e_rows = []
    for i, c in enumerate(cols):
        e_halves = []
        for parity in range(2):
            sink = sink_ref[kv * GQA_GROUP + 2 * c + parity]
            fill = jnp.where(key == 0, sink, NEG_INF)
            s = s_all[i * tq:(i + 1) * tq, parity * nk:(parity + 1) * nk]
            s = jnp.where(mask, s, fill)
            m = jnp.max(s, axis=-1, keepdims=True)
            e_halves.append(jnp.exp(s - m))
        e_rows.append(jnp.concatenate(e_halves, axis=1))
    return jnp.concatenate(e_rows, axis=0).astype(BF16)


def _key_blockdiag_t(kt, kv):
    del kv
    zero = jnp.zeros_like(kt)
    return jnp.concatenate([jnp.concatenate([kt, zero], axis=1),
                            jnp.concatenate([zero, kt], axis=1)], axis=0).astype(BF16)


def _value_blockdiag_t(vt, kv):
    del kv
    vt = jnp.where(lax.broadcasted_iota(jnp.int32, vt.shape, 1) == 0, 0.0, vt)
    zero = jnp.zeros_like(vt)
    one = jnp.ones_like(vt)
    return jnp.concatenate([jnp.concatenate([vt, zero], axis=1),
                            jnp.concatenate([zero, vt], axis=1),
                            jnp.concatenate([one, zero], axis=1),
                            jnp.concatenate([zero, one], axis=1)], axis=0).astype(BF16)


def _dot_nt(a, b):
    return lax.dot_general(a, b, (((1,), (1,)), ((), ())), preferred_element_type=F32)


def _dot_nn(a, b):
    return jnp.dot(a, b, preferred_element_type=F32)


ROW_MAJOR_KV = (_key_blockdiag, _dot_nt, _value_blockdiag, _dot_nn)
LANE_MAJOR_KV = (_key_blockdiag_t, _dot_nn, _value_blockdiag_t, _dot_nt)


def _attend_units(units, sink_ref, tq, kv_ops):
    make_kbd, score_dot, make_vbd, value_dot = kv_ops
    n = len(units)
    scores, kbd, vbd = {}, {}, {}

    def issue_scores(i):
        load_q, load_k, _, _, kv, _, _, kv_id = units[i]
        if kv_id not in kbd:
            kbd[kv_id] = make_kbd(load_k(), kv)
        scores[i] = score_dot(load_q(), kbd[kv_id])

    for i in range(min(ATTN_LOOKAHEAD, n)):
        issue_scores(i)
    for i in range(n):
        if i + ATTN_LOOKAHEAD < n:
            issue_scores(i + ATTN_LOOKAHEAD)
        _, _, load_v, mask, kv, cols, store_o, kv_id = units[i]
        if kv_id not in vbd:
            vbd[kv_id] = make_vbd(load_v(), kv)
        e_all = _attn_weights(scores.pop(i), mask, sink_ref, kv, cols, tq)
        o_den = value_dot(e_all, vbd[kv_id])
        store_o(o_den[:, :V7X_LANES] * (1.0 / o_den[:, V7X_LANES:]))


def _band_mask(tq, col_min):
    row = lax.broadcasted_iota(jnp.int32, (tq, 2 * WINDOW), 0)
    col = lax.broadcasted_iota(jnp.int32, (tq, 2 * WINDOW), 1)
    prev_ok = (col < WINDOW) & (col > row)
    own_ok = (col >= WINDOW) & (col - WINDOW <= row)
    return (prev_ok | own_ok) & (col >= col_min)


def _attn_prompt_kernel(x_ref, g_ref, wqkv_ref, bqkv_ref, wo_ref, bo_ref, cos_ref, sin_ref,
                        sink_ref, y_ref, knew_ref, vnew_ref,
                        q_scr, k_scr, v_scr, a_scr, *, tq_tile, n_tiles):
    j = pl.program_id(1)
    n_blk = tq_tile // WINDOW

    @pl.when(j == 0)
    def _():
        k_scr[0:WINDOW, :] = jnp.zeros((WINDOW, V7X_LANES), F32)
        v_scr[0:WINDOW, :] = jnp.zeros((WINDOW, V7X_LANES), F32)

    x = x_ref[...]
    h = _rms(x, g_ref[...])
    qkv = _mm(h, wqkv_ref[...]) + bqkv_ref[...]
    cos = cos_ref[...]
    sin = sin_ref[...]
    lane = lax.broadcasted_iota(jnp.int32, (tq_tile, V7X_LANES), 1)
    first_half = (lane % HEAD_DIM) < (HEAD_DIM // 2)
    scale = HEAD_DIM ** -0.5
    for c in range(Q_DIM // V7X_LANES):
        qc = _rope_cols(qkv[:, c * V7X_LANES:(c + 1) * V7X_LANES], cos, sin, first_half)
        q_scr[:, c * V7X_LANES:(c + 1) * V7X_LANES] = (qc * scale).astype(BF16)
    k_new = _rope_cols(qkv[:, Q_DIM:Q_DIM + KV_DIM], cos, sin, first_half)
    v_new = qkv[:, Q_DIM + KV_DIM:]
    k_scr[WINDOW:, :] = k_new
    v_scr[WINDOW:, :] = v_new

    def unit(r0, mask, kv, cols, kv_id):
        def load_q():
            return jnp.concatenate(
                [q_scr[pl.ds(r0, WINDOW),
                       (kv * COLS_PER_KV + c) * V7X_LANES:(kv * COLS_PER_KV + c + 1) * V7X_LANES]
                 for c in cols], axis=0)

        def store_o(o):
            for i, c in enumerate(cols):
                col = kv * COLS_PER_KV + c
                a_scr[pl.ds(r0, WINDOW), col * V7X_LANES:(col + 1) * V7X_LANES] = (
                    o[i * WINDOW:(i + 1) * WINDOW].astype(BF16))

        return (load_q, lambda: k_scr[pl.ds(r0, 2 * WINDOW), :],
                lambda: v_scr[pl.ds(r0, 2 * WINDOW), :], mask, kv, cols, store_o, kv_id)

    col_groups = [tuple(range(c, c + PROMPT_ATTN_COLS))
                  for c in range(0, COLS_PER_KV, PROMPT_ATTN_COLS)]

    def blocks(i, carry):
        units = []
        for local in range(PROMPT_ATTN_BLOCKS):
            blk = i * PROMPT_ATTN_BLOCKS + local
            r0 = pl.multiple_of(blk * WINDOW, WINDOW)
            first = jnp.logical_and(j == 0, blk == 0)
            mask = _band_mask(WINDOW, jnp.where(first, WINDOW, 0))
            units += [unit(r0, mask, kv, cols, (local, kv)) for kv in range(N_KV_HEADS)
                      for cols in col_groups]
        _attend_units(units, sink_ref, WINDOW, ROW_MAJOR_KV)
        return carry

    lax.fori_loop(0, n_blk // PROMPT_ATTN_BLOCKS, blocks, 0)

    y_ref[...] = _mm(a_scr[...], wo_ref[...]) + bo_ref[...] + x

    k_scr[0:WINDOW, :] = k_new[tq_tile - WINDOW:, :]
    v_scr[0:WINDOW, :] = v_new[tq_tile - WINDOW:, :]

    @pl.when(j == n_tiles - 1)
    def _():
        knew_ref[0] = k_new[tq_tile - WINDOW:, :]
        vnew_ref[0] = v_new[tq_tile - WINDOW:, :]


def _attn_prompt(x, g, wqkv, bqkv, wo, bo, cos_t, sin_t, sinks, batch, seq):
    tq = ATTN_ROW_TILE
    nt = seq // tq
    row = pl.BlockSpec((tq, D_MODEL), lambda b, j: (b * nt + j, 0))
    tab = pl.BlockSpec((tq, V7X_LANES), lambda b, j: (j, 0))
    cache = pl.BlockSpec((1, WINDOW, V7X_LANES), lambda b, j: (b, 0, 0))
    return pl.pallas_call(
        functools.partial(_attn_prompt_kernel, tq_tile=tq, n_tiles=nt),
        grid=(batch, nt),
        in_specs=[row, _const_spec((1, D_MODEL)), _spec_of(wqkv),
                  _const_spec((1, Q_DIM + 2 * KV_DIM)), _spec_of(wo), _const_spec((1, D_MODEL)),
                  tab, tab, pl.BlockSpec(memory_space=pltpu.SMEM)],
        out_specs=[row, cache, cache],
        out_shape=[jax.ShapeDtypeStruct((batch * seq, D_MODEL), F32),
                   jax.ShapeDtypeStruct((batch, WINDOW, V7X_LANES), F32),
                   jax.ShapeDtypeStruct((batch, WINDOW, V7X_LANES), F32)],
        scratch_shapes=[pltpu.VMEM((tq, Q_DIM), BF16),
                        pltpu.VMEM((WINDOW + tq, V7X_LANES), F32),
                        pltpu.VMEM((WINDOW + tq, V7X_LANES), F32),
                        pltpu.VMEM((tq, Q_DIM), BF16)],
        compiler_params=pltpu.CompilerParams(
            dimension_semantics=("arbitrary", "arbitrary"),
            vmem_limit_bytes=_vmem_limit(48 * 1024 * 1024)),
        name="attn_prompt",
    )(x, g, _arg_of(wqkv), bqkv, _arg_of(wo), bo, cos_t, sin_t, sinks)


def _attn_sample_kernel(x_ref, g_ref, wqkv_ref, bqkv_ref, wo_ref, bo_ref, cos_ref, sin_ref,
                        sink_ref, ck_ref, cv_ref, nk_in, nv_in, y_ref, nk_ref, nv_ref,
                        q_scr, k_scr, v_scr, a_scr, *, n_seq, t_new):
    del nk_in, nv_in
    tm = n_seq * t_new
    x = x_ref[...]
    h = _rms(x, g_ref[...])
    qkv = _mm(h, wqkv_ref[...]) + bqkv_ref[...]
    cos = cos_ref[...]
    sin = sin_ref[...]
    lane = lax.broadcasted_iota(jnp.int32, (tm, V7X_LANES), 1)
    first_half = (lane % HEAD_DIM) < (HEAD_DIM // 2)
    scale = HEAD_DIM ** -0.5
    for c in range(Q_DIM // V7X_LANES):
        qc = _rope_cols(qkv[:, c * V7X_LANES:(c + 1) * V7X_LANES], cos, sin, first_half)
        q_scr[:, c * V7X_LANES:(c + 1) * V7X_LANES] = qc * scale
    k_scr[...] = _rope_cols(qkv[:, Q_DIM:Q_DIM + KV_DIM], cos, sin, first_half)
    v_scr[...] = qkv[:, Q_DIM + KV_DIM:]
    mask = _band_mask(t_new, 0)
    pad = jnp.zeros((WINDOW - t_new, V7X_LANES), F32)
    lane = lax.broadcasted_iota(jnp.int32, (HEAD_DIM, WINDOW), 1)

    def new_rows_t(scr, b):
        r0 = pl.multiple_of(b * t_new, t_new)
        return jnp.concatenate([scr[pl.ds(r0, t_new), :], pad], axis=0).T

    def seq_group(i, carry):
        seqs = [i * SAMPLE_ATTN_UNROLL + u for u in range(SAMPLE_ATTN_UNROLL)]
        k_new_t = [new_rows_t(k_scr, b) for b in seqs]
        v_new_t = [new_rows_t(v_scr, b) for b in seqs]

        def unit(u, b, kv):
            r0 = pl.multiple_of(b * t_new, t_new)
            head_rows = slice(kv * HEAD_DIM, (kv + 1) * HEAD_DIM)

            def load_q():
                return jnp.concatenate(
                    [q_scr[pl.ds(r0, t_new), (kv * COLS_PER_KV + c) * V7X_LANES:
                           (kv * COLS_PER_KV + c + 1) * V7X_LANES]
                     for c in range(COLS_PER_KV)], axis=0).astype(BF16)

            def load_k():
                return jnp.concatenate([ck_ref[b, kv], k_new_t[u][head_rows, :]], axis=1)

            def load_v():
                return jnp.concatenate([cv_ref[b, kv], v_new_t[u][head_rows, :]], axis=1)

            def store_o(o):
                for c in range(COLS_PER_KV):
                    col = kv * COLS_PER_KV + c
                    a_scr[pl.ds(r0, t_new), col * V7X_LANES:(col + 1) * V7X_LANES] = (
                        o[c * t_new:(c + 1) * t_new])

            return (load_q, load_k, load_v, mask, kv, tuple(range(COLS_PER_KV)), store_o,
                    (u, kv))

        _attend_units([unit(u, b, kv) for u, b in enumerate(seqs) for kv in range(N_KV_HEADS)],
                      sink_ref, t_new, LANE_MAJOR_KV)
        keep = lane < WINDOW - t_new
        for u, b in enumerate(seqs):
            for kv in range(N_KV_HEADS):
                head_rows = slice(kv * HEAD_DIM, (kv + 1) * HEAD_DIM)
                nk_ref[b, kv] = jnp.where(
                    keep, pltpu.roll(ck_ref[b, kv], WINDOW - t_new, 1),
                    pltpu.roll(k_new_t[u][head_rows, :], WINDOW - t_new, 1))
                nv_ref[b, kv] = jnp.where(
                    keep, pltpu.roll(cv_ref[b, kv], WINDOW - t_new, 1),
                    pltpu.roll(v_new_t[u][head_rows, :], WINDOW - t_new, 1))
        return carry

    lax.fori_loop(0, n_seq // SAMPLE_ATTN_UNROLL, seq_group, 0)
    y_ref[...] = _mm(a_scr[...], wo_ref[...]) + bo_ref[...] + x


def _attn_sample(x, row0, g, wqkv, bqkv, wo, bo, cos_t, sin_t, sinks, ck_t, cv_t, nk_t, nv_t,
                 layer, t_new):
    bg = SAMPLE_ATTN_BATCHES
    tm = bg * t_new
    n_batch = ck_t.shape[1]
    row = pl.BlockSpec((tm, D_MODEL), lambda i: (i, 0))
    row_in = pl.BlockSpec((tm, D_MODEL), lambda i: (i + row0 // tm, 0))
    cache = pl.BlockSpec((None, bg, N_KV_HEADS, HEAD_DIM, WINDOW), lambda i: (layer, i, 0, 0, 0))
    whole = pl.BlockSpec(memory_space=pl.ANY)
    n_in = 13
    return pl.pallas_call(
        functools.partial(_attn_sample_kernel, n_seq=bg, t_new=t_new),
        grid=(n_batch // bg,),
        in_specs=[row_in, _const_spec((1, D_MODEL)), _spec_of(wqkv),
                  _const_spec((1, Q_DIM + 2 * KV_DIM)), _spec_of(wo), _const_spec((1, D_MODEL)),
                  _const_spec((tm, V7X_LANES)), _const_spec((tm, V7X_LANES)),
                  pl.BlockSpec(memory_space=pltpu.SMEM), cache, cache, whole, whole],
        out_specs=[row, cache, cache],
        out_shape=[jax.ShapeDtypeStruct((n_batch * t_new, D_MODEL), F32),
                   jax.ShapeDtypeStruct(nk_t.shape, F32),
                   jax.ShapeDtypeStruct(nv_t.shape, F32)],
        input_output_aliases={n_in - 2: 1, n_in - 1: 2},
        scratch_shapes=[pltpu.VMEM((tm, Q_DIM), F32),
                        pltpu.VMEM((tm, V7X_LANES), F32),
                        pltpu.VMEM((tm, V7X_LANES), F32),
                        pltpu.VMEM((tm, Q_DIM), F32)],
        compiler_params=pltpu.CompilerParams(
            dimension_semantics=("arbitrary",),
            vmem_limit_bytes=_vmem_limit(48 * 1024 * 1024)),
        name="attn_sample",
    )(x, g, _arg_of(wqkv), bqkv, _arg_of(wo), bo, cos_t, sin_t, sinks, ck_t, cv_t, nk_t, nv_t)


def _dwconv_groups(u, prev, w_ref, col0, ncol):
    rows = u.shape[0]
    kw = w_ref.shape[0]
    sub = lax.broadcasted_iota(jnp.int32, u.shape, 0) % V7X_SUBLANES
    y = u * w_ref[kw - 1:kw, col0:col0 + ncol]
    for s in range(1, kw):
        from_prev = pltpu.roll(prev, (rows - V7X_SUBLANES + s) % rows, 0)
        from_self = pltpu.roll(u, s, 0)
        shifted = jnp.where(sub < s, from_prev, from_self)
        y = y + shifted * w_ref[kw - 1 - s:kw - s, col0:col0 + ncol]
    return y


def _dwconv_rows(u, carry8, w_ref, col0, ncol):
    kw = w_ref.shape[0]
    sub8 = lax.broadcasted_iota(jnp.int32, carry8.shape, 0)
    y = u * w_ref[kw - 1:kw, col0:col0 + ncol]
    for s in range(1, kw):
        rolled = pltpu.roll(u, s, 0)
        head = jnp.where(sub8 < s, pltpu.roll(carry8, s, 0), rolled[:V7X_SUBLANES])
        shifted = jnp.concatenate([head, rolled[V7X_SUBLANES:]], axis=0)
        y = y + shifted * w_ref[kw - 1 - s:kw - s, col0:col0 + ncol]
    return y


def _scan_groups(a, b):
    pos = lax.broadcasted_iota(jnp.int32, a.shape, 0) % V7X_SUBLANES
    shift = 1
    while shift < V7X_SUBLANES:
        ok = pos >= shift
        a_sh = jnp.where(ok, pltpu.roll(a, shift, 0), 1.0)
        b_sh = jnp.where(ok, pltpu.roll(b, shift, 0), 0.0)
        b = a * b_sh + b
        a = a * a_sh
        shift *= 2
    return a, b


def _scan_rows(a, b, h_row):
    a_grp, b_grp = _scan_groups(a, b)
    out = []
    h = h_row
    for g in range(a.shape[0] // V7X_SUBLANES):
        rows = slice(g * V7X_SUBLANES, (g + 1) * V7X_SUBLANES)
        hs = a_grp[rows] * h + b_grp[rows]
        out.append(hs)
        h = hs[V7X_SUBLANES - 1:, :]
    return jnp.concatenate(out, axis=0)


def _log_sigmoid(x):
    return jnp.minimum(x, 0.0) - jnp.log1p(jnp.exp(-jnp.abs(x)))


def _rglru_body(x, g_ref, wgate_ref, win_ref, cw_ref, cb_ref, wa_ref, ba_ref, wx_ref, bx_ref,
                lam_ref, wout_ref, conv_of, scan_of, u_sink, h_sink):
    h = _rms(x, g_ref[...]).astype(BF16)
    acc = x
    for n in range(RG_BLOCKS):
        c0 = n * RG_BLOCK_W
        cs = slice(c0, c0 + RG_BLOCK_W)
        gate = jax.nn.gelu(jnp.dot(h, wgate_ref[:, cs], preferred_element_type=F32))
        u0 = jnp.dot(h, win_ref[:, cs], preferred_element_type=F32)
        u_sink(n, u0)
        u = conv_of(n, u0) + cb_ref[:, cs]
        ub = u.astype(BF16)
        r = jax.nn.sigmoid(jnp.dot(ub, wa_ref[n], preferred_element_type=F32) + ba_ref[:, cs])
        ig = jax.nn.sigmoid(jnp.dot(ub, wx_ref[n], preferred_element_type=F32) + bx_ref[:, cs])
        log_a = RG_C * r * _log_sigmoid(lam_ref[:, cs])
        a = jnp.exp(log_a)
        mult = jnp.sqrt(-jnp.tanh(log_a) * (1.0 + a * a))
        hs = scan_of(n, a, mult * (ig * u))
        h_sink(n, hs)
        acc = acc + jnp.dot((hs * gate).astype(BF16), wout_ref[cs, :], preferred_element_type=F32)
    return acc


def _seg_pitch(seg_len):
    assert seg_len % V7X_SUBLANES == 0
    return seg_len + V7X_SUBLANES // 2


def _rows_to_segments(scr, slab0, x, seg_len):
    pitch = _seg_pitch(seg_len)
    for s in range(x.shape[1] // V7X_LANES):
        for i in range(V7X_SUBLANES):
            scr[slab0 + s, i * pitch:i * pitch + seg_len, :] = (
                x[i * seg_len:(i + 1) * seg_len, s * V7X_LANES:(s + 1) * V7X_LANES])


def _segments_to_rows(scr, slab0, n_slabs, seg_len):
    pitch = _seg_pitch(seg_len)
    return jnp.concatenate(
        [jnp.concatenate([scr[slab0 + s, i * pitch:i * pitch + seg_len, :]
                          for i in range(V7X_SUBLANES)], axis=0)
         for s in range(n_slabs)], axis=1)


def _seg_step(scr, slab, k, seg_len):
    return scr[slab, pl.ds(k, V7X_SUBLANES, stride=_seg_pitch(seg_len)), :]


def _seg_step_store(scr, slab, k, seg_len, v):
    scr[slab, pl.ds(k, V7X_SUBLANES, stride=_seg_pitch(seg_len)), :] = v


def _from_prev_segment(v, first):
    sub = lax.broadcasted_iota(jnp.int32, v.shape, 0)
    return jnp.where(sub == 0, first, pltpu.roll(v, 1, 0))


def _rglru_prompt_kernel(x_ref, g_ref, wgate_ref, win_ref, cw_ref, cb_ref, wa_ref, ba_ref,
                         wx_ref, bx_ref, lam_ref, wout_ref, y_ref, hlast_ref, ulast_ref,
                         ucarry, hcarry, *seg_scr, tm, sub_rows):
    j = pl.program_id(1)
    slabs_per_chunk = RG_BLOCK_W // V7X_LANES
    kw = cw_ref.shape[0]
    n_sub = tm // sub_rows
    u0_scr, u_scr, r_scr, ig_scr, hs_scr = (seg_scr[i * n_sub:(i + 1) * n_sub]
                                            for i in range(RG_SEG_ARRAYS))

    @pl.when(j == 0)
    def _():
        ucarry[...] = jnp.zeros(ucarry.shape, F32)
        hcarry[...] = jnp.zeros(hcarry.shape, F32)

    n_slabs = D_MODEL // V7X_LANES
    seg_len = sub_rows // V7X_SUBLANES
    gates, conv_prev, h_prev = {}, {}, {}

    def project_in(sub):
        x = x_ref[sub * sub_rows:(sub + 1) * sub_rows, :]
        h = _rms(x, g_ref[...]).astype(BF16)
        gate_chunks, last_rows = [], []
        for n in range(RG_BLOCKS):
            cs = slice(n * RG_BLOCK_W, (n + 1) * RG_BLOCK_W)
            gate_chunks.append(
                jax.nn.gelu(jnp.dot(h, wgate_ref[:, cs], preferred_element_type=F32)))
            u0 = jnp.dot(h, win_ref[:, cs], preferred_element_type=F32)
            last_rows.append(u0[sub_rows - V7X_SUBLANES:, :])
            _rows_to_segments(u0_scr[sub], n * slabs_per_chunk, u0, seg_len)
            yield
        gates[sub] = gate_chunks
        conv_prev[sub + 1] = jnp.concatenate(last_rows, axis=1)

    def conv(sub):
        for slab in range(n_slabs):
            lanes = slice(slab * V7X_LANES, (slab + 1) * V7X_LANES)
            taps = [cw_ref[t:t + 1, lanes] for t in range(kw)]
            bias = cb_ref[:, lanes]
            steps = {k: _seg_step(u0_scr[sub], slab, k, seg_len) for k in range(seg_len)}
            for back in range(1, kw):
                steps[-back] = _from_prev_segment(
                    steps[seg_len - back],
                    conv_prev[sub][V7X_SUBLANES - back:V7X_SUBLANES - back + 1, lanes])
            for k in range(seg_len):
                u_k = steps[k] * taps[kw - 1] + bias
                for back in range(1, kw):
                    u_k = u_k + steps[k - back] * taps[kw - 1 - back]
                _seg_step_store(u_scr[sub], slab, k, seg_len, u_k)
            yield

    def project_gates(sub):
        for n in range(RG_BLOCKS):
            slab0 = n * slabs_per_chunk
            ub = _segments_to_rows(u_scr[sub], slab0, slabs_per_chunk, seg_len).astype(BF16)
            _rows_to_segments(r_scr[sub], slab0,
                              jnp.dot(ub, wa_ref[n], preferred_element_type=F32), seg_len)
            _rows_to_segments(ig_scr[sub], slab0,
                              jnp.dot(ub, wx_ref[n], preferred_element_type=F32), seg_len)
            yield

    def recur(sub):
        h_prev[sub + 1] = []
        for slab in range(n_slabs):
            lanes = slice(slab * V7X_LANES, (slab + 1) * V7X_LANES)
            half_scale = (-0.5 * RG_C) * _log_sigmoid(lam_ref[:, lanes])
            ba = 0.5 * ba_ref[:, lanes]
            bx = 0.5 * bx_ref[:, lanes]
            h_in = h_prev[sub][slab]
            a_cum, h_loc = [], []
            for k in range(seg_len):
                neg_log_a = half_scale * jnp.tanh(_seg_step(r_scr[sub], slab, k, seg_len) + ba) \
                    + half_scale
                ig = 0.5 * jnp.tanh(_seg_step(ig_scr[sub], slab, k, seg_len) + bx) + 0.5
                a = jnp.exp2(neg_log_a * (-LOG2_E))
                one_minus_a2 = jnp.tanh(neg_log_a) * (1.0 + a * a)
                mult = jnp.where(one_minus_a2 > 0.0, one_minus_a2 * lax.rsqrt(one_minus_a2), 0.0)
                b = mult * (ig * _seg_step(u_scr[sub], slab, k, seg_len))
                if k == 0:
                    a_cum.append(a)
                    h_loc.append(b)
                else:
                    a_cum.append(a * a_cum[-1])
                    h_loc.append(a * h_loc[-1] + b)
            a_seg, b_seg = _scan_groups(a_cum[-1], h_loc[-1])
            seg_end = a_seg * h_in + b_seg
            h_prev[sub + 1].append(seg_end[V7X_SUBLANES - 1:, :])
            h_start = _from_prev_segment(seg_end, h_in)
            for k in range(seg_len):
                _seg_step_store(hs_scr[sub], slab, k, seg_len, a_cum[k] * h_start + h_loc[k])
            yield

    def project_out(sub):
        rows = slice(sub * sub_rows, (sub + 1) * sub_rows)
        gate_chunks = gates.pop(sub)
        gated = []
        for n in range(RG_BLOCKS):
            hs = _segments_to_rows(hs_scr[sub], n * slabs_per_chunk, slabs_per_chunk, seg_len)
            if sub == n_sub - 1:
                hlast_ref[0, :, n * RG_BLOCK_W:(n + 1) * RG_BLOCK_W] = (
                    hs[sub_rows - V7X_SUBLANES:, :])
            gated.append((hs * gate_chunks[n]).astype(BF16))
        gated = jnp.concatenate(gated, axis=1)
        for n in range(RG_BLOCKS):
            cs = slice(n * RG_BLOCK_W, (n + 1) * RG_BLOCK_W)
            y_ref[rows, cs] = x_ref[rows, cs] + jnp.dot(gated, wout_ref[:, cs],
                                                        preferred_element_type=F32)
            yield

    conv_prev[0] = ucarry[...]
    h_prev[0] = [hcarry[V7X_SUBLANES - 1:V7X_SUBLANES, s * V7X_LANES:(s + 1) * V7X_LANES]
                 for s in range(n_slabs)]
    stages = [project_in, conv, project_gates, recur, project_out]
    for step in range(n_sub + len(stages) - 1):
        active = [stages[step - sub](sub) for sub in range(n_sub)
                  if 0 <= step - sub < len(stages)]
        while active:
            for piece in list(active):
                if next(piece, "done") == "done":
                    active.remove(piece)

    ulast_ref[0] = conv_prev[n_sub]
    ucarry[...] = conv_prev[n_sub]
    hcarry[...] = hlast_ref[0]


def _rglru_sample_kernel(x_ref, prev_ref, hinit_ref, g_ref, wgate_ref, win_ref, cw_ref, cb_ref,
                         wa_ref, ba_ref, wx_ref, bx_ref, lam_ref, wout_ref, y_ref, hs_ref, u_ref,
                         *, t_new):
    def conv_of(n, u0):
        c0 = n * RG_BLOCK_W
        return _dwconv_groups(u0, prev_ref[:, c0:c0 + RG_BLOCK_W], cw_ref, c0, RG_BLOCK_W)

    def scan_of(n, a, b):
        a_grp, b_grp = _scan_groups(a, b)
        return a_grp * hinit_ref[:, n * RG_BLOCK_W:(n + 1) * RG_BLOCK_W] + b_grp

    def u_sink(n, u0):
        u_ref[:, n * RG_BLOCK_W:(n + 1) * RG_BLOCK_W] = u0

    def h_sink(n, hs):
        hs_ref[:, n * RG_BLOCK_W:(n + 1) * RG_BLOCK_W] = hs

    y_ref[...] = _rglru_body(x_ref[...], g_ref, wgate_ref, win_ref, cw_ref, cb_ref, wa_ref, ba_ref,
                             wx_ref, bx_ref, lam_ref, wout_ref, conv_of, scan_of, u_sink, h_sink)


def _rglru_weight_specs(p):
    return [_const_spec(a.shape) for a in p]


def _rglru_prompt(x, params, batch, seq):
    tm = RG_PROMPT_TILE
    sub_rows = RG_PROMPT_SUB
    nt = seq // tm
    row = pl.BlockSpec((tm, D_MODEL), lambda b, j: (b * nt + j, 0))
    last = pl.BlockSpec((1, V7X_SUBLANES, D_MODEL), lambda b, j: (b, 0, 0))
    seg_scratch = pltpu.VMEM((D_MODEL // V7X_LANES,
                              V7X_SUBLANES * _seg_pitch(sub_rows // V7X_SUBLANES), V7X_LANES),
                             F32)
    return pl.pallas_call(
        functools.partial(_rglru_prompt_kernel, tm=tm, sub_rows=sub_rows),
        grid=(batch, nt),
        in_specs=[row] + _rglru_weight_specs(params),
        out_specs=[row, last, last],
        out_shape=[jax.ShapeDtypeStruct((batch * seq, D_MODEL), F32),
                   jax.ShapeDtypeStruct((batch, V7X_SUBLANES, D_MODEL), F32),
                   jax.ShapeDtypeStruct((batch, V7X_SUBLANES, D_MODEL), F32)],
        scratch_shapes=[pltpu.VMEM((V7X_SUBLANES, D_MODEL), F32),
                        pltpu.VMEM((V7X_SUBLANES, D_MODEL), F32)]
        + [seg_scratch] * (RG_SEG_ARRAYS * (tm // sub_rows)),
        compiler_params=pltpu.CompilerParams(
            dimension_semantics=("arbitrary", "arbitrary"),
            vmem_limit_bytes=_vmem_limit(48 * 1024 * 1024)),
        name="rglru_prompt",
    )(x, *params)


def _rglru_sample(x, row0, prev, hinit, params, t_new):
    m = prev.shape[0]
    tm = min(RG_ROW_TILE, m)
    row = pl.BlockSpec((tm, D_MODEL), lambda i: (i, 0))
    row_in = pl.BlockSpec((tm, D_MODEL), lambda i: (i + row0 // tm, 0))
    return pl.pallas_call(
        functools.partial(_rglru_sample_kernel, t_new=t_new),
        grid=(m // tm,),
        in_specs=[row_in, row, row] + _rglru_weight_specs(params),
        out_specs=[row, row, row],
        out_shape=[jax.ShapeDtypeStruct((m, D_MODEL), F32)] * 3,
        compiler_params=pltpu.CompilerParams(
            dimension_semantics=("arbitrary",),
            vmem_limit_bytes=_vmem_limit(48 * 1024 * 1024)),
        name="rglru_sample",
    )(x, prev, hinit, *params)


SCONV_CHUNK = V7X_MXU_DIM


def _sconv_body(x, g_ref, win_ref, cw_ref, wout_ref, conv_of, v_sink):
    h = _rms(x, g_ref[...]).astype(BF16)
    bcx = _mm(h, win_ref[...])
    gated = []
    for n in range(D_MODEL // SCONV_CHUNK):
        c0 = n * SCONV_CHUNK
        bg = bcx[:, c0:c0 + SCONV_CHUNK]
        cg = bcx[:, D_MODEL + c0:D_MODEL + c0 + SCONV_CHUNK]
        xv = bcx[:, 2 * D_MODEL + c0:2 * D_MODEL + c0 + SCONV_CHUNK]
        v = cg * xv
        v_sink(n, v)
        gated.append((bg * conv_of(n, v)).astype(BF16))
    return x + _mm(jnp.concatenate(gated, axis=1), wout_ref[...])


def _sconv_prompt_kernel(x_ref, g_ref, win_ref, cw_ref, wout_ref, y_ref, vlast_ref, vcarry,
                         *, tm):
    j = pl.program_id(1)

    @pl.when(j == 0)
    def _():
        vcarry[...] = jnp.zeros(vcarry.shape, F32)

    def conv_of(n, v):
        c0 = n * SCONV_CHUNK
        return _dwconv_rows(v, vcarry[:, c0:c0 + SCONV_CHUNK], cw_ref, c0, SCONV_CHUNK)

    def v_sink(n, v):
        vlast_ref[0, :, n * SCONV_CHUNK:(n + 1) * SCONV_CHUNK] = v[tm - V7X_SUBLANES:, :]

    y_ref[...] = _sconv_body(x_ref[...], g_ref, win_ref, cw_ref, wout_ref, conv_of, v_sink)
    vcarry[...] = vlast_ref[0]


def _sconv_sample_kernel(x_ref, prev_ref, g_ref, win_ref, cw_ref, wout_ref, y_ref, v_ref):
    def conv_of(n, v):
        c0 = n * SCONV_CHUNK
        return _dwconv_groups(v, prev_ref[:, c0:c0 + SCONV_CHUNK], cw_ref, c0, SCONV_CHUNK)

    def v_sink(n, v):
        v_ref[:, n * SCONV_CHUNK:(n + 1) * SCONV_CHUNK] = v

    y_ref[...] = _sconv_body(x_ref[...], g_ref, win_ref, cw_ref, wout_ref, conv_of, v_sink)


def _sconv_prompt(x, params, batch, seq):
    tm = SCONV_ROW_TILE
    nt = seq // tm
    row = pl.BlockSpec((tm, D_MODEL), lambda b, j: (b * nt + j, 0))
    last = pl.BlockSpec((1, V7X_SUBLANES, D_MODEL), lambda b, j: (b, 0, 0))
    return pl.pallas_call(
        functools.partial(_sconv_prompt_kernel, tm=tm),
        grid=(batch, nt),
        in_specs=[row] + [_spec_of(a) for a in params],
        out_specs=[row, last],
        out_shape=[jax.ShapeDtypeStruct((batch * seq, D_MODEL), F32),
                   jax.ShapeDtypeStruct((batch, V7X_SUBLANES, D_MODEL), F32)],
        scratch_shapes=[pltpu.VMEM((V7X_SUBLANES, D_MODEL), F32)],
        compiler_params=pltpu.CompilerParams(
            dimension_semantics=("arbitrary", "arbitrary"),
            vmem_limit_bytes=_vmem_limit(48 * 1024 * 1024)),
        name="sconv_prompt",
    )(x, *[_arg_of(a) for a in params])


def _sconv_sample(x, row0, prev, params):
    m = prev.shape[0]
    tm = min(ROW_TILE, m)
    row = pl.BlockSpec((tm, D_MODEL), lambda i: (i, 0))
    row_in = pl.BlockSpec((tm, D_MODEL), lambda i: (i + row0 // tm, 0))
    return pl.pallas_call(
        _sconv_sample_kernel,
        grid=(m // tm,),
        in_specs=[row_in, row] + [_spec_of(a) for a in params],
        out_specs=[row, row],
        out_shape=[jax.ShapeDtypeStruct((m, D_MODEL), F32)] * 2,
        compiler_params=pltpu.CompilerParams(
            dimension_semantics=("arbitrary",),
            vmem_limit_bytes=_vmem_limit(48 * 1024 * 1024)),
        name="sconv_sample",
    )(x, prev, *[_arg_of(a) for a in params])


def _rope_tables(pos):
    half = HEAD_DIM // 2
    inv = ROPE_THETA ** (-jnp.arange(half, dtype=F32) / half)
    ang = pos.astype(F32)[:, None] * inv[None, :]
    cos = jnp.cos(ang)
    sin = jnp.sin(ang)
    reps = V7X_LANES // HEAD_DIM
    cos_t = jnp.tile(jnp.concatenate([cos, cos], axis=-1), (1, reps))
    sin_t = jnp.tile(jnp.concatenate([-sin, sin], axis=-1), (1, reps))
    return cos_t, sin_t


def _row(v):
    return v.reshape(1, -1)


def _pad_state_rows(buf):
    b, k, c = buf.shape
    padded = jnp.concatenate([jnp.zeros((b, V7X_SUBLANES - k, c), buf.dtype), buf], axis=1)
    return padded.reshape(b * V7X_SUBLANES, c)


def kernel(x_prompt, x_sample, cache_k, cache_v, state_rglru_h, state_rglru_conv, state_shortconv,
           norm_mixer, norm_ffn, norm_final,
           attn_w_qkv, attn_b_qkv, attn_w_o, attn_b_o, attn_sinks,
           rglru_w_gate, rglru_w_in, rglru_conv_w, rglru_conv_b, rglru_wa, rglru_ba,
           rglru_wx, rglru_bx, rglru_lambda, rglru_w_out,
           sconv_w_in, sconv_conv_w, sconv_w_out,
           ffn_w_gate, ffn_w_up, ffn_w_down):
    bp, seq, _ = x_prompt.shape
    bs, t_new, _ = x_sample.shape
    depth = norm_mixer.shape[0]
    past_len = 8192
    assert t_new == V7X_SUBLANES

    xp = x_prompt.reshape(bp * seq, D_MODEL)
    xs = x_sample.reshape(bs * t_new, D_MODEL)
    s_row0 = 0

    cos_p, sin_p = _rope_tables(jnp.arange(seq, dtype=jnp.int32))
    cos_s, sin_s = _rope_tables(past_len + jnp.arange(t_new, dtype=jnp.int32))
    cos_s = jnp.tile(cos_s, (SAMPLE_ATTN_BATCHES, 1))
    sin_s = jnp.tile(sin_s, (SAMPLE_ATTN_BATCHES, 1))

    to_lane_major = (0, 1, 3, 4, 2)
    from_lane_major = (0, 1, 4, 2, 3)
    ck_t = jnp.transpose(cache_k, to_lane_major)
    cv_t = jnp.transpose(cache_v, to_lane_major)
    nk_t = jnp.zeros(ck_t.shape, F32)
    nv_t = jnp.zeros(cv_t.shape, F32)

    kp_l, vp_l = [], []
    hp_l, hs_l, rcp_l, rcs_l = [], [], [], []
    scp_l, scs_l = [], []

    for i in range(depth):
        kind = i % 3
        j = i // 3
        g_mix = _row(norm_mixer[i])
        if kind == 0:
            wqkv = _layer(attn_w_qkv, j)
            wo = _layer(attn_w_o, j)
            bqkv = _row(attn_b_qkv[j])
            bo = _row(attn_b_o[j])
            sinks = attn_sinks[j]
            xp, kp, vp = _attn_prompt(xp, g_mix, wqkv, bqkv, wo, bo, cos_p, sin_p, sinks, bp, seq)
            xs, nk_t, nv_t = _attn_sample(xs, s_row0, g_mix, wqkv, bqkv, wo, bo, cos_s, sin_s,
                                          sinks, ck_t, cv_t, nk_t, nv_t, j, t_new)
            kp_l.append(kp.reshape(bp, WINDOW, N_KV_HEADS, HEAD_DIM))
            vp_l.append(vp.reshape(bp, WINDOW, N_KV_HEADS, HEAD_DIM))
        elif kind == 1:
            params = (g_mix, rglru_w_gate[j].astype(BF16), rglru_w_in[j].astype(BF16),
                      rglru_conv_w[j], _row(rglru_conv_b[j]), rglru_wa[j].astype(BF16),
                      _row(rglru_ba[j]), rglru_wx[j].astype(BF16), _row(rglru_bx[j]),
                      _row(rglru_lambda[j]), rglru_w_out[j].astype(BF16))
            half = [(rglru_wa[j] * 0.5).astype(BF16), (rglru_wx[j] * 0.5).astype(BF16)]
            prompt_params = params[:5] + (half[0], params[6], half[1]) + params[8:]
            xp, hlast, ulast = _rglru_prompt(xp, prompt_params, bp, seq)
            hp_l.append(hlast[:, V7X_SUBLANES - 1])
            rcp_l.append(ulast[:, V7X_SUBLANES - (RG_CONV_W - 1):])
            prev = _pad_state_rows(state_rglru_conv[j])
            hinit = jnp.repeat(state_rglru_h[j], t_new, axis=0)
            xs, hs_all, u_all = _rglru_sample(xs, s_row0, prev, hinit, params, t_new)
            hs_l.append(hs_all.reshape(bs, t_new, D_MODEL)[:, t_new - 1])
            rcs_l.append(u_all.reshape(bs, t_new, D_MODEL)[:, t_new - (RG_CONV_W - 1):])
        else:
            params = (g_mix, _layer(sconv_w_in, j), sconv_conv_w[j], _layer(sconv_w_out, j))
            xp, vlast = _sconv_prompt(xp, params, bp, seq)
            scp_l.append(vlast[:, V7X_SUBLANES - (SCONV_W - 1):])
            prev = _pad_state_rows(state_shortconv[j])
            xs, v_all = _sconv_sample(xs, s_row0, prev, params)
            scs_l.append(v_all.reshape(bs, t_new, D_MODEL)[:, t_new - (SCONV_W - 1):])

        wg = _layer(ffn_w_gate, i)
        wu = _layer(ffn_w_up, i)
        wd = _layer(ffn_w_down, i)
        d_ff = ffn_w_down.shape[1]
        g_ffn = _row(norm_ffn[i])
        g_fin = _row(norm_final)
        last = i == depth - 1
        if last:
            xp = _ffn(xp, bp * seq, None, g_ffn, wg, wu, wd, d_ff, g_fin, True)
            xs = _ffn(xs, bs * t_new, None, g_ffn, wg, wu, wd, d_ff, g_fin, True)
        else:
            xp = xs = _ffn(xp, bp * seq, xs, g_ffn, wg, wu, wd, d_ff, g_fin, False)
            s_row0 = bp * seq

    return (xp.reshape(bp, seq, D_MODEL), xs.reshape(bs, t_new, D_MODEL),
            jnp.stack(kp_l), jnp.stack(vp_l),
            jnp.transpose(nk_t, from_lane_major), jnp.transpose(nv_t, from_lane_major),
            jnp.stack(hp_l), jnp.stack(hs_l), jnp.stack(rcp_l), jnp.stack(rcs_l),
            jnp.stack(scp_l), jnp.stack(scs_l))
```

```python
import functools

import jax
import jax.numpy as jnp
from jax import lax
from jax.experimental import pallas as pl
from jax.experimental.pallas import tpu as pltpu

D_MODEL = 1024
HEAD_DIM = 64
N_HEADS = 16
N_KV_HEADS = 2
GQA_GROUP = 8
Q_DIM = N_HEADS * HEAD_DIM
KV_DIM = N_KV_HEADS * HEAD_DIM
WINDOW = 128
ROPE_THETA = 10000.0
NEG_INF = -1e30
RG_BLOCKS = 4
RG_BLOCK_W = 256
RG_CONV_W = 4
RG_C = 8.0
SCONV_W = 3
LOG2_E = 1.4426950408889634
EPS = 1e-6
PAST_LEN = 8192

V7X_LANES = 128
V7X_SUBLANES = 8
V7X_MXU_DIM = 256
V7X_VMEM_BYTES = 64 * 1024 * 1024

BF16 = jnp.bfloat16
F32 = jnp.float32

FFN_CHUNK = V7X_MXU_DIM
FFN_ROW_TILE = 512
FFN_STAGING_SLOTS = 2
ROW_TILE = 512
SCONV_ROW_TILE = 1024
ATTN_ROW_TILE = 1024
RG_ROW_TILE = 256
RG_PROMPT_TILE = 512
RG_PROMPT_SUB = 512
RG_SEG_ARRAYS = 5
SAMPLE_ATTN_BATCHES = 32
SAMPLE_ATTN_UNROLL = 8
ATTN_LOOKAHEAD = 2
PROMPT_ATTN_COLS = 2
PROMPT_ATTN_BLOCKS = 1
COLS_PER_KV = (N_HEADS // N_KV_HEADS) * HEAD_DIM // V7X_LANES


def _vmem_limit(nbytes):
    return int(min(nbytes, V7X_VMEM_BYTES - 8 * 1024 * 1024))


def _const_spec(shape):
    nd = len(shape)
    return pl.BlockSpec(shape, lambda *_: (0,) * nd, pipeline_mode=pl.Buffered(1))


def _layer(stacked, layer):
    return (stacked, layer)


def _spec_of(w):
    if isinstance(w, tuple):
        stacked, layer = w
        nd = stacked.ndim - 1
        return pl.BlockSpec((None,) + stacked.shape[1:], lambda *_: (layer,) + (0,) * nd,
                            pipeline_mode=pl.Buffered(1))
    return _const_spec(w.shape)


def _arg_of(w):
    return w[0] if isinstance(w, tuple) else w


def _rms(x, g):
    ms = jnp.mean(x * x, axis=-1, keepdims=True)
    return x * lax.rsqrt(ms + EPS) * g


def _mm(a, w):
    return jnp.dot(a.astype(BF16), w.astype(BF16), preferred_element_type=F32)


def _ffn_kernel(*refs, n_chunks, final_norm, tiles_a, stacked, layer):
    if stacked:
        xa_ref, xb_ref = refs[:2]
        refs = refs[2:]
    else:
        xa_ref = refs[0]
        refs = refs[1:]
    (g_ref, wg_hbm, wu_hbm, wd_hbm, gf_ref, o_ref,
     wg_ref, wu_ref, wd_ref, stage_g, stage_u, stage_d, act_ref, sem) = refs
    step = pl.program_id(0)

    def chunk_copies(c):
        cs = slice(c * FFN_CHUNK, (c + 1) * FFN_CHUNK)
        slot = c % FFN_STAGING_SLOTS
        return (pltpu.make_async_copy(wg_hbm.at[layer, :, cs], stage_g.at[slot], sem.at[0, slot]),
                pltpu.make_async_copy(wu_hbm.at[layer, :, cs], stage_u.at[slot], sem.at[1, slot]),
                pltpu.make_async_copy(wd_hbm.at[layer, cs, :], stage_d.at[slot], sem.at[2, slot]))

    def run(fetch_weights):
        if fetch_weights:
            for c in range(min(FFN_STAGING_SLOTS, n_chunks)):
                for copy in chunk_copies(c):
                    copy.start()
        if stacked:
            x = jnp.where(step < tiles_a, xa_ref[...], xb_ref[...])
        else:
            x = xa_ref[...]
        h = _rms(x, g_ref[...]).astype(BF16)
        for c in range(n_chunks):
            cs = slice(c * FFN_CHUNK, (c + 1) * FFN_CHUNK)
            if fetch_weights:
                slot = c % FFN_STAGING_SLOTS
                for copy in chunk_copies(c):
                    copy.wait()
                wg_ref[:, cs] = stage_g[slot].astype(BF16)
                wu_ref[:, cs] = stage_u[slot].astype(BF16)
                wd_ref[cs, :] = stage_d[slot].astype(BF16)
                if c + FFN_STAGING_SLOTS < n_chunks:
                    for copy in chunk_copies(c + FFN_STAGING_SLOTS):
                        copy.start()
            gate = _mm(h, wg_ref[:, cs])
            up = _mm(h, wu_ref[:, cs])
            act_ref[:, cs] = ((gate * jax.nn.sigmoid(gate)) * up).astype(BF16)
        acc = x + jnp.dot(act_ref[...], wd_ref[...], preferred_element_type=F32)
        if final_norm:
            acc = _rms(acc, gf_ref[...])
        o_ref[...] = acc

    @pl.when(step == 0)
    def _():
        run(True)

    @pl.when(step != 0)
    def _():
        run(False)


def _ffn(xa, rows_a, xb, g, wg, wu, wd, d_ff, gf, final_norm):
    tm = FFN_ROW_TILE
    tiles_a = rows_a // tm
    stacked = xb is not None
    tiles_b = xb.shape[0] // tm if stacked else 0
    n_chunks = d_ff // FFN_CHUNK
    out_row = pl.BlockSpec((tm, D_MODEL), lambda i: (i, 0))
    if stacked:
        x_specs = [pl.BlockSpec((tm, D_MODEL), lambda i: (jnp.minimum(i, tiles_a - 1), 0)),
                   pl.BlockSpec((tm, D_MODEL), lambda i: (jnp.maximum(i - tiles_a, 0), 0),
                                pipeline_mode=pl.Buffered(1))]
        x_args = [xa, xb]
    else:
        x_specs = [out_row]
        x_args = [xa]
    (wg_all, layer), (wu_all, _), (wd_all, _) = wg, wu, wd
    bf16_weight_bytes = 2 * 3 * D_MODEL * d_ff
    staging_bytes = 4 * 3 * FFN_STAGING_SLOTS * D_MODEL * FFN_CHUNK
    in_hbm = pl.BlockSpec(memory_space=pl.ANY)
    return pl.pallas_call(
        functools.partial(_ffn_kernel, n_chunks=n_chunks, final_norm=final_norm,
                          tiles_a=tiles_a, stacked=stacked, layer=layer),
        grid=(tiles_a + tiles_b,),
        in_specs=x_specs + [_const_spec((1, D_MODEL)), in_hbm, in_hbm, in_hbm,
                            _const_spec((1, D_MODEL))],
        out_specs=out_row,
        out_shape=jax.ShapeDtypeStruct(((tiles_a + tiles_b) * tm, D_MODEL), F32),
        scratch_shapes=[pltpu.VMEM(wg_all.shape[1:], BF16),
                        pltpu.VMEM(wu_all.shape[1:], BF16),
                        pltpu.VMEM(wd_all.shape[1:], BF16),
                        pltpu.VMEM((FFN_STAGING_SLOTS, D_MODEL, FFN_CHUNK), F32),
                        pltpu.VMEM((FFN_STAGING_SLOTS, D_MODEL, FFN_CHUNK), F32),
                        pltpu.VMEM((FFN_STAGING_SLOTS, FFN_CHUNK, D_MODEL), F32),
                        pltpu.VMEM((tm, d_ff), BF16),
                        pltpu.SemaphoreType.DMA((3, FFN_STAGING_SLOTS))],
        compiler_params=pltpu.CompilerParams(
            dimension_semantics=("arbitrary",),
            vmem_limit_bytes=_vmem_limit(bf16_weight_bytes + staging_bytes
                                         + 16 * tm * D_MODEL * 4)),
        name="ffn",
    )(*x_args, g, wg_all, wu_all, wd_all, gf)


def _rope_cols(cols, cos, sin_signed, first_half):
    swapped = jnp.where(first_half,
                        pltpu.roll(cols, V7X_LANES - HEAD_DIM // 2, 1),
                        pltpu.roll(cols, HEAD_DIM // 2, 1))
    return cols * cos + swapped * sin_signed


def _pair_blockdiag(mat, mat_rolled, kv, lane_lt64):
    zero = jnp.zeros_like(mat)
    if kv == 0:
        top = jnp.where(lane_lt64, mat, zero)
        bottom = jnp.where(lane_lt64, zero, mat_rolled)
    else:
        top = jnp.where(lane_lt64, mat_rolled, zero)
        bottom = jnp.where(lane_lt64, zero, mat)
    return jnp.concatenate([top, bottom], axis=0).astype(BF16)


def _key_blockdiag(k2, kv):
    lane_lt64 = lax.broadcasted_iota(jnp.int32, k2.shape, 1) < HEAD_DIM
    return _pair_blockdiag(k2, pltpu.roll(k2, HEAD_DIM, 1), kv, lane_lt64)


def _value_blockdiag(v2, kv):
    v2 = jnp.where(lax.broadcasted_iota(jnp.int32, v2.shape, 0) == 0, 0.0, v2)
    vbd = _key_blockdiag(v2, kv)
    row = lax.broadcasted_iota(jnp.int32, vbd.shape, 0)
    lane = lax.broadcasted_iota(jnp.int32, vbd.shape, 1)
    ones_bd = jnp.where((lane < HEAD_DIM) == (row < 2 * WINDOW), 1.0, 0.0).astype(BF16)
    return jnp.concatenate([vbd, ones_bd], axis=1)


def _attn_weights(s_all, mask, sink_ref, kv, cols, tq):
    nk = 2 * WINDOW
    key = lax.broadcasted_iota(jnp.int32, (1, nk), 1)
    e_rows = []
    for i, c in enumerate(cols):
        e_halves = []
        for parity in range(2):
            sink = sink_ref[kv * GQA_GROUP + 2 * c + parity]
            fill = jnp.where(key == 0, sink, NEG_INF)
            s = s_all[i * tq:(i + 1) * tq, parity * nk:(parity + 1) * nk]
            s = jnp.where(mask, s, fill)
            m = jnp.max(s, axis=-1, keepdims=True)
            e_halves.append(jnp.exp(s - m))
        e_rows.append(jnp.concatenate(e_halves, axis=1))
    return jnp.concatenate(e_rows, axis=0).astype(BF16)


def _key_blockdiag_t(kt, kv):
    del kv
    zero = jnp.zeros_like(kt)
    return jnp.concatenate([jnp.concatenate([kt, zero], axis=1),
                            jnp.concatenate([zero, kt], axis=1)], axis=0).astype(BF16)


def _value_blockdiag_t(vt, kv):
    del kv
    vt = jnp.where(lax.broadcasted_iota(jnp.int32, vt.shape, 1) == 0, 0.0, vt)
    zero = jnp.zeros_like(vt)
    one = jnp.ones_like(vt)
    return jnp.concatenate([jnp.concatenate([vt, zero], axis=1),
                            jnp.concatenate([zero, vt], axis=1),
                            jnp.concatenate([one, zero], axis=1),
                            jnp.concatenate([zero, one], axis=1)], axis=0).astype(BF16)


def _dot_nt(a, b):
    return lax.dot_general(a, b, (((1,), (1,)), ((), ())), preferred_element_type=F32)


def _dot_nn(a, b):
    return jnp.dot(a, b, preferred_element_type=F32)


ROW_MAJOR_KV = (_key_blockdiag, _dot_nt, _value_blockdiag, _dot_nn)
LANE_MAJOR_KV = (_key_blockdiag_t, _dot_nn, _value_blockdiag_t, _dot_nt)


def _attend_units(units, sink_ref, tq, kv_ops):
    make_kbd, score_dot, make_vbd, value_dot = kv_ops
    n = len(units)
    scores, kbd, vbd = {}, {}, {}

    def issue_scores(i):
        load_q, load_k, _, _, kv, _, _, kv_id = units[i]
        if kv_id not in kbd:
            kbd[kv_id] = make_kbd(load_k(), kv)
        scores[i] = score_dot(load_q(), kbd[kv_id])

    for i in range(min(ATTN_LOOKAHEAD, n)):
        issue_scores(i)
    for i in range(n):
        if i + ATTN_LOOKAHEAD < n:
            issue_scores(i + ATTN_LOOKAHEAD)
        _, _, load_v, mask, kv, cols, store_o, kv_id = units[i]
        if kv_id not in vbd:
            vbd[kv_id] = make_vbd(load_v(), kv)
        e_all = _attn_weights(scores.pop(i), mask, sink_ref, kv, cols, tq)
        o_den = value_dot(e_all, vbd[kv_id])
        store_o(o_den[:, :V7X_LANES] * (1.0 / o_den[:, V7X_LANES:]))


def _band_mask(tq, col_min):
    row = lax.broadcasted_iota(jnp.int32, (tq, 2 * WINDOW), 0)
    col = lax.broadcasted_iota(jnp.int32, (tq, 2 * WINDOW), 1)
    prev_ok = (col < WINDOW) & (col > row)
    own_ok = (col >= WINDOW) & (col - WINDOW <= row)
    return (prev_ok | own_ok) & (col >= col_min)


def _attn_prompt_kernel(x_ref, g_ref, wqkv_ref, bqkv_ref, wo_ref, bo_ref, cos_ref, sin_ref,
                        sink_ref, y_ref, knew_ref, vnew_ref,
                        q_scr, k_scr, v_scr, a_scr, *, tq_tile, n_tiles):
    j = pl.program_id(1)
    n_blk = tq_tile // WINDOW

    @pl.when(j == 0)
    def _():
        k_scr[0:WINDOW, :] = jnp.zeros((WINDOW, V7X_LANES), F32)
        v_scr[0:WINDOW, :] = jnp.zeros((WINDOW, V7X_LANES), F32)

    x = x_ref[...]
    h = _rms(x, g_ref[...])
    qkv = _mm(h, wqkv_ref[...]) + bqkv_ref[...]
    cos = cos_ref[...]
    sin = sin_ref[...]
    lane = lax.broadcasted_iota(jnp.int32, (tq_tile, V7X_LANES), 1)
    first_half = (lane % HEAD_DIM) < (HEAD_DIM // 2)
    scale = HEAD_DIM ** -0.5
    for c in range(Q_DIM // V7X_LANES):
        qc = _rope_cols(qkv[:, c * V7X_LANES:(c + 1) * V7X_LANES], cos, sin, first_half)
        q_scr[:, c * V7X_LANES:(c + 1) * V7X_LANES] = (qc * scale).astype(BF16)
    k_new = _rope_cols(qkv[:, Q_DIM:Q_DIM + KV_DIM], cos, sin, first_half)
    v_new = qkv[:, Q_DIM + KV_DIM:]
    k_scr[WINDOW:, :] = k_new
    v_scr[WINDOW:, :] = v_new

    def unit(r0, mask, kv, cols, kv_id):
        def load_q():
            return jnp.concatenate(
                [q_scr[pl.ds(r0, WINDOW),
                       (kv * COLS_PER_KV + c) * V7X_LANES:(kv * COLS_PER_KV + c + 1) * V7X_LANES]
                 for c in cols], axis=0)

        def store_o(o):
            for i, c in enumerate(cols):
                col = kv * COLS_PER_KV + c
                a_scr[pl.ds(r0, WINDOW), col * V7X_LANES:(col + 1) * V7X_LANES] = (
                    o[i * WINDOW:(i + 1) * WINDOW].astype(BF16))

        return (load_q, lambda: k_scr[pl.ds(r0, 2 * WINDOW), :],
                lambda: v_scr[pl.ds(r0, 2 * WINDOW), :], mask, kv, cols, store_o, kv_id)

    col_groups = [tuple(range(c, c + PROMPT_ATTN_COLS))
                  for c in range(0, COLS_PER_KV, PROMPT_ATTN_COLS)]

    def blocks(i, carry):
        units = []
        for local in range(PROMPT_ATTN_BLOCKS):
            blk = i * PROMPT_ATTN_BLOCKS + local
            r0 = pl.multiple_of(blk * WINDOW, WINDOW)
            first = jnp.logical_and(j == 0, blk == 0)
            mask = _band_mask(WINDOW, jnp.where(first, WINDOW, 0))
            units += [unit(r0, mask, kv, cols, (local, kv)) for kv in range(N_KV_HEADS)
                      for cols in col_groups]
        _attend_units(units, sink_ref, WINDOW, ROW_MAJOR_KV)
        return carry

    lax.fori_loop(0, n_blk // PROMPT_ATTN_BLOCKS, blocks, 0)

    y_ref[...] = _mm(a_scr[...], wo_ref[...]) + bo_ref[...] + x

    k_scr[0:WINDOW, :] = k_new[tq_tile - WINDOW:, :]
    v_scr[0:WINDOW, :] = v_new[tq_tile - WINDOW:, :]

    @pl.when(j == n_tiles - 1)
    def _():
        knew_ref[0] = k_new[tq_tile - WINDOW:, :]
        vnew_ref[0] = v_new[tq_tile - WINDOW:, :]


def _attn_prompt(x, g, wqkv, bqkv, wo, bo, cos_t, sin_t, sinks, batch, seq):
    tq = ATTN_ROW_TILE
    nt = seq // tq
    row = pl.BlockSpec((tq, D_MODEL), lambda b, j: (b * nt + j, 0))
    tab = pl.BlockSpec((tq, V7X_LANES), lambda b, j: (j, 0))
    cache = pl.BlockSpec((1, WINDOW, V7X_LANES), lambda b, j: (b, 0, 0))
    return pl.pallas_call(
        functools.partial(_attn_prompt_kernel, tq_tile=tq, n_tiles=nt),
        grid=(batch, nt),
        in_specs=[row, _const_spec((1, D_MODEL)), _spec_of(wqkv),
                  _const_spec((1, Q_DIM + 2 * KV_DIM)), _spec_of(wo), _const_spec((1, D_MODEL)),
                  tab, tab, pl.BlockSpec(memory_space=pltpu.SMEM)],
        out_specs=[row, cache, cache],
        out_shape=[jax.ShapeDtypeStruct((batch * seq, D_MODEL), F32),
                   jax.ShapeDtypeStruct((batch, WINDOW, V7X_LANES), F32),
                   jax.ShapeDtypeStruct((batch, WINDOW, V7X_LANES), F32)],
        scratch_shapes=[pltpu.VMEM((tq, Q_DIM), BF16),
                        pltpu.VMEM((WINDOW + tq, V7X_LANES), F32),
                        pltpu.VMEM((WINDOW + tq, V7X_LANES), F32),
                        pltpu.VMEM((tq, Q_DIM), BF16)],
        compiler_params=pltpu.CompilerParams(
            dimension_semantics=("arbitrary", "arbitrary"),
            vmem_limit_bytes=_vmem_limit(48 * 1024 * 1024)),
        name="attn_prompt",
    )(x, g, _arg_of(wqkv), bqkv, _arg_of(wo), bo, cos_t, sin_t, sinks)


def _attn_sample_kernel(x_ref, g_ref, wqkv_ref, bqkv_ref, wo_ref, bo_ref, cos_ref, sin_ref,
                        sink_ref, ck_ref, cv_ref, nk_in, nv_in, y_ref, nk_ref, nv_ref,
                        q_scr, k_scr, v_scr, a_scr, *, n_seq, t_new):
    del nk_in, nv_in
    tm = n_seq * t_new
    x = x_ref[...]
    h = _rms(x, g_ref[...])
    qkv = _mm(h, wqkv_ref[...]) + bqkv_ref[...]
    cos = cos_ref[...]
    sin = sin_ref[...]
    lane = lax.broadcasted_iota(jnp.int32, (tm, V7X_LANES), 1)
    first_half = (lane % HEAD_DIM) < (HEAD_DIM // 2)
    scale = HEAD_DIM ** -0.5
    for c in range(Q_DIM // V7X_LANES):
        qc = _rope_cols(qkv[:, c * V7X_LANES:(c + 1) * V7X_LANES], cos, sin, first_half)
        q_scr[:, c * V7X_LANES:(c + 1) * V7X_LANES] = qc * scale
    k_scr[...] = _rope_cols(qkv[:, Q_DIM:Q_DIM + KV_DIM], cos, sin, first_half)
    v_scr[...] = qkv[:, Q_DIM + KV_DIM:]
    mask = _band_mask(t_new, 0)
    pad = jnp.zeros((WINDOW - t_new, V7X_LANES), F32)
    lane = lax.broadcasted_iota(jnp.int32, (HEAD_DIM, WINDOW), 1)

    def new_rows_t(scr, b):
        r0 = pl.multiple_of(b * t_new, t_new)
        return jnp.concatenate([scr[pl.ds(r0, t_new), :], pad], axis=0).T

    def seq_group(i, carry):
        seqs = [i * SAMPLE_ATTN_UNROLL + u for u in range(SAMPLE_ATTN_UNROLL)]
        k_new_t = [new_rows_t(k_scr, b) for b in seqs]
        v_new_t = [new_rows_t(v_scr, b) for b in seqs]

        def unit(u, b, kv):
            r0 = pl.multiple_of(b * t_new, t_new)
            head_rows = slice(kv * HEAD_DIM, (kv + 1) * HEAD_DIM)

            def load_q():
                return jnp.concatenate(
                    [q_scr[pl.ds(r0, t_new), (kv * COLS_PER_KV + c) * V7X_LANES:
                           (kv * COLS_PER_KV + c + 1) * V7X_LANES]
                     for c in range(COLS_PER_KV)], axis=0).astype(BF16)

            def load_k():
                return jnp.concatenate([ck_ref[b, kv], k_new_t[u][head_rows, :]], axis=1)

            def load_v():
                return jnp.concatenate([cv_ref[b, kv], v_new_t[u][head_rows, :]], axis=1)

            def store_o(o):
                for c in range(COLS_PER_KV):
                    col = kv * COLS_PER_KV + c
                    a_scr[pl.ds(r0, t_new), col * V7X_LANES:(col + 1) * V7X_LANES] = (
                        o[c * t_new:(c + 1) * t_new])

            return (load_q, load_k, load_v, mask, kv, tuple(range(COLS_PER_KV)), store_o,
                    (u, kv))

        _attend_units([unit(u, b, kv) for u, b in enumerate(seqs) for kv in range(N_KV_HEADS)],
                      sink_ref, t_new, LANE_MAJOR_KV)
        keep = lane < WINDOW - t_new
        for u, b in enumerate(seqs):
            for kv in range(N_KV_HEADS):
                head_rows = slice(kv * HEAD_DIM, (kv + 1) * HEAD_DIM)
                nk_ref[b, kv] = jnp.where(
                    keep, pltpu.roll(ck_ref[b, kv], WINDOW - t_new, 1),
                    pltpu.roll(k_new_t[u][head_rows, :], WINDOW - t_new, 1))
                nv_ref[b, kv] = jnp.where(
                    keep, pltpu.roll(cv_ref[b, kv], WINDOW - t_new, 1),
                    pltpu.roll(v_new_t[u][head_rows, :], WINDOW - t_new, 1))
        return carry

    lax.fori_loop(0, n_seq // SAMPLE_ATTN_UNROLL, seq_group, 0)
    y_ref[...] = _mm(a_scr[...], wo_ref[...]) + bo_ref[...] + x


def _attn_sample(x, row0, g, wqkv, bqkv, wo, bo, cos_t, sin_t, sinks, ck_t, cv_t, nk_t, nv_t,
                 layer, t_new):
    bg = SAMPLE_ATTN_BATCHES
    tm = bg * t_new
    n_batch = ck_t.shape[1]
    row = pl.BlockSpec((tm, D_MODEL), lambda i: (i, 0))
    row_in = pl.BlockSpec((tm, D_MODEL), lambda i: (i + row0 // tm, 0))
    cache = pl.BlockSpec((None, bg, N_KV_HEADS, HEAD_DIM, WINDOW), lambda i: (layer, i, 0, 0, 0))
    whole = pl.BlockSpec(memory_space=pl.ANY)
    n_in = 13
    return pl.pallas_call(
        functools.partial(_attn_sample_kernel, n_seq=bg, t_new=t_new),
        grid=(n_batch // bg,),
        in_specs=[row_in, _const_spec((1, D_MODEL)), _spec_of(wqkv),
                  _const_spec((1, Q_DIM + 2 * KV_DIM)), _spec_of(wo), _const_spec((1, D_MODEL)),
                  _const_spec((tm, V7X_LANES)), _const_spec((tm, V7X_LANES)),
                  pl.BlockSpec(memory_space=pltpu.SMEM), cache, cache, whole, whole],
        out_specs=[row, cache, cache],
        out_shape=[jax.ShapeDtypeStruct((n_batch * t_new, D_MODEL), F32),
                   jax.ShapeDtypeStruct(nk_t.shape, F32),
                   jax.ShapeDtypeStruct(nv_t.shape, F32)],
        input_output_aliases={n_in - 2: 1, n_in - 1: 2},
        scratch_shapes=[pltpu.VMEM((tm, Q_DIM), F32),
                        pltpu.VMEM((tm, V7X_LANES), F32),
                        pltpu.VMEM((tm, V7X_LANES), F32),
                        pltpu.VMEM((tm, Q_DIM), F32)],
        compiler_params=pltpu.CompilerParams(
            dimension_semantics=("arbitrary",),
            vmem_limit_bytes=_vmem_limit(48 * 1024 * 1024)),
        name="attn_sample",
    )(x, g, _arg_of(wqkv), bqkv, _arg_of(wo), bo, cos_t, sin_t, sinks, ck_t, cv_t, nk_t, nv_t)


def _dwconv_groups(u, prev, w_ref, col0, ncol):
    rows = u.shape[0]
    kw = w_ref.shape[0]
    sub = lax.broadcasted_iota(jnp.int32, u.shape, 0) % V7X_SUBLANES
    y = u * w_ref[kw - 1:kw, col0:col0 + ncol]
    for s in range(1, kw):
        from_prev = pltpu.roll(prev, (rows - V7X_SUBLANES + s) % rows, 0)
        from_self = pltpu.roll(u, s, 0)
        shifted = jnp.where(sub < s, from_prev, from_self)
        y = y + shifted * w_ref[kw - 1 - s:kw - s, col0:col0 + ncol]
    return y


def _dwconv_rows(u, carry8, w_ref, col0, ncol):
    kw = w_ref.shape[0]
    sub8 = lax.broadcasted_iota(jnp.int32, carry8.shape, 0)
    y = u * w_ref[kw - 1:kw, col0:col0 + ncol]
    for s in range(1, kw):
        rolled = pltpu.roll(u, s, 0)
        head = jnp.where(sub8 < s, pltpu.roll(carry8, s, 0), rolled[:V7X_SUBLANES])
        shifted = jnp.concatenate([head, rolled[V7X_SUBLANES:]], axis=0)
        y = y + shifted * w_ref[kw - 1 - s:kw - s, col0:col0 + ncol]
    return y


def _scan_groups(a, b):
    pos = lax.broadcasted_iota(jnp.int32, a.shape, 0) % V7X_SUBLANES
    shift = 1
    while shift < V7X_SUBLANES:
        ok = pos >= shift
        a_sh = jnp.where(ok, pltpu.roll(a, shift, 0), 1.0)
        b_sh = jnp.where(ok, pltpu.roll(b, shift, 0), 0.0)
        b = a * b_sh + b
        a = a * a_sh
        shift *= 2
    return a, b


def _log_sigmoid(x):
    return jnp.minimum(x, 0.0) - jnp.log1p(jnp.exp(-jnp.abs(x)))


def _rglru_body(x, g_ref, wgate_ref, win_ref, cw_ref, cb_ref, wa_ref, ba_ref, wx_ref, bx_ref,
                lam_ref, wout_ref, conv_of, scan_of, u_sink, h_sink):
    h = _rms(x, g_ref[...]).astype(BF16)
    acc = x
    for n in range(RG_BLOCKS):
        c0 = n * RG_BLOCK_W
        cs = slice(c0, c0 + RG_BLOCK_W)
        gate = jax.nn.gelu(jnp.dot(h, wgate_ref[:, cs], preferred_element_type=F32))
        u0 = jnp.dot(h, win_ref[:, cs], preferred_element_type=F32)
        u_sink(n, u0)
        u = conv_of(n, u0) + cb_ref[:, cs]
        ub = u.astype(BF16)
        r = jax.nn.sigmoid(jnp.dot(ub, wa_ref[n], preferred_element_type=F32) + ba_ref[:, cs])
        ig = jax.nn.sigmoid(jnp.dot(ub, wx_ref[n], preferred_element_type=F32) + bx_ref[:, cs])
        log_a = RG_C * r * _log_sigmoid(lam_ref[:, cs])
        a = jnp.exp(log_a)
        mult = jnp.sqrt(-jnp.tanh(log_a) * (1.0 + a * a))
        hs = scan_of(n, a, mult * (ig * u))
        h_sink(n, hs)
        acc = acc + jnp.dot((hs * gate).astype(BF16), wout_ref[cs, :], preferred_element_type=F32)
    return acc


def _seg_pitch(seg_len):
    assert seg_len % V7X_SUBLANES == 0
    return seg_len + V7X_SUBLANES // 2


def _rows_to_segments(scr, slab0, x, seg_len):
    pitch = _seg_pitch(seg_len)
    for s in range(x.shape[1] // V7X_LANES):
        for i in range(V7X_SUBLANES):
            scr[slab0 + s, i * pitch:i * pitch + seg_len, :] = (
                x[i * seg_len:(i + 1) * seg_len, s * V7X_LANES:(s + 1) * V7X_LANES])


def _segments_to_rows(scr, slab0, n_slabs, seg_len):
    pitch = _seg_pitch(seg_len)
    return jnp.concatenate(
        [jnp.concatenate([scr[slab0 + s, i * pitch:i * pitch + seg_len, :]
                          for i in range(V7X_SUBLANES)], axis=0)
         for s in range(n_slabs)], axis=1)


def _seg_step(scr, slab, k, seg_len):
    return scr[slab, pl.ds(k, V7X_SUBLANES, stride=_seg_pitch(seg_len)), :]


def _seg_step_store(scr, slab, k, seg_len, v):
    scr[slab, pl.ds(k, V7X_SUBLANES, stride=_seg_pitch(seg_len)), :] = v


def _from_prev_segment(v, first):
    sub = lax.broadcasted_iota(jnp.int32, v.shape, 0)
    return jnp.where(sub == 0, first, pltpu.roll(v, 1, 0))


def _rglru_prompt_kernel(x_ref, g_ref, wgate_ref, win_ref, cw_ref, cb_ref, wa_ref, ba_ref,
                         wx_ref, bx_ref, lam_ref, wout_ref, y_ref, hlast_ref, ulast_ref,
                         ucarry, hcarry, *seg_scr, tm, sub_rows):
    j = pl.program_id(1)
    slabs_per_chunk = RG_BLOCK_W // V7X_LANES
    kw = cw_ref.shape[0]
    n_sub = tm // sub_rows
    u0_scr, u_scr, r_scr, ig_scr, hs_scr = (seg_scr[i * n_sub:(i + 1) * n_sub]
                                            for i in range(RG_SEG_ARRAYS))

    @pl.when(j == 0)
    def _():
        ucarry[...] = jnp.zeros(ucarry.shape, F32)
        hcarry[...] = jnp.zeros(hcarry.shape, F32)

    n_slabs = D_MODEL // V7X_LANES
    seg_len = sub_rows // V7X_SUBLANES
    gates, conv_prev, h_prev = {}, {}, {}

    def project_in(sub):
        x = x_ref[sub * sub_rows:(sub + 1) * sub_rows, :]
        h = _rms(x, g_ref[...]).astype(BF16)
        gate_chunks, last_rows = [], []
        for n in range(RG_BLOCKS):
            cs = slice(n * RG_BLOCK_W, (n + 1) * RG_BLOCK_W)
            gate_chunks.append(
                jax.nn.gelu(jnp.dot(h, wgate_ref[:, cs], preferred_element_type=F32)))
            u0 = jnp.dot(h, win_ref[:, cs], preferred_element_type=F32)
            last_rows.append(u0[sub_rows - V7X_SUBLANES:, :])
            _rows_to_segments(u0_scr[sub], n * slabs_per_chunk, u0, seg_len)
            yield
        gates[sub] = gate_chunks
        conv_prev[sub + 1] = jnp.concatenate(last_rows, axis=1)

    def conv(sub):
        for slab in range(n_slabs):
            lanes = slice(slab * V7X_LANES, (slab + 1) * V7X_LANES)
            taps = [cw_ref[t:t + 1, lanes] for t in range(kw)]
            bias = cb_ref[:, lanes]
            steps = {k: _seg_step(u0_scr[sub], slab, k, seg_len) for k in range(seg_len)}
            for back in range(1, kw):
                steps[-back] = _from_prev_segment(
                    steps[seg_len - back],
                    conv_prev[sub][V7X_SUBLANES - back:V7X_SUBLANES - back + 1, lanes])
            for k in range(seg_len):
                u_k = steps[k] * taps[kw - 1] + bias
                for back in range(1, kw):
                    u_k = u_k + steps[k - back] * taps[kw - 1 - back]
                _seg_step_store(u_scr[sub], slab, k, seg_len, u_k)
            yield

    def project_gates(sub):
        for n in range(RG_BLOCKS):
            slab0 = n * slabs_per_chunk
            ub = _segments_to_rows(u_scr[sub], slab0, slabs_per_chunk, seg_len).astype(BF16)
            _rows_to_segments(r_scr[sub], slab0,
                              jnp.dot(ub, wa_ref[n], preferred_element_type=F32), seg_len)
            _rows_to_segments(ig_scr[sub], slab0,
                              jnp.dot(ub, wx_ref[n], preferred_element_type=F32), seg_len)
            yield

    def recur(sub):
        h_prev[sub + 1] = []
        for slab in range(n_slabs):
            lanes = slice(slab * V7X_LANES, (slab + 1) * V7X_LANES)
            half_scale = (-0.5 * RG_C) * _log_sigmoid(lam_ref[:, lanes])
            ba = 0.5 * ba_ref[:, lanes]
            bx = 0.5 * bx_ref[:, lanes]
            h_in = h_prev[sub][slab]
            a_cum, h_loc = [], []
            for k in range(seg_len):
                neg_log_a = half_scale * jnp.tanh(_seg_step(r_scr[sub], slab, k, seg_len) + ba) \
                    + half_scale
                ig = 0.5 * jnp.tanh(_seg_step(ig_scr[sub], slab, k, seg_len) + bx) + 0.5
                a = jnp.exp2(neg_log_a * (-LOG2_E))
                one_minus_a2 = jnp.tanh(neg_log_a) * (1.0 + a * a)
                mult = jnp.where(one_minus_a2 > 0.0, one_minus_a2 * lax.rsqrt(one_minus_a2), 0.0)
                b = mult * (ig * _seg_step(u_scr[sub], slab, k, seg_len))
                if k == 0:
                    a_cum.append(a)
                    h_loc.append(b)
                else:
                    a_cum.append(a * a_cum[-1])
                    h_loc.append(a * h_loc[-1] + b)
            a_seg, b_seg = _scan_groups(a_cum[-1], h_loc[-1])
            seg_end = a_seg * h_in + b_seg
            h_prev[sub + 1].append(seg_end[V7X_SUBLANES - 1:, :])
            h_start = _from_prev_segment(seg_end, h_in)
            for k in range(seg_len):
                _seg_step_store(hs_scr[sub], slab, k, seg_len, a_cum[k] * h_start + h_loc[k])
            yield

    def project_out(sub):
        rows = slice(sub * sub_rows, (sub + 1) * sub_rows)
        gate_chunks = gates.pop(sub)
        gated = []
        for n in range(RG_BLOCKS):
            hs = _segments_to_rows(hs_scr[sub], n * slabs_per_chunk, slabs_per_chunk, seg_len)
            if sub == n_sub - 1:
                hlast_ref[0, :, n * RG_BLOCK_W:(n + 1) * RG_BLOCK_W] = (
                    hs[sub_rows - V7X_SUBLANES:, :])
            gated.append((hs * gate_chunks[n]).astype(BF16))
        gated = jnp.concatenate(gated, axis=1)
        for n in range(RG_BLOCKS):
            cs = slice(n * RG_BLOCK_W, (n + 1) * RG_BLOCK_W)
            y_ref[rows, cs] = x_ref[rows, cs] + jnp.dot(gated, wout_ref[:, cs],
                                                        preferred_element_type=F32)
            yield

    conv_prev[0] = ucarry[...]
    h_prev[0] = [hcarry[V7X_SUBLANES - 1:V7X_SUBLANES, s * V7X_LANES:(s + 1) * V7X_LANES]
                 for s in range(n_slabs)]
    stages = [project_in, conv, project_gates, recur, project_out]
    for step in range(n_sub + len(stages) - 1):
        active = [stages[step - sub](sub) for sub in range(n_sub)
                  if 0 <= step - sub < len(stages)]
        while active:
            for piece in list(active):
                if next(piece, "done") == "done":
                    active.remove(piece)

    ulast_ref[0] = conv_prev[n_sub]
    ucarry[...] = conv_prev[n_sub]
    hcarry[...] = hlast_ref[0]


def _rglru_sample_kernel(x_ref, prev_ref, hinit_ref, g_ref, wgate_ref, win_ref, cw_ref, cb_ref,
                         wa_ref, ba_ref, wx_ref, bx_ref, lam_ref, wout_ref, y_ref, hs_ref, u_ref,
                         *, t_new):
    def conv_of(n, u0):
        c0 = n * RG_BLOCK_W
        return _dwconv_groups(u0, prev_ref[:, c0:c0 + RG_BLOCK_W], cw_ref, c0, RG_BLOCK_W)

    def scan_of(n, a, b):
        a_grp, b_grp = _scan_groups(a, b)
        return a_grp * hinit_ref[:, n * RG_BLOCK_W:(n + 1) * RG_BLOCK_W] + b_grp

    def u_sink(n, u0):
        u_ref[:, n * RG_BLOCK_W:(n + 1) * RG_BLOCK_W] = u0

    def h_sink(n, hs):
        hs_ref[:, n * RG_BLOCK_W:(n + 1) * RG_BLOCK_W] = hs

    y_ref[...] = _rglru_body(x_ref[...], g_ref, wgate_ref, win_ref, cw_ref, cb_ref, wa_ref, ba_ref,
                             wx_ref, bx_ref, lam_ref, wout_ref, conv_of, scan_of, u_sink, h_sink)


def _rglru_weight_specs(p):
    return [_const_spec(a.shape) for a in p]


def _rglru_prompt(x, params, batch, seq):
    tm = RG_PROMPT_TILE
    sub_rows = RG_PROMPT_SUB
    nt = seq // tm
    row = pl.BlockSpec((tm, D_MODEL), lambda b, j: (b * nt + j, 0))
    last = pl.BlockSpec((1, V7X_SUBLANES, D_MODEL), lambda b, j: (b, 0, 0))
    seg_scratch = pltpu.VMEM((D_MODEL // V7X_LANES,
                              V7X_SUBLANES * _seg_pitch(sub_rows // V7X_SUBLANES), V7X_LANES),
                             F32)
    return pl.pallas_call(
        functools.partial(_rglru_prompt_kernel, tm=tm, sub_rows=sub_rows),
        grid=(batch, nt),
        in_specs=[row] + _rglru_weight_specs(params),
        out_specs=[row, last, last],
        out_shape=[jax.ShapeDtypeStruct((batch * seq, D_MODEL), F32),
                   jax.ShapeDtypeStruct((batch, V7X_SUBLANES, D_MODEL), F32),
                   jax.ShapeDtypeStruct((batch, V7X_SUBLANES, D_MODEL), F32)],
        scratch_shapes=[pltpu.VMEM((V7X_SUBLANES, D_MODEL), F32),
                        pltpu.VMEM((V7X_SUBLANES, D_MODEL), F32)]
        + [seg_scratch] * (RG_SEG_ARRAYS * (tm // sub_rows)),
        compiler_params=pltpu.CompilerParams(
            dimension_semantics=("arbitrary", "arbitrary"),
            vmem_limit_bytes=_vmem_limit(48 * 1024 * 1024)),
        name="rglru_prompt",
    )(x, *params)


def _rglru_sample(x, row0, prev, hinit, params, t_new):
    m = prev.shape[0]
    tm = min(RG_ROW_TILE, m)
    row = pl.BlockSpec((tm, D_MODEL), lambda i: (i, 0))
    row_in = pl.BlockSpec((tm, D_MODEL), lambda i: (i + row0 // tm, 0))
    return pl.pallas_call(
        functools.partial(_rglru_sample_kernel, t_new=t_new),
        grid=(m // tm,),
        in_specs=[row_in, row, row] + _rglru_weight_specs(params),
        out_specs=[row, row, row],
        out_shape=[jax.ShapeDtypeStruct((m, D_MODEL), F32)] * 3,
        compiler_params=pltpu.CompilerParams(
            dimension_semantics=("arbitrary",),
            vmem_limit_bytes=_vmem_limit(48 * 1024 * 1024)),
        name="rglru_sample",
    )(x, prev, hinit, *params)


SCONV_CHUNK = V7X_MXU_DIM


def _sconv_body(x, g_ref, win_ref, cw_ref, wout_ref, conv_of, v_sink):
    h = _rms(x, g_ref[...]).astype(BF16)
    bcx = _mm(h, win_ref[...])
    gated = []
    for n in range(D_MODEL // SCONV_CHUNK):
        c0 = n * SCONV_CHUNK
        bg = bcx[:, c0:c0 + SCONV_CHUNK]
        cg = bcx[:, D_MODEL + c0:D_MODEL + c0 + SCONV_CHUNK]
        xv = bcx[:, 2 * D_MODEL + c0:2 * D_MODEL + c0 + SCONV_CHUNK]
        v = cg * xv
        v_sink(n, v)
        gated.append((bg * conv_of(n, v)).astype(BF16))
    return x + _mm(jnp.concatenate(gated, axis=1), wout_ref[...])


def _sconv_prompt_kernel(x_ref, g_ref, win_ref, cw_ref, wout_ref, y_ref, vlast_ref, vcarry,
                         *, tm):
    j = pl.program_id(1)

    @pl.when(j == 0)
    def _():
        vcarry[...] = jnp.zeros(vcarry.shape, F32)

    def conv_of(n, v):
        c0 = n * SCONV_CHUNK
        return _dwconv_rows(v, vcarry[:, c0:c0 + SCONV_CHUNK], cw_ref, c0, SCONV_CHUNK)

    def v_sink(n, v):
        vlast_ref[0, :, n * SCONV_CHUNK:(n + 1) * SCONV_CHUNK] = v[tm - V7X_SUBLANES:, :]

    y_ref[...] = _sconv_body(x_ref[...], g_ref, win_ref, cw_ref, wout_ref, conv_of, v_sink)
    vcarry[...] = vlast_ref[0]


def _sconv_sample_kernel(x_ref, prev_ref, g_ref, win_ref, cw_ref, wout_ref, y_ref, v_ref):
    def conv_of(n, v):
        c0 = n * SCONV_CHUNK
        return _dwconv_groups(v, prev_ref[:, c0:c0 + SCONV_CHUNK], cw_ref, c0, SCONV_CHUNK)

    def v_sink(n, v):
        v_ref[:, n * SCONV_CHUNK:(n + 1) * SCONV_CHUNK] = v

    y_ref[...] = _sconv_body(x_ref[...], g_ref, win_ref, cw_ref, wout_ref, conv_of, v_sink)


def _sconv_prompt(x, params, batch, seq):
    tm = SCONV_ROW_TILE
    nt = seq // tm
    row = pl.BlockSpec((tm, D_MODEL), lambda b, j: (b * nt + j, 0))
    last = pl.BlockSpec((1, V7X_SUBLANES, D_MODEL), lambda b, j: (b, 0, 0))
    return pl.pallas_call(
        functools.partial(_sconv_prompt_kernel, tm=tm),
        grid=(batch, nt),
        in_specs=[row] + [_spec_of(a) for a in params],
        out_specs=[row, last],
        out_shape=[jax.ShapeDtypeStruct((batch * seq, D_MODEL), F32),
                   jax.ShapeDtypeStruct((batch, V7X_SUBLANES, D_MODEL), F32)],
        scratch_shapes=[pltpu.VMEM((V7X_SUBLANES, D_MODEL), F32)],
        compiler_params=pltpu.CompilerParams(
            dimension_semantics=("arbitrary", "arbitrary"),
            vmem_limit_bytes=_vmem_limit(48 * 1024 * 1024)),
        name="sconv_prompt",
    )(x, *[_arg_of(a) for a in params])


def _sconv_sample(x, row0, prev, params):
    m = prev.shape[0]
    tm = min(ROW_TILE, m)
    row = pl.BlockSpec((tm, D_MODEL), lambda i: (i, 0))
    row_in = pl.BlockSpec((tm, D_MODEL), lambda i: (i + row0 // tm, 0))
    return pl.pallas_call(
        _sconv_sample_kernel,
        grid=(m // tm,),
        in_specs=[row_in, row] + [_spec_of(a) for a in params],
        out_specs=[row, row],
        out_shape=[jax.ShapeDtypeStruct((m, D_MODEL), F32)] * 2,
        compiler_params=pltpu.CompilerParams(
            dimension_semantics=("arbitrary",),
            vmem_limit_bytes=_vmem_limit(48 * 1024 * 1024)),
        name="sconv_sample",
    )(x, prev, *[_arg_of(a) for a in params])


def _rope_tables(pos):
    half = HEAD_DIM // 2
    inv = ROPE_THETA ** (-jnp.arange(half, dtype=F32) / half)
    ang = pos.astype(F32)[:, None] * inv[None, :]
    cos = jnp.cos(ang)
    sin = jnp.sin(ang)
    reps = V7X_LANES // HEAD_DIM
    cos_t = jnp.tile(jnp.concatenate([cos, cos], axis=-1), (1, reps))
    sin_t = jnp.tile(jnp.concatenate([-sin, sin], axis=-1), (1, reps))
    return cos_t, sin_t


def _row(v):
    return v.reshape(1, -1)


def _pad_state_rows(buf):
    b, k, c = buf.shape
    padded = jnp.concatenate([jnp.zeros((b, V7X_SUBLANES - k, c), buf.dtype), buf], axis=1)
    return padded.reshape(b * V7X_SUBLANES, c)


def kernel(x_prompt, x_sample, cache_k, cache_v, state_rglru_h, state_rglru_conv, state_shortconv,
           norm_mixer, norm_ffn, norm_final,
           attn_w_qkv, attn_b_qkv, attn_w_o, attn_b_o, attn_sinks,
           rglru_w_gate, rglru_w_in, rglru_conv_w, rglru_conv_b, rglru_wa, rglru_ba,
           rglru_wx, rglru_bx, rglru_lambda, rglru_w_out,
           sconv_w_in, sconv_conv_w, sconv_w_out,
           ffn_w_gate, ffn_w_up, ffn_w_down):
    bp, seq, _ = x_prompt.shape
    bs, t_new, _ = x_sample.shape
    depth = norm_mixer.shape[0]
    assert t_new == V7X_SUBLANES

    xp = x_prompt.reshape(bp * seq, D_MODEL)
    xs = x_sample.reshape(bs * t_new, D_MODEL)
    s_row0 = 0

    cos_p, sin_p = _rope_tables(jnp.arange(seq, dtype=jnp.int32))
    cos_s, sin_s = _rope_tables(PAST_LEN + jnp.arange(t_new, dtype=jnp.int32))
    cos_s = jnp.tile(cos_s, (SAMPLE_ATTN_BATCHES, 1))
    sin_s = jnp.tile(sin_s, (SAMPLE_ATTN_BATCHES, 1))

    to_lane_major = (0, 1, 3, 4, 2)
    from_lane_major = (0, 1, 4, 2, 3)
    ck_t = jnp.transpose(cache_k, to_lane_major)
    cv_t = jnp.transpose(cache_v, to_lane_major)
    nk_t = jnp.zeros(ck_t.shape, F32)
    nv_t = jnp.zeros(cv_t.shape, F32)

    kp_l, vp_l = [], []
    hp_l, hs_l, rcp_l, rcs_l = [], [], [], []
    scp_l, scs_l = [], []

    for i in range(depth):
        kind = i % 3
        j = i // 3
        g_mix = _row(norm_mixer[i])
        if kind == 0:
            wqkv = _layer(attn_w_qkv, j)
            wo = _layer(attn_w_o, j)
            bqkv = _row(attn_b_qkv[j])
            bo = _row(attn_b_o[j])
            sinks = attn_sinks[j]
            xp, kp, vp = _attn_prompt(xp, g_mix, wqkv, bqkv, wo, bo, cos_p, sin_p, sinks, bp, seq)
            xs, nk_t, nv_t = _attn_sample(xs, s_row0, g_mix, wqkv, bqkv, wo, bo, cos_s, sin_s,
                                          sinks, ck_t, cv_t, nk_t, nv_t, j, t_new)
            kp_l.append(kp.reshape(bp, WINDOW, N_KV_HEADS, HEAD_DIM))
            vp_l.append(vp.reshape(bp, WINDOW, N_KV_HEADS, HEAD_DIM))
        elif kind == 1:
            params = (g_mix, rglru_w_gate[j].astype(BF16), rglru_w_in[j].astype(BF16),
                      rglru_conv_w[j], _row(rglru_conv_b[j]), rglru_wa[j].astype(BF16),
                      _row(rglru_ba[j]), rglru_wx[j].astype(BF16), _row(rglru_bx[j]),
                      _row(rglru_lambda[j]), rglru_w_out[j].astype(BF16))
            half = [(rglru_wa[j] * 0.5).astype(BF16), (rglru_wx[j] * 0.5).astype(BF16)]
            prompt_params = params[:5] + (half[0], params[6], half[1]) + params[8:]
            xp, hlast, ulast = _rglru_prompt(xp, prompt_params, bp, seq)
            hp_l.append(hlast[:, V7X_SUBLANES - 1])
            rcp_l.append(ulast[:, V7X_SUBLANES - (RG_CONV_W - 1):])
            prev = _pad_state_rows(state_rglru_conv[j])
            hinit = jnp.repeat(state_rglru_h[j], t_new, axis=0)
            xs, hs_all, u_all = _rglru_sample(xs, s_row0, prev, hinit, params, t_new)
            hs_l.append(hs_all.reshape(bs, t_new, D_MODEL)[:, t_new - 1])
            rcs_l.append(u_all.reshape(bs, t_new, D_MODEL)[:, t_new - (RG_CONV_W - 1):])
        else:
            params = (g_mix, _layer(sconv_w_in, j), sconv_conv_w[j], _layer(sconv_w_out, j))
            xp, vlast = _sconv_prompt(xp, params, bp, seq)
            scp_l.append(vlast[:, V7X_SUBLANES - (SCONV_W - 1):])
            prev = _pad_state_rows(state_shortconv[j])
            xs, v_all = _sconv_sample(xs, s_row0, prev, params)
            scs_l.append(v_all.reshape(bs, t_new, D_MODEL)[:, t_new - (SCONV_W - 1):])

        wg = _layer(ffn_w_gate, i)
        wu = _layer(ffn_w_up, i)
        wd = _layer(ffn_w_down, i)
        d_ff = ffn_w_down.shape[1]
        g_ffn = _row(norm_ffn[i])
        g_fin = _row(norm_final)
        last = i == depth - 1
        if last:
            xp = _ffn(xp, bp * seq, None, g_ffn, wg, wu, wd, d_ff, g_fin, True)
            xs = _ffn(xs, bs * t_new, None, g_ffn, wg, wu, wd, d_ff, g_fin, True)
        else:
            xp = xs = _ffn(xp, bp * seq, xs, g_ffn, wg, wu, wd, d_ff, g_fin, False)
            s_row0 = bp * seq

    return (xp.reshape(bp, seq, D_MODEL), xs.reshape(bs, t_new, D_MODEL),
            jnp.stack(kp_l), jnp.stack(vp_l),
            jnp.transpose(nk_t, from_lane_major), jnp.transpose(nv_t, from_lane_major),
            jnp.stack(hp_l), jnp.stack(hs_l), jnp.stack(rcp_l), jnp.stack(rcs_l),
            jnp.stack(scp_l), jnp.stack(scs_l))
```

```python
import functools

import jax
import jax.numpy as jnp
from jax import lax
from jax.experimental import pallas as pl
from jax.experimental.pallas import tpu as pltpu

D_MODEL = 1024
HEAD_DIM = 64
N_HEADS = 16
N_KV_HEADS = 2
GQA_GROUP = 8
Q_DIM = N_HEADS * HEAD_DIM
KV_DIM = N_KV_HEADS * HEAD_DIM
WINDOW = 128
ROPE_THETA = 10000.0
NEG_INF = -1e30
RG_BLOCKS = 4
RG_BLOCK_W = 256
RG_CONV_W = 4
RG_C = 8.0
SCONV_W = 3
LOG2_E = 1.4426950408889634
EPS = 1e-6
PAST_LEN = 8192

V7X_LANES = 128
V7X_SUBLANES = 8
V7X_MXU_DIM = 256
V7X_VMEM_BYTES = 64 * 1024 * 1024

BF16 = jnp.bfloat16
F32 = jnp.float32

FFN_CHUNK = V7X_MXU_DIM
FFN_ROW_TILE = 512
FFN_STAGING_SLOTS = 2
ROW_TILE = 512
SCONV_ROW_TILE = 1024
ATTN_ROW_TILE = 1024
RG_ROW_TILE = 256
RG_PROMPT_TILE = 1024
RG_PROMPT_SUB = 1024
RG_SEG_ARRAYS = 5
SAMPLE_ATTN_BATCHES = 32
SAMPLE_ATTN_UNROLL = 8
ATTN_LOOKAHEAD = 2
PROMPT_ATTN_COLS = 2
PROMPT_ATTN_BLOCKS = 1
COLS_PER_KV = (N_HEADS // N_KV_HEADS) * HEAD_DIM // V7X_LANES


def _vmem_limit(nbytes):
    return int(min(nbytes, V7X_VMEM_BYTES - 8 * 1024 * 1024))


def _const_spec(shape):
    nd = len(shape)
    return pl.BlockSpec(shape, lambda *_: (0,) * nd, pipeline_mode=pl.Buffered(1))


def _layer(stacked, layer):
    return (stacked, layer)


def _spec_of(w):
    if isinstance(w, tuple):
        stacked, layer = w
        nd = stacked.ndim - 1
        return pl.BlockSpec((None,) + stacked.shape[1:], lambda *_: (layer,) + (0,) * nd,
                            pipeline_mode=pl.Buffered(1))
    return _const_spec(w.shape)


def _arg_of(w):
    return w[0] if isinstance(w, tuple) else w


def _rms(x, g):
    ms = jnp.mean(x * x, axis=-1, keepdims=True)
    return x * lax.rsqrt(ms + EPS) * g


def _mm(a, w):
    return jnp.dot(a.astype(BF16), w.astype(BF16), preferred_element_type=F32)


def _ffn_kernel(*refs, n_chunks, final_norm, tiles_a, stacked, layer):
    if stacked:
        xa_ref, xb_ref = refs[:2]
        refs = refs[2:]
    else:
        xa_ref = refs[0]
        refs = refs[1:]
    (g_ref, wg_hbm, wu_hbm, wd_hbm, gf_ref, o_ref,
     wg_ref, wu_ref, wd_ref, stage_g, stage_u, stage_d, act_ref, sem) = refs
    step = pl.program_id(0)

    def chunk_copies(c):
        cs = slice(c * FFN_CHUNK, (c + 1) * FFN_CHUNK)
        slot = c % FFN_STAGING_SLOTS
        return (pltpu.make_async_copy(wg_hbm.at[layer, :, cs], stage_g.at[slot], sem.at[0, slot]),
                pltpu.make_async_copy(wu_hbm.at[layer, :, cs], stage_u.at[slot], sem.at[1, slot]),
                pltpu.make_async_copy(wd_hbm.at[layer, cs, :], stage_d.at[slot], sem.at[2, slot]))

    def run(fetch_weights):
        if fetch_weights:
            for c in range(min(FFN_STAGING_SLOTS, n_chunks)):
                for copy in chunk_copies(c):
                    copy.start()
        if stacked:
            x = jnp.where(step < tiles_a, xa_ref[...], xb_ref[...])
        else:
            x = xa_ref[...]
        h = _rms(x, g_ref[...]).astype(BF16)
        for c in range(n_chunks):
            cs = slice(c * FFN_CHUNK, (c + 1) * FFN_CHUNK)
            if fetch_weights:
                slot = c % FFN_STAGING_SLOTS
                for copy in chunk_copies(c):
                    copy.wait()
                wg_ref[:, cs] = stage_g[slot].astype(BF16)
                wu_ref[:, cs] = stage_u[slot].astype(BF16)
                wd_ref[cs, :] = stage_d[slot].astype(BF16)
                if c + FFN_STAGING_SLOTS < n_chunks:
                    for copy in chunk_copies(c + FFN_STAGING_SLOTS):
                        copy.start()
            gate = _mm(h, wg_ref[:, cs])
            up = _mm(h, wu_ref[:, cs])
            act_ref[:, cs] = ((gate * jax.nn.sigmoid(gate)) * up).astype(BF16)
        acc = x + jnp.dot(act_ref[...], wd_ref[...], preferred_element_type=F32)
        if final_norm:
            acc = _rms(acc, gf_ref[...])
        o_ref[...] = acc

    @pl.when(step == 0)
    def _():
        run(True)

    @pl.when(step != 0)
    def _():
        run(False)


def _ffn(xa, rows_a, xb, g, wg, wu, wd, d_ff, gf, final_norm):
    tm = FFN_ROW_TILE
    tiles_a = rows_a // tm
    stacked = xb is not None
    tiles_b = xb.shape[0] // tm if stacked else 0
    n_chunks = d_ff // FFN_CHUNK
    out_row = pl.BlockSpec((tm, D_MODEL), lambda i: (i, 0))
    if stacked:
        x_specs = [pl.BlockSpec((tm, D_MODEL), lambda i: (jnp.minimum(i, tiles_a - 1), 0)),
                   pl.BlockSpec((tm, D_MODEL), lambda i: (jnp.maximum(i - tiles_a, 0), 0),
                                pipeline_mode=pl.Buffered(1))]
        x_args = [xa, xb]
    else:
        x_specs = [out_row]
        x_args = [xa]
    (wg_all, layer), (wu_all, _), (wd_all, _) = wg, wu, wd
    bf16_weight_bytes = 2 * 3 * D_MODEL * d_ff
    staging_bytes = 4 * 3 * FFN_STAGING_SLOTS * D_MODEL * FFN_CHUNK
    in_hbm = pl.BlockSpec(memory_space=pl.ANY)
    return pl.pallas_call(
        functools.partial(_ffn_kernel, n_chunks=n_chunks, final_norm=final_norm,
                          tiles_a=tiles_a, stacked=stacked, layer=layer),
        grid=(tiles_a + tiles_b,),
        in_specs=x_specs + [_const_spec((1, D_MODEL)), in_hbm, in_hbm, in_hbm,
                            _const_spec((1, D_MODEL))],
        out_specs=out_row,
        out_shape=jax.ShapeDtypeStruct(((tiles_a + tiles_b) * tm, D_MODEL), F32),
        scratch_shapes=[pltpu.VMEM(wg_all.shape[1:], BF16),
                        pltpu.VMEM(wu_all.shape[1:], BF16),
                        pltpu.VMEM(wd_all.shape[1:], BF16),
                        pltpu.VMEM((FFN_STAGING_SLOTS, D_MODEL, FFN_CHUNK), F32),
                        pltpu.VMEM((FFN_STAGING_SLOTS, D_MODEL, FFN_CHUNK), F32),
                        pltpu.VMEM((FFN_STAGING_SLOTS, FFN_CHUNK, D_MODEL), F32),
                        pltpu.VMEM((tm, d_ff), BF16),
                        pltpu.SemaphoreType.DMA((3, FFN_STAGING_SLOTS))],
        compiler_params=pltpu.CompilerParams(
            dimension_semantics=("arbitrary",),
            vmem_limit_bytes=_vmem_limit(bf16_weight_bytes + staging_bytes
                                         + 16 * tm * D_MODEL * 4)),
        name="ffn",
    )(*x_args, g, wg_all, wu_all, wd_all, gf)


def _rope_cols(cols, cos, sin_signed, first_half):
    swapped = jnp.where(first_half,
                        pltpu.roll(cols, V7X_LANES - HEAD_DIM // 2, 1),
                        pltpu.roll(cols, HEAD_DIM // 2, 1))
    return cols * cos + swapped * sin_signed


def _pair_blockdiag(mat, mat_rolled, kv, lane_lt64):
    zero = jnp.zeros_like(mat)
    if kv == 0:
        top = jnp.where(lane_lt64, mat, zero)
        bottom = jnp.where(lane_lt64, zero, mat_rolled)
    else:
        top = jnp.where(lane_lt64, mat_rolled, zero)
        bottom = jnp.where(lane_lt64, zero, mat)
    return jnp.concatenate([top, bottom], axis=0).astype(BF16)


def _key_blockdiag(k2, kv):
    lane_lt64 = lax.broadcasted_iota(jnp.int32, k2.shape, 1) < HEAD_DIM
    return _pair_blockdiag(k2, pltpu.roll(k2, HEAD_DIM, 1), kv, lane_lt64)


def _value_blockdiag(v2, kv):
    v2 = jnp.where(lax.broadcasted_iota(jnp.int32, v2.shape, 0) == 0, 0.0, v2)
    vbd = _key_blockdiag(v2, kv)
    row = lax.broadcasted_iota(jnp.int32, vbd.shape, 0)
    lane = lax.broadcasted_iota(jnp.int32, vbd.shape, 1)
    ones_bd = jnp.where((lane < HEAD_DIM) == (row < 2 * WINDOW), 1.0, 0.0).astype(BF16)
    return jnp.concatenate([vbd, ones_bd], axis=1)


def _attn_weights(s_all, mask, sink_ref, kv, cols, tq):
    nk = 2 * WINDOW
    key = lax.broadcasted_iota(jnp.int32, (1, nk), 1)
    e_rows = []
    for i, c in enumerate(cols):
        e_halves = []
        for parity in range(2):
            sink = sink_ref[kv * GQA_GROUP + 2 * c + parity]
            fill = jnp.where(key == 0, sink, NEG_INF)
            s = s_all[i * tq:(i + 1) * tq, parity * nk:(parity + 1) * nk]
            s = jnp.where(mask, s, fill)
            m = jnp.max(s, axis=-1, keepdims=True)
            e_halves.append(jnp.exp(s - m))
        e_rows.append(jnp.concatenate(e_halves, axis=1))
    return jnp.concatenate(e_rows, axis=0).astype(BF16)


def _key_blockdiag_t(kt, kv):
    del kv
    zero = jnp.zeros_like(kt)
    return jnp.concatenate([jnp.concatenate([kt, zero], axis=1),
                            jnp.concatenate([zero, kt], axis=1)], axis=0).astype(BF16)


def _value_blockdiag_t(vt, kv):
    del kv
    vt = jnp.where(lax.broadcasted_iota(jnp.int32, vt.shape, 1) == 0, 0.0, vt)
    zero = jnp.zeros_like(vt)
    one = jnp.ones_like(vt)
    return jnp.concatenate([jnp.concatenate([vt, zero], axis=1),
                            jnp.concatenate([zero, vt], axis=1),
                            jnp.concatenate([one, zero], axis=1),
                            jnp.concatenate([zero, one], axis=1)], axis=0).astype(BF16)


def _dot_nt(a, b):
    return lax.dot_general(a, b, (((1,), (1,)), ((), ())), preferred_element_type=F32)


def _dot_nn(a, b):
    return jnp.dot(a, b, preferred_element_type=F32)


ROW_MAJOR_KV = (_key_blockdiag, _dot_nt, _value_blockdiag, _dot_nn)
LANE_MAJOR_KV = (_key_blockdiag_t, _dot_nn, _value_blockdiag_t, _dot_nt)


def _attend_units(units, sink_ref, tq, kv_ops):
    make_kbd, score_dot, make_vbd, value_dot = kv_ops
    n = len(units)
    scores, kbd, vbd = {}, {}, {}

    def issue_scores(i):
        load_q, load_k, _, _, kv, _, _, kv_id = units[i]
        if kv_id not in kbd:
            kbd[kv_id] = make_kbd(load_k(), kv)
        scores[i] = score_dot(load_q(), kbd[kv_id])

    for i in range(min(ATTN_LOOKAHEAD, n)):
        issue_scores(i)
    for i in range(n):
        if i + ATTN_LOOKAHEAD < n:
            issue_scores(i + ATTN_LOOKAHEAD)
        _, _, load_v, mask, kv, cols, store_o, kv_id = units[i]
        if kv_id not in vbd:
            vbd[kv_id] = make_vbd(load_v(), kv)
        e_all = _attn_weights(scores.pop(i), mask, sink_ref, kv, cols, tq)
        o_den = value_dot(e_all, vbd[kv_id])
        store_o(o_den[:, :V7X_LANES] * (1.0 / o_den[:, V7X_LANES:]))


def _band_mask(tq, col_min):
    row = lax.broadcasted_iota(jnp.int32, (tq, 2 * WINDOW), 0)
    col = lax.broadcasted_iota(jnp.int32, (tq, 2 * WINDOW), 1)
    prev_ok = (col < WINDOW) & (col > row)
    own_ok = (col >= WINDOW) & (col - WINDOW <= row)
    return (prev_ok | own_ok) & (col >= col_min)


def _attn_prompt_kernel(x_ref, g_ref, wqkv_ref, bqkv_ref, wo_ref, bo_ref, cos_ref, sin_ref,
                        sink_ref, y_ref, knew_ref, vnew_ref,
                        q_scr, k_scr, v_scr, a_scr, *, tq_tile, n_tiles):
    j = pl.program_id(1)
    n_blk = tq_tile // WINDOW

    @pl.when(j == 0)
    def _():
        k_scr[0:WINDOW, :] = jnp.zeros((WINDOW, V7X_LANES), F32)
        v_scr[0:WINDOW, :] = jnp.zeros((WINDOW, V7X_LANES), F32)

    x = x_ref[...]
    h = _rms(x, g_ref[...])
    qkv = _mm(h, wqkv_ref[...]) + bqkv_ref[...]
    cos = cos_ref[...]
    sin = sin_ref[...]
    lane = lax.broadcasted_iota(jnp.int32, (tq_tile, V7X_LANES), 1)
    first_half = (lane % HEAD_DIM) < (HEAD_DIM // 2)
    scale = HEAD_DIM ** -0.5
    for c in range(Q_DIM // V7X_LANES):
        qc = _rope_cols(qkv[:, c * V7X_LANES:(c + 1) * V7X_LANES], cos, sin, first_half)
        q_scr[:, c * V7X_LANES:(c + 1) * V7X_LANES] = (qc * scale).astype(BF16)
    k_new = _rope_cols(qkv[:, Q_DIM:Q_DIM + KV_DIM], cos, sin, first_half)
    v_new = qkv[:, Q_DIM + KV_DIM:]
    k_scr[WINDOW:, :] = k_new
    v_scr[WINDOW:, :] = v_new

    def unit(r0, mask, kv, cols, kv_id):
        def load_q():
            return jnp.concatenate(
                [q_scr[pl.ds(r0, WINDOW),
                       (kv * COLS_PER_KV + c) * V7X_LANES:(kv * COLS_PER_KV + c + 1) * V7X_LANES]
                 for c in cols], axis=0)

        def store_o(o):
            for i, c in enumerate(cols):
                col = kv * COLS_PER_KV + c
                a_scr[pl.ds(r0, WINDOW), col * V7X_LANES:(col + 1) * V7X_LANES] = (
                    o[i * WINDOW:(i + 1) * WINDOW].astype(BF16))

        return (load_q, lambda: k_scr[pl.ds(r0, 2 * WINDOW), :],
                lambda: v_scr[pl.ds(r0, 2 * WINDOW), :], mask, kv, cols, store_o, kv_id)

    col_groups = [tuple(range(c, c + PROMPT_ATTN_COLS))
                  for c in range(0, COLS_PER_KV, PROMPT_ATTN_COLS)]

    def blocks(i, carry):
        units = []
        for local in range(PROMPT_ATTN_BLOCKS):
            blk = i * PROMPT_ATTN_BLOCKS + local
            r0 = pl.multiple_of(blk * WINDOW, WINDOW)
            first = jnp.logical_and(j == 0, blk == 0)
            mask = _band_mask(WINDOW, jnp.where(first, WINDOW, 0))
            units += [unit(r0, mask, kv, cols, (local, kv)) for kv in range(N_KV_HEADS)
                      for cols in col_groups]
        _attend_units(units, sink_ref, WINDOW, ROW_MAJOR_KV)
        return carry

    lax.fori_loop(0, n_blk // PROMPT_ATTN_BLOCKS, blocks, 0)

    y_ref[...] = _mm(a_scr[...], wo_ref[...]) + bo_ref[...] + x

    k_scr[0:WINDOW, :] = k_new[tq_tile - WINDOW:, :]
    v_scr[0:WINDOW, :] = v_new[tq_tile - WINDOW:, :]

    @pl.when(j == n_tiles - 1)
    def _():
        knew_ref[0] = k_new[tq_tile - WINDOW:, :]
        vnew_ref[0] = v_new[tq_tile - WINDOW:, :]


def _attn_prompt(x, g, wqkv, bqkv, wo, bo, cos_t, sin_t, sinks, batch, seq):
    tq = ATTN_ROW_TILE
    nt = seq // tq
    row = pl.BlockSpec((tq, D_MODEL), lambda b, j: (b * nt + j, 0))
    tab = pl.BlockSpec((tq, V7X_LANES), lambda b, j: (j, 0))
    cache = pl.BlockSpec((1, WINDOW, V7X_LANES), lambda b, j: (b, 0, 0))
    return pl.pallas_call(
        functools.partial(_attn_prompt_kernel, tq_tile=tq, n_tiles=nt),
        grid=(batch, nt),
        in_specs=[row, _const_spec((1, D_MODEL)), _spec_of(wqkv),
                  _const_spec((1, Q_DIM + 2 * KV_DIM)), _spec_of(wo), _const_spec((1, D_MODEL)),
                  tab, tab, pl.BlockSpec(memory_space=pltpu.SMEM)],
        out_specs=[row, cache, cache],
        out_shape=[jax.ShapeDtypeStruct((batch * seq, D_MODEL), F32),
                   jax.ShapeDtypeStruct((batch, WINDOW, V7X_LANES), F32),
                   jax.ShapeDtypeStruct((batch, WINDOW, V7X_LANES), F32)],
        scratch_shapes=[pltpu.VMEM((tq, Q_DIM), BF16),
                        pltpu.VMEM((WINDOW + tq, V7X_LANES), F32),
                        pltpu.VMEM((WINDOW + tq, V7X_LANES), F32),
                        pltpu.VMEM((tq, Q_DIM), BF16)],
        compiler_params=pltpu.CompilerParams(
            dimension_semantics=("arbitrary", "arbitrary"),
            vmem_limit_bytes=_vmem_limit(48 * 1024 * 1024)),
        name="attn_prompt",
    )(x, g, _arg_of(wqkv), bqkv, _arg_of(wo), bo, cos_t, sin_t, sinks)


def _attn_sample_kernel(x_ref, g_ref, wqkv_ref, bqkv_ref, wo_ref, bo_ref, cos_ref, sin_ref,
                        sink_ref, ck_ref, cv_ref, nk_in, nv_in, y_ref, nk_ref, nv_ref,
                        q_scr, k_scr, v_scr, a_scr, *, n_seq, t_new):
    del nk_in, nv_in
    tm = n_seq * t_new
    x = x_ref[...]
    h = _rms(x, g_ref[...])
    qkv = _mm(h, wqkv_ref[...]) + bqkv_ref[...]
    cos = cos_ref[...]
    sin = sin_ref[...]
    lane = lax.broadcasted_iota(jnp.int32, (tm, V7X_LANES), 1)
    first_half = (lane % HEAD_DIM) < (HEAD_DIM // 2)
    scale = HEAD_DIM ** -0.5
    for c in range(Q_DIM // V7X_LANES):
        qc = _rope_cols(qkv[:, c * V7X_LANES:(c + 1) * V7X_LANES], cos, sin, first_half)
        q_scr[:, c * V7X_LANES:(c + 1) * V7X_LANES] = qc * scale
    k_scr[...] = _rope_cols(qkv[:, Q_DIM:Q_DIM + KV_DIM], cos, sin, first_half)
    v_scr[...] = qkv[:, Q_DIM + KV_DIM:]
    mask = _band_mask(t_new, 0)
    pad = jnp.zeros((WINDOW - t_new, V7X_LANES), F32)
    lane = lax.broadcasted_iota(jnp.int32, (HEAD_DIM, WINDOW), 1)

    def new_rows_t(scr, b):
        r0 = pl.multiple_of(b * t_new, t_new)
        return jnp.concatenate([scr[pl.ds(r0, t_new), :], pad], axis=0).T

    def seq_group(i, carry):
        seqs = [i * SAMPLE_ATTN_UNROLL + u for u in range(SAMPLE_ATTN_UNROLL)]
        k_new_t = [new_rows_t(k_scr, b) for b in seqs]
        v_new_t = [new_rows_t(v_scr, b) for b in seqs]

        def unit(u, b, kv):
            r0 = pl.multiple_of(b * t_new, t_new)
            head_rows = slice(kv * HEAD_DIM, (kv + 1) * HEAD_DIM)

            def load_q():
                return jnp.concatenate(
                    [q_scr[pl.ds(r0, t_new), (kv * COLS_PER_KV + c) * V7X_LANES:
                           (kv * COLS_PER_KV + c + 1) * V7X_LANES]
                     for c in range(COLS_PER_KV)], axis=0).astype(BF16)

            def load_k():
                return jnp.concatenate([ck_ref[b, kv], k_new_t[u][head_rows, :]], axis=1)

            def load_v():
                return jnp.concatenate([cv_ref[b, kv], v_new_t[u][head_rows, :]], axis=1)

            def store_o(o):
                for c in range(COLS_PER_KV):
                    col = kv * COLS_PER_KV + c
                    a_scr[pl.ds(r0, t_new), col * V7X_LANES:(col + 1) * V7X_LANES] = (
                        o[c * t_new:(c + 1) * t_new])

            return (load_q, load_k, load_v, mask, kv, tuple(range(COLS_PER_KV)), store_o,
                    (u, kv))

        _attend_units([unit(u, b, kv) for u, b in enumerate(seqs) for kv in range(N_KV_HEADS)],
                      sink_ref, t_new, LANE_MAJOR_KV)
        keep = lane < WINDOW - t_new
        for u, b in enumerate(seqs):
            for kv in range(N_KV_HEADS):
                head_rows = slice(kv * HEAD_DIM, (kv + 1) * HEAD_DIM)
                nk_ref[b, kv] = jnp.where(
                    keep, pltpu.roll(ck_ref[b, kv], WINDOW - t_new, 1),
                    pltpu.roll(k_new_t[u][head_rows, :], WINDOW - t_new, 1))
                nv_ref[b, kv] = jnp.where(
                    keep, pltpu.roll(cv_ref[b, kv], WINDOW - t_new, 1),
                    pltpu.roll(v_new_t[u][head_rows, :], WINDOW - t_new, 1))
        return carry

    lax.fori_loop(0, n_seq // SAMPLE_ATTN_UNROLL, seq_group, 0)
    y_ref[...] = _mm(a_scr[...], wo_ref[...]) + bo_ref[...] + x


def _attn_sample(x, row0, g, wqkv, bqkv, wo, bo, cos_t, sin_t, sinks, ck_t, cv_t, nk_t, nv_t,
                 layer, t_new):
    bg = SAMPLE_ATTN_BATCHES
    tm = bg * t_new
    n_batch = ck_t.shape[1]
    row = pl.BlockSpec((tm, D_MODEL), lambda i: (i, 0))
    row_in = pl.BlockSpec((tm, D_MODEL), lambda i: (i + row0 // tm, 0))
    cache = pl.BlockSpec((None, bg, N_KV_HEADS, HEAD_DIM, WINDOW), lambda i: (layer, i, 0, 0, 0))
    whole = pl.BlockSpec(memory_space=pl.ANY)
    n_in = 13
    return pl.pallas_call(
        functools.partial(_attn_sample_kernel, n_seq=bg, t_new=t_new),
        grid=(n_batch // bg,),
        in_specs=[row_in, _const_spec((1, D_MODEL)), _spec_of(wqkv),
                  _const_spec((1, Q_DIM + 2 * KV_DIM)), _spec_of(wo), _const_spec((1, D_MODEL)),
                  _const_spec((tm, V7X_LANES)), _const_spec((tm, V7X_LANES)),
                  pl.BlockSpec(memory_space=pltpu.SMEM), cache, cache, whole, whole],
        out_specs=[row, cache, cache],
        out_shape=[jax.ShapeDtypeStruct((n_batch * t_new, D_MODEL), F32),
                   jax.ShapeDtypeStruct(nk_t.shape, F32),
                   jax.ShapeDtypeStruct(nv_t.shape, F32)],
        input_output_aliases={n_in - 2: 1, n_in - 1: 2},
        scratch_shapes=[pltpu.VMEM((tm, Q_DIM), F32),
                        pltpu.VMEM((tm, V7X_LANES), F32),
                        pltpu.VMEM((tm, V7X_LANES), F32),
                        pltpu.VMEM((tm, Q_DIM), F32)],
        compiler_params=pltpu.CompilerParams(
            dimension_semantics=("arbitrary",),
            vmem_limit_bytes=_vmem_limit(48 * 1024 * 1024)),
        name="attn_sample",
    )(x, g, _arg_of(wqkv), bqkv, _arg_of(wo), bo, cos_t, sin_t, sinks, ck_t, cv_t, nk_t, nv_t)


def _dwconv_groups(u, prev, w_ref, col0, ncol):
    rows = u.shape[0]
    kw = w_ref.shape[0]
    sub = lax.broadcasted_iota(jnp.int32, u.shape, 0) % V7X_SUBLANES
    y = u * w_ref[kw - 1:kw, col0:col0 + ncol]
    for s in range(1, kw):
        from_prev = pltpu.roll(prev, (rows - V7X_SUBLANES + s) % rows, 0)
        from_self = pltpu.roll(u, s, 0)
        shifted = jnp.where(sub < s, from_prev, from_self)
        y = y + shifted * w_ref[kw - 1 - s:kw - s, col0:col0 + ncol]
    return y


def _dwconv_rows(u, carry8, w_ref, col0, ncol):
    kw = w_ref.shape[0]
    sub8 = lax.broadcasted_iota(jnp.int32, carry8.shape, 0)
    y = u * w_ref[kw - 1:kw, col0:col0 + ncol]
    for s in range(1, kw):
        rolled = pltpu.roll(u, s, 0)
        head = jnp.where(sub8 < s, pltpu.roll(carry8, s, 0), rolled[:V7X_SUBLANES])
        shifted = jnp.concatenate([head, rolled[V7X_SUBLANES:]], axis=0)
        y = y + shifted * w_ref[kw - 1 - s:kw - s, col0:col0 + ncol]
    return y


def _scan_groups(a, b):
    pos = lax.broadcasted_iota(jnp.int32, a.shape, 0) % V7X_SUBLANES
    shift = 1
    while shift < V7X_SUBLANES:
        ok = pos >= shift
        a_sh = jnp.where(ok, pltpu.roll(a, shift, 0), 1.0)
        b_sh = jnp.where(ok, pltpu.roll(b, shift, 0), 0.0)
        b = a * b_sh + b
        a = a * a_sh
        shift *= 2
    return a, b


def _log_sigmoid(x):
    return jnp.minimum(x, 0.0) - jnp.log1p(jnp.exp(-jnp.abs(x)))


def _rglru_body(x, g_ref, wgate_ref, win_ref, cw_ref, cb_ref, wa_ref, ba_ref, wx_ref, bx_ref,
                lam_ref, wout_ref, conv_of, scan_of, u_sink, h_sink):
    h = _rms(x, g_ref[...]).astype(BF16)
    acc = x
    for n in range(RG_BLOCKS):
        c0 = n * RG_BLOCK_W
        cs = slice(c0, c0 + RG_BLOCK_W)
        gate = jax.nn.gelu(jnp.dot(h, wgate_ref[:, cs], preferred_element_type=F32))
        u0 = jnp.dot(h, win_ref[:, cs], preferred_element_type=F32)
        u_sink(n, u0)
        u = conv_of(n, u0) + cb_ref[:, cs]
        ub = u.astype(BF16)
        r = jax.nn.sigmoid(jnp.dot(ub, wa_ref[n], preferred_element_type=F32) + ba_ref[:, cs])
        ig = jax.nn.sigmoid(jnp.dot(ub, wx_ref[n], preferred_element_type=F32) + bx_ref[:, cs])
        log_a = RG_C * r * _log_sigmoid(lam_ref[:, cs])
        a = jnp.exp(log_a)
        mult = jnp.sqrt(-jnp.tanh(log_a) * (1.0 + a * a))
        hs = scan_of(n, a, mult * (ig * u))
        h_sink(n, hs)
        acc = acc + jnp.dot((hs * gate).astype(BF16), wout_ref[cs, :], preferred_element_type=F32)
    return acc


def _seg_pitch(seg_len):
    assert seg_len % V7X_SUBLANES == 0
    return seg_len + V7X_SUBLANES // 2


def _rows_to_segments(scr, slab0, x, seg_len):
    pitch = _seg_pitch(seg_len)
    for s in range(x.shape[1] // V7X_LANES):
        for i in range(V7X_SUBLANES):
            scr[slab0 + s, i * pitch:i * pitch + seg_len, :] = (
                x[i * seg_len:(i + 1) * seg_len, s * V7X_LANES:(s + 1) * V7X_LANES])


def _segments_to_rows(scr, slab0, n_slabs, seg_len):
    pitch = _seg_pitch(seg_len)
    return jnp.concatenate(
        [jnp.concatenate([scr[slab0 + s, i * pitch:i * pitch + seg_len, :]
                          for i in range(V7X_SUBLANES)], axis=0)
         for s in range(n_slabs)], axis=1)


def _seg_step(scr, slab, k, seg_len):
    return scr[slab, pl.ds(k, V7X_SUBLANES, stride=_seg_pitch(seg_len)), :]


def _seg_step_store(scr, slab, k, seg_len, v):
    scr[slab, pl.ds(k, V7X_SUBLANES, stride=_seg_pitch(seg_len)), :] = v


def _from_prev_segment(v, first):
    sub = lax.broadcasted_iota(jnp.int32, v.shape, 0)
    return jnp.where(sub == 0, first, pltpu.roll(v, 1, 0))


def _rglru_prompt_kernel(x_ref, g_ref, wgate_ref, win_ref, cw_ref, cb_ref, wa_ref, ba_ref,
                         wx_ref, bx_ref, lam_ref, wout_ref, y_ref, hlast_ref, ulast_ref,
                         ucarry, hcarry, *seg_scr, tm, sub_rows):
    j = pl.program_id(1)
    slabs_per_chunk = RG_BLOCK_W // V7X_LANES
    kw = cw_ref.shape[0]
    n_sub = tm // sub_rows
    u0_scr, u_scr, r_scr, ig_scr, hs_scr = (seg_scr[i * n_sub:(i + 1) * n_sub]
                                            for i in range(RG_SEG_ARRAYS))

    @pl.when(j == 0)
    def _():
        ucarry[...] = jnp.zeros(ucarry.shape, F32)
        hcarry[...] = jnp.zeros(hcarry.shape, F32)

    n_slabs = D_MODEL // V7X_LANES
    seg_len = sub_rows // V7X_SUBLANES
    gates, conv_prev, h_prev = {}, {}, {}

    def project_in(sub):
        x = x_ref[sub * sub_rows:(sub + 1) * sub_rows, :]
        h = _rms(x, g_ref[...]).astype(BF16)
        gate_chunks, last_rows = [], []
        for n in range(RG_BLOCKS):
            cs = slice(n * RG_BLOCK_W, (n + 1) * RG_BLOCK_W)
            gate_chunks.append(
                jax.nn.gelu(jnp.dot(h, wgate_ref[:, cs], preferred_element_type=F32)))
            u0 = jnp.dot(h, win_ref[:, cs], preferred_element_type=F32)
            last_rows.append(u0[sub_rows - V7X_SUBLANES:, :])
            _rows_to_segments(u0_scr[sub], n * slabs_per_chunk, u0, seg_len)
            yield
        gates[sub] = gate_chunks
        conv_prev[sub + 1] = jnp.concatenate(last_rows, axis=1)

    def conv(sub):
        for slab in range(n_slabs):
            lanes = slice(slab * V7X_LANES, (slab + 1) * V7X_LANES)
            taps = [cw_ref[t:t + 1, lanes] for t in range(kw)]
            bias = cb_ref[:, lanes]
            steps = {k: _seg_step(u0_scr[sub], slab, k, seg_len) for k in range(seg_len)}
            for back in range(1, kw):
                steps[-back] = _from_prev_segment(
                    steps[seg_len - back],
                    conv_prev[sub][V7X_SUBLANES - back:V7X_SUBLANES - back + 1, lanes])
            for k in range(seg_len):
                u_k = steps[k] * taps[kw - 1] + bias
                for back in range(1, kw):
                    u_k = u_k + steps[k - back] * taps[kw - 1 - back]
                _seg_step_store(u_scr[sub], slab, k, seg_len, u_k)
            yield

    def project_gates(sub):
        for n in range(RG_BLOCKS):
            slab0 = n * slabs_per_chunk
            ub = _segments_to_rows(u_scr[sub], slab0, slabs_per_chunk, seg_len).astype(BF16)
            _rows_to_segments(r_scr[sub], slab0,
                              jnp.dot(ub, wa_ref[n], preferred_element_type=F32), seg_len)
            _rows_to_segments(ig_scr[sub], slab0,
                              jnp.dot(ub, wx_ref[n], preferred_element_type=F32), seg_len)
            yield

    def recur(sub):
        h_prev[sub + 1] = []
        for slab in range(n_slabs):
            lanes = slice(slab * V7X_LANES, (slab + 1) * V7X_LANES)
            half_scale = (-0.5 * RG_C) * _log_sigmoid(lam_ref[:, lanes])
            ba = 0.5 * ba_ref[:, lanes]
            bx = 0.5 * bx_ref[:, lanes]
            h_in = h_prev[sub][slab]
            a_cum, h_loc = [], []
            for k in range(seg_len):
                neg_log_a = half_scale * jnp.tanh(_seg_step(r_scr[sub], slab, k, seg_len) + ba) \
                    + half_scale
                ig = 0.5 * jnp.tanh(_seg_step(ig_scr[sub], slab, k, seg_len) + bx) + 0.5
                a = jnp.exp2(neg_log_a * (-LOG2_E))
                one_minus_a2 = jnp.tanh(neg_log_a) * (1.0 + a * a)
                mult = jnp.where(one_minus_a2 > 0.0, one_minus_a2 * lax.rsqrt(one_minus_a2), 0.0)
                b = mult * (ig * _seg_step(u_scr[sub], slab, k, seg_len))
                if k == 0:
                    a_cum.append(a)
                    h_loc.append(b)
                else:
                    a_cum.append(a * a_cum[-1])
                    h_loc.append(a * h_loc[-1] + b)
            a_seg, b_seg = _scan_groups(a_cum[-1], h_loc[-1])
            seg_end = a_seg * h_in + b_seg
            h_prev[sub + 1].append(seg_end[V7X_SUBLANES - 1:, :])
            h_start = _from_prev_segment(seg_end, h_in)
            for k in range(seg_len):
                _seg_step_store(hs_scr[sub], slab, k, seg_len, a_cum[k] * h_start + h_loc[k])
            yield

    def project_out(sub):
        rows = slice(sub * sub_rows, (sub + 1) * sub_rows)
        gate_chunks = gates.pop(sub)
        gated = []
        for n in range(RG_BLOCKS):
            hs = _segments_to_rows(hs_scr[sub], n * slabs_per_chunk, slabs_per_chunk, seg_len)
            if sub == n_sub - 1:
                hlast_ref[0, :, n * RG_BLOCK_W:(n + 1) * RG_BLOCK_W] = (
                    hs[sub_rows - V7X_SUBLANES:, :])
            gated.append((hs * gate_chunks[n]).astype(BF16))
        gated = jnp.concatenate(gated, axis=1)
        for n in range(RG_BLOCKS):
            cs = slice(n * RG_BLOCK_W, (n + 1) * RG_BLOCK_W)
            y_ref[rows, cs] = x_ref[rows, cs] + jnp.dot(gated, wout_ref[:, cs],
                                                        preferred_element_type=F32)
            yield

    conv_prev[0] = ucarry[...]
    h_prev[0] = [hcarry[V7X_SUBLANES - 1:V7X_SUBLANES, s * V7X_LANES:(s + 1) * V7X_LANES]
                 for s in range(n_slabs)]
    stages = [project_in, conv, project_gates, recur, project_out]
    for step in range(n_sub + len(stages) - 1):
        active = [stages[step - sub](sub) for sub in range(n_sub)
                  if 0 <= step - sub < len(stages)]
        while active:
            for piece in list(active):
                if next(piece, "done") == "done":
                    active.remove(piece)

    ulast_ref[0] = conv_prev[n_sub]
    ucarry[...] = conv_prev[n_sub]
    hcarry[...] = hlast_ref[0]


def _rglru_sample_kernel(x_ref, prev_ref, hinit_ref, g_ref, wgate_ref, win_ref, cw_ref, cb_ref,
                         wa_ref, ba_ref, wx_ref, bx_ref, lam_ref, wout_ref, y_ref, hs_ref, u_ref,
                         *, t_new):
    def conv_of(n, u0):
        c0 = n * RG_BLOCK_W
        return _dwconv_groups(u0, prev_ref[:, c0:c0 + RG_BLOCK_W], cw_ref, c0, RG_BLOCK_W)

    def scan_of(n, a, b):
        a_grp, b_grp = _scan_groups(a, b)
        return a_grp * hinit_ref[:, n * RG_BLOCK_W:(n + 1) * RG_BLOCK_W] + b_grp

    def u_sink(n, u0):
        u_ref[:, n * RG_BLOCK_W:(n + 1) * RG_BLOCK_W] = u0

    def h_sink(n, hs):
        hs_ref[:, n * RG_BLOCK_W:(n + 1) * RG_BLOCK_W] = hs

    y_ref[...] = _rglru_body(x_ref[...], g_ref, wgate_ref, win_ref, cw_ref, cb_ref, wa_ref, ba_ref,
                             wx_ref, bx_ref, lam_ref, wout_ref, conv_of, scan_of, u_sink, h_sink)


def _rglru_weight_specs(p):
    return [_const_spec(a.shape) for a in p]


def _rglru_prompt(x, params, batch, seq):
    tm = RG_PROMPT_TILE
    sub_rows = RG_PROMPT_SUB
    nt = seq // tm
    row = pl.BlockSpec((tm, D_MODEL), lambda b, j: (b * nt + j, 0))
    last = pl.BlockSpec((1, V7X_SUBLANES, D_MODEL), lambda b, j: (b, 0, 0))
    seg_scratch = pltpu.VMEM((D_MODEL // V7X_LANES,
                              V7X_SUBLANES * _seg_pitch(sub_rows // V7X_SUBLANES), V7X_LANES),
                             F32)
    return pl.pallas_call(
        functools.partial(_rglru_prompt_kernel, tm=tm, sub_rows=sub_rows),
        grid=(batch, nt),
        in_specs=[row] + _rglru_weight_specs(params),
        out_specs=[row, last, last],
        out_shape=[jax.ShapeDtypeStruct((batch * seq, D_MODEL), F32),
                   jax.ShapeDtypeStruct((batch, V7X_SUBLANES, D_MODEL), F32),
                   jax.ShapeDtypeStruct((batch, V7X_SUBLANES, D_MODEL), F32)],
        scratch_shapes=[pltpu.VMEM((V7X_SUBLANES, D_MODEL), F32),
                        pltpu.VMEM((V7X_SUBLANES, D_MODEL), F32)]
        + [seg_scratch] * (RG_SEG_ARRAYS * (tm // sub_rows)),
        compiler_params=pltpu.CompilerParams(
            dimension_semantics=("arbitrary", "arbitrary"),
            vmem_limit_bytes=_vmem_limit(48 * 1024 * 1024)),
        name="rglru_prompt",
    )(x, *params)


def _rglru_sample(x, row0, prev, hinit, params, t_new):
    m = prev.shape[0]
    tm = min(RG_ROW_TILE, m)
    row = pl.BlockSpec((tm, D_MODEL), lambda i: (i, 0))
    row_in = pl.BlockSpec((tm, D_MODEL), lambda i: (i + row0 // tm, 0))
    return pl.pallas_call(
        functools.partial(_rglru_sample_kernel, t_new=t_new),
        grid=(m // tm,),
        in_specs=[row_in, row, row] + _rglru_weight_specs(params),
        out_specs=[row, row, row],
        out_shape=[jax.ShapeDtypeStruct((m, D_MODEL), F32)] * 3,
        compiler_params=pltpu.CompilerParams(
            dimension_semantics=("arbitrary",),
            vmem_limit_bytes=_vmem_limit(48 * 1024 * 1024)),
        name="rglru_sample",
    )(x, prev, hinit, *params)


SCONV_CHUNK = V7X_MXU_DIM


def _sconv_body(x, g_ref, win_ref, cw_ref, wout_ref, conv_of, v_sink):
    h = _rms(x, g_ref[...]).astype(BF16)
    bcx = _mm(h, win_ref[...])
    gated = []
    for n in range(D_MODEL // SCONV_CHUNK):
        c0 = n * SCONV_CHUNK
        bg = bcx[:, c0:c0 + SCONV_CHUNK]
        cg = bcx[:, D_MODEL + c0:D_MODEL + c0 + SCONV_CHUNK]
        xv = bcx[:, 2 * D_MODEL + c0:2 * D_MODEL + c0 + SCONV_CHUNK]
        v = cg * xv
        v_sink(n, v)
        gated.append((bg * conv_of(n, v)).astype(BF16))
    return x + _mm(jnp.concatenate(gated, axis=1), wout_ref[...])


def _sconv_prompt_kernel(x_ref, g_ref, win_ref, cw_ref, wout_ref, y_ref, vlast_ref, vcarry,
                         *, tm):
    j = pl.program_id(1)

    @pl.when(j == 0)
    def _():
        vcarry[...] = jnp.zeros(vcarry.shape, F32)

    def conv_of(n, v):
        c0 = n * SCONV_CHUNK
        return _dwconv_rows(v, vcarry[:, c0:c0 + SCONV_CHUNK], cw_ref, c0, SCONV_CHUNK)

    def v_sink(n, v):
        vlast_ref[0, :, n * SCONV_CHUNK:(n + 1) * SCONV_CHUNK] = v[tm - V7X_SUBLANES:, :]

    y_ref[...] = _sconv_body(x_ref[...], g_ref, win_ref, cw_ref, wout_ref, conv_of, v_sink)
    vcarry[...] = vlast_ref[0]


def _sconv_sample_kernel(x_ref, prev_ref, g_ref, win_ref, cw_ref, wout_ref, y_ref, v_ref):
    def conv_of(n, v):
        c0 = n * SCONV_CHUNK
        return _dwconv_groups(v, prev_ref[:, c0:c0 + SCONV_CHUNK], cw_ref, c0, SCONV_CHUNK)

    def v_sink(n, v):
        v_ref[:, n * SCONV_CHUNK:(n + 1) * SCONV_CHUNK] = v

    y_ref[...] = _sconv_body(x_ref[...], g_ref, win_ref, cw_ref, wout_ref, conv_of, v_sink)


def _sconv_prompt(x, params, batch, seq):
    tm = SCONV_ROW_TILE
    nt = seq // tm
    row = pl.BlockSpec((tm, D_MODEL), lambda b, j: (b * nt + j, 0))
    last = pl.BlockSpec((1, V7X_SUBLANES, D_MODEL), lambda b, j: (b, 0, 0))
    return pl.pallas_call(
        functools.partial(_sconv_prompt_kernel, tm=tm),
        grid=(batch, nt),
        in_specs=[row] + [_spec_of(a) for a in params],
        out_specs=[row, last],
        out_shape=[jax.ShapeDtypeStruct((batch * seq, D_MODEL), F32),
                   jax.ShapeDtypeStruct((batch, V7X_SUBLANES, D_MODEL), F32)],
        scratch_shapes=[pltpu.VMEM((V7X_SUBLANES, D_MODEL), F32)],
        compiler_params=pltpu.CompilerParams(
            dimension_semantics=("arbitrary", "arbitrary"),
            vmem_limit_bytes=_vmem_limit(48 * 1024 * 1024)),
        name="sconv_prompt",
    )(x, *[_arg_of(a) for a in params])


def _sconv_sample(x, row0, prev, params):
    m = prev.shape[0]
    tm = min(ROW_TILE, m)
    row = pl.BlockSpec((tm, D_MODEL), lambda i: (i, 0))
    row_in = pl.BlockSpec((tm, D_MODEL), lambda i: (i + row0 // tm, 0))
    return pl.pallas_call(
        _sconv_sample_kernel,
        grid=(m // tm,),
        in_specs=[row_in, row] + [_spec_of(a) for a in params],
        out_specs=[row, row],
        out_shape=[jax.ShapeDtypeStruct((m, D_MODEL), F32)] * 2,
        compiler_params=pltpu.CompilerParams(
            dimension_semantics=("arbitrary",),
            vmem_limit_bytes=_vmem_limit(48 * 1024 * 1024)),
        name="sconv_sample",
    )(x, prev, *[_arg_of(a) for a in params])


def _rope_tables(pos):
    half = HEAD_DIM // 2
    inv = ROPE_THETA ** (-jnp.arange(half, dtype=F32) / half)
    ang = pos.astype(F32)[:, None] * inv[None, :]
    cos = jnp.cos(ang)
    sin = jnp.sin(ang)
    reps = V7X_LANES // HEAD_DIM
    cos_t = jnp.tile(jnp.concatenate([cos, cos], axis=-1), (1, reps))
    sin_t = jnp.tile(jnp.concatenate([-sin, sin], axis=-1), (1, reps))
    return cos_t, sin_t


def _row(v):
    return v.reshape(1, -1)


def _pad_state_rows(buf):
    b, k, c = buf.shape
    padded = jnp.concatenate([jnp.zeros((b, V7X_SUBLANES - k, c), buf.dtype), buf], axis=1)
    return padded.reshape(b * V7X_SUBLANES, c)


def kernel(x_prompt, x_sample, cache_k, cache_v, state_rglru_h, state_rglru_conv, state_shortconv,
           norm_mixer, norm_ffn, norm_final,
           attn_w_qkv, attn_b_qkv, attn_w_o, attn_b_o, attn_sinks,
           rglru_w_gate, rglru_w_in, rglru_conv_w, rglru_conv_b, rglru_wa, rglru_ba,
           rglru_wx, rglru_bx, rglru_lambda, rglru_w_out,
           sconv_w_in, sconv_conv_w, sconv_w_out,
           ffn_w_gate, ffn_w_up, ffn_w_down):
    bp, seq, _ = x_prompt.shape
    bs, t_new, _ = x_sample.shape
    depth = norm_mixer.shape[0]
    assert t_new == V7X_SUBLANES

    xp = x_prompt.reshape(bp * seq, D_MODEL)
    xs = x_sample.reshape(bs * t_new, D_MODEL)
    s_row0 = 0

    cos_p, sin_p = _rope_tables(jnp.arange(seq, dtype=jnp.int32))
    cos_s, sin_s = _rope_tables(PAST_LEN + jnp.arange(t_new, dtype=jnp.int32))
    cos_s = jnp.tile(cos_s, (SAMPLE_ATTN_BATCHES, 1))
    sin_s = jnp.tile(sin_s, (SAMPLE_ATTN_BATCHES, 1))

    to_lane_major = (0, 1, 3, 4, 2)
    from_lane_major = (0, 1, 4, 2, 3)
    ck_t = jnp.transpose(cache_k, to_lane_major)
    cv_t = jnp.transpose(cache_v, to_lane_major)
    nk_t = jnp.zeros(ck_t.shape, F32)
    nv_t = jnp.zeros(cv_t.shape, F32)

    kp_l, vp_l = [], []
    hp_l, hs_l, rcp_l, rcs_l = [], [], [], []
    scp_l, scs_l = [], []

    for i in range(depth):
        kind = i % 3
        j = i // 3
        g_mix = _row(norm_mixer[i])
        if kind == 0:
            wqkv = _layer(attn_w_qkv, j)
            wo = _layer(attn_w_o, j)
            bqkv = _row(attn_b_qkv[j])
            bo = _row(attn_b_o[j])
            sinks = attn_sinks[j]
            xp, kp, vp = _attn_prompt(xp, g_mix, wqkv, bqkv, wo, bo, cos_p, sin_p, sinks, bp, seq)
            xs, nk_t, nv_t = _attn_sample(xs, s_row0, g_mix, wqkv, bqkv, wo, bo, cos_s, sin_s,
                                          sinks, ck_t, cv_t, nk_t, nv_t, j, t_new)
            kp_l.append(kp.reshape(bp, WINDOW, N_KV_HEADS, HEAD_DIM))
            vp_l.append(vp.reshape(bp, WINDOW, N_KV_HEADS, HEAD_DIM))
        elif kind == 1:
            params = (g_mix, rglru_w_gate[j].astype(BF16), rglru_w_in[j].astype(BF16),
                      rglru_conv_w[j], _row(rglru_conv_b[j]), rglru_wa[j].astype(BF16),
                      _row(rglru_ba[j]), rglru_wx[j].astype(BF16), _row(rglru_bx[j]),
                      _row(rglru_lambda[j]), rglru_w_out[j].astype(BF16))
            half = [(rglru_wa[j] * 0.5).astype(BF16), (rglru_wx[j] * 0.5).astype(BF16)]
            prompt_params = params[:5] + (half[0], params[6], half[1]) + params[8:]
            xp, hlast, ulast = _rglru_prompt(xp, prompt_params, bp, seq)
            hp_l.append(hlast[:, V7X_SUBLANES - 1])
            rcp_l.append(ulast[:, V7X_SUBLANES - (RG_CONV_W - 1):])
            prev = _pad_state_rows(state_rglru_conv[j])
            hinit = jnp.repeat(state_rglru_h[j], t_new, axis=0)
            xs, hs_all, u_all = _rglru_sample(xs, s_row0, prev, hinit, params, t_new)
            hs_l.append(hs_all.reshape(bs, t_new, D_MODEL)[:, t_new - 1])
            rcs_l.append(u_all.reshape(bs, t_new, D_MODEL)[:, t_new - (RG_CONV_W - 1):])
        else:
            params = (g_mix, _layer(sconv_w_in, j), sconv_conv_w[j], _layer(sconv_w_out, j))
            xp, vlast = _sconv_prompt(xp, params, bp, seq)
            scp_l.append(vlast[:, V7X_SUBLANES - (SCONV_W - 1):])
            prev = _pad_state_rows(state_shortconv[j])
            xs, v_all = _sconv_sample(xs, s_row0, prev, params)
            scs_l.append(v_all.reshape(bs, t_new, D_MODEL)[:, t_new - (SCONV_W - 1):])

        wg = _layer(ffn_w_gate, i)
        wu = _layer(ffn_w_up, i)
        wd = _layer(ffn_w_down, i)
        d_ff = ffn_w_down.shape[1]
        g_ffn = _row(norm_ffn[i])
        g_fin = _row(norm_final)
        last = i == depth - 1
        if last:
            xp = _ffn(xp, bp * seq, None, g_ffn, wg, wu, wd, d_ff, g_fin, True)
            xs = _ffn(xs, bs * t_new, None, g_ffn, wg, wu, wd, d_ff, g_fin, True)
        else:
            xp = xs = _ffn(xp, bp * seq, xs, g_ffn, wg, wu, wd, d_ff, g_fin, False)
            s_row0 = bp * seq

    return (xp.reshape(bp, seq, D_MODEL), xs.reshape(bs, t_new, D_MODEL),
            jnp.stack(kp_l), jnp.stack(vp_l),
            jnp.transpose(nk_t, from_lane_major), jnp.transpose(nv_t, from_lane_major),
            jnp.stack(hp_l), jnp.stack(hs_l), jnp.stack(rcp_l), jnp.stack(rcs_l),
            jnp.stack(scp_l), jnp.stack(scs_l))
```

```python
import functools

import jax
import jax.numpy as jnp
from jax import lax
from jax.experimental import pallas as pl
from jax.experimental.pallas import tpu as pltpu

D_MODEL = 1024
HEAD_DIM = 64
N_HEADS = 16
N_KV_HEADS = 2
GQA_GROUP = 8
Q_DIM = N_HEADS * HEAD_DIM
KV_DIM = N_KV_HEADS * HEAD_DIM
WINDOW = 128
ROPE_THETA = 10000.0
NEG_INF = -1e30
RG_BLOCKS = 4
RG_BLOCK_W = 256
RG_CONV_W = 4
RG_C = 8.0
SCONV_W = 3
LOG2_E = 1.4426950408889634
EPS = 1e-6
PAST_LEN = 8192

V7X_LANES = 128
V7X_SUBLANES = 8
V7X_MXU_DIM = 256
V7X_VMEM_BYTES = 64 * 1024 * 1024

BF16 = jnp.bfloat16
F32 = jnp.float32

FFN_CHUNK = V7X_MXU_DIM
FFN_ROW_TILE = 512
FFN_STAGING_SLOTS = 2
ROW_TILE = 1024
SCONV_ROW_TILE = 1024
ATTN_ROW_TILE = 1024
RG_ROW_TILE = 512
RG_PROMPT_TILE = 1024
RG_PROMPT_SUB = 1024
RG_SEG_ARRAYS = 5
SAMPLE_ATTN_BATCHES = 32
SAMPLE_ATTN_UNROLL = 8
ATTN_LOOKAHEAD = 2
PROMPT_ATTN_COLS = 2
PROMPT_ATTN_BLOCKS = 1
COLS_PER_KV = (N_HEADS // N_KV_HEADS) * HEAD_DIM // V7X_LANES


def _vmem_limit(nbytes):
    return int(min(nbytes, V7X_VMEM_BYTES - 8 * 1024 * 1024))


def _const_spec(shape):
    nd = len(shape)
    return pl.BlockSpec(shape, lambda *_: (0,) * nd, pipeline_mode=pl.Buffered(1))


def _layer(stacked, layer):
    return (stacked, layer)


def _spec_of(w):
    if isinstance(w, tuple):
        stacked, layer = w
        nd = stacked.ndim - 1
        return pl.BlockSpec((None,) + stacked.shape[1:], lambda *_: (layer,) + (0,) * nd,
                            pipeline_mode=pl.Buffered(1))
    return _const_spec(w.shape)


def _arg_of(w):
    return w[0] if isinstance(w, tuple) else w


def _rms(x, g):
    ms = jnp.mean(x * x, axis=-1, keepdims=True)
    return x * lax.rsqrt(ms + EPS) * g


def _mm(a, w):
    return jnp.dot(a.astype(BF16), w.astype(BF16), preferred_element_type=F32)


def _ffn_kernel(*refs, n_chunks, final_norm, tiles_a, stacked, layer):
    if stacked:
        xa_ref, xb_ref = refs[:2]
        refs = refs[2:]
    else:
        xa_ref = refs[0]
        refs = refs[1:]
    (g_ref, wg_hbm, wu_hbm, wd_hbm, gf_ref, o_ref,
     wg_ref, wu_ref, wd_ref, stage_g, stage_u, stage_d, act_ref, sem) = refs
    step = pl.program_id(0)

    def chunk_copies(c):
        cs = slice(c * FFN_CHUNK, (c + 1) * FFN_CHUNK)
        slot = c % FFN_STAGING_SLOTS
        return (pltpu.make_async_copy(wg_hbm.at[layer, :, cs], stage_g.at[slot], sem.at[0, slot]),
                pltpu.make_async_copy(wu_hbm.at[layer, :, cs], stage_u.at[slot], sem.at[1, slot]),
                pltpu.make_async_copy(wd_hbm.at[layer, cs, :], stage_d.at[slot], sem.at[2, slot]))

    def run(fetch_weights):
        if fetch_weights:
            for c in range(min(FFN_STAGING_SLOTS, n_chunks)):
                for copy in chunk_copies(c):
                    copy.start()
        if stacked:
            x = jnp.where(step < tiles_a, xa_ref[...], xb_ref[...])
        else:
            x = xa_ref[...]
        h = _rms(x, g_ref[...]).astype(BF16)
        for c in range(n_chunks):
            cs = slice(c * FFN_CHUNK, (c + 1) * FFN_CHUNK)
            if fetch_weights:
                slot = c % FFN_STAGING_SLOTS
                for copy in chunk_copies(c):
                    copy.wait()
                wg_ref[:, cs] = stage_g[slot].astype(BF16)
                wu_ref[:, cs] = stage_u[slot].astype(BF16)
                wd_ref[cs, :] = stage_d[slot].astype(BF16)
                if c + FFN_STAGING_SLOTS < n_chunks:
                    for copy in chunk_copies(c + FFN_STAGING_SLOTS):
                        copy.start()
            gate = _mm(h, wg_ref[:, cs])
            up = _mm(h, wu_ref[:, cs])
            act_ref[:, cs] = ((gate * jax.nn.sigmoid(gate)) * up).astype(BF16)
        acc = x + jnp.dot(act_ref[...], wd_ref[...], preferred_element_type=F32)
        if final_norm:
            acc = _rms(acc, gf_ref[...])
        o_ref[...] = acc

    @pl.when(step == 0)
    def _():
        run(True)

    @pl.when(step != 0)
    def _():
        run(False)


def _ffn(xa, rows_a, xb, g, wg, wu, wd, d_ff, gf, final_norm):
    tm = FFN_ROW_TILE
    tiles_a = rows_a // tm
    stacked = xb is not None
    tiles_b = xb.shape[0] // tm if stacked else 0
    n_chunks = d_ff // FFN_CHUNK
    out_row = pl.BlockSpec((tm, D_MODEL), lambda i: (i, 0))
    if stacked:
        x_specs = [pl.BlockSpec((tm, D_MODEL), lambda i: (jnp.minimum(i, tiles_a - 1), 0)),
                   pl.BlockSpec((tm, D_MODEL), lambda i: (jnp.maximum(i - tiles_a, 0), 0),
                                pipeline_mode=pl.Buffered(1))]
        x_args = [xa, xb]
    else:
        x_specs = [out_row]
        x_args = [xa]
    (wg_all, layer), (wu_all, _), (wd_all, _) = wg, wu, wd
    bf16_weight_bytes = 2 * 3 * D_MODEL * d_ff
    staging_bytes = 4 * 3 * FFN_STAGING_SLOTS * D_MODEL * FFN_CHUNK
    in_hbm = pl.BlockSpec(memory_space=pl.ANY)
    return pl.pallas_call(
        functools.partial(_ffn_kernel, n_chunks=n_chunks, final_norm=final_norm,
                          tiles_a=tiles_a, stacked=stacked, layer=layer),
        grid=(tiles_a + tiles_b,),
        in_specs=x_specs + [_const_spec((1, D_MODEL)), in_hbm, in_hbm, in_hbm,
                            _const_spec((1, D_MODEL))],
        out_specs=out_row,
        out_shape=jax.ShapeDtypeStruct(((tiles_a + tiles_b) * tm, D_MODEL), F32),
        scratch_shapes=[pltpu.VMEM(wg_all.shape[1:], BF16),
                        pltpu.VMEM(wu_all.shape[1:], BF16),
                        pltpu.VMEM(wd_all.shape[1:], BF16),
                        pltpu.VMEM((FFN_STAGING_SLOTS, D_MODEL, FFN_CHUNK), F32),
                        pltpu.VMEM((FFN_STAGING_SLOTS, D_MODEL, FFN_CHUNK), F32),
                        pltpu.VMEM((FFN_STAGING_SLOTS, FFN_CHUNK, D_MODEL), F32),
                        pltpu.VMEM((tm, d_ff), BF16),
                        pltpu.SemaphoreType.DMA((3, FFN_STAGING_SLOTS))],
        compiler_params=pltpu.CompilerParams(
            dimension_semantics=("arbitrary",),
            vmem_limit_bytes=_vmem_limit(bf16_weight_bytes + staging_bytes
                                         + 16 * tm * D_MODEL * 4)),
        name="ffn",
    )(*x_args, g, wg_all, wu_all, wd_all, gf)


def _rope_cols(cols, cos, sin_signed, first_half):
    swapped = jnp.where(first_half,
                        pltpu.roll(cols, V7X_LANES - HEAD_DIM // 2, 1),
                        pltpu.roll(cols, HEAD_DIM // 2, 1))
    return cols * cos + swapped * sin_signed


def _pair_blockdiag(mat, mat_rolled, kv, lane_lt64):
    zero = jnp.zeros_like(mat)
    if kv == 0:
        top = jnp.where(lane_lt64, mat, zero)
        bottom = jnp.where(lane_lt64, zero, mat_rolled)
    else:
        top = jnp.where(lane_lt64, mat_rolled, zero)
        bottom = jnp.where(lane_lt64, zero, mat)
    return jnp.concatenate([top, bottom], axis=0).astype(BF16)


def _key_blockdiag(k2, kv):
    lane_lt64 = lax.broadcasted_iota(jnp.int32, k2.shape, 1) < HEAD_DIM
    return _pair_blockdiag(k2, pltpu.roll(k2, HEAD_DIM, 1), kv, lane_lt64)


def _value_blockdiag(v2, kv):
    v2 = jnp.where(lax.broadcasted_iota(jnp.int32, v2.shape, 0) == 0, 0.0, v2)
    vbd = _key_blockdiag(v2, kv)
    row = lax.broadcasted_iota(jnp.int32, vbd.shape, 0)
    lane = lax.broadcasted_iota(jnp.int32, vbd.shape, 1)
    ones_bd = jnp.where((lane < HEAD_DIM) == (row < 2 * WINDOW), 1.0, 0.0).astype(BF16)
    return jnp.concatenate([vbd, ones_bd], axis=1)


def _attn_weights(s_all, mask, sink_ref, kv, cols, tq):
    nk = 2 * WINDOW
    key = lax.broadcasted_iota(jnp.int32, (1, nk), 1)
    e_rows = []
    for i, c in enumerate(cols):
        e_halves = []
        for parity in range(2):
            sink = sink_ref[kv * GQA_GROUP + 2 * c + parity]
            fill = jnp.where(key == 0, sink, NEG_INF)
            s = s_all[i * tq:(i + 1) * tq, parity * nk:(parity + 1) * nk]
            s = jnp.where(mask, s, fill)
            m = jnp.max(s, axis=-1, keepdims=True)
            e_halves.append(jnp.exp(s - m))
        e_rows.append(jnp.concatenate(e_halves, axis=1))
    return jnp.concatenate(e_rows, axis=0).astype(BF16)


def _key_blockdiag_t(kt, kv):
    del kv
    zero = jnp.zeros_like(kt)
    return jnp.concatenate([jnp.concatenate([kt, zero], axis=1),
                            jnp.concatenate([zero, kt], axis=1)], axis=0).astype(BF16)


def _value_blockdiag_t(vt, kv):
    del kv
    vt = jnp.where(lax.broadcasted_iota(jnp.int32, vt.shape, 1) == 0, 0.0, vt)
    zero = jnp.zeros_like(vt)
    one = jnp.ones_like(vt)
    return jnp.concatenate([jnp.concatenate([vt, zero], axis=1),
                            jnp.concatenate([zero, vt], axis=1),
                            jnp.concatenate([one, zero], axis=1),
                            jnp.concatenate([zero, one], axis=1)], axis=0).astype(BF16)


def _dot_nt(a, b):
    return lax.dot_general(a, b, (((1,), (1,)), ((), ())), preferred_element_type=F32)


def _dot_nn(a, b):
    return jnp.dot(a, b, preferred_element_type=F32)


ROW_MAJOR_KV = (_key_blockdiag, _dot_nt, _value_blockdiag, _dot_nn)
LANE_MAJOR_KV = (_key_blockdiag_t, _dot_nn, _value_blockdiag_t, _dot_nt)


def _attend_units(units, sink_ref, tq, kv_ops):
    make_kbd, score_dot, make_vbd, value_dot = kv_ops
    n = len(units)
    scores, kbd, vbd = {}, {}, {}

    def issue_scores(i):
        load_q, load_k, _, _, kv, _, _, kv_id = units[i]
        if kv_id not in kbd:
            kbd[kv_id] = make_kbd(load_k(), kv)
        scores[i] = score_dot(load_q(), kbd[kv_id])

    for i in range(min(ATTN_LOOKAHEAD, n)):
        issue_scores(i)
    for i in range(n):
        if i + ATTN_LOOKAHEAD < n:
            issue_scores(i + ATTN_LOOKAHEAD)
        _, _, load_v, mask, kv, cols, store_o, kv_id = units[i]
        if kv_id not in vbd:
            vbd[kv_id] = make_vbd(load_v(), kv)
        e_all = _attn_weights(scores.pop(i), mask, sink_ref, kv, cols, tq)
        o_den = value_dot(e_all, vbd[kv_id])
        store_o(o_den[:, :V7X_LANES] * (1.0 / o_den[:, V7X_LANES:]))


def _band_mask(tq, col_min):
    row = lax.broadcasted_iota(jnp.int32, (tq, 2 * WINDOW), 0)
    col = lax.broadcasted_iota(jnp.int32, (tq, 2 * WINDOW), 1)
    prev_ok = (col < WINDOW) & (col > row)
    own_ok = (col >= WINDOW) & (col - WINDOW <= row)
    return (prev_ok | own_ok) & (col >= col_min)


def _attn_prompt_kernel(x_ref, g_ref, wqkv_ref, bqkv_ref, wo_ref, bo_ref, cos_ref, sin_ref,
                        sink_ref, y_ref, knew_ref, vnew_ref,
                        q_scr, k_scr, v_scr, a_scr, *, tq_tile, n_tiles):
    j = pl.program_id(1)
    n_blk = tq_tile // WINDOW

    @pl.when(j == 0)
    def _():
        k_scr[0:WINDOW, :] = jnp.zeros((WINDOW, V7X_LANES), F32)
        v_scr[0:WINDOW, :] = jnp.zeros((WINDOW, V7X_LANES), F32)

    x = x_ref[...]
    h = _rms(x, g_ref[...])
    qkv = _mm(h, wqkv_ref[...]) + bqkv_ref[...]
    cos = cos_ref[...]
    sin = sin_ref[...]
    lane = lax.broadcasted_iota(jnp.int32, (tq_tile, V7X_LANES), 1)
    first_half = (lane % HEAD_DIM) < (HEAD_DIM // 2)
    scale = HEAD_DIM ** -0.5
    for c in range(Q_DIM // V7X_LANES):
        qc = _rope_cols(qkv[:, c * V7X_LANES:(c + 1) * V7X_LANES], cos, sin, first_half)
        q_scr[:, c * V7X_LANES:(c + 1) * V7X_LANES] = (qc * scale).astype(BF16)
    k_new = _rope_cols(qkv[:, Q_DIM:Q_DIM + KV_DIM], cos, sin, first_half)
    v_new = qkv[:, Q_DIM + KV_DIM:]
    k_scr[WINDOW:, :] = k_new
    v_scr[WINDOW:, :] = v_new

    def unit(r0, mask, kv, cols, kv_id):
        def load_q():
            return jnp.concatenate(
                [q_scr[pl.ds(r0, WINDOW),
                       (kv * COLS_PER_KV + c) * V7X_LANES:(kv * COLS_PER_KV + c + 1) * V7X_LANES]
                 for c in cols], axis=0)

        def store_o(o):
            for i, c in enumerate(cols):
                col = kv * COLS_PER_KV + c
                a_scr[pl.ds(r0, WINDOW), col * V7X_LANES:(col + 1) * V7X_LANES] = (
                    o[i * WINDOW:(i + 1) * WINDOW].astype(BF16))

        return (load_q, lambda: k_scr[pl.ds(r0, 2 * WINDOW), :],
                lambda: v_scr[pl.ds(r0, 2 * WINDOW), :], mask, kv, cols, store_o, kv_id)

    col_groups = [tuple(range(c, c + PROMPT_ATTN_COLS))
                  for c in range(0, COLS_PER_KV, PROMPT_ATTN_COLS)]

    def blocks(i, carry):
        units = []
        for local in range(PROMPT_ATTN_BLOCKS):
            blk = i * PROMPT_ATTN_BLOCKS + local
            r0 = pl.multiple_of(blk * WINDOW, WINDOW)
            first = jnp.logical_and(j == 0, blk == 0)
            mask = _band_mask(WINDOW, jnp.where(first, WINDOW, 0))
            units += [unit(r0, mask, kv, cols, (local, kv)) for kv in range(N_KV_HEADS)
                      for cols in col_groups]
        _attend_units(units, sink_ref, WINDOW, ROW_MAJOR_KV)
        return carry

    lax.fori_loop(0, n_blk // PROMPT_ATTN_BLOCKS, blocks, 0)

    y_ref[...] = _mm(a_scr[...], wo_ref[...]) + bo_ref[...] + x

    k_scr[0:WINDOW, :] = k_new[tq_tile - WINDOW:, :]
    v_scr[0:WINDOW, :] = v_new[tq_tile - WINDOW:, :]

    @pl.when(j == n_tiles - 1)
    def _():
        knew_ref[0] = k_new[tq_tile - WINDOW:, :]
        vnew_ref[0] = v_new[tq_tile - WINDOW:, :]


def _attn_prompt(x, g, wqkv, bqkv, wo, bo, cos_t, sin_t, sinks, batch, seq):
    tq = ATTN_ROW_TILE
    nt = seq // tq
    row = pl.BlockSpec((tq, D_MODEL), lambda b, j: (b * nt + j, 0))
    tab = pl.BlockSpec((tq, V7X_LANES), lambda b, j: (j, 0))
    cache = pl.BlockSpec((1, WINDOW, V7X_LANES), lambda b, j: (b, 0, 0))
    return pl.pallas_call(
        functools.partial(_attn_prompt_kernel, tq_tile=tq, n_tiles=nt),
        grid=(batch, nt),
        in_specs=[row, _const_spec((1, D_MODEL)), _spec_of(wqkv),
                  _const_spec((1, Q_DIM + 2 * KV_DIM)), _spec_of(wo), _const_spec((1, D_MODEL)),
                  tab, tab, pl.BlockSpec(memory_space=pltpu.SMEM)],
        out_specs=[row, cache, cache],
        out_shape=[jax.ShapeDtypeStruct((batch * seq, D_MODEL), F32),
                   jax.ShapeDtypeStruct((batch, WINDOW, V7X_LANES), F32),
                   jax.ShapeDtypeStruct((batch, WINDOW, V7X_LANES), F32)],
        scratch_shapes=[pltpu.VMEM((tq, Q_DIM), BF16),
                        pltpu.VMEM((WINDOW + tq, V7X_LANES), F32),
                        pltpu.VMEM((WINDOW + tq, V7X_LANES), F32),
                        pltpu.VMEM((tq, Q_DIM), BF16)],
        compiler_params=pltpu.CompilerParams(
            dimension_semantics=("arbitrary", "arbitrary"),
            vmem_limit_bytes=_vmem_limit(48 * 1024 * 1024)),
        name="attn_prompt",
    )(x, g, _arg_of(wqkv), bqkv, _arg_of(wo), bo, cos_t, sin_t, sinks)


def _attn_sample_kernel(x_ref, g_ref, wqkv_ref, bqkv_ref, wo_ref, bo_ref, cos_ref, sin_ref,
                        sink_ref, ck_ref, cv_ref, nk_in, nv_in, y_ref, nk_ref, nv_ref,
                        q_scr, k_scr, v_scr, a_scr, *, n_seq, t_new):
    del nk_in, nv_in
    tm = n_seq * t_new
    x = x_ref[...]
    h = _rms(x, g_ref[...])
    qkv = _mm(h, wqkv_ref[...]) + bqkv_ref[...]
    cos = cos_ref[...]
    sin = sin_ref[...]
    lane = lax.broadcasted_iota(jnp.int32, (tm, V7X_LANES), 1)
    first_half = (lane % HEAD_DIM) < (HEAD_DIM // 2)
    scale = HEAD_DIM ** -0.5
    for c in range(Q_DIM // V7X_LANES):
        qc = _rope_cols(qkv[:, c * V7X_LANES:(c + 1) * V7X_LANES], cos, sin, first_half)
        q_scr[:, c * V7X_LANES:(c + 1) * V7X_LANES] = qc * scale
    k_scr[...] = _rope_cols(qkv[:, Q_DIM:Q_DIM + KV_DIM], cos, sin, first_half)
    v_scr[...] = qkv[:, Q_DIM + KV_DIM:]
    mask = _band_mask(t_new, 0)
    pad = jnp.zeros((WINDOW - t_new, V7X_LANES), F32)
    lane = lax.broadcasted_iota(jnp.int32, (HEAD_DIM, WINDOW), 1)

    def new_rows_t(scr, b):
        r0 = pl.multiple_of(b * t_new, t_new)
        return jnp.concatenate([scr[pl.ds(r0, t_new), :], pad], axis=0).T

    def seq_group(i, carry):
        seqs = [i * SAMPLE_ATTN_UNROLL + u for u in range(SAMPLE_ATTN_UNROLL)]
        k_new_t = [new_rows_t(k_scr, b) for b in seqs]
        v_new_t = [new_rows_t(v_scr, b) for b in seqs]

        def unit(u, b, kv):
            r0 = pl.multiple_of(b * t_new, t_new)
            head_rows = slice(kv * HEAD_DIM, (kv + 1) * HEAD_DIM)

            def load_q():
                return jnp.concatenate(
                    [q_scr[pl.ds(r0, t_new), (kv * COLS_PER_KV + c) * V7X_LANES:
                           (kv * COLS_PER_KV + c + 1) * V7X_LANES]
                     for c in range(COLS_PER_KV)], axis=0).astype(BF16)

            def load_k():
                return jnp.concatenate([ck_ref[b, kv], k_new_t[u][head_rows, :]], axis=1)

            def load_v():
                return jnp.concatenate([cv_ref[b, kv], v_new_t[u][head_rows, :]], axis=1)

            def store_o(o):
                for c in range(COLS_PER_KV):
                    col = kv * COLS_PER_KV + c
                    a_scr[pl.ds(r0, t_new), col * V7X_LANES:(col + 1) * V7X_LANES] = (
                        o[c * t_new:(c + 1) * t_new])

            return (load_q, load_k, load_v, mask, kv, tuple(range(COLS_PER_KV)), store_o,
                    (u, kv))

        _attend_units([unit(u, b, kv) for u, b in enumerate(seqs) for kv in range(N_KV_HEADS)],
                      sink_ref, t_new, LANE_MAJOR_KV)
        keep = lane < WINDOW - t_new
        for u, b in enumerate(seqs):
            for kv in range(N_KV_HEADS):
                head_rows = slice(kv * HEAD_DIM, (kv + 1) * HEAD_DIM)
                nk_ref[b, kv] = jnp.where(
                    keep, pltpu.roll(ck_ref[b, kv], WINDOW - t_new, 1),
                    pltpu.roll(k_new_t[u][head_rows, :], WINDOW - t_new, 1))
                nv_ref[b, kv] = jnp.where(
                    keep, pltpu.roll(cv_ref[b, kv], WINDOW - t_new, 1),
                    pltpu.roll(v_new_t[u][head_rows, :], WINDOW - t_new, 1))
        return carry

    lax.fori_loop(0, n_seq // SAMPLE_ATTN_UNROLL, seq_group, 0)
    y_ref[...] = _mm(a_scr[...], wo_ref[...]) + bo_ref[...] + x


def _attn_sample(x, row0, g, wqkv, bqkv, wo, bo, cos_t, sin_t, sinks, ck_t, cv_t, nk_t, nv_t,
                 layer, t_new):
    bg = SAMPLE_ATTN_BATCHES
    tm = bg * t_new
    n_batch = ck_t.shape[1]
    row = pl.BlockSpec((tm, D_MODEL), lambda i: (i, 0))
    row_in = pl.BlockSpec((tm, D_MODEL), lambda i: (i + row0 // tm, 0))
    cache = pl.BlockSpec((None, bg, N_KV_HEADS, HEAD_DIM, WINDOW), lambda i: (layer, i, 0, 0, 0))
    whole = pl.BlockSpec(memory_space=pl.ANY)
    n_in = 13
    return pl.pallas_call(
        functools.partial(_attn_sample_kernel, n_seq=bg, t_new=t_new),
        grid=(n_batch // bg,),
        in_specs=[row_in, _const_spec((1, D_MODEL)), _spec_of(wqkv),
                  _const_spec((1, Q_DIM + 2 * KV_DIM)), _spec_of(wo), _const_spec((1, D_MODEL)),
                  _const_spec((tm, V7X_LANES)), _const_spec((tm, V7X_LANES)),
                  pl.BlockSpec(memory_space=pltpu.SMEM), cache, cache, whole, whole],
        out_specs=[row, cache, cache],
        out_shape=[jax.ShapeDtypeStruct((n_batch * t_new, D_MODEL), F32),
                   jax.ShapeDtypeStruct(nk_t.shape, F32),
                   jax.ShapeDtypeStruct(nv_t.shape, F32)],
        input_output_aliases={n_in - 2: 1, n_in - 1: 2},
        scratch_shapes=[pltpu.VMEM((tm, Q_DIM), F32),
                        pltpu.VMEM((tm, V7X_LANES), F32),
                        pltpu.VMEM((tm, V7X_LANES), F32),
                        pltpu.VMEM((tm, Q_DIM), F32)],
        compiler_params=pltpu.CompilerParams(
            dimension_semantics=("arbitrary",),
            vmem_limit_bytes=_vmem_limit(48 * 1024 * 1024)),
        name="attn_sample",
    )(x, g, _arg_of(wqkv), bqkv, _arg_of(wo), bo, cos_t, sin_t, sinks, ck_t, cv_t, nk_t, nv_t)


def _dwconv_groups(u, prev, w_ref, col0, ncol):
    rows = u.shape[0]
    kw = w_ref.shape[0]
    sub = lax.broadcasted_iota(jnp.int32, u.shape, 0) % V7X_SUBLANES
    y = u * w_ref[kw - 1:kw, col0:col0 + ncol]
    for s in range(1, kw):
        from_prev = pltpu.roll(prev, (rows - V7X_SUBLANES + s) % rows, 0)
        from_self = pltpu.roll(u, s, 0)
        shifted = jnp.where(sub < s, from_prev, from_self)
        y = y + shifted * w_ref[kw - 1 - s:kw - s, col0:col0 + ncol]
    return y


def _dwconv_rows(u, carry8, w_ref, col0, ncol):
    kw = w_ref.shape[0]
    sub8 = lax.broadcasted_iota(jnp.int32, carry8.shape, 0)
    y = u * w_ref[kw - 1:kw, col0:col0 + ncol]
    for s in range(1, kw):
        rolled = pltpu.roll(u, s, 0)
        head = jnp.where(sub8 < s, pltpu.roll(carry8, s, 0), rolled[:V7X_SUBLANES])
        shifted = jnp.concatenate([head, rolled[V7X_SUBLANES:]], axis=0)
        y = y + shifted * w_ref[kw - 1 - s:kw - s, col0:col0 + ncol]
    return y


def _scan_groups(a, b):
    pos = lax.broadcasted_iota(jnp.int32, a.shape, 0) % V7X_SUBLANES
    shift = 1
    while shift < V7X_SUBLANES:
        ok = pos >= shift
        a_sh = jnp.where(ok, pltpu.roll(a, shift, 0), 1.0)
        b_sh = jnp.where(ok, pltpu.roll(b, shift, 0), 0.0)
        b = a * b_sh + b
        a = a * a_sh
        shift *= 2
    return a, b


def _log_sigmoid(x):
    return jnp.minimum(x, 0.0) - jnp.log1p(jnp.exp(-jnp.abs(x)))


def _rglru_body(x, g_ref, wgate_ref, win_ref, cw_ref, cb_ref, wa_ref, ba_ref, wx_ref, bx_ref,
                lam_ref, wout_ref, conv_of, scan_of, u_sink, h_sink):
    h = _rms(x, g_ref[...]).astype(BF16)
    acc = x
    for n in range(RG_BLOCKS):
        c0 = n * RG_BLOCK_W
        cs = slice(c0, c0 + RG_BLOCK_W)
        gate = jax.nn.gelu(jnp.dot(h, wgate_ref[:, cs], preferred_element_type=F32))
        u0 = jnp.dot(h, win_ref[:, cs], preferred_element_type=F32)
        u_sink(n, u0)
        u = conv_of(n, u0) + cb_ref[:, cs]
        ub = u.astype(BF16)
        r = jax.nn.sigmoid(jnp.dot(ub, wa_ref[n], preferred_element_type=F32) + ba_ref[:, cs])
        ig = jax.nn.sigmoid(jnp.dot(ub, wx_ref[n], preferred_element_type=F32) + bx_ref[:, cs])
        log_a = RG_C * r * _log_sigmoid(lam_ref[:, cs])
        a = jnp.exp(log_a)
        mult = jnp.sqrt(-jnp.tanh(log_a) * (1.0 + a * a))
        hs = scan_of(n, a, mult * (ig * u))
        h_sink(n, hs)
        acc = acc + jnp.dot((hs * gate).astype(BF16), wout_ref[cs, :], preferred_element_type=F32)
    return acc


def _seg_pitch(seg_len):
    assert seg_len % V7X_SUBLANES == 0
    return seg_len + V7X_SUBLANES // 2


def _rows_to_segments(scr, slab0, x, seg_len):
    pitch = _seg_pitch(seg_len)
    for s in range(x.shape[1] // V7X_LANES):
        for i in range(V7X_SUBLANES):
            scr[slab0 + s, i * pitch:i * pitch + seg_len, :] = (
                x[i * seg_len:(i + 1) * seg_len, s * V7X_LANES:(s + 1) * V7X_LANES])


def _segments_to_rows(scr, slab0, n_slabs, seg_len):
    pitch = _seg_pitch(seg_len)
    return jnp.concatenate(
        [jnp.concatenate([scr[slab0 + s, i * pitch:i * pitch + seg_len, :]
                          for i in range(V7X_SUBLANES)], axis=0)
         for s in range(n_slabs)], axis=1)


def _seg_step(scr, slab, k, seg_len):
    return scr[slab, pl.ds(k, V7X_SUBLANES, stride=_seg_pitch(seg_len)), :]


def _seg_step_store(scr, slab, k, seg_len, v):
    scr[slab, pl.ds(k, V7X_SUBLANES, stride=_seg_pitch(seg_len)), :] = v


def _from_prev_segment(v, first):
    sub = lax.broadcasted_iota(jnp.int32, v.shape, 0)
    return jnp.where(sub == 0, first, pltpu.roll(v, 1, 0))


def _rglru_prompt_kernel(x_ref, g_ref, wgate_ref, win_ref, cw_ref, cb_ref, wa_ref, ba_ref,
                         wx_ref, bx_ref, lam_ref, wout_ref, y_ref, hlast_ref, ulast_ref,
                         ucarry, hcarry, *seg_scr, tm, sub_rows):
    j = pl.program_id(1)
    slabs_per_chunk = RG_BLOCK_W // V7X_LANES
    kw = cw_ref.shape[0]
    n_sub = tm // sub_rows
    u0_scr, u_scr, r_scr, ig_scr, hs_scr = (seg_scr[i * n_sub:(i + 1) * n_sub]
                                            for i in range(RG_SEG_ARRAYS))

    @pl.when(j == 0)
    def _():
        ucarry[...] = jnp.zeros(ucarry.shape, F32)
        hcarry[...] = jnp.zeros(hcarry.shape, F32)

    n_slabs = D_MODEL // V7X_LANES
    seg_len = sub_rows // V7X_SUBLANES
    gates, conv_prev, h_prev = {}, {}, {}

    def project_in(sub):
        x = x_ref[sub * sub_rows:(sub + 1) * sub_rows, :]
        h = _rms(x, g_ref[...]).astype(BF16)
        gate_chunks, last_rows = [], []
        for n in range(RG_BLOCKS):
            cs = slice(n * RG_BLOCK_W, (n + 1) * RG_BLOCK_W)
            gate_chunks.append(
                jax.nn.gelu(jnp.dot(h, wgate_ref[:, cs], preferred_element_type=F32)))
            u0 = jnp.dot(h, win_ref[:, cs], preferred_element_type=F32)
            last_rows.append(u0[sub_rows - V7X_SUBLANES:, :])
            _rows_to_segments(u0_scr[sub], n * slabs_per_chunk, u0, seg_len)
            yield
        gates[sub] = gate_chunks
        conv_prev[sub + 1] = jnp.concatenate(last_rows, axis=1)

    def conv(sub):
        for slab in range(n_slabs):
            lanes = slice(slab * V7X_LANES, (slab + 1) * V7X_LANES)
            taps = [cw_ref[t:t + 1, lanes] for t in range(kw)]
            bias = cb_ref[:, lanes]
            steps = {k: _seg_step(u0_scr[sub], slab, k, seg_len) for k in range(seg_len)}
            for back in range(1, kw):
                steps[-back] = _from_prev_segment(
                    steps[seg_len - back],
                    conv_prev[sub][V7X_SUBLANES - back:V7X_SUBLANES - back + 1, lanes])
            for k in range(seg_len):
                u_k = steps[k] * taps[kw - 1] + bias
                for back in range(1, kw):
                    u_k = u_k + steps[k - back] * taps[kw - 1 - back]
                _seg_step_store(u_scr[sub], slab, k, seg_len, u_k)
            yield

    def project_gates(sub):
        for n in range(RG_BLOCKS):
            slab0 = n * slabs_per_chunk
            ub = _segments_to_rows(u_scr[sub], slab0, slabs_per_chunk, seg_len).astype(BF16)
            _rows_to_segments(r_scr[sub], slab0,
                              jnp.dot(ub, wa_ref[n], preferred_element_type=F32), seg_len)
            _rows_to_segments(ig_scr[sub], slab0,
                              jnp.dot(ub, wx_ref[n], preferred_element_type=F32), seg_len)
            yield

    def recur(sub):
        h_prev[sub + 1] = []
        for slab in range(n_slabs):
            lanes = slice(slab * V7X_LANES, (slab + 1) * V7X_LANES)
            half_scale = (-0.5 * RG_C) * _log_sigmoid(lam_ref[:, lanes])
            ba = 0.5 * ba_ref[:, lanes]
            bx = 0.5 * bx_ref[:, lanes]
            h_in = h_prev[sub][slab]
            a_cum, h_loc = [], []
            for k in range(seg_len):
                neg_log_a = half_scale * jnp.tanh(_seg_step(r_scr[sub], slab, k, seg_len) + ba) \
                    + half_scale
                ig = 0.5 * jnp.tanh(_seg_step(ig_scr[sub], slab, k, seg_len) + bx) + 0.5
                a = jnp.exp2(neg_log_a * (-LOG2_E))
                one_minus_a2 = jnp.tanh(neg_log_a) * (1.0 + a * a)
                mult = jnp.where(one_minus_a2 > 0.0, one_minus_a2 * lax.rsqrt(one_minus_a2), 0.0)
                b = mult * (ig * _seg_step(u_scr[sub], slab, k, seg_len))
                if k == 0:
                    a_cum.append(a)
                    h_loc.append(b)
                else:
                    a_cum.append(a * a_cum[-1])
                    h_loc.append(a * h_loc[-1] + b)
            a_seg, b_seg = _scan_groups(a_cum[-1], h_loc[-1])
            seg_end = a_seg * h_in + b_seg
            h_prev[sub + 1].append(seg_end[V7X_SUBLANES - 1:, :])
            h_start = _from_prev_segment(seg_end, h_in)
            for k in range(seg_len):
                _seg_step_store(hs_scr[sub], slab, k, seg_len, a_cum[k] * h_start + h_loc[k])
            yield

    def project_out(sub):
        rows = slice(sub * sub_rows, (sub + 1) * sub_rows)
        gate_chunks = gates.pop(sub)
        gated = []
        for n in range(RG_BLOCKS):
            hs = _segments_to_rows(hs_scr[sub], n * slabs_per_chunk, slabs_per_chunk, seg_len)
            if sub == n_sub - 1:
                hlast_ref[0, :, n * RG_BLOCK_W:(n + 1) * RG_BLOCK_W] = (
                    hs[sub_rows - V7X_SUBLANES:, :])
            gated.append((hs * gate_chunks[n]).astype(BF16))
        gated = jnp.concatenate(gated, axis=1)
        for n in range(RG_BLOCKS):
            cs = slice(n * RG_BLOCK_W, (n + 1) * RG_BLOCK_W)
            y_ref[rows, cs] = x_ref[rows, cs] + jnp.dot(gated, wout_ref[:, cs],
                                                        preferred_element_type=F32)
            yield

    conv_prev[0] = ucarry[...]
    h_prev[0] = [hcarry[V7X_SUBLANES - 1:V7X_SUBLANES, s * V7X_LANES:(s + 1) * V7X_LANES]
                 for s in range(n_slabs)]
    stages = [project_in, conv, project_gates, recur, project_out]
    for step in range(n_sub + len(stages) - 1):
        active = [stages[step - sub](sub) for sub in range(n_sub)
                  if 0 <= step - sub < len(stages)]
        while active:
            for piece in list(active):
                if next(piece, "done") == "done":
                    active.remove(piece)

    ulast_ref[0] = conv_prev[n_sub]
    ucarry[...] = conv_prev[n_sub]
    hcarry[...] = hlast_ref[0]


def _rglru_sample_kernel(x_ref, prev_ref, hinit_ref, g_ref, wgate_ref, win_ref, cw_ref, cb_ref,
                         wa_ref, ba_ref, wx_ref, bx_ref, lam_ref, wout_ref, y_ref, hs_ref, u_ref,
                         *, t_new):
    def conv_of(n, u0):
        c0 = n * RG_BLOCK_W
        return _dwconv_groups(u0, prev_ref[:, c0:c0 + RG_BLOCK_W], cw_ref, c0, RG_BLOCK_W)

    def scan_of(n, a, b):
        a_grp, b_grp = _scan_groups(a, b)
        return a_grp * hinit_ref[:, n * RG_BLOCK_W:(n + 1) * RG_BLOCK_W] + b_grp

    def u_sink(n, u0):
        u_ref[:, n * RG_BLOCK_W:(n + 1) * RG_BLOCK_W] = u0

    def h_sink(n, hs):
        hs_ref[:, n * RG_BLOCK_W:(n + 1) * RG_BLOCK_W] = hs

    y_ref[...] = _rglru_body(x_ref[...], g_ref, wgate_ref, win_ref, cw_ref, cb_ref, wa_ref, ba_ref,
                             wx_ref, bx_ref, lam_ref, wout_ref, conv_of, scan_of, u_sink, h_sink)


def _rglru_weight_specs(p):
    return [_const_spec(a.shape) for a in p]


def _rglru_prompt(x, params, batch, seq):
    tm = RG_PROMPT_TILE
    sub_rows = RG_PROMPT_SUB
    nt = seq // tm
    row = pl.BlockSpec((tm, D_MODEL), lambda b, j: (b * nt + j, 0))
    last = pl.BlockSpec((1, V7X_SUBLANES, D_MODEL), lambda b, j: (b, 0, 0))
    seg_scratch = pltpu.VMEM((D_MODEL // V7X_LANES,
                              V7X_SUBLANES * _seg_pitch(sub_rows // V7X_SUBLANES), V7X_LANES),
                             F32)
    return pl.pallas_call(
        functools.partial(_rglru_prompt_kernel, tm=tm, sub_rows=sub_rows),
        grid=(batch, nt),
        in_specs=[row] + _rglru_weight_specs(params),
        out_specs=[row, last, last],
        out_shape=[jax.ShapeDtypeStruct((batch * seq, D_MODEL), F32),
                   jax.ShapeDtypeStruct((batch, V7X_SUBLANES, D_MODEL), F32),
                   jax.ShapeDtypeStruct((batch, V7X_SUBLANES, D_MODEL), F32)],
        scratch_shapes=[pltpu.VMEM((V7X_SUBLANES, D_MODEL), F32),
                        pltpu.VMEM((V7X_SUBLANES, D_MODEL), F32)]
        + [seg_scratch] * (RG_SEG_ARRAYS * (tm // sub_rows)),
        compiler_params=pltpu.CompilerParams(
            dimension_semantics=("arbitrary", "arbitrary"),
            vmem_limit_bytes=_vmem_limit(48 * 1024 * 1024)),
        name="rglru_prompt",
    )(x, *params)


def _rglru_sample(x, row0, prev, hinit, params, t_new):
    m = prev.shape[0]
    tm = min(RG_ROW_TILE, m)
    row = pl.BlockSpec((tm, D_MODEL), lambda i: (i, 0))
    row_in = pl.BlockSpec((tm, D_MODEL), lambda i: (i + row0 // tm, 0))
    return pl.pallas_call(
        functools.partial(_rglru_sample_kernel, t_new=t_new),
        grid=(m // tm,),
        in_specs=[row_in, row, row] + _rglru_weight_specs(params),
        out_specs=[row, row, row],
        out_shape=[jax.ShapeDtypeStruct((m, D_MODEL), F32)] * 3,
        compiler_params=pltpu.CompilerParams(
            dimension_semantics=("arbitrary",),
            vmem_limit_bytes=_vmem_limit(48 * 1024 * 1024)),
        name="rglru_sample",
    )(x, prev, hinit, *params)


SCONV_CHUNK = V7X_MXU_DIM


def _sconv_body(x, g_ref, win_ref, cw_ref, wout_ref, conv_of, v_sink):
    h = _rms(x, g_ref[...]).astype(BF16)
    bcx = _mm(h, win_ref[...])
    gated = []
    for n in range(D_MODEL // SCONV_CHUNK):
        c0 = n * SCONV_CHUNK
        bg = bcx[:, c0:c0 + SCONV_CHUNK]
        cg = bcx[:, D_MODEL + c0:D_MODEL + c0 + SCONV_CHUNK]
        xv = bcx[:, 2 * D_MODEL + c0:2 * D_MODEL + c0 + SCONV_CHUNK]
        v = cg * xv
        v_sink(n, v)
        gated.append((bg * conv_of(n, v)).astype(BF16))
    return x + _mm(jnp.concatenate(gated, axis=1), wout_ref[...])


def _sconv_prompt_kernel(x_ref, g_ref, win_ref, cw_ref, wout_ref, y_ref, vlast_ref, vcarry,
                         *, tm):
    j = pl.program_id(1)

    @pl.when(j == 0)
    def _():
        vcarry[...] = jnp.zeros(vcarry.shape, F32)

    def conv_of(n, v):
        c0 = n * SCONV_CHUNK
        return _dwconv_rows(v, vcarry[:, c0:c0 + SCONV_CHUNK], cw_ref, c0, SCONV_CHUNK)

    def v_sink(n, v):
        vlast_ref[0, :, n * SCONV_CHUNK:(n + 1) * SCONV_CHUNK] = v[tm - V7X_SUBLANES:, :]

    y_ref[...] = _sconv_body(x_ref[...], g_ref, win_ref, cw_ref, wout_ref, conv_of, v_sink)
    vcarry[...] = vlast_ref[0]


def _sconv_sample_kernel(x_ref, prev_ref, g_ref, win_ref, cw_ref, wout_ref, y_ref, v_ref):
    def conv_of(n, v):
        c0 = n * SCONV_CHUNK
        return _dwconv_groups(v, prev_ref[:, c0:c0 + SCONV_CHUNK], cw_ref, c0, SCONV_CHUNK)

    def v_sink(n, v):
        v_ref[:, n * SCONV_CHUNK:(n + 1) * SCONV_CHUNK] = v

    y_ref[...] = _sconv_body(x_ref[...], g_ref, win_ref, cw_ref, wout_ref, conv_of, v_sink)


def _sconv_prompt(x, params, batch, seq):
    tm = SCONV_ROW_TILE
    nt = seq // tm
    row = pl.BlockSpec((tm, D_MODEL), lambda b, j: (b * nt + j, 0))
    last = pl.BlockSpec((1, V7X_SUBLANES, D_MODEL), lambda b, j: (b, 0, 0))
    return pl.pallas_call(
        functools.partial(_sconv_prompt_kernel, tm=tm),
        grid=(batch, nt),
        in_specs=[row] + [_spec_of(a) for a in params],
        out_specs=[row, last],
        out_shape=[jax.ShapeDtypeStruct((batch * seq, D_MODEL), F32),
                   jax.ShapeDtypeStruct((batch, V7X_SUBLANES, D_MODEL), F32)],
        scratch_shapes=[pltpu.VMEM((V7X_SUBLANES, D_MODEL), F32)],
        compiler_params=pltpu.CompilerParams(
            dimension_semantics=("arbitrary", "arbitrary"),
            vmem_limit_bytes=_vmem_limit(48 * 1024 * 1024)),
        name="sconv_prompt",
    )(x, *[_arg_of(a) for a in params])


def _sconv_sample(x, row0, prev, params):
    m = prev.shape[0]
    tm = min(ROW_TILE, m)
    row = pl.BlockSpec((tm, D_MODEL), lambda i: (i, 0))
    row_in = pl.BlockSpec((tm, D_MODEL), lambda i: (i + row0 // tm, 0))
    return pl.pallas_call(
        _sconv_sample_kernel,
        grid=(m // tm,),
        in_specs=[row_in, row] + [_spec_of(a) for a in params],
        out_specs=[row, row],
        out_shape=[jax.ShapeDtypeStruct((m, D_MODEL), F32)] * 2,
        compiler_params=pltpu.CompilerParams(
            dimension_semantics=("arbitrary",),
            vmem_limit_bytes=_vmem_limit(48 * 1024 * 1024)),
        name="sconv_sample",
    )(x, prev, *[_arg_of(a) for a in params])


def _rope_tables(pos):
    half = HEAD_DIM // 2
    inv = ROPE_THETA ** (-jnp.arange(half, dtype=F32) / half)
    ang = pos.astype(F32)[:, None] * inv[None, :]
    cos = jnp.cos(ang)
    sin = jnp.sin(ang)
    reps = V7X_LANES // HEAD_DIM
    cos_t = jnp.tile(jnp.concatenate([cos, cos], axis=-1), (1, reps))
    sin_t = jnp.tile(jnp.concatenate([-sin, sin], axis=-1), (1, reps))
    return cos_t, sin_t


def _row(v):
    return v.reshape(1, -1)


def _pad_state_rows(buf):
    b, k, c = buf.shape
    padded = jnp.concatenate([jnp.zeros((b, V7X_SUBLANES - k, c), buf.dtype), buf], axis=1)
    return padded.reshape(b * V7X_SUBLANES, c)


def kernel(x_prompt, x_sample, cache_k, cache_v, state_rglru_h, state_rglru_conv, state_shortconv,
           norm_mixer, norm_ffn, norm_final,
           attn_w_qkv, attn_b_qkv, attn_w_o, attn_b_o, attn_sinks,
           rglru_w_gate, rglru_w_in, rglru_conv_w, rglru_conv_b, rglru_wa, rglru_ba,
           rglru_wx, rglru_bx, rglru_lambda, rglru_w_out,
           sconv_w_in, sconv_conv_w, sconv_w_out,
           ffn_w_gate, ffn_w_up, ffn_w_down):
    bp, seq, _ = x_prompt.shape
    bs, t_new, _ = x_sample.shape
    depth = norm_mixer.shape[0]
    assert t_new == V7X_SUBLANES

    xp = x_prompt.reshape(bp * seq, D_MODEL)
    xs = x_sample.reshape(bs * t_new, D_MODEL)
    s_row0 = 0

    cos_p, sin_p = _rope_tables(jnp.arange(seq, dtype=jnp.int32))
    cos_s, sin_s = _rope_tables(PAST_LEN + jnp.arange(t_new, dtype=jnp.int32))
    cos_s = jnp.tile(cos_s, (SAMPLE_ATTN_BATCHES, 1))
    sin_s = jnp.tile(sin_s, (SAMPLE_ATTN_BATCHES, 1))

    to_lane_major = (0, 1, 3, 4, 2)
    from_lane_major = (0, 1, 4, 2, 3)
    ck_t = jnp.transpose(cache_k, to_lane_major)
    cv_t = jnp.transpose(cache_v, to_lane_major)
    nk_t = jnp.zeros(ck_t.shape, F32)
    nv_t = jnp.zeros(cv_t.shape, F32)

    kp_l, vp_l = [], []
    hp_l, hs_l, rcp_l, rcs_l = [], [], [], []
    scp_l, scs_l = [], []

    for i in range(depth):
        kind = i % 3
        j = i // 3
        g_mix = _row(norm_mixer[i])
        if kind == 0:
            wqkv = _layer(attn_w_qkv, j)
            wo = _layer(attn_w_o, j)
            bqkv = _row(attn_b_qkv[j])
            bo = _row(attn_b_o[j])
            sinks = attn_sinks[j]
            xp, kp, vp = _attn_prompt(xp, g_mix, wqkv, bqkv, wo, bo, cos_p, sin_p, sinks, bp, seq)
            xs, nk_t, nv_t = _attn_sample(xs, s_row0, g_mix, wqkv, bqkv, wo, bo, cos_s, sin_s,
                                          sinks, ck_t, cv_t, nk_t, nv_t, j, t_new)
            kp_l.append(kp.reshape(bp, WINDOW, N_KV_HEADS, HEAD_DIM))
            vp_l.append(vp.reshape(bp, WINDOW, N_KV_HEADS, HEAD_DIM))
        elif kind == 1:
            params = (g_mix, rglru_w_gate[j].astype(BF16), rglru_w_in[j].astype(BF16),
                      rglru_conv_w[j], _row(rglru_conv_b[j]), rglru_wa[j].astype(BF16),
                      _row(rglru_ba[j]), rglru_wx[j].astype(BF16), _row(rglru_bx[j]),
                      _row(rglru_lambda[j]), rglru_w_out[j].astype(BF16))
            half = [(rglru_wa[j] * 0.5).astype(BF16), (rglru_wx[j] * 0.5).astype(BF16)]
            prompt_params = params[:5] + (half[0], params[6], half[1]) + params[8:]
            xp, hlast, ulast = _rglru_prompt(xp, prompt_params, bp, seq)
            hp_l.append(hlast[:, V7X_SUBLANES - 1])
            rcp_l.append(ulast[:, V7X_SUBLANES - (RG_CONV_W - 1):])
            prev = _pad_state_rows(state_rglru_conv[j])
            hinit = jnp.repeat(state_rglru_h[j], t_new, axis=0)
            xs, hs_all, u_all = _rglru_sample(xs, s_row0, prev, hinit, params, t_new)
            hs_l.append(hs_all.reshape(bs, t_new, D_MODEL)[:, t_new - 1])
            rcs_l.append(u_all.reshape(bs, t_new, D_MODEL)[:, t_new - (RG_CONV_W - 1):])
        else:
            params = (g_mix, _layer(sconv_w_in, j), sconv_conv_w[j], _layer(sconv_w_out, j))
            xp, vlast = _sconv_prompt(xp, params, bp, seq)
            scp_l.append(vlast[:, V7X_SUBLANES - (SCONV_W - 1):])
            prev = _pad_state_rows(state_shortconv[j])
            xs, v_all = _sconv_sample(xs, s_row0, prev, params)
            scs_l.append(v_all.reshape(bs, t_new, D_MODEL)[:, t_new - (SCONV_W - 1):])

        wg = _layer(ffn_w_gate, i)
        wu = _layer(ffn_w_up, i)
        wd = _layer(ffn_w_down, i)
        d_ff = ffn_w_down.shape[1]
        g_ffn = _row(norm_ffn[i])
        g_fin = _row(norm_final)
        last = i == depth - 1
        if last:
            xp = _ffn(xp, bp * seq, None, g_ffn, wg, wu, wd, d_ff, g_fin, True)
            xs = _ffn(xs, bs * t_new, None, g_ffn, wg, wu, wd, d_ff, g_fin, True)
        else:
            xp = xs = _ffn(xp, bp * seq, xs, g_ffn, wg, wu, wd, d_ff, g_fin, False)
            s_row0 = bp * seq

    return (xp.reshape(bp, seq, D_MODEL), xs.reshape(bs, t_new, D_MODEL),
            jnp.stack(kp_l), jnp.stack(vp_l),
            jnp.transpose(nk_t, from_lane_major), jnp.transpose(nv_t, from_lane_major),
            jnp.stack(hp_l), jnp.stack(hs_l), jnp.stack(rcp_l), jnp.stack(rcs_l),
            jnp.stack(scp_l), jnp.stack(scs_l))
```
